```python
import math
import jax, jax.numpy as jnp
from jax import lax
import numpy as np

D_MODEL = 1024
BATCH = 4
SEQ = 4096
DEPTH = 2
DEC_BATCH = 32
DEC_SEQ = 4
PAST_LEN = 8192
PAGE_SIZE = 128

N_EVEN = (DEPTH + 1) // 2
N_ODD = DEPTH // 2
EPS = 1e-6
A_WIDTH = D_MODEL // 2
A_CONV = 3
B_WIDTH = D_MODEL // 2
B_WINDOWS = (2, 4, 8, 16)
B_GROUP = B_WIDTH // len(B_WINDOWS)
B_PREV = max(B_WINDOWS) - 1
C_PAIRS = ((128, 1), (512, 4), (2048, 16))
C_HPG = 4
C_HEAD_DIM = 64
C_HEADS = len(C_PAIRS) * C_HPG
C_QKV = 3 * C_HEADS * C_HEAD_DIM
C_OUT = C_HPG * C_HEAD_DIM
ATTN_SCALE = C_HEAD_DIM ** -0.5
Q_BLOCK = 128
N_BUCKETS = 32
MAX_DISTANCE = 2048
D_WIDTH = D_MODEL // 2
D_CONV = 31
D_FF = 2816
FFN_CONV = 3

kernel_name = "hybrid_conv_pool_dilattn_conformer_step"


def _rmsnorm(x, g):
    xf = x.astype(jnp.float32)
    y = xf * lax.rsqrt(jnp.mean(xf * xf, axis=-1, keepdims=True) + EPS)
    return (y * g.astype(jnp.float32)).astype(x.dtype)


def _layernorm(x, g, b):
    xf = x.astype(jnp.float32)
    mu = jnp.mean(xf, axis=-1, keepdims=True)
    var = jnp.mean(jnp.square(xf - mu), axis=-1, keepdims=True)
    y = (xf - mu) * lax.rsqrt(var + EPS) * g.astype(jnp.float32) + b.astype(jnp.float32)
    return y.astype(x.dtype)


def _causal_dwconv(u, prev, w):
    ext = jnp.concatenate([prev.astype(u.dtype), u], axis=1)
    out = lax.conv_general_dilated(ext, w[:, None, :].astype(u.dtype), window_strides=(1,), padding="VALID",
                                   dimension_numbers=("NWC", "WIO", "NWC"), feature_group_count=u.shape[-1])
    return out, ext[:, -(w.shape[0] - 1):]


def _causal_multiscale_pool(u, prev, pos0):
    T = u.shape[1]
    ext = jnp.concatenate([prev.astype(u.dtype), u], axis=1)
    ef = ext.astype(jnp.float32)
    cs = jnp.concatenate([jnp.zeros_like(ef[:, :1]), jnp.cumsum(ef, axis=1)], axis=1)
    end = cs[:, B_PREV + 1:B_PREV + 1 + T]
    pos = pos0 + jnp.arange(T)
    outs = []
    for gi, win in enumerate(B_WINDOWS):
        sl = slice(gi * B_GROUP, (gi + 1) * B_GROUP)
        start = cs[:, B_PREV + 1 - win:B_PREV + 1 - win + T, sl]
        cnt = jnp.minimum(pos + 1, win).astype(jnp.float32)[None, :, None]
        outs.append((end[..., sl] - start) / cnt)
    mean = jnp.concatenate(outs, axis=-1)
    return (mean - u.astype(jnp.float32)).astype(u.dtype), ext[:, -B_PREV:]


def _t5_bucket(dist):
    dist = np.asarray(dist)
    max_exact = N_BUCKETS // 2
    large = max_exact + (np.log(np.maximum(dist, max_exact) / max_exact) / np.log(MAX_DISTANCE / max_exact)
                         * (N_BUCKETS - max_exact)).astype(np.int32)
    large = np.minimum(large, N_BUCKETS - 1)
    return np.where(dist < max_exact, dist, large).astype(np.int32)


def _group_bias(rel_bias, g, window, dil):
    buckets = _t5_bucket(dil * np.arange(window // dil + 1))
    return rel_bias[buckets][:, g * C_HPG:(g + 1) * C_HPG].T


def _dilated_attn_prompt(q, k, v, bias_g, window, dil):
    n, S, H, E = q.shape
    L = S // dil
    taps = window // dil
    bq = math.gcd(L, Q_BLOCK)
    nb = L // bq

    def by_residue(t):
        return t.reshape(n, L, dil, H, E).transpose(0, 2, 1, 3, 4)

    pad = ((0, 0), (0, 0), (taps, 0), (0, 0), (0, 0))
    kp = jnp.pad(by_residue(k), pad)
    vp = jnp.pad(by_residue(v), pad)
    kidx = (np.arange(nb) * bq)[:, None] + np.arange(bq + taps)[None, :]
    kb = kp[:, :, kidx]
    vb = vp[:, :, kidx]
    qb = by_residue(q).reshape(n, dil, nb, bq, H, E)
    dist = np.arange(bq)[:, None] + taps - np.arange(bq + taps)[None, :]
    mask = ((dist >= 0) & (dist <= taps))[None] & (kidx >= taps)[:, None, :]
    bias = bias_g[:, np.clip(dist, 0, taps)].astype(jnp.float32)
    logits = jnp.einsum("brnqhe,brnkhe->brnhqk", qb, kb, preferred_element_type=jnp.float32) * ATTN_SCALE
    logits = jnp.where(mask[None, None, :, None], logits + bias[None, None, None], -jnp.inf)
    m = jnp.max(logits, axis=-1, keepdims=True)
    p = jnp.exp(logits - m)
    s = jnp.sum(p, axis=-1, keepdims=True)
    o = jnp.einsum("brnhqk,brnkhe->brnqhe", (p / s).astype(v.dtype), vb)
    lse = (m + jnp.log(s))[..., 0]
    o = o.reshape(n, dil, L, H, E).transpose(0, 2, 1, 3, 4).reshape(n, S, H, E)
    lse = lse.transpose(0, 1, 2, 4, 3).reshape(n, dil, L, H).transpose(0, 2, 1, 3).reshape(n, S, H)
    wb = min(window, S)
    buf = jnp.stack([k[:, -wb:], v[:, -wb:]], axis=2)
    return o, lse, buf


def _dilated_attn_sample(q, k, v, kv_buf, bias_g, window, dil):
    n, T, H, E = q.shape
    wb = kv_buf.shape[1]
    taps = window // dil
    ek = jnp.concatenate([kv_buf[:, :, 0].astype(k.dtype), k], axis=1)
    ev = jnp.concatenate([kv_buf[:, :, 1].astype(v.dtype), v], axis=1)
    idx = wb + np.arange(T)[:, None] - dil * np.arange(taps + 1)[None, :]
    valid = idx >= 0
    idx = np.maximum(idx, 0)
    kg = ek[:, idx]
    vg = ev[:, idx]
    logits = jnp.einsum("bthe,btkhe->bhtk", q, kg, preferred_element_type=jnp.float32) * ATTN_SCALE
    logits = jnp.where(valid[None, None], logits + bias_g.astype(jnp.float32)[:, None, :], -jnp.inf)
    m = jnp.max(logits, axis=-1, keepdims=True)
    p = jnp.exp(logits - m)
    s = jnp.sum(p, axis=-1, keepdims=True)
    o = jnp.einsum("bhtk,btkhe->bthe", (p / s).astype(v.dtype), vg)
    lse = (m + jnp.log(s))[..., 0].transpose(0, 2, 1)
    buf = jnp.stack([ek[:, -wb:], ev[:, -wb:]], axis=2)
    return o, lse, buf


def _mixer_ab(h, a_prev, b_prev, pos0, w_in, a_conv_w, b_w_grp, b_scale, w_out):
    n, T, _ = h.shape
    hh, bg, cg, u = jnp.split(h @ w_in, [A_WIDTH, 2 * A_WIDTH, 3 * A_WIDTH], axis=-1)
    z, a_new = _causal_dwconv(cg * hh, a_prev, a_conv_w)
    ya = bg * z
    pooled, b_new = _causal_multiscale_pool(u, b_prev, pos0)
    yb = jnp.einsum("btgc,gcd->btgd", pooled.reshape(n, T, len(B_WINDOWS), B_GROUP), b_w_grp)
    yb = yb.reshape(n, T, B_WIDTH) * b_scale
    return jnp.concatenate([ya, yb], axis=-1) @ w_out, a_new, b_new


def _mixer_cd(h, kv_bufs, d_prev, rel_bias, w_in, d_conv_w, d_conv_b, d_ln_g, d_ln_b, w_out):
    n, T, _ = h.shape
    proj = h @ w_in
    qkv = proj[..., :C_QKV].reshape(n, T, 3, C_HEADS, C_HEAD_DIM)
    q, k, v = qkv[:, :, 0], qkv[:, :, 1], qkv[:, :, 2]
    outs, lses, new_bufs = [], [], []
    for g, (win, dil) in enumerate(C_PAIRS):
        hs = slice(g * C_HPG, (g + 1) * C_HPG)
        bias_g = _group_bias(rel_bias, g, win, dil)
        if kv_bufs is None:
            o, lse, buf = _dilated_attn_prompt(q[:, :, hs], k[:, :, hs], v[:, :, hs], bias_g, win, dil)
        else:
            o, lse, buf = _dilated_attn_sample(q[:, :, hs], k[:, :, hs], v[:, :, hs], kv_bufs[g], bias_g, win, dil)
        outs.append(o)
        lses.append(lse)
        new_bufs.append(buf)
    wgt = jax.nn.softmax(jnp.stack(lses, axis=2), axis=2)
    yc = jnp.einsum("btgh,btghe->bthe", wgt.astype(q.dtype), jnp.stack(outs, axis=2)).reshape(n, T, C_OUT)
    dv, dg = jnp.split(proj[..., C_QKV:], 2, axis=-1)
    z, d_new = _causal_dwconv(dv * jax.nn.sigmoid(dg), d_prev, d_conv_w)
    yd = jax.nn.silu(_layernorm(z + d_conv_b, d_ln_g, d_ln_b))
    return jnp.concatenate([yc, yd], axis=-1) @ w_out, new_bufs, d_new


def _conv_ffn(h, prev, w_up, conv_w, conv_b, w_down):
    u, new_prev = _causal_dwconv(h @ w_up, prev, conv_w)
    a, g = jnp.split(u + conv_b, 2, axis=-1)
    return (a * jax.nn.silu(g)) @ w_down, new_prev


def _layer_stack(x, c, pos0, st, w):
    n = x.shape[0]
    dt = x.dtype
    new = {key: [] for key in ("a", "b", "c0", "c1", "c2", "d", "f")}
    c_act = jax.nn.silu(c)
    for l in range(DEPTH):
        mod = (c_act @ w["ada_w"][l] + w["ada_b"][l])[:, None, :]
        sh1, sc1, g1, sh2, sc2, g2 = jnp.split(mod, 6, axis=-1)
        ng = w["norm_g"][l]
        i = l // 2
        h = _rmsnorm(x, ng[0]) * (1 + sc1) + sh1
        if l % 2 == 0:
            a_prev = st["a"][i] if st is not None else jnp.zeros((n, A_CONV - 1, A_WIDTH), dt)
            b_prev = st["b"][i] if st is not None else jnp.zeros((n, B_PREV, B_WIDTH), dt)
            y, a_new, b_new = _mixer_ab(h, a_prev, b_prev, pos0, w["ab_w_in"][i], w["a_conv_w"][i],
                                        w["b_w_grp"][i], w["b_scale"][i], w["ab_w_out"][i])
            new["a"].append(a_new)
            new["b"].append(b_new)
        else:
            kv_bufs = [buf[i] for buf in st["c"]] if st is not None else None
            d_prev = st["d"][i] if st is not None else jnp.zeros((n, D_CONV - 1, D_WIDTH), dt)
            y, bufs, d_new = _mixer_cd(h, kv_bufs, d_prev, w["rel_bias"], w["cd_w_in"][i], w["d_conv_w"][i],
                                       w["d_conv_b"][i], w["d_ln_g"][i], w["d_ln_b"][i], w["cd_w_out"][i])
            for g in range(len(C_PAIRS)):
                new["c%d" % g].append(bufs[g])
            new["d"].append(d_new)
        x = x + g1 * _rmsnorm(y, ng[1])
        h = _rmsnorm(x, ng[2]) * (1 + sc2) + sh2
        f_prev = st["f"][l] if st is not None else jnp.zeros((n, FFN_CONV - 1, 2 * D_FF), dt)
        y, f_new = _conv_ffn(h, f_prev, w["ffn_w_up"][l], w["ffn_conv_w"][l], w["ffn_conv_b"][l], w["ffn_w_down"][l])
        new["f"].append(f_new)
        x = x + g2 * _rmsnorm(y, ng[3])
    return x, {key: jnp.stack(val) for key, val in new.items()}


def setup_inputs(seed: int = 0) -> dict:
    key = jax.random.key(seed)
    ks = iter(jax.random.split(key, 40))

    def nrm(shape, scale):
        return jax.random.normal(next(ks), shape, jnp.float32) * scale

    cw = lambda wlen: (N_ODD, DEC_BATCH, min(wlen, PAST_LEN), 2, C_HPG, C_HEAD_DIM)
    return {
        "x_prompt": nrm((BATCH, SEQ, D_MODEL), 1.0),
        "x_sample": nrm((DEC_BATCH, DEC_SEQ, D_MODEL), 1.0),
        "state_a_conv": nrm((N_EVEN, DEC_BATCH, A_CONV - 1, A_WIDTH), 1.0),
        "state_b_pool": nrm((N_EVEN, DEC_BATCH, B_PREV, B_WIDTH), 1.0),
        "cache_c_win128": nrm(cw(C_PAIRS[0][0]), 1.0),
        "cache_c_win512": nrm(cw(C_PAIRS[1][0]), 1.0),
        "cache_c_win2048": nrm(cw(C_PAIRS[2][0]), 1.0),
        "state_d_conv": nrm((N_ODD, DEC_BATCH, D_CONV - 1, D_WIDTH), 0.5),
        "state_ffn_conv": nrm((DEPTH, DEC_BATCH, FFN_CONV - 1, 2 * D_FF), 1.0),
        "c_prompt": nrm((BATCH, D_MODEL), 1.0),
        "c_sample": nrm((DEC_BATCH, D_MODEL), 1.0),
        "ada_w": nrm((DEPTH, D_MODEL, 6 * D_MODEL), 0.5 * D_MODEL ** -0.5),
        "ada_b": nrm((DEPTH, 6 * D_MODEL), 0.02),
        "norm_g": 1.0 + nrm((DEPTH, 4, D_MODEL), 0.05),
        "rel_bias": nrm((N_BUCKETS, C_HEADS), 0.5),
        "ab_w_in": nrm((N_EVEN, D_MODEL, 3 * A_WIDTH + B_WIDTH), D_MODEL ** -0.5),
        "a_conv_w": nrm((N_EVEN, A_CONV, A_WIDTH), A_CONV ** -0.5),
        "b_w_grp": nrm((N_EVEN, len(B_WINDOWS), B_GROUP, B_GROUP), B_GROUP ** -0.5),
        "b_scale": 1.0 + nrm((N_EVEN, B_WIDTH), 0.1),
        "ab_w_out": nrm((N_EVEN, A_WIDTH + B_WIDTH, D_MODEL), (A_WIDTH + B_WIDTH) ** -0.5),
        "cd_w_in": nrm((N_ODD, D_MODEL, C_QKV + 2 * D_WIDTH), D_MODEL ** -0.5),
        "d_conv_w": nrm((N_ODD, D_CONV, D_WIDTH), D_CONV ** -0.5),
        "d_conv_b": nrm((N_ODD, D_WIDTH), 0.02),
        "d_ln_g": 1.0 + nrm((N_ODD, D_WIDTH), 0.05),
        "d_ln_b": nrm((N_ODD, D_WIDTH), 0.02),
        "cd_w_out": nrm((N_ODD, C_OUT + D_WIDTH, D_MODEL), (C_OUT + D_WIDTH) ** -0.5),
        "ffn_w_up": nrm((DEPTH, D_MODEL, 2 * D_FF), D_MODEL ** -0.5),
        "ffn_conv_w": nrm((DEPTH, FFN_CONV, 2 * D_FF), FFN_CONV ** -0.5),
        "ffn_conv_b": nrm((DEPTH, 2 * D_FF), 0.02),
        "ffn_w_down": nrm((DEPTH, D_FF, D_MODEL), D_FF ** -0.5),
    }


def reference(x_prompt, x_sample, state_a_conv, state_b_pool, cache_c_win128, cache_c_win512, cache_c_win2048,
              state_d_conv, state_ffn_conv, c_prompt, c_sample, ada_w, ada_b, norm_g, rel_bias, ab_w_in, a_conv_w,
              b_w_grp, b_scale, ab_w_out, cd_w_in, d_conv_w, d_conv_b, d_ln_g, d_ln_b, cd_w_out, ffn_w_up,
              ffn_conv_w, ffn_conv_b, ffn_w_down):
    w = dict(ada_w=ada_w, ada_b=ada_b, norm_g=norm_g, rel_bias=rel_bias, ab_w_in=ab_w_in, a_conv_w=a_conv_w,
             b_w_grp=b_w_grp, b_scale=b_scale, ab_w_out=ab_w_out, cd_w_in=cd_w_in, d_conv_w=d_conv_w,
             d_conv_b=d_conv_b, d_ln_g=d_ln_g, d_ln_b=d_ln_b, cd_w_out=cd_w_out, ffn_w_up=ffn_w_up,
             ffn_conv_w=ffn_conv_w, ffn_conv_b=ffn_conv_b, ffn_w_down=ffn_w_down)
    st = dict(a=state_a_conv, b=state_b_pool, c=(cache_c_win128, cache_c_win512, cache_c_win2048),
              d=state_d_conv, f=state_ffn_conv)
    y_prompt, sp = _layer_stack(x_prompt, c_prompt, 0, None, w)
    y_sample, ss = _layer_stack(x_sample, c_sample, PAST_LEN, st, w)
    return (y_prompt, y_sample, sp["a"], ss["a"], sp["b"], ss["b"], sp["c0"], ss["c0"], sp["c1"], ss["c1"],
            sp["c2"], ss["c2"], sp["d"], ss["d"], sp["f"], ss["f"])
```

```python
import functools
import math

import numpy as np
import jax
import jax.numpy as jnp
from jax import lax
from jax.experimental import pallas as pl
from jax.experimental.pallas import tpu as pltpu

D_MODEL = 1024
BATCH = 4
SEQ = 4096
DEPTH = 2
DEC_BATCH = 32
DEC_SEQ = 4
PAST_LEN = 8192
EPS = 1e-6
A_WIDTH = 512
A_CONV = 3
B_WIDTH = 512
B_WINDOWS = (2, 4, 8, 16)
B_GROUP = 128
B_PREV = 15
C_PAIRS = ((128, 1), (512, 4), (2048, 16))
C_HPG = 4
C_HEAD_DIM = 64
C_HEADS = 12
C_QKV = 2304
C_GW = C_HPG * C_HEAD_DIM
C_TAPS = 128
ATTN_SCALE = C_HEAD_DIM ** -0.5
Q_BLOCK = 128
N_BUCKETS = 32
MAX_DISTANCE = 2048
D_WIDTH = 512
D_CONV = 31
D_FF = 2816
FFN_CONV = 3

SUBLANES = 8
VMEM_LIMIT = 56 * 1024 * 1024
NEG = -1e30
TM_PROMPT = 512
FF_CHUNK = 256
D_ROWS = 32

F32 = jnp.float32
BF16 = jnp.bfloat16


def _ru8(n):
    return -(-n // SUBLANES) * SUBLANES


def _pool_levels(rs):
    l1 = _ru8(rs)
    l2 = _ru8(l1 + 2 * rs)
    l3 = _ru8(l2 + 4 * rs)
    l4 = _ru8(l3 + 8 * rs)
    return l1, l2, l3, l4


def _rms(x, g):
    return x * lax.rsqrt(jnp.mean(x * x, axis=-1, keepdims=True) + EPS) * g


def _sigmoid(x):
    return 1.0 / (1.0 + jnp.exp(-x))


def _cparams(n_axes):
    return pltpu.CompilerParams(dimension_semantics=("arbitrary",) * n_axes, vmem_limit_bytes=VMEM_LIMIT)


def _const_spec(shape):
    nd = len(shape)
    return pl.BlockSpec(shape, lambda *_: (0,) * nd, pipeline_mode=pl.Buffered(1))


def _ada_body(c_ref, w_ref, b_ref, o_ref):
    c = c_ref[...]
    ca = c * _sigmoid(c)
    o_ref[...] = jnp.dot(ca.astype(BF16), w_ref[...].astype(BF16), preferred_element_type=F32) + b_ref[...]


def _ada(c_all, ada_w, ada_b):
    rows = c_all.shape[0]
    tn = 1536
    return pl.pallas_call(
        _ada_body,
        grid=(DEPTH, 6 * D_MODEL // tn),
        in_specs=[pl.BlockSpec((rows, D_MODEL), lambda l, n: (0, 0)),
                  pl.BlockSpec((None, D_MODEL, tn), lambda l, n: (l, 0, n)),
                  pl.BlockSpec((None, 1, tn), lambda l, n: (l, 0, n))],
        out_specs=pl.BlockSpec((None, rows, tn), lambda l, n: (l, 0, n)),
        out_shape=jax.ShapeDtypeStruct((DEPTH, rows, 6 * D_MODEL), F32),
        compiler_params=_cparams(2),
        name="ada",
    )(c_all, ada_w, ada_b.reshape(DEPTH, 1, 6 * D_MODEL))


def _ab_body(x_ref, mod_ref, ng_ref, ap_ref, bp_ref, win_ref, aw_ref, bw_ref, bs_ref, wout_ref,
             x1_ref, an_ref, bn_ref, ea, eb, s2, s4, s8, *, TM, RS, NT, pos0):
    j = pl.program_id(1)
    HA = _ru8((A_CONV - 1) * RS)
    L1, L2, L3, HB = _pool_levels(RS)
    E = HB + TM
    na, nbp = (A_CONV - 1) * RS, B_PREV * RS

    @pl.when(j == 0)
    def _():
        ea[HA - na:HA, :] = ap_ref[...]
        if HB > nbp:
            eb[0:HB - nbp, :] = jnp.zeros((HB - nbp, B_WIDTH), F32)
        eb[HB - nbp:HB, :] = bp_ref[...]

    x = x_ref[...]
    sh1, sc1, g1 = mod_ref[0], mod_ref[1], mod_ref[2]
    h = _rms(x, ng_ref[0:1, :]) * (1.0 + sc1) + sh1
    proj = jnp.dot(h.astype(BF16), win_ref[...], preferred_element_type=F32)
    hh, bg = proj[:, 0:A_WIDTH], proj[:, A_WIDTH:2 * A_WIDTH]
    cg, u = proj[:, 2 * A_WIDTH:3 * A_WIDTH], proj[:, 3 * A_WIDTH:]

    v = cg * hh
    ea[HA:HA + TM, :] = v
    z = (aw_ref[2:3, :] * v + aw_ref[1:2, :] * ea[HA - RS:HA - RS + TM, :]
         + aw_ref[0:1, :] * ea[HA - 2 * RS:HA - 2 * RS + TM, :])
    ya = bg * z
    an_ref[...] = ea[HA + TM - na:HA + TM, :]
    if NT > 1:
        ea[HA - na:HA, :] = ea[HA + TM - na:HA + TM, :]

    eb[HB:E, :] = u
    s2[L1:E, :] = eb[L1:E, :] + eb[L1 - RS:E - RS, :]
    s4[L2:E, :] = s2[L2:E, 128:512] + s2[L2 - 2 * RS:E - 2 * RS, 128:512]
    s8[L3:E, :] = s4[L3:E, 128:384] + s4[L3 - 4 * RS:E - 4 * RS, 128:384]
    wsum = (s2[HB:E, 0:128], s4[HB:E, 0:128], s8[HB:E, 0:128],
            s8[HB:E, 128:256] + s8[HB - 8 * RS:E - 8 * RS, 128:256])
    bn_ref[...] = eb[E - nbp:E, :]
    if NT > 1:
        eb[HB - nbp:HB, :] = eb[E - nbp:E, :]
    row = lax.broadcasted_iota(jnp.int32, (TM, B_GROUP), 0) + j * TM
    pos1 = lax.shift_right_logical(row, int(math.log2(RS))) + (pos0 + 1)
    ybs = []
    for g, win in enumerate(B_WINDOWS):
        cnt = jnp.minimum(pos1, win).astype(F32)
        pooled = wsum[g] / cnt - u[:, g * B_GROUP:(g + 1) * B_GROUP]
        ybs.append(jnp.dot(pooled.astype(BF16), bw_ref[g], preferred_element_type=F32))
    yb = jnp.concatenate(ybs, axis=-1) * bs_ref[...]

    ycat = jnp.concatenate([ya, yb], axis=-1).astype(BF16)
    y = jnp.dot(ycat, wout_ref[...], preferred_element_type=F32)
    x1_ref[...] = x + g1 * _rms(y, ng_ref[1:2, :])


def _mixer_ab(x, mod, ng, a_prev, b_prev, w_in, a_w, b_w, b_scale, w_out, *, TM, RS, pos0):
    NB, R, _ = x.shape
    NT = R // TM
    MR = mod.shape[2]
    _, _, _, HB = _pool_levels(RS)
    HA = _ru8((A_CONV - 1) * RS)
    E = HB + TM
    na, nbp = (A_CONV - 1) * RS, B_PREV * RS
    tile = lambda c: pl.BlockSpec((None, TM, c), lambda n, j: (n, j, 0))
    per_n = lambda r, c: pl.BlockSpec((None, r, c), lambda n, j: (n, 0, 0))
    return pl.pallas_call(
        functools.partial(_ab_body, TM=TM, RS=RS, NT=NT, pos0=pos0),
        grid=(NB, NT),
        in_specs=[tile(D_MODEL),
                  pl.BlockSpec((None, 6, MR, D_MODEL), lambda n, j: (n, 0, 0, 0)),
                  _const_spec((4, D_MODEL)),
                  per_n(na, A_WIDTH), per_n(nbp, B_WIDTH),
                  _const_spec(w_in.shape), _const_spec(a_w.shape), _const_spec(b_w.shape),
                  _const_spec(b_scale.shape), _const_spec(w_out.shape)],
        out_specs=[tile(D_MODEL), per_n(na, A_WIDTH), per_n(nbp, B_WIDTH)],
        out_shape=[jax.ShapeDtypeStruct((NB, R, D_MODEL), F32),
                   jax.ShapeDtypeStruct((NB, na, A_WIDTH), F32),
                   jax.ShapeDtypeStruct((NB, nbp, B_WIDTH), F32)],
        scratch_shapes=[pltpu.VMEM((HA + TM, A_WIDTH), F32), pltpu.VMEM((E, B_WIDTH), F32),
                        pltpu.VMEM((E, 512), F32), pltpu.VMEM((E, 384), F32), pltpu.VMEM((E, 256), F32)],
        compiler_params=_cparams(2),
        name="mixer_ab",
    )(x, mod, ng, a_prev, b_prev, w_in, a_w, b_w, b_scale, w_out)


def _ffn_body(x_ref, mod_ref, ng_ref, fp_ref, wup_ref, cw_ref, cb_ref, wdn_ref,
              y_ref, fn_ref, halo, ext, acc, *, TM, RS, NT):
    j = pl.program_id(1)
    nf = (FFN_CONV - 1) * RS
    HF = _ru8(nf)

    @pl.when(j == 0)
    def _():
        halo[...] = fp_ref[...]

    x = x_ref[...]
    sh2, sc2, g2 = mod_ref[3], mod_ref[4], mod_ref[5]
    hb = (_rms(x, ng_ref[2:3, :]) * (1.0 + sc2) + sh2).astype(BF16)

    def conv_part(col):
        up = jnp.dot(hb, wup_ref[:, col:col + FF_CHUNK], preferred_element_type=F32)
        ext[HF - nf:HF, :] = halo[:, col:col + FF_CHUNK]
        ext[HF:HF + TM, :] = up
        out = (cw_ref[2:3, col:col + FF_CHUNK] * up
               + cw_ref[1:2, col:col + FF_CHUNK] * ext[HF - RS:HF - RS + TM, :]
               + cw_ref[0:1, col:col + FF_CHUNK] * ext[HF - 2 * RS:HF - 2 * RS + TM, :]
               + cb_ref[:, col:col + FF_CHUNK])
        halo[:, col:col + FF_CHUNK] = ext[HF + TM - nf:HF + TM, :]
        return out

    for c in range(D_FF // FF_CHUNK):
        a = conv_part(c * FF_CHUNK)
        g = conv_part(D_FF + c * FF_CHUNK)
        act = (a * (g * _sigmoid(g))).astype(BF16)
        part = jnp.dot(act, wdn_ref[c * FF_CHUNK:(c + 1) * FF_CHUNK, :], preferred_element_type=F32)
        if c == 0:
            acc[...] = part
        else:
            acc[...] += part
    fn_ref[...] = halo[...]
    y_ref[...] = x + g2 * _rms(acc[...], ng_ref[3:4, :])


def _conv_ffn(x, mod, ng, f_prev, w_up, conv_w, conv_b, w_down, *, TM, RS):
    NB, R, _ = x.shape
    NT = R // TM
    MR = mod.shape[2]
    nf = (FFN_CONV - 1) * RS
    tile = pl.BlockSpec((None, TM, D_MODEL), lambda n, j: (n, j, 0))
    per_n = pl.BlockSpec((None, nf, 2 * D_FF), lambda n, j: (n, 0, 0))
    return pl.pallas_call(
        functools.partial(_ffn_body, TM=TM, RS=RS, NT=NT),
        grid=(NB, NT),
        in_specs=[tile,
                  pl.BlockSpec((None, 6, MR, D_MODEL), lambda n, j: (n, 0, 0, 0)),
                  _const_spec((4, D_MODEL)),
                  per_n,
                  _const_spec(w_up.shape), _const_spec(conv_w.shape), _const_spec(conv_b.shape),
                  _const_spec(w_down.shape)],
        out_specs=[tile, per_n],
        out_shape=[jax.ShapeDtypeStruct((NB, R, D_MODEL), F32),
                   jax.ShapeDtypeStruct((NB, nf, 2 * D_FF), F32)],
        scratch_shapes=[pltpu.VMEM((nf, 2 * D_FF), F32), pltpu.VMEM((_ru8(nf) + TM, FF_CHUNK), F32),
                        pltpu.VMEM((TM, D_MODEL), F32)],
        compiler_params=_cparams(2),
        name="conv_ffn",
    )(x, mod, ng, f_prev, w_up, conv_w, conv_b, w_down)


def _cd_in_body(x_ref, mod_ref, ng_ref, dp_ref, win_ref, dw_ref, db_ref, lg_ref, lb_ref,
                q_ref, k_ref, v_ref, kvf_ref, yd_ref, dn_ref, ed, *, TM, RS, NT):
    j = pl.program_id(1)
    nd = (D_CONV - 1) * RS
    HD = _ru8(nd)

    @pl.when(j == 0)
    def _():
        ed[HD - nd:HD, :] = dp_ref[...]

    x = x_ref[...]
    sh1, sc1 = mod_ref[0], mod_ref[1]
    h = _rms(x, ng_ref[0:1, :]) * (1.0 + sc1) + sh1
    proj = jnp.dot(h.astype(BF16), win_ref[...], preferred_element_type=F32)
    q_ref[...] = (proj[:, 0:768] * ATTN_SCALE).astype(BF16)
    k_ref[...] = proj[:, 768:1536].astype(BF16)
    v_ref[...] = proj[:, 1536:2304].astype(BF16)
    kvf_ref[...] = proj[:, 768:2304]
    dv, dg = proj[:, C_QKV:C_QKV + D_WIDTH], proj[:, C_QKV + D_WIDTH:]
    ed[HD:HD + TM, :] = dv * _sigmoid(dg)

    db, lg, lb = db_ref[...], lg_ref[...], lb_ref[...]
    for c in range(TM // D_ROWS):
        acc = jnp.zeros((D_ROWS, D_WIDTH), F32)
        for kk in range(D_CONV):
            off = HD - (D_CONV - 1 - kk) * RS + c * D_ROWS
            acc = acc + dw_ref[kk:kk + 1, :] * ed[off:off + D_ROWS, :]
        zc = acc + db
        mu = jnp.mean(zc, axis=-1, keepdims=True)
        zc = zc - mu
        var = jnp.mean(zc * zc, axis=-1, keepdims=True)
        yl = zc * lax.rsqrt(var + EPS) * lg + lb
        yd_ref[c * D_ROWS:(c + 1) * D_ROWS, :] = (yl * _sigmoid(yl)).astype(BF16)

    dn_ref[...] = ed[HD + TM - nd:HD + TM, :]
    if NT > 1:
        ed[HD - nd:HD, :] = ed[HD + TM - nd:HD + TM, :]


def _cd_in(x, mod, ng, d_prev, w_in, d_w, d_b, ln_g, ln_b, *, TM, RS, tail_rows):
    NB, R, _ = x.shape
    NT = R // TM
    MR = mod.shape[2]
    nd = (D_CONV - 1) * RS
    n_tail = tail_rows // TM
    tile = lambda c: pl.BlockSpec((None, TM, c), lambda n, j: (n, j, 0))
    per_n = pl.BlockSpec((None, nd, D_WIDTH), lambda n, j: (n, 0, 0))
    tail = pl.BlockSpec((None, TM, 1536), lambda n, j: (n, jnp.maximum(j - (NT - n_tail), 0), 0))
    return pl.pallas_call(
        functools.partial(_cd_in_body, TM=TM, RS=RS, NT=NT),
        grid=(NB, NT),
        in_specs=[tile(D_MODEL),
                  pl.BlockSpec((None, 6, MR, D_MODEL), lambda n, j: (n, 0, 0, 0)),
                  _const_spec((4, D_MODEL)),
                  per_n,
                  _const_spec(w_in.shape), _const_spec(d_w.shape), _const_spec(d_b.shape),
                  _const_spec(ln_g.shape), _const_spec(ln_b.shape)],
        out_specs=[tile(768), tile(768), tile(768), tail, tile(D_WIDTH), per_n],
        out_shape=[jax.ShapeDtypeStruct((NB, R, 768), BF16), jax.ShapeDtypeStruct((NB, R, 768), BF16),
                   jax.ShapeDtypeStruct((NB, R, 768), BF16),
                   jax.ShapeDtypeStruct((NB, tail_rows, 1536), F32),
                   jax.ShapeDtypeStruct((NB, R, D_WIDTH), BF16),
                   jax.ShapeDtypeStruct((NB, nd, D_WIDTH), F32)],
        scratch_shapes=[pltpu.VMEM((_ru8(nd) + TM, D_WIDTH), F32)],
        compiler_params=_cparams(2),
        name="cd_in",
    )(x, mod, ng, d_prev, w_in, d_w, d_b, ln_g, ln_b)


def _head_of_lane(shape):
    return lax.shift_right_logical(lax.broadcasted_iota(jnp.int32, shape, 1), 6)


def _attn_prompt_body(q_ref, kp_ref, kc_ref, vp_ref, vc_ref, bias_ref, o_ref, lse_ref):
    q = q_ref[...]
    lane_head = _head_of_lane(q.shape)
    zero = jnp.zeros_like(q)
    qs = jnp.concatenate([jnp.where(lane_head == h, q, zero) for h in range(C_HPG)], axis=0)
    kcat = jnp.concatenate([kp_ref[...], kc_ref[...]], axis=0)
    vcat = jnp.concatenate([vp_ref[...], vc_ref[...]], axis=0)
    logits = lax.dot_general(qs, kcat, (((1,), (1,)), ((), ())), preferred_element_type=F32) + bias_ref[...]
    m = jnp.max(logits, axis=-1, keepdims=True)
    p = jnp.exp(logits - m)
    s = jnp.sum(p, axis=-1, keepdims=True)
    pv = jnp.dot(p.astype(BF16), vcat, preferred_element_type=F32)
    on = pv / s
    lse = m + jnp.log(s)
    o = jnp.zeros((Q_BLOCK, C_GW), F32)
    l = jnp.zeros((Q_BLOCK, C_GW), F32)
    for h in range(C_HPG):
        sel = lane_head == h
        o = jnp.where(sel, on[h * Q_BLOCK:(h + 1) * Q_BLOCK, :], o)
        l = jnp.where(sel, lse[h * Q_BLOCK:(h + 1) * Q_BLOCK, :], l)
    o_ref[...] = o
    lse_ref[...] = l


def _attn_prompt(q, k, v, bias, g, dil):
    N, S, _ = q.shape
    L = S // dil
    nb = L // Q_BLOCK
    q3, k3, v3 = (t.reshape(N, L, dil * 768) for t in (q, k, v))
    cur = pl.BlockSpec((None, Q_BLOCK, C_GW), lambda n, r, i: (n, i, r * 3 + g))
    prev = pl.BlockSpec((None, Q_BLOCK, C_GW), lambda n, r, i: (n, jnp.maximum(i - 1, 0), r * 3 + g))
    out = pl.BlockSpec((None, Q_BLOCK, C_GW), lambda n, r, i: (n, i, r))
    o, lse = pl.pallas_call(
        _attn_prompt_body,
        grid=(N, dil, nb),
        in_specs=[cur, prev, cur, prev, cur,
                  pl.BlockSpec((None, C_HPG * Q_BLOCK, 2 * Q_BLOCK), lambda n, r, i: (jnp.minimum(i, 1), 0, 0))],
        out_specs=[out, out],
        out_shape=[jax.ShapeDtypeStruct((N, L, dil * C_GW), F32)] * 2,
        compiler_params=_cparams(3),
        name="attn_prompt_g%d" % g,
    )(q3, k3, k3, v3, v3, bias)
    return o.reshape(N, S, C_GW), lse.reshape(N, S, C_GW)


SROWS = 32


def _attn_sample_body(q_ref, kvn_ref, c0_ref, c1_ref, c2_ref, bc0_ref, bc1_ref, bc2_ref, bn_ref,
                      yc_ref, n0_ref, n1_ref, n2_ref):
    caches = (c0_ref, c1_ref, c2_ref)
    bcs = (bc0_ref, bc1_ref, bc2_ref)
    news = (n0_ref, n1_ref, n2_ref)
    lane_head = _head_of_lane((SROWS, C_GW))
    row_head = lax.shift_right_logical(lax.broadcasted_iota(jnp.int32, (SROWS, C_GW), 0), 3)
    own = lane_head == row_head
    lane128 = lax.broadcasted_iota(jnp.int32, (SROWS, 128), 1)
    kvn = kvn_ref[...]
    outs, lses = [], []
    for g, (win, dil) in enumerate(C_PAIRS):
        wb = win
        qg = q_ref[:, g * C_GW:(g + 1) * C_GW]
        qs = jnp.where(own, qg, jnp.zeros_like(qg))
        cache = caches[g]
        kc = cache[:, 0:C_GW].astype(BF16)
        vc = cache[:, C_GW:2 * C_GW].astype(BF16)
        kn = kvn[:, g * C_GW:(g + 1) * C_GW]
        vn = kvn[:, 768 + g * C_GW:768 + (g + 1) * C_GW]
        lc = lax.dot_general(qs, kc, (((1,), (1,)), ((), ())), preferred_element_type=F32) + bcs[g][...]
        qf = qs.astype(F32)
        ln = bn_ref[g]
        for c in range(DEC_SEQ):
            d = jnp.sum(qf * kn[c:c + 1, :], axis=-1, keepdims=True)
            ln = ln + jnp.where(lane128 == c, d, 0.0)
        m = jnp.maximum(jnp.max(lc, axis=-1, keepdims=True), jnp.max(ln, axis=-1, keepdims=True))
        pc = jnp.exp(lc - m)
        pn = jnp.exp(ln - m)
        s = jnp.sum(pc, axis=-1, keepdims=True) + jnp.sum(pn, axis=-1, keepdims=True)
        pv = jnp.dot(pc.astype(BF16), vc, preferred_element_type=F32)
        for c in range(DEC_SEQ):
            pcol = jnp.sum(jnp.where(lane128 == c, pn, 0.0), axis=-1, keepdims=True)
            pv = pv + pcol * vn[c:c + 1, :]
        outs.append(pv / s)
        lses.append(m + jnp.log(s))
        news[g][0:wb - DEC_SEQ, :] = cache[DEC_SEQ:wb, :]
        news[g][wb - DEC_SEQ:wb, :] = jnp.concatenate([kn, vn], axis=-1)
    mm = jnp.maximum(jnp.maximum(lses[0], lses[1]), lses[2])
    es = [jnp.exp(l - mm) for l in lses]
    den = es[0] + es[1] + es[2]
    y = (es[0] / den) * outs[0] + (es[1] / den) * outs[1] + (es[2] / den) * outs[2]
    y = jnp.where(own, y, 0.0)
    yc_ref[...] = y[0:8, :] + y[8:16, :] + y[16:24, :] + y[24:32, :]


def _attn_sample(q_rep, kvn, caches, bias_c, bias_n):
    NBt = q_rep.shape[0]
    per_b = lambda r, c: pl.BlockSpec((None, r, c), lambda b: (b, 0, 0))
    wbs = [w for w, _ in C_PAIRS]
    return pl.pallas_call(
        _attn_sample_body,
        grid=(NBt,),
        in_specs=[per_b(SROWS, 768), per_b(DEC_SEQ, 1536)] + [per_b(w, 512) for w in wbs]
                 + [_const_spec((SROWS, w)) for w in wbs] + [_const_spec((3, SROWS, 128))],
        out_specs=[per_b(8, C_GW)] + [per_b(w, 512) for w in wbs],
        out_shape=[jax.ShapeDtypeStruct((NBt, 8, C_GW), F32)]
                  + [jax.ShapeDtypeStruct((NBt, w, 512), F32) for w in wbs],
        compiler_params=_cparams(1),
        name="attn_sample",
    )(q_rep, kvn, *caches, *bias_c, bias_n)


def _cd_out_body(*refs, combine):
    if combine:
        (x_ref, mod_ref, ng_ref, o0, o1, o2, l0, l1, l2, yd_ref, wout_ref, x1_ref) = refs
        la, lb, lc = l0[...], l1[...], l2[...]
        mm = jnp.maximum(jnp.maximum(la, lb), lc)
        ea, eb, ec = jnp.exp(la - mm), jnp.exp(lb - mm), jnp.exp(lc - mm)
        den = ea + eb + ec
        yc = (ea / den) * o0[...] + (eb / den) * o1[...] + (ec / den) * o2[...]
    else:
        (x_ref, mod_ref, ng_ref, yc_ref, yd_ref, wout_ref, x1_ref) = refs
        yc = yc_ref[...]
    x = x_ref[...]
    g1 = mod_ref[2]
    y = (jnp.dot(yc.astype(BF16), wout_ref[0:C_GW, :], preferred_element_type=F32)
         + jnp.dot(yd_ref[...], wout_ref[C_GW:, :], preferred_element_type=F32))
    x1_ref[...] = x + g1 * _rms(y, ng_ref[1:2, :])


def _cd_out(x, mod, ng, yc_parts, yd, w_out, *, TM):
    NB, R, _ = x.shape
    NT = R // TM
    MR = mod.shape[2]
    tile = lambda c: pl.BlockSpec((None, TM, c), lambda n, j: (n, j, 0))
    combine = len(yc_parts) > 1
    return pl.pallas_call(
        functools.partial(_cd_out_body, combine=combine),
        grid=(NB, NT),
        in_specs=[tile(D_MODEL),
                  pl.BlockSpec((None, 6, MR, D_MODEL), lambda n, j: (n, 0, 0, 0)),
                  _const_spec((4, D_MODEL))]
                 + [tile(C_GW)] * len(yc_parts) + [tile(D_WIDTH), _const_spec(w_out.shape)],
        out_specs=tile(D_MODEL),
        out_shape=jax.ShapeDtypeStruct((NB, R, D_MODEL), F32),
        compiler_params=_cparams(2),
        name="cd_out",
    )(x, mod, ng, *yc_parts, yd, w_out)


def _t5_bucket(dist):
    dist = np.asarray(dist)
    max_exact = N_BUCKETS // 2
    large = max_exact + (np.log(np.maximum(dist, max_exact) / max_exact) / np.log(MAX_DISTANCE / max_exact)
                         * (N_BUCKETS - max_exact)).astype(np.int32)
    large = np.minimum(large, N_BUCKETS - 1)
    return np.where(dist < max_exact, dist, large).astype(np.int32)


def _group_bias(rel_bias, g, dil):
    buckets = _t5_bucket(dil * np.arange(C_TAPS + 1))
    return rel_bias[buckets][:, g * C_HPG:(g + 1) * C_HPG].T


def _prompt_bias(bias_g):
    dist = np.arange(Q_BLOCK)[:, None] + Q_BLOCK - np.arange(2 * Q_BLOCK)[None, :]
    valid = (dist >= 0) & (dist <= C_TAPS)
    b = bias_g[:, np.clip(dist, 0, C_TAPS)].astype(F32)
    general = jnp.where(valid[None], b, NEG)
    first = jnp.where((valid & (np.arange(2 * Q_BLOCK)[None, :] >= Q_BLOCK))[None], b, NEG)
    return jnp.stack([first, general]).reshape(2, C_HPG * Q_BLOCK, 2 * Q_BLOCK)


def _sample_bias(bias_g, wb, dil):
    t = np.arange(DEC_SEQ)[:, None]
    d = wb + t - np.arange(wb)[None, :]
    valid = (d % dil == 0) & (d // dil <= C_TAPS)
    bc = jnp.where(valid[None], bias_g[:, np.clip(d // dil, 0, C_TAPS)].astype(F32), NEG)
    bc = jnp.pad(bc, ((0, 0), (0, 8 - DEC_SEQ), (0, 0))).reshape(SROWS, wb)
    dn = t - np.arange(DEC_SEQ)[None, :]
    validn = (dn >= 0) & (dn % dil == 0)
    bn = jnp.where(validn[None], bias_g[:, np.clip(dn // dil, 0, C_TAPS)].astype(F32), NEG)
    bn = jnp.pad(bn, ((0, 0), (0, 8 - DEC_SEQ), (0, 128 - DEC_SEQ)), constant_values=NEG).reshape(SROWS, 128)
    return bc, bn


def _time_major(s):
    b, k, c = s.shape
    return s.transpose(1, 0, 2).reshape(1, k * b, c)


def _batch_major(s, b):
    _, r, c = s.shape
    return s.reshape(r // b, b, c).transpose(1, 0, 2)


def _stack(x, mod, st, w, *, TM, RS, pos0):
    NB = x.shape[0]
    new = {}
    ng = w["norm_g"]
    x, new["a"], new["b"] = _mixer_ab(x, mod[0], ng[0], st["a"], st["b"], w["ab_w_in"], w["a_conv_w"],
                                      w["b_w_grp"], w["b_scale"], w["ab_w_out"], TM=TM, RS=RS, pos0=pos0)
    x, new["f0"] = _conv_ffn(x, mod[0], ng[0], st["f0"], w["ffn_w_up"][0], w["ffn_conv_w"][0],
                             w["ffn_conv_b"][0], w["ffn_w_down"][0], TM=TM, RS=RS)
    tail_rows = min(C_PAIRS[-1][0], x.shape[1])
    q, k, v, kvf, yd, new["d"] = _cd_in(x, mod[1], ng[1], st["d"], w["cd_w_in"], w["d_conv_w"], w["d_conv_b"],
                                        w["d_ln_g"], w["d_ln_b"], TM=TM, RS=RS, tail_rows=tail_rows)
    if RS == 1:
        parts_o, parts_l = [], []
        for g, (win, dil) in enumerate(C_PAIRS):
            o, lse = _attn_prompt(q, k, v, _prompt_bias(w["bias_g"][g]), g, dil)
            parts_o.append(o)
            parts_l.append(lse)
            wb = min(win, x.shape[1])
            kt = kvf[:, tail_rows - wb:, g * C_GW:(g + 1) * C_GW]
            vt = kvf[:, tail_rows - wb:, 768 + g * C_GW:768 + (g + 1) * C_GW]
            new["c%d" % g] = jnp.stack([kt, vt], axis=2).reshape(NB, wb, 2, C_HPG, C_HEAD_DIM)
        yc_parts = parts_o + parts_l
    else:
        B = RS
        qb = _batch_major(q, B)
        q_rep = jnp.pad(jnp.broadcast_to(qb[:, None], (B, C_HPG, DEC_SEQ, 768)),
                        ((0, 0), (0, 0), (0, 8 - DEC_SEQ), (0, 0))).reshape(B, SROWS, 768)
        kvn = _batch_major(kvf, B)
        bias = [_sample_bias(w["bias_g"][g], win, dil) for g, (win, dil) in enumerate(C_PAIRS)]
        yc, n0, n1, n2 = _attn_sample(q_rep, kvn, st["c"], [b[0] for b in bias],
                                      jnp.stack([b[1] for b in bias]))
        for g, n in enumerate((n0, n1, n2)):
            new["c%d" % g] = n.reshape(B, n.shape[1], 2, C_HPG, C_HEAD_DIM)
        yc_parts = [_time_major(yc[:, :DEC_SEQ])]
    x = _cd_out(x, mod[1], ng[1], yc_parts, yd, w["cd_w_out"], TM=TM)
    x, new["f1"] = _conv_ffn(x, mod[1], ng[1], st["f1"], w["ffn_w_up"][1], w["ffn_conv_w"][1],
                             w["ffn_conv_b"][1], w["ffn_w_down"][1], TM=TM, RS=RS)
    return x, new


def kernel(x_prompt, x_sample, state_a_conv, state_b_pool, cache_c_win128, cache_c_win512, cache_c_win2048,
           state_d_conv, state_ffn_conv, c_prompt, c_sample, ada_w, ada_b, norm_g, rel_bias, ab_w_in, a_conv_w,
           b_w_grp, b_scale, ab_w_out, cd_w_in, d_conv_w, d_conv_b, d_ln_g, d_ln_b, cd_w_out, ffn_w_up,
           ffn_conv_w, ffn_conv_b, ffn_w_down):
    B, T = DEC_BATCH, DEC_SEQ
    w = dict(norm_g=norm_g,
             ab_w_in=ab_w_in[0].astype(BF16), a_conv_w=a_conv_w[0], b_w_grp=b_w_grp[0].astype(BF16),
             b_scale=b_scale, ab_w_out=ab_w_out[0].astype(BF16),
             cd_w_in=cd_w_in[0].astype(BF16), d_conv_w=d_conv_w[0], d_conv_b=d_conv_b, d_ln_g=d_ln_g,
             d_ln_b=d_ln_b, cd_w_out=cd_w_out[0].astype(BF16),
             ffn_w_up=ffn_w_up.astype(BF16), ffn_conv_w=ffn_conv_w, ffn_conv_b=ffn_conv_b[:, None, :],
             ffn_w_down=ffn_w_down.astype(BF16),
             bias_g=[_group_bias(rel_bias, g, dil) for g, (_, dil) in enumerate(C_PAIRS)])

    mod = _ada(jnp.concatenate([c_prompt, c_sample], axis=0), ada_w, ada_b)
    mod_p = mod[:, :BATCH].reshape(DEPTH, BATCH, 6, 1, D_MODEL)
    mod_s = mod[:, BATCH:].reshape(DEPTH, B, 6, D_MODEL).transpose(0, 2, 1, 3)
    mod_s = jnp.broadcast_to(mod_s[:, :, None], (DEPTH, 6, T, B, D_MODEL)).reshape(DEPTH, 1, 6, T * B, D_MODEL)

    zeros = lambda k, c: jnp.zeros((BATCH, k, c), F32)
    st_p = dict(a=zeros(A_CONV - 1, A_WIDTH), b=zeros(B_PREV, B_WIDTH), d=zeros(D_CONV - 1, D_WIDTH),
                f0=zeros(FFN_CONV - 1, 2 * D_FF), f1=zeros(FFN_CONV - 1, 2 * D_FF))
    y_p, np_ = _stack(x_prompt, mod_p, st_p, w, TM=TM_PROMPT, RS=1, pos0=0)

    st_s = dict(a=_time_major(state_a_conv[0]), b=_time_major(state_b_pool[0]), d=_time_major(state_d_conv[0]),
                f0=_time_major(state_ffn_conv[0]), f1=_time_major(state_ffn_conv[1]),
                c=[c[0].reshape(B, c.shape[2], 512) for c in (cache_c_win128, cache_c_win512, cache_c_win2048)])
    y_s, ns = _stack(_time_major(x_sample), mod_s, st_s, w, TM=T * B, RS=B, pos0=PAST_LEN)

    bm = lambda s: _batch_major(s, B)
    return (y_p, bm(y_s),
            np_["a"][None], bm(ns["a"])[None], np_["b"][None], bm(ns["b"])[None],
            np_["c0"][None], ns["c0"][None], np_["c1"][None], ns["c1"][None], np_["c2"][None], ns["c2"][None],
            np_["d"][None], bm(ns["d"])[None],
            jnp.stack([np_["f0"], np_["f1"]]), jnp.stack([bm(ns["f0"]), bm(ns["f1"])]))
```

```python
import functools
import math

import numpy as np
import jax
import jax.numpy as jnp
from jax import lax
from jax.experimental import pallas as pl
from jax.experimental.pallas import tpu as pltpu

D_MODEL = 1024
BATCH = 4
SEQ = 4096
DEPTH = 2
DEC_BATCH = 32
DEC_SEQ = 4
PAST_LEN = 8192
EPS = 1e-6
A_WIDTH = 512
A_CONV = 3
B_WIDTH = 512
B_WINDOWS = (2, 4, 8, 16)
B_GROUP = 128
B_PREV = 15
C_PAIRS = ((128, 1), (512, 4), (2048, 16))
C_HPG = 4
C_HEAD_DIM = 64
C_HEADS = 12
C_QKV = 2304
C_GW = C_HPG * C_HEAD_DIM
C_TAPS = 128
ATTN_SCALE = C_HEAD_DIM ** -0.5
Q_BLOCK = 128
N_BUCKETS = 32
MAX_DISTANCE = 2048
D_WIDTH = 512
D_CONV = 31
D_FF = 2816
FFN_CONV = 3

SUBLANES = 8
VMEM_LIMIT = 56 * 1024 * 1024
NEG = -1e30
TM_PROMPT = 512
FF_CHUNK = 256
D_ROWS = 32

F32 = jnp.float32
BF16 = jnp.bfloat16


def _ru8(n):
    return -(-n // SUBLANES) * SUBLANES


def _pool_levels(rs):
    l1 = _ru8(rs)
    l2 = _ru8(l1 + 2 * rs)
    l3 = _ru8(l2 + 4 * rs)
    l4 = _ru8(l3 + 8 * rs)
    return l1, l2, l3, l4


def _rms(x, g):
    return x * lax.rsqrt(jnp.mean(x * x, axis=-1, keepdims=True) + EPS) * g


def _sigmoid(x):
    return 1.0 / (1.0 + jnp.exp(-x))


def _cparams(n_axes):
    return pltpu.CompilerParams(dimension_semantics=("arbitrary",) * n_axes, vmem_limit_bytes=VMEM_LIMIT)


def _const_spec(shape):
    nd = len(shape)
    return pl.BlockSpec(shape, lambda *_: (0,) * nd, pipeline_mode=pl.Buffered(1))


def _ada_body(c_ref, w_ref, b_ref, o_ref):
    c = c_ref[...]
    ca = c * _sigmoid(c)
    o_ref[...] = jnp.dot(ca.astype(BF16), w_ref[...].astype(BF16), preferred_element_type=F32) + b_ref[...]


def _ada(c_all, ada_w, ada_b):
    rows = c_all.shape[0]
    tn = 1536
    return pl.pallas_call(
        _ada_body,
        grid=(DEPTH, 6 * D_MODEL // tn),
        in_specs=[pl.BlockSpec((rows, D_MODEL), lambda l, n: (0, 0)),
                  pl.BlockSpec((None, D_MODEL, tn), lambda l, n: (l, 0, n)),
                  pl.BlockSpec((None, 1, tn), lambda l, n: (l, 0, n))],
        out_specs=pl.BlockSpec((None, rows, tn), lambda l, n: (l, 0, n)),
        out_shape=jax.ShapeDtypeStruct((DEPTH, rows, 6 * D_MODEL), F32),
        compiler_params=_cparams(2),
        name="ada",
    )(c_all, ada_w, ada_b.reshape(DEPTH, 1, 6 * D_MODEL))


def _ab_body(x_ref, mod_ref, ng_ref, ap_ref, bp_ref, win_ref, aw_ref, bw_ref, bs_ref, wout_ref,
             x1_ref, an_ref, bn_ref, ea, eb, s2, s4, s8, *, TM, RS, NT, pos0):
    j = pl.program_id(1)
    HA = _ru8((A_CONV - 1) * RS)
    L1, L2, L3, HB = _pool_levels(RS)
    E = HB + TM
    na, nbp = (A_CONV - 1) * RS, B_PREV * RS

    @pl.when(j == 0)
    def _():
        ea[HA - na:HA, :] = ap_ref[...]
        if HB > nbp:
            eb[0:HB - nbp, :] = jnp.zeros((HB - nbp, B_WIDTH), F32)
        eb[HB - nbp:HB, :] = bp_ref[...]

    x = x_ref[...]
    sh1, sc1, g1 = mod_ref[0], mod_ref[1], mod_ref[2]
    h = _rms(x, ng_ref[0:1, :]) * (1.0 + sc1) + sh1
    proj = jnp.dot(h.astype(BF16), win_ref[...], preferred_element_type=F32)
    hh, bg = proj[:, 0:A_WIDTH], proj[:, A_WIDTH:2 * A_WIDTH]
    cg, u = proj[:, 2 * A_WIDTH:3 * A_WIDTH], proj[:, 3 * A_WIDTH:]

    v = cg * hh
    ea[HA:HA + TM, :] = v
    z = (aw_ref[2:3, :] * v + aw_ref[1:2, :] * ea[HA - RS:HA - RS + TM, :]
         + aw_ref[0:1, :] * ea[HA - 2 * RS:HA - 2 * RS + TM, :])
    ya = bg * z
    an_ref[...] = ea[HA + TM - na:HA + TM, :]
    if NT > 1:
        ea[HA - na:HA, :] = ea[HA + TM - na:HA + TM, :]

    eb[HB:E, :] = u
    s2[L1:E, :] = eb[L1:E, :] + eb[L1 - RS:E - RS, :]
    s4[L2:E, :] = s2[L2:E, 128:512] + s2[L2 - 2 * RS:E - 2 * RS, 128:512]
    s8[L3:E, :] = s4[L3:E, 128:384] + s4[L3 - 4 * RS:E - 4 * RS, 128:384]
    wsum = (s2[HB:E, 0:128], s4[HB:E, 0:128], s8[HB:E, 0:128],
            s8[HB:E, 128:256] + s8[HB - 8 * RS:E - 8 * RS, 128:256])
    bn_ref[...] = eb[E - nbp:E, :]
    if NT > 1:
        eb[HB - nbp:HB, :] = eb[E - nbp:E, :]
    row = lax.broadcasted_iota(jnp.int32, (TM, B_GROUP), 0) + j * TM
    pos1 = lax.shift_right_logical(row, int(math.log2(RS))) + (pos0 + 1)
    ybs = []
    for g, win in enumerate(B_WINDOWS):
        cnt = jnp.minimum(pos1, win).astype(F32)
        pooled = wsum[g] / cnt - u[:, g * B_GROUP:(g + 1) * B_GROUP]
        ybs.append(jnp.dot(pooled.astype(BF16), bw_ref[g], preferred_element_type=F32))
    yb = jnp.concatenate(ybs, axis=-1) * bs_ref[...]

    ycat = jnp.concatenate([ya, yb], axis=-1).astype(BF16)
    y = jnp.dot(ycat, wout_ref[...], preferred_element_type=F32)
    x1_ref[...] = x + g1 * _rms(y, ng_ref[1:2, :])


def _mixer_ab(x, mod, ng, a_prev, b_prev, w_in, a_w, b_w, b_scale, w_out, *, TM, RS, pos0):
    NB, R, _ = x.shape
    NT = R // TM
    MR = mod.shape[2]
    _, _, _, HB = _pool_levels(RS)
    HA = _ru8((A_CONV - 1) * RS)
    E = HB + TM
    na, nbp = (A_CONV - 1) * RS, B_PREV * RS
    tile = lambda c: pl.BlockSpec((None, TM, c), lambda n, j: (n, j, 0))
    per_n = lambda r, c: pl.BlockSpec((None, r, c), lambda n, j: (n, 0, 0))
    return pl.pallas_call(
        functools.partial(_ab_body, TM=TM, RS=RS, NT=NT, pos0=pos0),
        grid=(NB, NT),
        in_specs=[tile(D_MODEL),
                  pl.BlockSpec((None, 6, MR, D_MODEL), lambda n, j: (n, 0, 0, 0)),
                  _const_spec((4, D_MODEL)),
                  per_n(na, A_WIDTH), per_n(nbp, B_WIDTH),
                  _const_spec(w_in.shape), _const_spec(a_w.shape), _const_spec(b_w.shape),
                  _const_spec(b_scale.shape), _const_spec(w_out.shape)],
        out_specs=[tile(D_MODEL), per_n(na, A_WIDTH), per_n(nbp, B_WIDTH)],
        out_shape=[jax.ShapeDtypeStruct((NB, R, D_MODEL), F32),
                   jax.ShapeDtypeStruct((NB, na, A_WIDTH), F32),
                   jax.ShapeDtypeStruct((NB, nbp, B_WIDTH), F32)],
        scratch_shapes=[pltpu.VMEM((HA + TM, A_WIDTH), F32), pltpu.VMEM((E, B_WIDTH), F32),
                        pltpu.VMEM((E, 512), F32), pltpu.VMEM((E, 384), F32), pltpu.VMEM((E, 256), F32)],
        compiler_params=_cparams(2),
        name="mixer_ab",
    )(x, mod, ng, a_prev, b_prev, w_in, a_w, b_w, b_scale, w_out)


def _ffn_body(x_ref, mod_ref, ng_ref, fp_ref, wup_ref, cw_ref, cb_ref, wdn_ref,
              y_ref, fn_ref, halo, ext, acc, *, TM, RS, NT):
    j = pl.program_id(1)
    nf = (FFN_CONV - 1) * RS
    HF = _ru8(nf)

    @pl.when(j == 0)
    def _():
        halo[...] = fp_ref[...]

    x = x_ref[...]
    sh2, sc2, g2 = mod_ref[3], mod_ref[4], mod_ref[5]
    hb = (_rms(x, ng_ref[2:3, :]) * (1.0 + sc2) + sh2).astype(BF16)

    def conv_part(col):
        up = jnp.dot(hb, wup_ref[:, col:col + FF_CHUNK], preferred_element_type=F32)
        ext[HF - nf:HF, :] = halo[:, col:col + FF_CHUNK]
        ext[HF:HF + TM, :] = up
        out = (cw_ref[2:3, col:col + FF_CHUNK] * up
               + cw_ref[1:2, col:col + FF_CHUNK] * ext[HF - RS:HF - RS + TM, :]
               + cw_ref[0:1, col:col + FF_CHUNK] * ext[HF - 2 * RS:HF - 2 * RS + TM, :]
               + cb_ref[:, col:col + FF_CHUNK])
        halo[:, col:col + FF_CHUNK] = ext[HF + TM - nf:HF + TM, :]
        return out

    for c in range(D_FF // FF_CHUNK):
        a = conv_part(c * FF_CHUNK)
        g = conv_part(D_FF + c * FF_CHUNK)
        act = (a * (g * _sigmoid(g))).astype(BF16)
        part = jnp.dot(act, wdn_ref[c * FF_CHUNK:(c + 1) * FF_CHUNK, :], preferred_element_type=F32)
        if c == 0:
            acc[...] = part
        else:
            acc[...] += part
    fn_ref[...] = halo[...]
    y_ref[...] = x + g2 * _rms(acc[...], ng_ref[3:4, :])


def _conv_ffn(x, mod, ng, f_prev, w_up, conv_w, conv_b, w_down, *, TM, RS):
    NB, R, _ = x.shape
    NT = R // TM
    MR = mod.shape[2]
    nf = (FFN_CONV - 1) * RS
    tile = pl.BlockSpec((None, TM, D_MODEL), lambda n, j: (n, j, 0))
    per_n = pl.BlockSpec((None, nf, 2 * D_FF), lambda n, j: (n, 0, 0))
    return pl.pallas_call(
        functools.partial(_ffn_body, TM=TM, RS=RS, NT=NT),
        grid=(NB, NT),
        in_specs=[tile,
                  pl.BlockSpec((None, 6, MR, D_MODEL), lambda n, j: (n, 0, 0, 0)),
                  _const_spec((4, D_MODEL)),
                  per_n,
                  _const_spec(w_up.shape), _const_spec(conv_w.shape), _const_spec(conv_b.shape),
                  _const_spec(w_down.shape)],
        out_specs=[tile, per_n],
        out_shape=[jax.ShapeDtypeStruct((NB, R, D_MODEL), F32),
                   jax.ShapeDtypeStruct((NB, nf, 2 * D_FF), F32)],
        scratch_shapes=[pltpu.VMEM((nf, 2 * D_FF), F32), pltpu.VMEM((_ru8(nf) + TM, FF_CHUNK), F32),
                        pltpu.VMEM((TM, D_MODEL), F32)],
        compiler_params=_cparams(2),
        name="conv_ffn",
    )(x, mod, ng, f_prev, w_up, conv_w, conv_b, w_down)


def _cd_in_body(x_ref, mod_ref, ng_ref, dp_ref, win_ref, dw_ref, db_ref, lg_ref, lb_ref,
                q_ref, k_ref, v_ref, kvf_ref, yd_ref, dn_ref, ed, *, TM, RS, NT):
    j = pl.program_id(1)
    nd = (D_CONV - 1) * RS
    HD = _ru8(nd)

    @pl.when(j == 0)
    def _():
        ed[HD - nd:HD, :] = dp_ref[...]

    x = x_ref[...]
    sh1, sc1 = mod_ref[0], mod_ref[1]
    h = _rms(x, ng_ref[0:1, :]) * (1.0 + sc1) + sh1
    proj = jnp.dot(h.astype(BF16), win_ref[...], preferred_element_type=F32)
    q_ref[...] = (proj[:, 0:768] * ATTN_SCALE).astype(BF16)
    k_ref[...] = proj[:, 768:1536].astype(BF16)
    v_ref[...] = proj[:, 1536:2304].astype(BF16)
    kvf_ref[...] = proj[:, 768:2304]
    dv, dg = proj[:, C_QKV:C_QKV + D_WIDTH], proj[:, C_QKV + D_WIDTH:]
    ed[HD:HD + TM, :] = dv * _sigmoid(dg)

    db, lg, lb = db_ref[...], lg_ref[...], lb_ref[...]
    for c in range(TM // D_ROWS):
        acc = jnp.zeros((D_ROWS, D_WIDTH), F32)
        for kk in range(D_CONV):
            off = HD - (D_CONV - 1 - kk) * RS + c * D_ROWS
            acc = acc + dw_ref[kk:kk + 1, :] * ed[off:off + D_ROWS, :]
        zc = acc + db
        mu = jnp.mean(zc, axis=-1, keepdims=True)
        zc = zc - mu
        var = jnp.mean(zc * zc, axis=-1, keepdims=True)
        yl = zc * lax.rsqrt(var + EPS) * lg + lb
        yd_ref[c * D_ROWS:(c + 1) * D_ROWS, :] = (yl * _sigmoid(yl)).astype(BF16)

    dn_ref[...] = ed[HD + TM - nd:HD + TM, :]
    if NT > 1:
        ed[HD - nd:HD, :] = ed[HD + TM - nd:HD + TM, :]


def _cd_in(x, mod, ng, d_prev, w_in, d_w, d_b, ln_g, ln_b, *, TM, RS, tail_rows):
    NB, R, _ = x.shape
    NT = R // TM
    MR = mod.shape[2]
    nd = (D_CONV - 1) * RS
    n_tail = tail_rows // TM
    tile = lambda c: pl.BlockSpec((None, TM, c), lambda n, j: (n, j, 0))
    per_n = pl.BlockSpec((None, nd, D_WIDTH), lambda n, j: (n, 0, 0))
    tail = pl.BlockSpec((None, TM, 1536), lambda n, j: (n, jnp.maximum(j - (NT - n_tail), 0), 0))
    return pl.pallas_call(
        functools.partial(_cd_in_body, TM=TM, RS=RS, NT=NT),
        grid=(NB, NT),
        in_specs=[tile(D_MODEL),
                  pl.BlockSpec((None, 6, MR, D_MODEL), lambda n, j: (n, 0, 0, 0)),
                  _const_spec((4, D_MODEL)),
                  per_n,
                  _const_spec(w_in.shape), _const_spec(d_w.shape), _const_spec(d_b.shape),
                  _const_spec(ln_g.shape), _const_spec(ln_b.shape)],
        out_specs=[tile(768), tile(768), tile(768), tail, tile(D_WIDTH), per_n],
        out_shape=[jax.ShapeDtypeStruct((NB, R, 768), BF16), jax.ShapeDtypeStruct((NB, R, 768), BF16),
                   jax.ShapeDtypeStruct((NB, R, 768), BF16),
                   jax.ShapeDtypeStruct((NB, tail_rows, 1536), F32),
                   jax.ShapeDtypeStruct((NB, R, D_WIDTH), BF16),
                   jax.ShapeDtypeStruct((NB, nd, D_WIDTH), F32)],
        scratch_shapes=[pltpu.VMEM((_ru8(nd) + TM, D_WIDTH), F32)],
        compiler_params=_cparams(2),
        name="cd_in",
    )(x, mod, ng, d_prev, w_in, d_w, d_b, ln_g, ln_b)


def _head_of_lane(shape):
    return lax.shift_right_logical(lax.broadcasted_iota(jnp.int32, shape, 1), 6)


def _attn_prompt_body(q_ref, kp_ref, kc_ref, vp_ref, vc_ref, bias_ref, o_ref, lse_ref):
    q = q_ref[...]
    lane_head = _head_of_lane(q.shape)
    zero = jnp.zeros_like(q)
    qs = jnp.concatenate([jnp.where(lane_head == h, q, zero) for h in range(C_HPG)], axis=0)
    kcat = jnp.concatenate([kp_ref[...], kc_ref[...]], axis=0)
    vcat = jnp.concatenate([vp_ref[...], vc_ref[...]], axis=0)
    logits = lax.dot_general(qs, kcat, (((1,), (1,)), ((), ())), preferred_element_type=F32) + bias_ref[...]
    m = jnp.max(logits, axis=-1, keepdims=True)
    p = jnp.exp(logits - m)
    s = jnp.sum(p, axis=-1, keepdims=True)
    pv = jnp.dot(p.astype(BF16), vcat, preferred_element_type=F32)
    on = pv / s
    lse = m + jnp.log(s)
    o = jnp.zeros((Q_BLOCK, C_GW), F32)
    l = jnp.zeros((Q_BLOCK, C_GW), F32)
    for h in range(C_HPG):
        sel = lane_head == h
        o = jnp.where(sel, on[h * Q_BLOCK:(h + 1) * Q_BLOCK, :], o)
        l = jnp.where(sel, lse[h * Q_BLOCK:(h + 1) * Q_BLOCK, :], l)
    o_ref[...] = o
    lse_ref[...] = l


def _attn_prompt(q, k, v, bias, g, dil):
    N, S, _ = q.shape
    L = S // dil
    nb = L // Q_BLOCK
    q3, k3, v3 = (t.reshape(N, L, dil * 768) for t in (q, k, v))
    cur = pl.BlockSpec((None, Q_BLOCK, C_GW), lambda n, r, i: (n, i, r * 3 + g))
    prev = pl.BlockSpec((None, Q_BLOCK, C_GW), lambda n, r, i: (n, jnp.maximum(i - 1, 0), r * 3 + g))
    out = pl.BlockSpec((None, Q_BLOCK, C_GW), lambda n, r, i: (n, i, r))
    o, lse = pl.pallas_call(
        _attn_prompt_body,
        grid=(N, dil, nb),
        in_specs=[cur, prev, cur, prev, cur,
                  pl.BlockSpec((None, C_HPG * Q_BLOCK, 2 * Q_BLOCK), lambda n, r, i: (jnp.minimum(i, 1), 0, 0))],
        out_specs=[out, out],
        out_shape=[jax.ShapeDtypeStruct((N, L, dil * C_GW), F32)] * 2,
        compiler_params=_cparams(3),
        name="attn_prompt_g%d" % g,
    )(q3, k3, k3, v3, v3, bias)
    return o.reshape(N, S, C_GW), lse.reshape(N, S, C_GW)


SROWS = 32


def _attn_sample_body(q_ref, kvn_ref, c0_ref, c1_ref, c2_ref, bc0_ref, bc1_ref, bc2_ref, bn_ref,
                      yc_ref, n0_ref, n1_ref, n2_ref):
    caches = (c0_ref, c1_ref, c2_ref)
    bcs = (bc0_ref, bc1_ref, bc2_ref)
    news = (n0_ref, n1_ref, n2_ref)
    lane_head = _head_of_lane((SROWS, C_GW))
    row_head = lax.shift_right_logical(lax.broadcasted_iota(jnp.int32, (SROWS, C_GW), 0), 3)
    own = lane_head == row_head
    lane128 = lax.broadcasted_iota(jnp.int32, (SROWS, 128), 1)
    kvn = kvn_ref[...]
    r0 = 8 - DEC_SEQ
    outs, lses = [], []
    for g, (win, dil) in enumerate(C_PAIRS):
        wb = win
        qg = q_ref[:, g * C_GW:(g + 1) * C_GW]
        qs = jnp.where(own, qg, jnp.zeros_like(qg))
        cache = caches[g]
        kt = cache[0:C_GW, :].astype(BF16)
        vt = cache[C_GW:2 * C_GW, :].astype(BF16)
        kn = kvn[:, g * C_GW:(g + 1) * C_GW]
        vn = kvn[:, 768 + g * C_GW:768 + (g + 1) * C_GW]
        lc = jnp.dot(qs, kt, preferred_element_type=F32) + bcs[g][...]
        qf = qs.astype(F32)
        ln = bn_ref[g]
        for c in range(DEC_SEQ):
            d = jnp.sum(qf * kn[r0 + c:r0 + c + 1, :], axis=-1, keepdims=True)
            ln = ln + jnp.where(lane128 == c, d, 0.0)
        m = jnp.maximum(jnp.max(lc, axis=-1, keepdims=True), jnp.max(ln, axis=-1, keepdims=True))
        pc = jnp.exp(lc - m)
        pn = jnp.exp(ln - m)
        s = jnp.sum(pc, axis=-1, keepdims=True) + jnp.sum(pn, axis=-1, keepdims=True)
        pv = lax.dot_general(pc.astype(BF16), vt, (((1,), (1,)), ((), ())), preferred_element_type=F32)
        for c in range(DEC_SEQ):
            pcol = jnp.sum(jnp.where(lane128 == c, pn, 0.0), axis=-1, keepdims=True)
            pv = pv + pcol * vn[r0 + c:r0 + c + 1, :]
        outs.append(pv / s)
        lses.append(m + jnp.log(s))
        rolled = pltpu.roll(cache[...], wb - DEC_SEQ, 1)
        new_rows = jnp.concatenate([jnp.zeros((128 - 8, 2 * C_GW), F32), jnp.concatenate([kn, vn], axis=-1)], axis=0)
        new_cols = new_rows.T
        lane_t = lax.broadcasted_iota(jnp.int32, (2 * C_GW, 128), 1)
        if wb > 128:
            news[g][:, 0:wb - 128] = rolled[:, 0:wb - 128]
        news[g][:, wb - 128:wb] = jnp.where(lane_t >= 128 - DEC_SEQ, new_cols, rolled[:, wb - 128:wb])
    mm = jnp.maximum(jnp.maximum(lses[0], lses[1]), lses[2])
    es = [jnp.exp(l - mm) for l in lses]
    den = es[0] + es[1] + es[2]
    y = (es[0] / den) * outs[0] + (es[1] / den) * outs[1] + (es[2] / den) * outs[2]
    y = jnp.where(own, y, 0.0)
    yc_ref[...] = y[0:8, :] + y[8:16, :] + y[16:24, :] + y[24:32, :]


def _attn_sample(q_rep, kvn, caches, bias_c, bias_n):
    NBt = q_rep.shape[0]
    per_b = lambda r, c: pl.BlockSpec((None, r, c), lambda b: (b, 0, 0))
    wbs = [w for w, _ in C_PAIRS]
    return pl.pallas_call(
        _attn_sample_body,
        grid=(NBt,),
        in_specs=[per_b(SROWS, 768), per_b(8, 1536)] + [per_b(512, w) for w in wbs]
                 + [_const_spec((SROWS, w)) for w in wbs] + [_const_spec((3, SROWS, 128))],
        out_specs=[per_b(8, C_GW)] + [per_b(512, w) for w in wbs],
        out_shape=[jax.ShapeDtypeStruct((NBt, 8, C_GW), F32)]
                  + [jax.ShapeDtypeStruct((NBt, 512, w), F32) for w in wbs],
        compiler_params=_cparams(1),
        name="attn_sample",
    )(q_rep, kvn, *caches, *bias_c, bias_n)


def _cd_out_body(*refs, combine):
    if combine:
        (x_ref, mod_ref, ng_ref, o0, o1, o2, l0, l1, l2, yd_ref, wout_ref, x1_ref) = refs
        la, lb, lc = l0[...], l1[...], l2[...]
        mm = jnp.maximum(jnp.maximum(la, lb), lc)
        ea, eb, ec = jnp.exp(la - mm), jnp.exp(lb - mm), jnp.exp(lc - mm)
        den = ea + eb + ec
        yc = (ea / den) * o0[...] + (eb / den) * o1[...] + (ec / den) * o2[...]
    else:
        (x_ref, mod_ref, ng_ref, yc_ref, yd_ref, wout_ref, x1_ref) = refs
        yc = yc_ref[...]
    x = x_ref[...]
    g1 = mod_ref[2]
    y = (jnp.dot(yc.astype(BF16), wout_ref[0:C_GW, :], preferred_element_type=F32)
         + jnp.dot(yd_ref[...], wout_ref[C_GW:, :], preferred_element_type=F32))
    x1_ref[...] = x + g1 * _rms(y, ng_ref[1:2, :])


def _cd_out(x, mod, ng, yc_parts, yd, w_out, *, TM):
    NB, R, _ = x.shape
    NT = R // TM
    MR = mod.shape[2]
    tile = lambda c: pl.BlockSpec((None, TM, c), lambda n, j: (n, j, 0))
    combine = len(yc_parts) > 1
    return pl.pallas_call(
        functools.partial(_cd_out_body, combine=combine),
        grid=(NB, NT),
        in_specs=[tile(D_MODEL),
                  pl.BlockSpec((None, 6, MR, D_MODEL), lambda n, j: (n, 0, 0, 0)),
                  _const_spec((4, D_MODEL))]
                 + [tile(C_GW)] * len(yc_parts) + [tile(D_WIDTH), _const_spec(w_out.shape)],
        out_specs=tile(D_MODEL),
        out_shape=jax.ShapeDtypeStruct((NB, R, D_MODEL), F32),
        compiler_params=_cparams(2),
        name="cd_out",
    )(x, mod, ng, *yc_parts, yd, w_out)


def _t5_bucket(dist):
    dist = np.asarray(dist)
    max_exact = N_BUCKETS // 2
    large = max_exact + (np.log(np.maximum(dist, max_exact) / max_exact) / np.log(MAX_DISTANCE / max_exact)
                         * (N_BUCKETS - max_exact)).astype(np.int32)
    large = np.minimum(large, N_BUCKETS - 1)
    return np.where(dist < max_exact, dist, large).astype(np.int32)


def _group_bias(rel_bias, g, dil):
    buckets = _t5_bucket(dil * np.arange(C_TAPS + 1))
    return rel_bias[buckets][:, g * C_HPG:(g + 1) * C_HPG].T


def _toeplitz_body(c_ref, o_ref):
    keep = lax.broadcasted_iota(jnp.int32, (Q_BLOCK, 2 * Q_BLOCK), 1) >= Q_BLOCK
    for h in range(C_HPG):
        taps = jnp.broadcast_to(c_ref[h:h + 1, :], (Q_BLOCK, 2 * Q_BLOCK))
        t = pltpu.roll(taps, 0, 1, stride=1, stride_axis=0)
        o_ref[1, h * Q_BLOCK:(h + 1) * Q_BLOCK, :] = t
        o_ref[0, h * Q_BLOCK:(h + 1) * Q_BLOCK, :] = jnp.where(keep, t, NEG)


def _prompt_bias(bias_gs):
    c = jnp.stack([jnp.concatenate([b[:, ::-1].astype(F32), jnp.full((C_HPG, Q_BLOCK - 1), NEG, F32)], axis=1)
                   for b in bias_gs])
    return pl.pallas_call(
        _toeplitz_body,
        grid=(len(bias_gs),),
        in_specs=[pl.BlockSpec((None, C_HPG, 2 * Q_BLOCK), lambda g: (g, 0, 0))],
        out_specs=pl.BlockSpec((None, 2, C_HPG * Q_BLOCK, 2 * Q_BLOCK), lambda g: (g, 0, 0, 0)),
        out_shape=jax.ShapeDtypeStruct((len(bias_gs), 2, C_HPG * Q_BLOCK, 2 * Q_BLOCK), F32),
        compiler_params=_cparams(1),
        name="attn_bias",
    )(c)


def _sample_bias(bias_g, wb, dil):
    z = jnp.stack([bias_g.astype(F32)] + [jnp.full(bias_g.shape, NEG, F32)] * (dil - 1), axis=-1)
    z = z.reshape(C_HPG, (C_TAPS + 1) * dil)
    n = wb + DEC_SEQ
    z = z[:, :n] if z.shape[1] >= n else jnp.pad(z, ((0, 0), (0, n - z.shape[1])), constant_values=NEG)
    fl = z[:, ::-1]
    bc = jnp.stack([fl[:, DEC_SEQ - 1 - t:DEC_SEQ - 1 - t + wb] for t in range(DEC_SEQ)], axis=1)
    bc = jnp.pad(bc, ((0, 0), (0, 8 - DEC_SEQ), (0, 0))).reshape(SROWS, wb)
    bn = jnp.stack([jnp.pad(z[:, :t + 1][:, ::-1], ((0, 0), (0, DEC_SEQ - 1 - t)), constant_values=NEG)
                    for t in range(DEC_SEQ)], axis=1)
    bn = jnp.pad(bn, ((0, 0), (0, 8 - DEC_SEQ), (0, 128 - DEC_SEQ)), constant_values=NEG).reshape(SROWS, 128)
    return bc, bn


def _time_major(s):
    b, k, c = s.shape
    return s.transpose(1, 0, 2).reshape(1, k * b, c)


def _batch_major(s, b):
    _, r, c = s.shape
    return s.reshape(r // b, b, c).transpose(1, 0, 2)


def _stack(x, mod, st, w, *, TM, RS, pos0):
    NB = x.shape[0]
    new = {}
    ng = w["norm_g"]
    x, new["a"], new["b"] = _mixer_ab(x, mod[0], ng[0], st["a"], st["b"], w["ab_w_in"], w["a_conv_w"],
                                      w["b_w_grp"], w["b_scale"], w["ab_w_out"], TM=TM, RS=RS, pos0=pos0)
    x, new["f0"] = _conv_ffn(x, mod[0], ng[0], st["f0"], w["ffn_w_up"][0], w["ffn_conv_w"][0],
                             w["ffn_conv_b"][0], w["ffn_w_down"][0], TM=TM, RS=RS)
    tail_rows = min(C_PAIRS[-1][0], x.shape[1])
    q, k, v, kvf, yd, new["d"] = _cd_in(x, mod[1], ng[1], st["d"], w["cd_w_in"], w["d_conv_w"], w["d_conv_b"],
                                        w["d_ln_g"], w["d_ln_b"], TM=TM, RS=RS, tail_rows=tail_rows)
    if RS == 1:
        parts_o, parts_l = [], []
        pbias = _prompt_bias(w["bias_g"])
        for g, (win, dil) in enumerate(C_PAIRS):
            o, lse = _attn_prompt(q, k, v, pbias[g], g, dil)
            parts_o.append(o)
            parts_l.append(lse)
            wb = min(win, x.shape[1])
            kt = kvf[:, tail_rows - wb:, g * C_GW:(g + 1) * C_GW]
            vt = kvf[:, tail_rows - wb:, 768 + g * C_GW:768 + (g + 1) * C_GW]
            new["c%d" % g] = jnp.stack([kt, vt], axis=2).reshape(NB, wb, 2, C_HPG, C_HEAD_DIM)
        yc_parts = parts_o + parts_l
    else:
        B = RS
        qb = _batch_major(q, B)
        q_rep = jnp.pad(jnp.broadcast_to(qb[:, None], (B, C_HPG, DEC_SEQ, 768)),
                        ((0, 0), (0, 0), (0, 8 - DEC_SEQ), (0, 0))).reshape(B, SROWS, 768)
        kvn = jnp.pad(_batch_major(kvf, B), ((0, 0), (8 - DEC_SEQ, 0), (0, 0)))
        bias = [_sample_bias(w["bias_g"][g], win, dil) for g, (win, dil) in enumerate(C_PAIRS)]
        yc, n0, n1, n2 = _attn_sample(q_rep, kvn, st["c"], [b[0] for b in bias],
                                      jnp.stack([b[1] for b in bias]))
        for g, n in enumerate((n0, n1, n2)):
            new["c%d" % g] = n.reshape(B, 2, C_HPG, C_HEAD_DIM, n.shape[2]).transpose(0, 4, 1, 2, 3)
        yc_parts = [_time_major(yc[:, :DEC_SEQ])]
    x = _cd_out(x, mod[1], ng[1], yc_parts, yd, w["cd_w_out"], TM=TM)
    x, new["f1"] = _conv_ffn(x, mod[1], ng[1], st["f1"], w["ffn_w_up"][1], w["ffn_conv_w"][1],
                             w["ffn_conv_b"][1], w["ffn_w_down"][1], TM=TM, RS=RS)
    return x, new


def kernel(x_prompt, x_sample, state_a_conv, state_b_pool, cache_c_win128, cache_c_win512, cache_c_win2048,
           state_d_conv, state_ffn_conv, c_prompt, c_sample, ada_w, ada_b, norm_g, rel_bias, ab_w_in, a_conv_w,
           b_w_grp, b_scale, ab_w_out, cd_w_in, d_conv_w, d_conv_b, d_ln_g, d_ln_b, cd_w_out, ffn_w_up,
           ffn_conv_w, ffn_conv_b, ffn_w_down):
    B, T = DEC_BATCH, DEC_SEQ
    w = dict(norm_g=norm_g,
             ab_w_in=ab_w_in[0].astype(BF16), a_conv_w=a_conv_w[0], b_w_grp=b_w_grp[0].astype(BF16),
             b_scale=b_scale, ab_w_out=ab_w_out[0].astype(BF16),
             cd_w_in=cd_w_in[0].astype(BF16), d_conv_w=d_conv_w[0], d_conv_b=d_conv_b, d_ln_g=d_ln_g,
             d_ln_b=d_ln_b, cd_w_out=cd_w_out[0].astype(BF16),
             ffn_w_up=ffn_w_up.astype(BF16), ffn_conv_w=ffn_conv_w, ffn_conv_b=ffn_conv_b[:, None, :],
             ffn_w_down=ffn_w_down.astype(BF16),
             bias_g=[_group_bias(rel_bias, g, dil) for g, (_, dil) in enumerate(C_PAIRS)])

    mod = _ada(jnp.concatenate([c_prompt, c_sample], axis=0), ada_w, ada_b)
    mod_p = mod[:, :BATCH].reshape(DEPTH, BATCH, 6, 1, D_MODEL)
    mod_s = mod[:, BATCH:].reshape(DEPTH, B, 6, D_MODEL).transpose(0, 2, 1, 3)
    mod_s = jnp.broadcast_to(mod_s[:, :, None], (DEPTH, 6, T, B, D_MODEL)).reshape(DEPTH, 1, 6, T * B, D_MODEL)

    zeros = lambda k, c: jnp.zeros((BATCH, k, c), F32)
    st_p = dict(a=zeros(A_CONV - 1, A_WIDTH), b=zeros(B_PREV, B_WIDTH), d=zeros(D_CONV - 1, D_WIDTH),
                f0=zeros(FFN_CONV - 1, 2 * D_FF), f1=zeros(FFN_CONV - 1, 2 * D_FF))
    y_p, np_ = _stack(x_prompt, mod_p, st_p, w, TM=TM_PROMPT, RS=1, pos0=0)

    st_s = dict(a=_time_major(state_a_conv[0]), b=_time_major(state_b_pool[0]), d=_time_major(state_d_conv[0]),
                f0=_time_major(state_ffn_conv[0]), f1=_time_major(state_ffn_conv[1]),
                c=[c[0].transpose(0, 2, 3, 4, 1).reshape(B, 512, c.shape[2])
                   for c in (cache_c_win128, cache_c_win512, cache_c_win2048)])
    y_s, ns = _stack(_time_major(x_sample), mod_s, st_s, w, TM=T * B, RS=B, pos0=PAST_LEN)

    bm = lambda s: _batch_major(s, B)
    return (y_p, bm(y_s),
            np_["a"][None], bm(ns["a"])[None], np_["b"][None], bm(ns["b"])[None],
            np_["c0"][None], ns["c0"][None], np_["c1"][None], ns["c1"][None], np_["c2"][None], ns["c2"][None],
            np_["d"][None], bm(ns["d"])[None],
            jnp.stack([np_["f0"], np_["f1"]]), jnp.stack([bm(ns["f0"]), bm(ns["f1"])]))
```

```python
import functools
import math

import numpy as np
import jax
import jax.numpy as jnp
from jax import lax
from jax.experimental import pallas as pl
from jax.experimental.pallas import tpu as pltpu

D_MODEL = 1024
BATCH = 4
SEQ = 4096
DEPTH = 2
DEC_BATCH = 32
DEC_SEQ = 4
PAST_LEN = 8192
EPS = 1e-6
A_WIDTH = 512
A_CONV = 3
B_WIDTH = 512
B_WINDOWS = (2, 4, 8, 16)
B_GROUP = 128
B_PREV = 15
C_PAIRS = ((128, 1), (512, 4), (2048, 16))
C_HPG = 4
C_HEAD_DIM = 64
C_HEADS = 12
C_QKV = 2304
C_GW = C_HPG * C_HEAD_DIM
C_TAPS = 128
ATTN_SCALE = C_HEAD_DIM ** -0.5
Q_BLOCK = 128
N_BUCKETS = 32
MAX_DISTANCE = 2048
D_WIDTH = 512
D_CONV = 31
D_FF = 2816
FFN_CONV = 3

SUBLANES = 8
LANES = 128
VMEM_LIMIT = 56 * 1024 * 1024
NEG = -1e30
TM_PROMPT = 512
FF_CHUNK = 256
D_ROWS = 32

F32 = jnp.float32
BF16 = jnp.bfloat16


def _ru8(n):
    return -(-n // SUBLANES) * SUBLANES


def _pool_levels(rs):
    l1 = _ru8(rs)
    l2 = _ru8(l1 + 2 * rs)
    l3 = _ru8(l2 + 4 * rs)
    l4 = _ru8(l3 + 8 * rs)
    return l1, l2, l3, l4


def _slab_put(ref, r0, val):
    n = val.shape[0]
    for s in range(val.shape[1] // LANES):
        ref[s, r0:r0 + n, :] = val[:, s * LANES:(s + 1) * LANES]


def _slab_get(ref, r0, n, s0=0, ns=None):
    ns = ref.shape[0] - s0 if ns is None else ns
    return jnp.concatenate([ref[s, r0:r0 + n, :] for s in range(s0, s0 + ns)], axis=-1)


def _rms(x, g):
    return x * lax.rsqrt(jnp.mean(x * x, axis=-1, keepdims=True) + EPS) * g


def _sigmoid(x):
    return 1.0 / (1.0 + jnp.exp(-x))


def _cparams(n_axes):
    return pltpu.CompilerParams(dimension_semantics=("arbitrary",) * n_axes, vmem_limit_bytes=VMEM_LIMIT)


def _const_spec(shape):
    nd = len(shape)
    return pl.BlockSpec(shape, lambda *_: (0,) * nd, pipeline_mode=pl.Buffered(1))


def _ada_body(c_ref, w_ref, b_ref, o_ref):
    c = c_ref[...]
    ca = c * _sigmoid(c)
    o_ref[...] = jnp.dot(ca.astype(BF16), w_ref[...].astype(BF16), preferred_element_type=F32) + b_ref[...]


def _ada(c_all, ada_w, ada_b):
    rows = c_all.shape[0]
    tn = 1536
    return pl.pallas_call(
        _ada_body,
        grid=(DEPTH, 6 * D_MODEL // tn),
        in_specs=[pl.BlockSpec((rows, D_MODEL), lambda l, n: (0, 0)),
                  pl.BlockSpec((None, D_MODEL, tn), lambda l, n: (l, 0, n)),
                  pl.BlockSpec((None, 1, tn), lambda l, n: (l, 0, n))],
        out_specs=pl.BlockSpec((None, rows, tn), lambda l, n: (l, 0, n)),
        out_shape=jax.ShapeDtypeStruct((DEPTH, rows, 6 * D_MODEL), F32),
        compiler_params=_cparams(2),
        name="ada",
    )(c_all, ada_w, ada_b.reshape(DEPTH, 1, 6 * D_MODEL))


def _ab_body(x_ref, mod_ref, ng_ref, ap_ref, bp_ref, win_ref, aw_ref, bw_ref, bs_ref, wout_ref,
             x1_ref, an_ref, bn_ref, ea, eb, s2, s4, s8, *, TM, RS, NT, pos0):
    j = pl.program_id(1)
    HA = _ru8((A_CONV - 1) * RS)
    L1, L2, L3, HB = _pool_levels(RS)
    E = HB + TM
    na, nbp = (A_CONV - 1) * RS, B_PREV * RS

    @pl.when(j == 0)
    def _():
        _slab_put(ea, HA - na, ap_ref[...])
        if HB > nbp:
            _slab_put(eb, 0, jnp.zeros((HB - nbp, B_WIDTH), F32))
        _slab_put(eb, HB - nbp, bp_ref[...])

    x = x_ref[...]
    sh1, sc1, g1 = mod_ref[0], mod_ref[1], mod_ref[2]
    h = _rms(x, ng_ref[0:1, :]) * (1.0 + sc1) + sh1
    proj = jnp.dot(h.astype(BF16), win_ref[...], preferred_element_type=F32)
    hh, bg = proj[:, 0:A_WIDTH], proj[:, A_WIDTH:2 * A_WIDTH]
    cg, u = proj[:, 2 * A_WIDTH:3 * A_WIDTH], proj[:, 3 * A_WIDTH:]

    v = cg * hh
    _slab_put(ea, HA, v)
    z = (aw_ref[2:3, :] * v + aw_ref[1:2, :] * _slab_get(ea, HA - RS, TM)
         + aw_ref[0:1, :] * _slab_get(ea, HA - 2 * RS, TM))
    ya = bg * z
    a_last = _slab_get(ea, HA + TM - na, na)
    an_ref[...] = a_last
    if NT > 1:
        _slab_put(ea, HA - na, a_last)

    _slab_put(eb, HB, u)
    _slab_put(s2, L1, _slab_get(eb, L1, E - L1) + _slab_get(eb, L1 - RS, E - L1))
    _slab_put(s4, L2, _slab_get(s2, L2, E - L2, 1) + _slab_get(s2, L2 - 2 * RS, E - L2, 1))
    _slab_put(s8, L3, _slab_get(s4, L3, E - L3, 1) + _slab_get(s4, L3 - 4 * RS, E - L3, 1))
    wsum = (s2[0, HB:E, :], s4[0, HB:E, :], s8[0, HB:E, :], s8[1, HB:E, :] + s8[1, HB - 8 * RS:E - 8 * RS, :])
    b_last = _slab_get(eb, E - nbp, nbp)
    bn_ref[...] = b_last
    if NT > 1:
        _slab_put(eb, HB - nbp, b_last)
    row = lax.broadcasted_iota(jnp.int32, (TM, B_GROUP), 0) + j * TM
    pos1 = lax.shift_right_logical(row, int(math.log2(RS))) + (pos0 + 1)
    ybs = []
    for g, win in enumerate(B_WINDOWS):
        cnt = jnp.minimum(pos1, win).astype(F32)
        pooled = wsum[g] / cnt - u[:, g * B_GROUP:(g + 1) * B_GROUP]
        ybs.append(jnp.dot(pooled.astype(BF16), bw_ref[g], preferred_element_type=F32))
    yb = jnp.concatenate(ybs, axis=-1) * bs_ref[...]

    ycat = jnp.concatenate([ya, yb], axis=-1).astype(BF16)
    y = jnp.dot(ycat, wout_ref[...], preferred_element_type=F32)
    x1_ref[...] = x + g1 * _rms(y, ng_ref[1:2, :])


def _mixer_ab(x, mod, ng, a_prev, b_prev, w_in, a_w, b_w, b_scale, w_out, *, TM, RS, pos0):
    NB, R, _ = x.shape
    NT = R // TM
    MR = mod.shape[2]
    _, _, _, HB = _pool_levels(RS)
    HA = _ru8((A_CONV - 1) * RS)
    E = HB + TM
    na, nbp = (A_CONV - 1) * RS, B_PREV * RS
    tile = lambda c: pl.BlockSpec((None, TM, c), lambda n, j: (n, j, 0))
    per_n = lambda r, c: pl.BlockSpec((None, r, c), lambda n, j: (n, 0, 0))
    return pl.pallas_call(
        functools.partial(_ab_body, TM=TM, RS=RS, NT=NT, pos0=pos0),
        grid=(NB, NT),
        in_specs=[tile(D_MODEL),
                  pl.BlockSpec((None, 6, MR, D_MODEL), lambda n, j: (n, 0, 0, 0)),
                  _const_spec((4, D_MODEL)),
                  per_n(na, A_WIDTH), per_n(nbp, B_WIDTH),
                  _const_spec(w_in.shape), _const_spec(a_w.shape), _const_spec(b_w.shape),
                  _const_spec(b_scale.shape), _const_spec(w_out.shape)],
        out_specs=[tile(D_MODEL), per_n(na, A_WIDTH), per_n(nbp, B_WIDTH)],
        out_shape=[jax.ShapeDtypeStruct((NB, R, D_MODEL), F32),
                   jax.ShapeDtypeStruct((NB, na, A_WIDTH), F32),
                   jax.ShapeDtypeStruct((NB, nbp, B_WIDTH), F32)],
        scratch_shapes=[pltpu.VMEM((4, HA + TM, LANES), F32), pltpu.VMEM((4, E, LANES), F32),
                        pltpu.VMEM((4, E, LANES), F32), pltpu.VMEM((3, E, LANES), F32),
                        pltpu.VMEM((2, E, LANES), F32)],
        compiler_params=_cparams(2),
        name="mixer_ab",
    )(x, mod, ng, a_prev, b_prev, w_in, a_w, b_w, b_scale, w_out)


def _ffn_body(x_ref, mod_ref, ng_ref, fp_ref, wup_ref, cw_ref, cb_ref, wdn_ref,
              y_ref, fn_ref, halo, ext, act, *, TM, RS, NT):
    j = pl.program_id(1)
    nf = (FFN_CONV - 1) * RS
    HF = _ru8(nf)

    @pl.when(j == 0)
    def _():
        halo[...] = fp_ref[...]

    x = x_ref[...]
    sh2, sc2, g2 = mod_ref[3], mod_ref[4], mod_ref[5]
    hb = (_rms(x, ng_ref[2:3, :]) * (1.0 + sc2) + sh2).astype(BF16)

    def conv_part(col, buf):
        up = jnp.dot(hb, wup_ref[:, col:col + FF_CHUNK], preferred_element_type=F32)
        _slab_put(buf, HF - nf, halo[:, col:col + FF_CHUNK])
        _slab_put(buf, HF, up)
        out = (cw_ref[2:3, col:col + FF_CHUNK] * up
               + cw_ref[1:2, col:col + FF_CHUNK] * _slab_get(buf, HF - RS, TM)
               + cw_ref[0:1, col:col + FF_CHUNK] * _slab_get(buf, HF - 2 * RS, TM)
               + cb_ref[:, col:col + FF_CHUNK])
        halo[:, col:col + FF_CHUNK] = _slab_get(buf, HF + TM - nf, nf)
        return out

    for c in range(D_FF // FF_CHUNK):
        a = conv_part(c * FF_CHUNK, ext.at[2 * (c % 2)])
        g = conv_part(D_FF + c * FF_CHUNK, ext.at[2 * (c % 2) + 1])
        act[:, c * FF_CHUNK:(c + 1) * FF_CHUNK] = (a * (g * _sigmoid(g))).astype(BF16)
    fn_ref[...] = halo[...]
    y = jnp.dot(act[...], wdn_ref[...], preferred_element_type=F32)
    y_ref[...] = x + g2 * _rms(y, ng_ref[3:4, :])


def _conv_ffn(x, mod, ng, f_prev, w_up, conv_w, conv_b, w_down, *, TM, RS):
    NB, R, _ = x.shape
    NT = R // TM
    MR = mod.shape[2]
    nf = (FFN_CONV - 1) * RS
    tile = pl.BlockSpec((None, TM, D_MODEL), lambda n, j: (n, j, 0))
    per_n = pl.BlockSpec((None, nf, 2 * D_FF), lambda n, j: (n, 0, 0))
    return pl.pallas_call(
        functools.partial(_ffn_body, TM=TM, RS=RS, NT=NT),
        grid=(NB, NT),
        in_specs=[tile,
                  pl.BlockSpec((None, 6, MR, D_MODEL), lambda n, j: (n, 0, 0, 0)),
                  _const_spec((4, D_MODEL)),
                  per_n,
                  _const_spec(w_up.shape), _const_spec(conv_w.shape), _const_spec(conv_b.shape),
                  _const_spec(w_down.shape)],
        out_specs=[tile, per_n],
        out_shape=[jax.ShapeDtypeStruct((NB, R, D_MODEL), F32),
                   jax.ShapeDtypeStruct((NB, nf, 2 * D_FF), F32)],
        scratch_shapes=[pltpu.VMEM((nf, 2 * D_FF), F32),
                        pltpu.VMEM((4, FF_CHUNK // LANES, _ru8(nf) + TM, LANES), F32),
                        pltpu.VMEM((TM, D_FF), BF16)],
        compiler_params=_cparams(2),
        name="conv_ffn",
    )(x, mod, ng, f_prev, w_up, conv_w, conv_b, w_down)


def _cd_in_body(x_ref, mod_ref, ng_ref, dp_ref, win_ref, dw_ref, db_ref, lg_ref, lb_ref,
                q_ref, k_ref, v_ref, kvf_ref, yd_ref, dn_ref, ed, *, TM, RS, NT):
    j = pl.program_id(1)
    nd = (D_CONV - 1) * RS
    HD = _ru8(nd)

    @pl.when(j == 0)
    def _():
        _slab_put(ed, HD - nd, dp_ref[...])

    x = x_ref[...]
    sh1, sc1 = mod_ref[0], mod_ref[1]
    h = _rms(x, ng_ref[0:1, :]) * (1.0 + sc1) + sh1
    proj = jnp.dot(h.astype(BF16), win_ref[...], preferred_element_type=F32)
    q_ref[...] = (proj[:, 0:768] * ATTN_SCALE).astype(BF16)
    k_ref[...] = proj[:, 768:1536].astype(BF16)
    v_ref[...] = proj[:, 1536:2304].astype(BF16)
    kvf_ref[...] = proj[:, 768:2304]
    dv, dg = proj[:, C_QKV:C_QKV + D_WIDTH], proj[:, C_QKV + D_WIDTH:]
    _slab_put(ed, HD, dv * _sigmoid(dg))

    db, lg, lb = db_ref[...], lg_ref[...], lb_ref[...]
    tiles = D_ROWS // SUBLANES
    for c in range(TM // D_ROWS):
        accs = []
        for s in range(D_WIDTH // LANES):
            acc = jnp.zeros((tiles, SUBLANES, LANES), F32)
            for kk in range(D_CONV):
                off = HD - (D_CONV - 1 - kk) * RS + c * D_ROWS
                tap = ed[s, off:off + D_ROWS, :].reshape(tiles, SUBLANES, LANES)
                acc = acc + dw_ref[kk, :, s * LANES:(s + 1) * LANES] * tap
            accs.append(acc.reshape(D_ROWS, LANES))
        zc = jnp.concatenate(accs, axis=-1) + db
        mu = jnp.mean(zc, axis=-1, keepdims=True)
        zc = zc - mu
        var = jnp.mean(zc * zc, axis=-1, keepdims=True)
        yl = zc * lax.rsqrt(var + EPS) * lg + lb
        yd_ref[c * D_ROWS:(c + 1) * D_ROWS, :] = (yl * _sigmoid(yl)).astype(BF16)

    d_last = _slab_get(ed, HD + TM - nd, nd)
    dn_ref[...] = d_last
    if NT > 1:
        _slab_put(ed, HD - nd, d_last)


def _cd_in(x, mod, ng, d_prev, w_in, d_w, d_b, ln_g, ln_b, *, TM, RS, tail_rows):
    NB, R, _ = x.shape
    NT = R // TM
    MR = mod.shape[2]
    nd = (D_CONV - 1) * RS
    n_tail = tail_rows // TM
    tile = lambda c: pl.BlockSpec((None, TM, c), lambda n, j: (n, j, 0))
    per_n = pl.BlockSpec((None, nd, D_WIDTH), lambda n, j: (n, 0, 0))
    tail = pl.BlockSpec((None, TM, 1536), lambda n, j: (n, jnp.maximum(j - (NT - n_tail), 0), 0))
    return pl.pallas_call(
        functools.partial(_cd_in_body, TM=TM, RS=RS, NT=NT),
        grid=(NB, NT),
        in_specs=[tile(D_MODEL),
                  pl.BlockSpec((None, 6, MR, D_MODEL), lambda n, j: (n, 0, 0, 0)),
                  _const_spec((4, D_MODEL)),
                  per_n,
                  _const_spec(w_in.shape), _const_spec(d_w.shape), _const_spec(d_b.shape),
                  _const_spec(ln_g.shape), _const_spec(ln_b.shape)],
        out_specs=[tile(768), tile(768), tile(768), tail, tile(D_WIDTH), per_n],
        out_shape=[jax.ShapeDtypeStruct((NB, R, 768), BF16), jax.ShapeDtypeStruct((NB, R, 768), BF16),
                   jax.ShapeDtypeStruct((NB, R, 768), BF16),
                   jax.ShapeDtypeStruct((NB, tail_rows, 1536), F32),
                   jax.ShapeDtypeStruct((NB, R, D_WIDTH), BF16),
                   jax.ShapeDtypeStruct((NB, nd, D_WIDTH), F32)],
        scratch_shapes=[pltpu.VMEM((D_WIDTH // LANES, _ru8(nd) + TM, LANES), F32)],
        compiler_params=_cparams(2),
        name="cd_in",
    )(x, mod, ng, d_prev, w_in, d_w, d_b, ln_g, ln_b)


def _head_of_lane(shape):
    return lax.shift_right_logical(lax.broadcasted_iota(jnp.int32, shape, 1), 6)


def _attn_prompt_body(q_ref, kp_ref, kc_ref, vp_ref, vc_ref, bias_ref, o_ref, lse_ref):
    q = q_ref[...]
    lane_head = _head_of_lane(q.shape)
    zero = jnp.zeros_like(q)
    qs = jnp.concatenate([jnp.where(lane_head == h, q, zero) for h in range(C_HPG)], axis=0)
    kcat = jnp.concatenate([kp_ref[...], kc_ref[...]], axis=0)
    vcat = jnp.concatenate([vp_ref[...], vc_ref[...]], axis=0)
    logits = lax.dot_general(qs, kcat, (((1,), (1,)), ((), ())), preferred_element_type=F32) + bias_ref[...]
    m = jnp.max(logits, axis=-1, keepdims=True)
    p = jnp.exp(logits - m)
    s = jnp.sum(p, axis=-1, keepdims=True)
    pv = jnp.dot(p.astype(BF16), vcat, preferred_element_type=F32)
    on = pv / s
    lse = m + jnp.log(s)
    o = jnp.zeros((Q_BLOCK, C_GW), F32)
    l = jnp.zeros((Q_BLOCK, C_GW), F32)
    for h in range(C_HPG):
        sel = lane_head == h
        o = jnp.where(sel, on[h * Q_BLOCK:(h + 1) * Q_BLOCK, :], o)
        l = jnp.where(sel, lse[h * Q_BLOCK:(h + 1) * Q_BLOCK, :], l)
    o_ref[...] = o
    lse_ref[...] = l


def _attn_prompt(q, k, v, bias, g, dil):
    N, S, _ = q.shape
    L = S // dil
    nb = L // Q_BLOCK
    q3, k3, v3 = (t.reshape(N, L, dil * 768) for t in (q, k, v))
    cur = pl.BlockSpec((None, Q_BLOCK, C_GW), lambda n, r, i: (n, i, r * 3 + g))
    prev = pl.BlockSpec((None, Q_BLOCK, C_GW), lambda n, r, i: (n, jnp.maximum(i - 1, 0), r * 3 + g))
    out = pl.BlockSpec((None, Q_BLOCK, C_GW), lambda n, r, i: (n, i, r))
    o, lse = pl.pallas_call(
        _attn_prompt_body,
        grid=(N, dil, nb),
        in_specs=[cur, prev, cur, prev, cur,
                  pl.BlockSpec((None, C_HPG * Q_BLOCK, 2 * Q_BLOCK), lambda n, r, i: (jnp.minimum(i, 1), 0, 0))],
        out_specs=[out, out],
        out_shape=[jax.ShapeDtypeStruct((N, L, dil * C_GW), F32)] * 2,
        compiler_params=_cparams(3),
        name="attn_prompt_g%d" % g,
    )(q3, k3, k3, v3, v3, bias)
    return o.reshape(N, S, C_GW), lse.reshape(N, S, C_GW)


SROWS = 32


def _attn_sample_body(q_ref, kvn_ref, c0_ref, c1_ref, c2_ref, bc0_ref, bc1_ref, bc2_ref, bn_ref,
                      yc_ref, n0_ref, n1_ref, n2_ref):
    caches = (c0_ref, c1_ref, c2_ref)
    bcs = (bc0_ref, bc1_ref, bc2_ref)
    news = (n0_ref, n1_ref, n2_ref)
    lane_head = _head_of_lane((SROWS, C_GW))
    row_head = lax.shift_right_logical(lax.broadcasted_iota(jnp.int32, (SROWS, C_GW), 0), 3)
    own = lane_head == row_head
    lane128 = lax.broadcasted_iota(jnp.int32, (SROWS, 128), 1)
    kvn = kvn_ref[...]
    r0 = 8 - DEC_SEQ
    outs, lses = [], []
    for g, (win, dil) in enumerate(C_PAIRS):
        wb = win
        qg = q_ref[:, g * C_GW:(g + 1) * C_GW]
        qs = jnp.where(own, qg, jnp.zeros_like(qg))
        cache = caches[g]
        kt = cache[0:C_GW, :].astype(BF16)
        vt = cache[C_GW:2 * C_GW, :].astype(BF16)
        kn = kvn[:, g * C_GW:(g + 1) * C_GW]
        vn = kvn[:, 768 + g * C_GW:768 + (g + 1) * C_GW]
        lc = jnp.dot(qs, kt, preferred_element_type=F32) + bcs[g][...]
        qf = qs.astype(F32)
        ln = bn_ref[g]
        for c in range(DEC_SEQ):
            d = jnp.sum(qf * kn[r0 + c:r0 + c + 1, :], axis=-1, keepdims=True)
            ln = ln + jnp.where(lane128 == c, d, 0.0)
        m = jnp.maximum(jnp.max(lc, axis=-1, keepdims=True), jnp.max(ln, axis=-1, keepdims=True))
        pc = jnp.exp(lc - m)
        pn = jnp.exp(ln - m)
        s = jnp.sum(pc, axis=-1, keepdims=True) + jnp.sum(pn, axis=-1, keepdims=True)
        pv = lax.dot_general(pc.astype(BF16), vt, (((1,), (1,)), ((), ())), preferred_element_type=F32)
        for c in range(DEC_SEQ):
            pcol = jnp.sum(jnp.where(lane128 == c, pn, 0.0), axis=-1, keepdims=True)
            pv = pv + pcol * vn[r0 + c:r0 + c + 1, :]
        outs.append(pv / s)
        lses.append(m + jnp.log(s))
        rolled = pltpu.roll(cache[...], wb - DEC_SEQ, 1)
        new_rows = jnp.concatenate([jnp.zeros((128 - 8, 2 * C_GW), F32), jnp.concatenate([kn, vn], axis=-1)], axis=0)
        new_cols = new_rows.T
        lane_t = lax.broadcasted_iota(jnp.int32, (2 * C_GW, 128), 1)
        if wb > 128:
            news[g][:, 0:wb - 128] = rolled[:, 0:wb - 128]
        news[g][:, wb - 128:wb] = jnp.where(lane_t >= 128 - DEC_SEQ, new_cols, rolled[:, wb - 128:wb])
    mm = jnp.maximum(jnp.maximum(lses[0], lses[1]), lses[2])
    es = [jnp.exp(l - mm) for l in lses]
    den = es[0] + es[1] + es[2]
    y = (es[0] / den) * outs[0] + (es[1] / den) * outs[1] + (es[2] / den) * outs[2]
    y = jnp.where(own, y, 0.0)
    yc_ref[...] = y[0:8, :] + y[8:16, :] + y[16:24, :] + y[24:32, :]


def _attn_sample(q_rep, kvn, caches, bias_c, bias_n):
    NBt = q_rep.shape[0]
    per_b = lambda r, c: pl.BlockSpec((None, r, c), lambda b: (b, 0, 0))
    wbs = [w for w, _ in C_PAIRS]
    return pl.pallas_call(
        _attn_sample_body,
        grid=(NBt,),
        in_specs=[per_b(SROWS, 768), per_b(8, 1536)] + [per_b(512, w) for w in wbs]
                 + [_const_spec((SROWS, w)) for w in wbs] + [_const_spec((3, SROWS, 128))],
        out_specs=[per_b(8, C_GW)] + [per_b(512, w) for w in wbs],
        out_shape=[jax.ShapeDtypeStruct((NBt, 8, C_GW), F32)]
                  + [jax.ShapeDtypeStruct((NBt, 512, w), F32) for w in wbs],
        compiler_params=_cparams(1),
        name="attn_sample",
    )(q_rep, kvn, *caches, *bias_c, bias_n)


def _cd_out_body(*refs, combine):
    if combine:
        (x_ref, mod_ref, ng_ref, o0, o1, o2, l0, l1, l2, yd_ref, wout_ref, x1_ref) = refs
        la, lb, lc = l0[...], l1[...], l2[...]
        mm = jnp.maximum(jnp.maximum(la, lb), lc)
        ea, eb, ec = jnp.exp(la - mm), jnp.exp(lb - mm), jnp.exp(lc - mm)
        den = ea + eb + ec
        yc = (ea / den) * o0[...] + (eb / den) * o1[...] + (ec / den) * o2[...]
    else:
        (x_ref, mod_ref, ng_ref, yc_ref, yd_ref, wout_ref, x1_ref) = refs
        yc = yc_ref[...]
    x = x_ref[...]
    g1 = mod_ref[2]
    y = (jnp.dot(yc.astype(BF16), wout_ref[0:C_GW, :], preferred_element_type=F32)
         + jnp.dot(yd_ref[...], wout_ref[C_GW:, :], preferred_element_type=F32))
    x1_ref[...] = x + g1 * _rms(y, ng_ref[1:2, :])


def _cd_out(x, mod, ng, yc_parts, yd, w_out, *, TM):
    NB, R, _ = x.shape
    NT = R // TM
    MR = mod.shape[2]
    tile = lambda c: pl.BlockSpec((None, TM, c), lambda n, j: (n, j, 0))
    combine = len(yc_parts) > 1
    return pl.pallas_call(
        functools.partial(_cd_out_body, combine=combine),
        grid=(NB, NT),
        in_specs=[tile(D_MODEL),
                  pl.BlockSpec((None, 6, MR, D_MODEL), lambda n, j: (n, 0, 0, 0)),
                  _const_spec((4, D_MODEL))]
                 + [tile(C_GW)] * len(yc_parts) + [tile(D_WIDTH), _const_spec(w_out.shape)],
        out_specs=tile(D_MODEL),
        out_shape=jax.ShapeDtypeStruct((NB, R, D_MODEL), F32),
        compiler_params=_cparams(2),
        name="cd_out",
    )(x, mod, ng, *yc_parts, yd, w_out)


def _t5_bucket(dist):
    dist = np.asarray(dist)
    max_exact = N_BUCKETS // 2
    large = max_exact + (np.log(np.maximum(dist, max_exact) / max_exact) / np.log(MAX_DISTANCE / max_exact)
                         * (N_BUCKETS - max_exact)).astype(np.int32)
    large = np.minimum(large, N_BUCKETS - 1)
    return np.where(dist < max_exact, dist, large).astype(np.int32)


def _group_bias(rel_bias, g, dil):
    buckets = _t5_bucket(dil * np.arange(C_TAPS + 1))
    return rel_bias[buckets][:, g * C_HPG:(g + 1) * C_HPG].T


def _toeplitz_body(c_ref, o_ref):
    keep = lax.broadcasted_iota(jnp.int32, (Q_BLOCK, 2 * Q_BLOCK), 1) >= Q_BLOCK
    for h in range(C_HPG):
        taps = jnp.broadcast_to(c_ref[h:h + 1, :], (Q_BLOCK, 2 * Q_BLOCK))
        t = pltpu.roll(taps, 0, 1, stride=1, stride_axis=0)
        o_ref[1, h * Q_BLOCK:(h + 1) * Q_BLOCK, :] = t
        o_ref[0, h * Q_BLOCK:(h + 1) * Q_BLOCK, :] = jnp.where(keep, t, NEG)


def _prompt_bias(bias_gs):
    c = jnp.stack([jnp.concatenate([b[:, ::-1].astype(F32), jnp.full((C_HPG, Q_BLOCK - 1), NEG, F32)], axis=1)
                   for b in bias_gs])
    return pl.pallas_call(
        _toeplitz_body,
        grid=(len(bias_gs),),
        in_specs=[pl.BlockSpec((None, C_HPG, 2 * Q_BLOCK), lambda g: (g, 0, 0))],
        out_specs=pl.BlockSpec((None, 2, C_HPG * Q_BLOCK, 2 * Q_BLOCK), lambda g: (g, 0, 0, 0)),
        out_shape=jax.ShapeDtypeStruct((len(bias_gs), 2, C_HPG * Q_BLOCK, 2 * Q_BLOCK), F32),
        compiler_params=_cparams(1),
        name="attn_bias",
    )(c)


def _sample_bias(bias_g, wb, dil):
    z = jnp.stack([bias_g.astype(F32)] + [jnp.full(bias_g.shape, NEG, F32)] * (dil - 1), axis=-1)
    z = z.reshape(C_HPG, (C_TAPS + 1) * dil)
    n = wb + DEC_SEQ
    z = z[:, :n] if z.shape[1] >= n else jnp.pad(z, ((0, 0), (0, n - z.shape[1])), constant_values=NEG)
    fl = z[:, ::-1]
    bc = jnp.stack([fl[:, DEC_SEQ - 1 - t:DEC_SEQ - 1 - t + wb] for t in range(DEC_SEQ)], axis=1)
    bc = jnp.pad(bc, ((0, 0), (0, 8 - DEC_SEQ), (0, 0))).reshape(SROWS, wb)
    bn = jnp.stack([jnp.pad(z[:, :t + 1][:, ::-1], ((0, 0), (0, DEC_SEQ - 1 - t)), constant_values=NEG)
                    for t in range(DEC_SEQ)], axis=1)
    bn = jnp.pad(bn, ((0, 0), (0, 8 - DEC_SEQ), (0, 128 - DEC_SEQ)), constant_values=NEG).reshape(SROWS, 128)
    return bc, bn


def _time_major(s):
    b, k, c = s.shape
    return s.transpose(1, 0, 2).reshape(1, k * b, c)


def _batch_major(s, b):
    _, r, c = s.shape
    return s.reshape(r // b, b, c).transpose(1, 0, 2)


def _stack(x, mod, st, w, *, TM, RS, pos0):
    NB = x.shape[0]
    new = {}
    ng = w["norm_g"]
    x, new["a"], new["b"] = _mixer_ab(x, mod[0], ng[0], st["a"], st["b"], w["ab_w_in"], w["a_conv_w"],
                                      w["b_w_grp"], w["b_scale"], w["ab_w_out"], TM=TM, RS=RS, pos0=pos0)
    x, new["f0"] = _conv_ffn(x, mod[0], ng[0], st["f0"], w["ffn_w_up"][0], w["ffn_conv_w"][0],
                             w["ffn_conv_b"][0], w["ffn_w_down"][0], TM=TM, RS=RS)
    tail_rows = min(C_PAIRS[-1][0], x.shape[1])
    q, k, v, kvf, yd, new["d"] = _cd_in(x, mod[1], ng[1], st["d"], w["cd_w_in"], w["d_conv_w"], w["d_conv_b"],
                                        w["d_ln_g"], w["d_ln_b"], TM=TM, RS=RS, tail_rows=tail_rows)
    if RS == 1:
        parts_o, parts_l = [], []
        pbias = _prompt_bias(w["bias_g"])
        for g, (win, dil) in enumerate(C_PAIRS):
            o, lse = _attn_prompt(q, k, v, pbias[g], g, dil)
            parts_o.append(o)
            parts_l.append(lse)
            wb = min(win, x.shape[1])
            kt = kvf[:, tail_rows - wb:, g * C_GW:(g + 1) * C_GW]
            vt = kvf[:, tail_rows - wb:, 768 + g * C_GW:768 + (g + 1) * C_GW]
            new["c%d" % g] = jnp.stack([kt, vt], axis=2).reshape(NB, wb, 2, C_HPG, C_HEAD_DIM)
        yc_parts = parts_o + parts_l
    else:
        B = RS
        qb = _batch_major(q, B)
        q_rep = jnp.pad(jnp.broadcast_to(qb[:, None], (B, C_HPG, DEC_SEQ, 768)),
                        ((0, 0), (0, 0), (0, 8 - DEC_SEQ), (0, 0))).reshape(B, SROWS, 768)
        kvn = jnp.pad(_batch_major(kvf, B), ((0, 0), (8 - DEC_SEQ, 0), (0, 0)))
        bias = [_sample_bias(w["bias_g"][g], win, dil) for g, (win, dil) in enumerate(C_PAIRS)]
        yc, n0, n1, n2 = _attn_sample(q_rep, kvn, st["c"], [b[0] for b in bias],
                                      jnp.stack([b[1] for b in bias]))
        for g, n in enumerate((n0, n1, n2)):
            new["c%d" % g] = n.reshape(B, 2, C_HPG, C_HEAD_DIM, n.shape[2]).transpose(0, 4, 1, 2, 3)
        yc_parts = [_time_major(yc[:, :DEC_SEQ])]
    x = _cd_out(x, mod[1], ng[1], yc_parts, yd, w["cd_w_out"], TM=TM)
    x, new["f1"] = _conv_ffn(x, mod[1], ng[1], st["f1"], w["ffn_w_up"][1], w["ffn_conv_w"][1],
                             w["ffn_conv_b"][1], w["ffn_w_down"][1], TM=TM, RS=RS)
    return x, new


def kernel(x_prompt, x_sample, state_a_conv, state_b_pool, cache_c_win128, cache_c_win512, cache_c_win2048,
           state_d_conv, state_ffn_conv, c_prompt, c_sample, ada_w, ada_b, norm_g, rel_bias, ab_w_in, a_conv_w,
           b_w_grp, b_scale, ab_w_out, cd_w_in, d_conv_w, d_conv_b, d_ln_g, d_ln_b, cd_w_out, ffn_w_up,
           ffn_conv_w, ffn_conv_b, ffn_w_down):
    B, T = DEC_BATCH, DEC_SEQ
    w = dict(norm_g=norm_g,
             ab_w_in=ab_w_in[0].astype(BF16), a_conv_w=a_conv_w[0], b_w_grp=b_w_grp[0].astype(BF16),
             b_scale=b_scale, ab_w_out=ab_w_out[0].astype(BF16),
             cd_w_in=cd_w_in[0].astype(BF16),
             d_conv_w=jnp.broadcast_to(d_conv_w[0][:, None, :], (D_CONV, SUBLANES, D_WIDTH)),
             d_conv_b=d_conv_b, d_ln_g=d_ln_g,
             d_ln_b=d_ln_b, cd_w_out=cd_w_out[0].astype(BF16),
             ffn_w_up=ffn_w_up.astype(BF16), ffn_conv_w=ffn_conv_w, ffn_conv_b=ffn_conv_b[:, None, :],
             ffn_w_down=ffn_w_down.astype(BF16),
             bias_g=[_group_bias(rel_bias, g, dil) for g, (_, dil) in enumerate(C_PAIRS)])

    mod = _ada(jnp.concatenate([c_prompt, c_sample], axis=0), ada_w, ada_b)
    mod_p = mod[:, :BATCH].reshape(DEPTH, BATCH, 6, 1, D_MODEL)
    mod_s = mod[:, BATCH:].reshape(DEPTH, B, 6, D_MODEL).transpose(0, 2, 1, 3)
    mod_s = jnp.broadcast_to(mod_s[:, :, None], (DEPTH, 6, T, B, D_MODEL)).reshape(DEPTH, 1, 6, T * B, D_MODEL)

    zeros = lambda k, c: jnp.zeros((BATCH, k, c), F32)
    st_p = dict(a=zeros(A_CONV - 1, A_WIDTH), b=zeros(B_PREV, B_WIDTH), d=zeros(D_CONV - 1, D_WIDTH),
                f0=zeros(FFN_CONV - 1, 2 * D_FF), f1=zeros(FFN_CONV - 1, 2 * D_FF))
    y_p, np_ = _stack(x_prompt, mod_p, st_p, w, TM=TM_PROMPT, RS=1, pos0=0)

    st_s = dict(a=_time_major(state_a_conv[0]), b=_time_major(state_b_pool[0]), d=_time_major(state_d_conv[0]),
                f0=_time_major(state_ffn_conv[0]), f1=_time_major(state_ffn_conv[1]),
                c=[c[0].transpose(0, 2, 3, 4, 1).reshape(B, 512, c.shape[2])
                   for c in (cache_c_win128, cache_c_win512, cache_c_win2048)])
    y_s, ns = _stack(_time_major(x_sample), mod_s, st_s, w, TM=T * B, RS=B, pos0=PAST_LEN)

    bm = lambda s: _batch_major(s, B)
    return (y_p, bm(y_s),
            np_["a"][None], bm(ns["a"])[None], np_["b"][None], bm(ns["b"])[None],
            np_["c0"][None], ns["c0"][None], np_["c1"][None], ns["c1"][None], np_["c2"][None], ns["c2"][None],
            np_["d"][None], bm(ns["d"])[None],
            jnp.stack([np_["f0"], np_["f1"]]), jnp.stack([bm(ns["f0"]), bm(ns["f1"])]))
```

```python
import functools
import math

import numpy as np
import jax
import jax.numpy as jnp
from jax import lax
from jax.experimental import pallas as pl
from jax.experimental.pallas import tpu as pltpu

D_MODEL = 1024
BATCH = 4
SEQ = 4096
DEPTH = 2
DEC_BATCH = 32
DEC_SEQ = 4
PAST_LEN = 8192
EPS = 1e-6
A_WIDTH = 512
A_CONV = 3
B_WIDTH = 512
B_WINDOWS = (2, 4, 8, 16)
B_GROUP = 128
B_PREV = 15
C_PAIRS = ((128, 1), (512, 4), (2048, 16))
C_HPG = 4
C_HEAD_DIM = 64
C_HEADS = 12
C_QKV = 2304
C_GW = C_HPG * C_HEAD_DIM
C_TAPS = 128
ATTN_SCALE = C_HEAD_DIM ** -0.5
Q_BLOCK = 128
N_BUCKETS = 32
MAX_DISTANCE = 2048
D_WIDTH = 512
D_CONV = 31
D_FF = 2816
FFN_CONV = 3

SUBLANES = 8
LANES = 128
VMEM_LIMIT = 56 * 1024 * 1024
NEG = -1e30
TM_PROMPT = 512
ATT_ROWS = 2048
FF_CHUNK = 256
D_ROWS = 32

F32 = jnp.float32
BF16 = jnp.bfloat16


def _ru8(n):
    return -(-n // SUBLANES) * SUBLANES


def _pool_levels(rs):
    l1 = _ru8(rs)
    l2 = _ru8(l1 + 2 * rs)
    l3 = _ru8(l2 + 4 * rs)
    l4 = _ru8(l3 + 8 * rs)
    return l1, l2, l3, l4


def _slab_put(ref, r0, val):
    n = val.shape[0]
    for s in range(val.shape[1] // LANES):
        ref[s, r0:r0 + n, :] = val[:, s * LANES:(s + 1) * LANES]


def _slab_get(ref, r0, n, s0=0, ns=None):
    ns = ref.shape[0] - s0 if ns is None else ns
    return jnp.concatenate([ref[s, r0:r0 + n, :] for s in range(s0, s0 + ns)], axis=-1)


def _rms(x, g):
    return x * lax.rsqrt(jnp.mean(x * x, axis=-1, keepdims=True) + EPS) * g


def _sigmoid(x):
    return 1.0 / (1.0 + jnp.exp(-x))


def _cparams(n_axes):
    return pltpu.CompilerParams(dimension_semantics=("arbitrary",) * n_axes, vmem_limit_bytes=VMEM_LIMIT)


def _const_spec(shape):
    nd = len(shape)
    return pl.BlockSpec(shape, lambda *_: (0,) * nd, pipeline_mode=pl.Buffered(1))


def _ada_body(c_ref, w_ref, b_ref, o_ref):
    c = c_ref[...]
    ca = c * _sigmoid(c)
    o_ref[...] = jnp.dot(ca.astype(BF16), w_ref[...].astype(BF16), preferred_element_type=F32) + b_ref[...]


def _ada(c_all, ada_w, ada_b):
    rows = c_all.shape[0]
    tn = 1536
    return pl.pallas_call(
        _ada_body,
        grid=(DEPTH, 6 * D_MODEL // tn),
        in_specs=[pl.BlockSpec((rows, D_MODEL), lambda l, n: (0, 0)),
                  pl.BlockSpec((None, D_MODEL, tn), lambda l, n: (l, 0, n)),
                  pl.BlockSpec((None, 1, tn), lambda l, n: (l, 0, n))],
        out_specs=pl.BlockSpec((None, rows, tn), lambda l, n: (l, 0, n)),
        out_shape=jax.ShapeDtypeStruct((DEPTH, rows, 6 * D_MODEL), F32),
        compiler_params=_cparams(2),
        name="ada",
    )(c_all, ada_w, ada_b.reshape(DEPTH, 1, 6 * D_MODEL))


def _ab_body(x_ref, mod_ref, ng_ref, ap_ref, bp_ref, win_ref, aw_ref, bw_ref, bs_ref, wout_ref,
             x1_ref, an_ref, bn_ref, ea, eb, s2, s4, s8, *, TM, RS, NT, pos0):
    j = pl.program_id(1)
    HA = _ru8((A_CONV - 1) * RS)
    L1, L2, L3, HB = _pool_levels(RS)
    E = HB + TM
    na, nbp = (A_CONV - 1) * RS, B_PREV * RS

    @pl.when(j == 0)
    def _():
        _slab_put(ea, HA - na, ap_ref[...])
        if HB > nbp:
            _slab_put(eb, 0, jnp.zeros((HB - nbp, B_WIDTH), F32))
        _slab_put(eb, HB - nbp, bp_ref[...])

    x = x_ref[...]
    sh1, sc1, g1 = mod_ref[0], mod_ref[1], mod_ref[2]
    h = _rms(x, ng_ref[0:1, :]) * (1.0 + sc1) + sh1
    proj = jnp.dot(h.astype(BF16), win_ref[...], preferred_element_type=F32)
    hh, bg = proj[:, 0:A_WIDTH], proj[:, A_WIDTH:2 * A_WIDTH]
    cg, u = proj[:, 2 * A_WIDTH:3 * A_WIDTH], proj[:, 3 * A_WIDTH:]

    v = cg * hh
    _slab_put(ea, HA, v)
    z = (aw_ref[2:3, :] * v + aw_ref[1:2, :] * _slab_get(ea, HA - RS, TM)
         + aw_ref[0:1, :] * _slab_get(ea, HA - 2 * RS, TM))
    ya = bg * z
    a_last = _slab_get(ea, HA + TM - na, na)
    an_ref[...] = a_last
    if NT > 1:
        _slab_put(ea, HA - na, a_last)

    _slab_put(eb, HB, u)
    _slab_put(s2, L1, _slab_get(eb, L1, E - L1) + _slab_get(eb, L1 - RS, E - L1))
    _slab_put(s4, L2, _slab_get(s2, L2, E - L2, 1) + _slab_get(s2, L2 - 2 * RS, E - L2, 1))
    _slab_put(s8, L3, _slab_get(s4, L3, E - L3, 1) + _slab_get(s4, L3 - 4 * RS, E - L3, 1))
    wsum = (s2[0, HB:E, :], s4[0, HB:E, :], s8[0, HB:E, :], s8[1, HB:E, :] + s8[1, HB - 8 * RS:E - 8 * RS, :])
    b_last = _slab_get(eb, E - nbp, nbp)
    bn_ref[...] = b_last
    if NT > 1:
        _slab_put(eb, HB - nbp, b_last)
    row = lax.broadcasted_iota(jnp.int32, (TM, B_GROUP), 0) + j * TM
    pos1 = lax.shift_right_logical(row, int(math.log2(RS))) + (pos0 + 1)
    ybs = []
    for g, win in enumerate(B_WINDOWS):
        cnt = jnp.minimum(pos1, win).astype(F32)
        pooled = wsum[g] / cnt - u[:, g * B_GROUP:(g + 1) * B_GROUP]
        ybs.append(jnp.dot(pooled.astype(BF16), bw_ref[g], preferred_element_type=F32))
    yb = jnp.concatenate(ybs, axis=-1) * bs_ref[...]

    ycat = jnp.concatenate([ya, yb], axis=-1).astype(BF16)
    y = jnp.dot(ycat, wout_ref[...], preferred_element_type=F32)
    x1_ref[...] = x + g1 * _rms(y, ng_ref[1:2, :])


def _mixer_ab(x, mod, ng, a_prev, b_prev, w_in, a_w, b_w, b_scale, w_out, *, TM, RS, pos0):
    NB, R, _ = x.shape
    NT = R // TM
    MR = mod.shape[2]
    _, _, _, HB = _pool_levels(RS)
    HA = _ru8((A_CONV - 1) * RS)
    E = HB + TM
    na, nbp = (A_CONV - 1) * RS, B_PREV * RS
    tile = lambda c: pl.BlockSpec((None, TM, c), lambda n, j: (n, j, 0))
    per_n = lambda r, c: pl.BlockSpec((None, r, c), lambda n, j: (n, 0, 0))
    return pl.pallas_call(
        functools.partial(_ab_body, TM=TM, RS=RS, NT=NT, pos0=pos0),
        grid=(NB, NT),
        in_specs=[tile(D_MODEL),
                  pl.BlockSpec((None, 6, MR, D_MODEL), lambda n, j: (n, 0, 0, 0)),
                  _const_spec((4, D_MODEL)),
                  per_n(na, A_WIDTH), per_n(nbp, B_WIDTH),
                  _const_spec(w_in.shape), _const_spec(a_w.shape), _const_spec(b_w.shape),
                  _const_spec(b_scale.shape), _const_spec(w_out.shape)],
        out_specs=[tile(D_MODEL), per_n(na, A_WIDTH), per_n(nbp, B_WIDTH)],
        out_shape=[jax.ShapeDtypeStruct((NB, R, D_MODEL), F32),
                   jax.ShapeDtypeStruct((NB, na, A_WIDTH), F32),
                   jax.ShapeDtypeStruct((NB, nbp, B_WIDTH), F32)],
        scratch_shapes=[pltpu.VMEM((4, HA + TM, LANES), F32), pltpu.VMEM((4, E, LANES), F32),
                        pltpu.VMEM((4, E, LANES), F32), pltpu.VMEM((3, E, LANES), F32),
                        pltpu.VMEM((2, E, LANES), F32)],
        compiler_params=_cparams(2),
        name="mixer_ab",
    )(x, mod, ng, a_prev, b_prev, w_in, a_w, b_w, b_scale, w_out)


def _ffn_body(x_ref, mod_ref, ng_ref, fp_ref, wup_ref, cw_ref, cb_ref, wdn_ref,
              y_ref, fn_ref, halo, ext, act, *, TM, RS, NT):
    j = pl.program_id(1)
    nf = (FFN_CONV - 1) * RS
    HF = _ru8(nf)

    @pl.when(j == 0)
    def _():
        halo[...] = fp_ref[...]

    x = x_ref[...]
    sh2, sc2, g2 = mod_ref[3], mod_ref[4], mod_ref[5]
    hb = (_rms(x, ng_ref[2:3, :]) * (1.0 + sc2) + sh2).astype(BF16)

    def conv_part(col, buf):
        up = jnp.dot(hb, wup_ref[:, col:col + FF_CHUNK], preferred_element_type=F32)
        _slab_put(buf, HF - nf, halo[:, col:col + FF_CHUNK])
        _slab_put(buf, HF, up)
        out = (cw_ref[2:3, col:col + FF_CHUNK] * up
               + cw_ref[1:2, col:col + FF_CHUNK] * _slab_get(buf, HF - RS, TM)
               + cw_ref[0:1, col:col + FF_CHUNK] * _slab_get(buf, HF - 2 * RS, TM)
               + cb_ref[:, col:col + FF_CHUNK])
        halo[:, col:col + FF_CHUNK] = _slab_get(buf, HF + TM - nf, nf)
        return out

    for c in range(D_FF // FF_CHUNK):
        a = conv_part(c * FF_CHUNK, ext.at[2 * (c % 2)])
        g = conv_part(D_FF + c * FF_CHUNK, ext.at[2 * (c % 2) + 1])
        act[:, c * FF_CHUNK:(c + 1) * FF_CHUNK] = (a * (g * _sigmoid(g))).astype(BF16)
    fn_ref[...] = halo[...]
    y = jnp.dot(act[...], wdn_ref[...], preferred_element_type=F32)
    y_ref[...] = x + g2 * _rms(y, ng_ref[3:4, :])


def _conv_ffn(x, mod, ng, f_prev, w_up, conv_w, conv_b, w_down, *, TM, RS):
    NB, R, _ = x.shape
    NT = R // TM
    MR = mod.shape[2]
    nf = (FFN_CONV - 1) * RS
    tile = pl.BlockSpec((None, TM, D_MODEL), lambda n, j: (n, j, 0))
    per_n = pl.BlockSpec((None, nf, 2 * D_FF), lambda n, j: (n, 0, 0))
    return pl.pallas_call(
        functools.partial(_ffn_body, TM=TM, RS=RS, NT=NT),
        grid=(NB, NT),
        in_specs=[tile,
                  pl.BlockSpec((None, 6, MR, D_MODEL), lambda n, j: (n, 0, 0, 0)),
                  _const_spec((4, D_MODEL)),
                  per_n,
                  _const_spec(w_up.shape), _const_spec(conv_w.shape), _const_spec(conv_b.shape),
                  _const_spec(w_down.shape)],
        out_specs=[tile, per_n],
        out_shape=[jax.ShapeDtypeStruct((NB, R, D_MODEL), F32),
                   jax.ShapeDtypeStruct((NB, nf, 2 * D_FF), F32)],
        scratch_shapes=[pltpu.VMEM((nf, 2 * D_FF), F32),
                        pltpu.VMEM((4, FF_CHUNK // LANES, _ru8(nf) + TM, LANES), F32),
                        pltpu.VMEM((TM, D_FF), BF16)],
        compiler_params=_cparams(2),
        name="conv_ffn",
    )(x, mod, ng, f_prev, w_up, conv_w, conv_b, w_down)


def _cd_in_body(*refs, TM, RS, NT, permute):
    if permute:
        (x_ref, mod_ref, ng_ref, dp_ref, win_ref, dw_ref, db_ref, lg_ref, lb_ref, perm_ref,
         q0_ref, q1_ref, q2_ref, kv0_ref, kv1_ref, kv2_ref, kvf_ref, yd_ref, dn_ref, ed) = refs
    else:
        (x_ref, mod_ref, ng_ref, dp_ref, win_ref, dw_ref, db_ref, lg_ref, lb_ref,
         q0_ref, q1_ref, q2_ref, kv0_ref, kv1_ref, kv2_ref, kvf_ref, yd_ref, dn_ref, ed) = refs
    q_refs, kv_refs = (q0_ref, q1_ref, q2_ref), (kv0_ref, kv1_ref, kv2_ref)
    j = pl.program_id(1)
    nd = (D_CONV - 1) * RS
    HD = _ru8(nd)

    @pl.when(j == 0)
    def _():
        _slab_put(ed, HD - nd, dp_ref[...])

    x = x_ref[...]
    sh1, sc1 = mod_ref[0], mod_ref[1]
    h = _rms(x, ng_ref[0:1, :]) * (1.0 + sc1) + sh1
    proj = jnp.dot(h.astype(BF16), win_ref[...], preferred_element_type=F32)
    qs = (proj[:, 0:768] * ATTN_SCALE).astype(BF16)
    ks = proj[:, 768:1536].astype(BF16)
    vs = proj[:, 1536:2304].astype(BF16)
    for g in range(len(C_PAIRS)):
        cols = slice(g * C_GW, (g + 1) * C_GW)
        xg = jnp.concatenate([qs[:, cols], ks[:, cols], vs[:, cols]], axis=-1)
        if permute and C_PAIRS[g][1] > 1:
            xg = jnp.dot(perm_ref[g - 1], xg, preferred_element_type=F32).astype(BF16)
        q_refs[g][...] = xg[:, 0:C_GW]
        kv_refs[g][...] = xg[:, C_GW:]
    kvf_ref[...] = proj[:, 768:2304]
    dv, dg = proj[:, C_QKV:C_QKV + D_WIDTH], proj[:, C_QKV + D_WIDTH:]
    _slab_put(ed, HD, dv * _sigmoid(dg))

    db, lg, lb = db_ref[...], lg_ref[...], lb_ref[...]
    tiles = D_ROWS // SUBLANES
    for c in range(TM // D_ROWS):
        accs = []
        for s in range(D_WIDTH // LANES):
            acc = jnp.zeros((tiles, SUBLANES, LANES), F32)
            for kk in range(D_CONV):
                off = HD - (D_CONV - 1 - kk) * RS + c * D_ROWS
                tap = ed[s, off:off + D_ROWS, :].reshape(tiles, SUBLANES, LANES)
                acc = acc + dw_ref[kk, :, s * LANES:(s + 1) * LANES] * tap
            accs.append(acc.reshape(D_ROWS, LANES))
        zc = jnp.concatenate(accs, axis=-1) + db
        mu = jnp.mean(zc, axis=-1, keepdims=True)
        zc = zc - mu
        var = jnp.mean(zc * zc, axis=-1, keepdims=True)
        yl = zc * lax.rsqrt(var + EPS) * lg + lb
        yd_ref[c * D_ROWS:(c + 1) * D_ROWS, :] = (yl * _sigmoid(yl)).astype(BF16)

    d_last = _slab_get(ed, HD + TM - nd, nd)
    dn_ref[...] = d_last
    if NT > 1:
        _slab_put(ed, HD - nd, d_last)


def _residue_perm(tm, dil):
    p = np.zeros((tm, tm), np.float32)
    a, r = np.meshgrid(np.arange(tm // dil), np.arange(dil), indexing="ij")
    p[(r * (tm // dil) + a).ravel(), (dil * a + r).ravel()] = 1.0
    return p


def _cd_in(x, mod, ng, d_prev, w_in, d_w, d_b, ln_g, ln_b, *, TM, RS, tail_rows, permute):
    NB, R, _ = x.shape
    NT = R // TM
    MR = mod.shape[2]
    nd = (D_CONV - 1) * RS
    n_tail = tail_rows // TM
    perm_in, perm_spec = [], []
    if permute:
        perm_in = [jnp.asarray(np.stack([_residue_perm(TM, dil) for _, dil in C_PAIRS[1:]]), BF16)]
        perm_spec = [_const_spec((len(C_PAIRS) - 1, TM, TM))]
    tile = lambda c: pl.BlockSpec((None, TM, c), lambda n, j: (n, j, 0))
    per_n = pl.BlockSpec((None, nd, D_WIDTH), lambda n, j: (n, 0, 0))
    tail = pl.BlockSpec((None, TM, 1536), lambda n, j: (n, jnp.maximum(j - (NT - n_tail), 0), 0))
    return pl.pallas_call(
        functools.partial(_cd_in_body, TM=TM, RS=RS, NT=NT, permute=permute),
        grid=(NB, NT),
        in_specs=[tile(D_MODEL),
                  pl.BlockSpec((None, 6, MR, D_MODEL), lambda n, j: (n, 0, 0, 0)),
                  _const_spec((4, D_MODEL)),
                  per_n,
                  _const_spec(w_in.shape), _const_spec(d_w.shape), _const_spec(d_b.shape),
                  _const_spec(ln_g.shape), _const_spec(ln_b.shape)] + perm_spec,
        out_specs=[tile(C_GW)] * 3 + [tile(2 * C_GW)] * 3 + [tail, tile(D_WIDTH), per_n],
        out_shape=[jax.ShapeDtypeStruct((NB, R, C_GW), BF16)] * 3
                  + [jax.ShapeDtypeStruct((NB, R, 2 * C_GW), BF16)] * 3
                  + [jax.ShapeDtypeStruct((NB, tail_rows, 1536), F32),
                     jax.ShapeDtypeStruct((NB, R, D_WIDTH), BF16),
                     jax.ShapeDtypeStruct((NB, nd, D_WIDTH), F32)],
        scratch_shapes=[pltpu.VMEM((D_WIDTH // LANES, _ru8(nd) + TM, LANES), F32)],
        compiler_params=_cparams(2),
        name="cd_in",
    )(x, mod, ng, d_prev, w_in, d_w, d_b, ln_g, ln_b, *perm_in)


def _head_of_lane(shape):
    return lax.shift_right_logical(lax.broadcasted_iota(jnp.int32, shape, 1), 6)


def _attn_block(q, kvp, kvc, bias):
    lane_head = _head_of_lane(q.shape)
    zero = jnp.zeros_like(q)
    qs = jnp.concatenate([jnp.where(lane_head == h, q, zero) for h in range(C_HPG)], axis=0)
    kcat = jnp.concatenate([kvp[:, 0:C_GW], kvc[:, 0:C_GW]], axis=0)
    vcat = jnp.concatenate([kvp[:, C_GW:], kvc[:, C_GW:]], axis=0)
    logits = lax.dot_general(qs, kcat, (((1,), (1,)), ((), ())), preferred_element_type=F32) + bias
    m = jnp.max(logits, axis=-1, keepdims=True)
    p = jnp.exp(logits - m)
    s = jnp.sum(p, axis=-1, keepdims=True)
    pv = jnp.dot(p.astype(BF16), vcat, preferred_element_type=F32)
    on = pv / s
    lse = m + jnp.log(s)
    o = jnp.zeros((Q_BLOCK, C_GW), F32)
    l = jnp.zeros((Q_BLOCK, C_GW), F32)
    for h in range(C_HPG):
        sel = lane_head == h
        o = jnp.where(sel, on[h * Q_BLOCK:(h + 1) * Q_BLOCK, :], o)
        l = jnp.where(sel, lse[h * Q_BLOCK:(h + 1) * Q_BLOCK, :], l)
    return o, l


def _attn_prompt_body(q0, q1, q2, kv0, kv1, kv2, p0, p1, p2, bias_ref, yc_ref, osc, lsc, *, TM):
    qs_, kvs, prevs = (q0, q1, q2), (kv0, kv1, kv2), (p0, p1, p2)
    var = jnp.minimum(pl.program_id(1), 1)
    n_blocks = ATT_ROWS // Q_BLOCK

    def put(g, start, n, stride, o, l, r0):
        rows = pl.ds(start, n, stride=stride) if stride > 1 else pl.ds(start, n)
        for s in range(C_GW // LANES):
            osc[g, s, rows, :] = o[r0:r0 + n, s * LANES:(s + 1) * LANES]
            lsc[g, s, rows, :] = l[r0:r0 + n, s * LANES:(s + 1) * LANES]

    for g, (_, dil) in enumerate(C_PAIRS):
        q_ref, kv_ref, p_ref = qs_[g], kvs[g], prevs[g]
        cls = TM // dil
        if cls >= Q_BLOCK:
            prev_off = Q_BLOCK if dil == 1 else TM
            n_first = prev_off // Q_BLOCK
            per_tile = TM // Q_BLOCK

            def token_start(idx, dil=dil, per_tile=per_tile):
                return idx * Q_BLOCK if dil == 1 else (idx // per_tile) * TM + idx % per_tile

            for idx in range(n_first):
                rows = slice(idx * Q_BLOCK, (idx + 1) * Q_BLOCK)
                o, l = _attn_block(q_ref[rows, :], p_ref[rows, :], kv_ref[rows, :], bias_ref[g, var])
                put(g, token_start(idx), Q_BLOCK, dil, o, l, 0)

            def body(idx, carry, g=g, dil=dil, q_ref=q_ref, kv_ref=kv_ref, prev_off=prev_off,
                     token_start=token_start):
                st = pl.multiple_of(idx * Q_BLOCK, Q_BLOCK)
                o, l = _attn_block(q_ref[pl.ds(st, Q_BLOCK), :], kv_ref[pl.ds(st - prev_off, Q_BLOCK), :],
                                   kv_ref[pl.ds(st, Q_BLOCK), :], bias_ref[g, 1])
                put(g, token_start(idx), Q_BLOCK, dil, o, l, 0)
                return carry

            lax.fori_loop(n_first, n_blocks, body, 0)
        else:
            tiles = Q_BLOCK // cls

            def body(r, carry, g=g, dil=dil, q_ref=q_ref, kv_ref=kv_ref, p_ref=p_ref, cls=cls, tiles=tiles):
                st = pl.multiple_of(r * cls, cls)
                gather = lambda ref: jnp.concatenate([ref[pl.ds(c * TM + st, cls), :] for c in range(tiles)], axis=0)
                o, l = _attn_block(gather(q_ref), gather(p_ref), gather(kv_ref), bias_ref[g, var])
                for c in range(tiles):
                    put(g, c * TM + r, cls, dil, o, l, c * cls)
                return carry

            lax.fori_loop(0, dil, body, 0)

    def merge(ch, carry):
        st = pl.multiple_of(ch * Q_BLOCK, Q_BLOCK)
        get = lambda ref, g: jnp.concatenate([ref[g, s, pl.ds(st, Q_BLOCK), :] for s in range(C_GW // LANES)], axis=-1)
        ls = [get(lsc, g) for g in range(len(C_PAIRS))]
        mm = jnp.maximum(jnp.maximum(ls[0], ls[1]), ls[2])
        es = [jnp.exp(l - mm) for l in ls]
        den = es[0] + es[1] + es[2]
        yc = (es[0] / den) * get(osc, 0) + (es[1] / den) * get(osc, 1) + (es[2] / den) * get(osc, 2)
        yc_ref[pl.ds(st, Q_BLOCK), :] = yc.astype(BF16)
        return carry

    lax.fori_loop(0, n_blocks, merge, 0)


def _attn_prompt(qs, kvs, bias, *, TM):
    N, S, _ = qs[0].shape
    assert S % ATT_ROWS == 0 and ATT_ROWS == Q_BLOCK * C_PAIRS[-1][1] and ATT_ROWS % TM == 0
    cur = lambda c: pl.BlockSpec((None, ATT_ROWS, c), lambda n, i: (n, i, 0))
    prev_rows = [Q_BLOCK if dil == 1 else (TM if TM // dil >= Q_BLOCK else ATT_ROWS) for _, dil in C_PAIRS]
    prev = [pl.BlockSpec((None, pr, 2 * C_GW), lambda n, i, k=ATT_ROWS // pr: (n, jnp.maximum(i * k - 1, 0), 0))
            for pr in prev_rows]
    slabs = C_GW // LANES
    return pl.pallas_call(
        functools.partial(_attn_prompt_body, TM=TM),
        grid=(N, S // ATT_ROWS),
        in_specs=[cur(C_GW)] * 3 + [cur(2 * C_GW)] * 3 + prev + [_const_spec(bias.shape)],
        out_specs=cur(C_GW),
        out_shape=jax.ShapeDtypeStruct((N, S, C_GW), BF16),
        scratch_shapes=[pltpu.VMEM((len(C_PAIRS), slabs, ATT_ROWS, LANES), F32)] * 2,
        compiler_params=_cparams(2),
        name="attn_prompt",
    )(*qs, *kvs, *kvs, bias)


SROWS = 32


def _attn_sample_body(q_ref, kvn_ref, c0_ref, c1_ref, c2_ref, bc0_ref, bc1_ref, bc2_ref, bn_ref,
                      yc_ref, n0_ref, n1_ref, n2_ref):
    caches = (c0_ref, c1_ref, c2_ref)
    bcs = (bc0_ref, bc1_ref, bc2_ref)
    news = (n0_ref, n1_ref, n2_ref)
    lane_head = _head_of_lane((SROWS, C_GW))
    row_head = lax.shift_right_logical(lax.broadcasted_iota(jnp.int32, (SROWS, C_GW), 0), 3)
    own = lane_head == row_head
    lane128 = lax.broadcasted_iota(jnp.int32, (SROWS, 128), 1)
    kvn = kvn_ref[...]
    r0 = 8 - DEC_SEQ
    outs, lses = [], []
    for g, (win, dil) in enumerate(C_PAIRS):
        wb = win
        qg = q_ref[:, g * C_GW:(g + 1) * C_GW]
        qs = jnp.where(own, qg, jnp.zeros_like(qg))
        cache = caches[g]
        kt = cache[0:C_GW, :].astype(BF16)
        vt = cache[C_GW:2 * C_GW, :].astype(BF16)
        kn = kvn[:, g * C_GW:(g + 1) * C_GW]
        vn = kvn[:, 768 + g * C_GW:768 + (g + 1) * C_GW]
        lc = jnp.dot(qs, kt, preferred_element_type=F32) + bcs[g][...]
        qf = qs.astype(F32)
        ln = bn_ref[g]
        for c in range(DEC_SEQ):
            d = jnp.sum(qf * kn[r0 + c:r0 + c + 1, :], axis=-1, keepdims=True)
            ln = ln + jnp.where(lane128 == c, d, 0.0)
        m = jnp.maximum(jnp.max(lc, axis=-1, keepdims=True), jnp.max(ln, axis=-1, keepdims=True))
        pc = jnp.exp(lc - m)
        pn = jnp.exp(ln - m)
        s = jnp.sum(pc, axis=-1, keepdims=True) + jnp.sum(pn, axis=-1, keepdims=True)
        pv = lax.dot_general(pc.astype(BF16), vt, (((1,), (1,)), ((), ())), preferred_element_type=F32)
        for c in range(DEC_SEQ):
            pcol = jnp.sum(jnp.where(lane128 == c, pn, 0.0), axis=-1, keepdims=True)
            pv = pv + pcol * vn[r0 + c:r0 + c + 1, :]
        outs.append(pv / s)
        lses.append(m + jnp.log(s))
        rolled = pltpu.roll(cache[...], wb - DEC_SEQ, 1)
        new_rows = jnp.concatenate([jnp.zeros((128 - 8, 2 * C_GW), F32), jnp.concatenate([kn, vn], axis=-1)], axis=0)
        new_cols = new_rows.T
        lane_t = lax.broadcasted_iota(jnp.int32, (2 * C_GW, 128), 1)
        if wb > 128:
            news[g][:, 0:wb - 128] = rolled[:, 0:wb - 128]
        news[g][:, wb - 128:wb] = jnp.where(lane_t >= 128 - DEC_SEQ, new_cols, rolled[:, wb - 128:wb])
    mm = jnp.maximum(jnp.maximum(lses[0], lses[1]), lses[2])
    es = [jnp.exp(l - mm) for l in lses]
    den = es[0] + es[1] + es[2]
    y = (es[0] / den) * outs[0] + (es[1] / den) * outs[1] + (es[2] / den) * outs[2]
    y = jnp.where(own, y, 0.0)
    yc_ref[...] = y[0:8, :] + y[8:16, :] + y[16:24, :] + y[24:32, :]


def _attn_sample(q_rep, kvn, caches, bias_c, bias_n):
    NBt = q_rep.shape[0]
    per_b = lambda r, c: pl.BlockSpec((None, r, c), lambda b: (b, 0, 0))
    wbs = [w for w, _ in C_PAIRS]
    return pl.pallas_call(
        _attn_sample_body,
        grid=(NBt,),
        in_specs=[per_b(SROWS, 768), per_b(8, 1536)] + [per_b(512, w) for w in wbs]
                 + [_const_spec((SROWS, w)) for w in wbs] + [_const_spec((3, SROWS, 128))],
        out_specs=[per_b(8, C_GW)] + [per_b(512, w) for w in wbs],
        out_shape=[jax.ShapeDtypeStruct((NBt, 8, C_GW), F32)]
                  + [jax.ShapeDtypeStruct((NBt, 512, w), F32) for w in wbs],
        compiler_params=_cparams(1),
        name="attn_sample",
    )(q_rep, kvn, *caches, *bias_c, bias_n)


def _cd_out_body(x_ref, mod_ref, ng_ref, yc_ref, yd_ref, wout_ref, x1_ref):
    x = x_ref[...]
    g1 = mod_ref[2]
    y = (jnp.dot(yc_ref[...].astype(BF16), wout_ref[0:C_GW, :], preferred_element_type=F32)
         + jnp.dot(yd_ref[...], wout_ref[C_GW:, :], preferred_element_type=F32))
    x1_ref[...] = x + g1 * _rms(y, ng_ref[1:2, :])


def _cd_out(x, mod, ng, yc, yd, w_out, *, TM):
    NB, R, _ = x.shape
    NT = R // TM
    MR = mod.shape[2]
    tile = lambda c: pl.BlockSpec((None, TM, c), lambda n, j: (n, j, 0))
    return pl.pallas_call(
        _cd_out_body,
        grid=(NB, NT),
        in_specs=[tile(D_MODEL),
                  pl.BlockSpec((None, 6, MR, D_MODEL), lambda n, j: (n, 0, 0, 0)),
                  _const_spec((4, D_MODEL)),
                  tile(C_GW), tile(D_WIDTH), _const_spec(w_out.shape)],
        out_specs=tile(D_MODEL),
        out_shape=jax.ShapeDtypeStruct((NB, R, D_MODEL), F32),
        compiler_params=_cparams(2),
        name="cd_out",
    )(x, mod, ng, yc, yd, w_out)


def _t5_bucket(dist):
    dist = np.asarray(dist)
    max_exact = N_BUCKETS // 2
    large = max_exact + (np.log(np.maximum(dist, max_exact) / max_exact) / np.log(MAX_DISTANCE / max_exact)
                         * (N_BUCKETS - max_exact)).astype(np.int32)
    large = np.minimum(large, N_BUCKETS - 1)
    return np.where(dist < max_exact, dist, large).astype(np.int32)


def _group_bias(rel_bias, g, dil):
    buckets = _t5_bucket(dil * np.arange(C_TAPS + 1))
    return rel_bias[buckets][:, g * C_HPG:(g + 1) * C_HPG].T


def _toeplitz_body(c_ref, o_ref):
    keep = lax.broadcasted_iota(jnp.int32, (Q_BLOCK, 2 * Q_BLOCK), 1) >= Q_BLOCK
    for h in range(C_HPG):
        taps = jnp.broadcast_to(c_ref[h:h + 1, :], (Q_BLOCK, 2 * Q_BLOCK))
        t = pltpu.roll(taps, 0, 1, stride=1, stride_axis=0)
        o_ref[1, h * Q_BLOCK:(h + 1) * Q_BLOCK, :] = t
        o_ref[0, h * Q_BLOCK:(h + 1) * Q_BLOCK, :] = jnp.where(keep, t, NEG)


def _prompt_bias(bias_gs):
    c = jnp.stack([jnp.concatenate([b[:, ::-1].astype(F32), jnp.full((C_HPG, Q_BLOCK - 1), NEG, F32)], axis=1)
                   for b in bias_gs])
    return pl.pallas_call(
        _toeplitz_body,
        grid=(len(bias_gs),),
        in_specs=[pl.BlockSpec((None, C_HPG, 2 * Q_BLOCK), lambda g: (g, 0, 0))],
        out_specs=pl.BlockSpec((None, 2, C_HPG * Q_BLOCK, 2 * Q_BLOCK), lambda g: (g, 0, 0, 0)),
        out_shape=jax.ShapeDtypeStruct((len(bias_gs), 2, C_HPG * Q_BLOCK, 2 * Q_BLOCK), F32),
        compiler_params=_cparams(1),
        name="attn_bias",
    )(c)


def _sample_bias(bias_g, wb, dil):
    z = jnp.stack([bias_g.astype(F32)] + [jnp.full(bias_g.shape, NEG, F32)] * (dil - 1), axis=-1)
    z = z.reshape(C_HPG, (C_TAPS + 1) * dil)
    n = wb + DEC_SEQ
    z = z[:, :n] if z.shape[1] >= n else jnp.pad(z, ((0, 0), (0, n - z.shape[1])), constant_values=NEG)
    fl = z[:, ::-1]
    bc = jnp.stack([fl[:, DEC_SEQ - 1 - t:DEC_SEQ - 1 - t + wb] for t in range(DEC_SEQ)], axis=1)
    bc = jnp.pad(bc, ((0, 0), (0, 8 - DEC_SEQ), (0, 0))).reshape(SROWS, wb)
    bn = jnp.stack([jnp.pad(z[:, :t + 1][:, ::-1], ((0, 0), (0, DEC_SEQ - 1 - t)), constant_values=NEG)
                    for t in range(DEC_SEQ)], axis=1)
    bn = jnp.pad(bn, ((0, 0), (0, 8 - DEC_SEQ), (0, 128 - DEC_SEQ)), constant_values=NEG).reshape(SROWS, 128)
    return bc, bn


def _time_major(s):
    b, k, c = s.shape
    return s.transpose(1, 0, 2).reshape(1, k * b, c)


def _batch_major(s, b):
    _, r, c = s.shape
    return s.reshape(r // b, b, c).transpose(1, 0, 2)


def _stack(x, mod, st, w, *, TM, RS, pos0):
    NB = x.shape[0]
    new = {}
    ng = w["norm_g"]
    x, new["a"], new["b"] = _mixer_ab(x, mod[0], ng[0], st["a"], st["b"], w["ab_w_in"], w["a_conv_w"],
                                      w["b_w_grp"], w["b_scale"], w["ab_w_out"], TM=TM, RS=RS, pos0=pos0)
    x, new["f0"] = _conv_ffn(x, mod[0], ng[0], st["f0"], w["ffn_w_up"][0], w["ffn_conv_w"][0],
                             w["ffn_conv_b"][0], w["ffn_w_down"][0], TM=TM, RS=RS)
    tail_rows = min(C_PAIRS[-1][0], x.shape[1])
    prompt = RS == 1
    q0, q1, q2, kv0, kv1, kv2, kvf, yd, new["d"] = _cd_in(
        x, mod[1], ng[1], st["d"], w["cd_w_in"], w["d_conv_w"], w["d_conv_b"], w["d_ln_g"], w["d_ln_b"],
        TM=TM, RS=RS, tail_rows=tail_rows, permute=prompt)
    if prompt:
        yc = _attn_prompt([q0, q1, q2], [kv0, kv1, kv2], _prompt_bias(w["bias_g"]), TM=TM)
        for g, (win, dil) in enumerate(C_PAIRS):
            wb = min(win, x.shape[1])
            kt = kvf[:, tail_rows - wb:, g * C_GW:(g + 1) * C_GW]
            vt = kvf[:, tail_rows - wb:, 768 + g * C_GW:768 + (g + 1) * C_GW]
            new["c%d" % g] = jnp.stack([kt, vt], axis=2).reshape(NB, wb, 2, C_HPG, C_HEAD_DIM)
    else:
        B = RS
        qb = _batch_major(jnp.concatenate([q0, q1, q2], axis=-1), B)
        q_rep = jnp.pad(jnp.broadcast_to(qb[:, None], (B, C_HPG, DEC_SEQ, 768)),
                        ((0, 0), (0, 0), (0, 8 - DEC_SEQ), (0, 0))).reshape(B, SROWS, 768)
        kvn = jnp.pad(_batch_major(kvf, B), ((0, 0), (8 - DEC_SEQ, 0), (0, 0)))
        bias = [_sample_bias(w["bias_g"][g], win, dil) for g, (win, dil) in enumerate(C_PAIRS)]
        yc, n0, n1, n2 = _attn_sample(q_rep, kvn, st["c"], [b[0] for b in bias],
                                      jnp.stack([b[1] for b in bias]))
        for g, n in enumerate((n0, n1, n2)):
            new["c%d" % g] = n.reshape(B, 2, C_HPG, C_HEAD_DIM, n.shape[2]).transpose(0, 4, 1, 2, 3)
        yc = _time_major(yc[:, :DEC_SEQ])
    x = _cd_out(x, mod[1], ng[1], yc, yd, w["cd_w_out"], TM=TM)
    x, new["f1"] = _conv_ffn(x, mod[1], ng[1], st["f1"], w["ffn_w_up"][1], w["ffn_conv_w"][1],
                             w["ffn_conv_b"][1], w["ffn_w_down"][1], TM=TM, RS=RS)
    return x, new


def kernel(x_prompt, x_sample, state_a_conv, state_b_pool, cache_c_win128, cache_c_win512, cache_c_win2048,
           state_d_conv, state_ffn_conv, c_prompt, c_sample, ada_w, ada_b, norm_g, rel_bias, ab_w_in, a_conv_w,
           b_w_grp, b_scale, ab_w_out, cd_w_in, d_conv_w, d_conv_b, d_ln_g, d_ln_b, cd_w_out, ffn_w_up,
           ffn_conv_w, ffn_conv_b, ffn_w_down):
    B, T = DEC_BATCH, DEC_SEQ
    w = dict(norm_g=norm_g,
             ab_w_in=ab_w_in[0].astype(BF16), a_conv_w=a_conv_w[0], b_w_grp=b_w_grp[0].astype(BF16),
             b_scale=b_scale, ab_w_out=ab_w_out[0].astype(BF16),
             cd_w_in=cd_w_in[0].astype(BF16),
             d_conv_w=jnp.broadcast_to(d_conv_w[0][:, None, :], (D_CONV, SUBLANES, D_WIDTH)),
             d_conv_b=d_conv_b, d_ln_g=d_ln_g,
             d_ln_b=d_ln_b, cd_w_out=cd_w_out[0].astype(BF16),
             ffn_w_up=ffn_w_up.astype(BF16), ffn_conv_w=ffn_conv_w, ffn_conv_b=ffn_conv_b[:, None, :],
             ffn_w_down=ffn_w_down.astype(BF16),
             bias_g=[_group_bias(rel_bias, g, dil) for g, (_, dil) in enumerate(C_PAIRS)])

    mod = _ada(jnp.concatenate([c_prompt, c_sample], axis=0), ada_w, ada_b)
    mod_p = mod[:, :BATCH].reshape(DEPTH, BATCH, 6, 1, D_MODEL)
    mod_s = mod[:, BATCH:].reshape(DEPTH, B, 6, D_MODEL).transpose(0, 2, 1, 3)
    mod_s = jnp.broadcast_to(mod_s[:, :, None], (DEPTH, 6, T, B, D_MODEL)).reshape(DEPTH, 1, 6, T * B, D_MODEL)

    zeros = lambda k, c: jnp.zeros((BATCH, k, c), F32)
    st_p = dict(a=zeros(A_CONV - 1, A_WIDTH), b=zeros(B_PREV, B_WIDTH), d=zeros(D_CONV - 1, D_WIDTH),
                f0=zeros(FFN_CONV - 1, 2 * D_FF), f1=zeros(FFN_CONV - 1, 2 * D_FF))
    y_p, np_ = _stack(x_prompt, mod_p, st_p, w, TM=TM_PROMPT, RS=1, pos0=0)

    st_s = dict(a=_time_major(state_a_conv[0]), b=_time_major(state_b_pool[0]), d=_time_major(state_d_conv[0]),
                f0=_time_major(state_ffn_conv[0]), f1=_time_major(state_ffn_conv[1]),
                c=[c[0].transpose(0, 2, 3, 4, 1).reshape(B, 512, c.shape[2])
                   for c in (cache_c_win128, cache_c_win512, cache_c_win2048)])
    y_s, ns = _stack(_time_major(x_sample), mod_s, st_s, w, TM=T * B, RS=B, pos0=PAST_LEN)

    bm = lambda s: _batch_major(s, B)
    return (y_p, bm(y_s),
            np_["a"][None], bm(ns["a"])[None], np_["b"][None], bm(ns["b"])[None],
            np_["c0"][None], ns["c0"][None], np_["c1"][None], ns["c1"][None], np_["c2"][None], ns["c2"][None],
            np_["d"][None], bm(ns["d"])[None],
            jnp.stack([np_["f0"], np_["f1"]]), jnp.stack([bm(ns["f0"]), bm(ns["f1"])]))
```

```python
import functools
import math

import numpy as np
import jax
import jax.numpy as jnp
from jax import lax
from jax.experimental import pallas as pl
from jax.experimental.pallas import tpu as pltpu

D_MODEL = 1024
BATCH = 4
SEQ = 4096
DEPTH = 2
DEC_BATCH = 32
DEC_SEQ = 4
PAST_LEN = 8192
EPS = 1e-6
A_WIDTH = 512
A_CONV = 3
B_WIDTH = 512
B_WINDOWS = (2, 4, 8, 16)
B_GROUP = 128
B_PREV = 15
C_PAIRS = ((128, 1), (512, 4), (2048, 16))
C_HPG = 4
C_HEAD_DIM = 64
C_HEADS = 12
C_QKV = 2304
C_GW = C_HPG * C_HEAD_DIM
C_TAPS = 128
ATTN_SCALE = C_HEAD_DIM ** -0.5
Q_BLOCK = 128
N_BUCKETS = 32
MAX_DISTANCE = 2048
D_WIDTH = 512
D_CONV = 31
D_FF = 2816
FFN_CONV = 3

SUBLANES = 8
LANES = 128
VMEM_LIMIT = 56 * 1024 * 1024
NEG = -1e30
TM_PROMPT = 512
TM_FFN_PROMPT = 1024
ATT_ROWS = 2048
FF_CHUNK = 256
D_ROWS = 32

F32 = jnp.float32
BF16 = jnp.bfloat16


def _ru8(n):
    return -(-n // SUBLANES) * SUBLANES


def _pool_levels(rs):
    l1 = _ru8(rs)
    l2 = _ru8(l1 + 2 * rs)
    l3 = _ru8(l2 + 4 * rs)
    l4 = _ru8(l3 + 8 * rs)
    return l1, l2, l3, l4


def _slab_put(ref, r0, val):
    n = val.shape[0]
    for s in range(val.shape[1] // LANES):
        ref[s, r0:r0 + n, :] = val[:, s * LANES:(s + 1) * LANES]


def _slab_get(ref, r0, n, s0=0, ns=None):
    ns = ref.shape[0] - s0 if ns is None else ns
    return jnp.concatenate([ref[s, r0:r0 + n, :] for s in range(s0, s0 + ns)], axis=-1)


def _rms(x, g):
    return x * lax.rsqrt(jnp.mean(x * x, axis=-1, keepdims=True) + EPS) * g


def _sigmoid(x):
    return 1.0 / (1.0 + jnp.exp(-x))


def _cparams(n_axes):
    return pltpu.CompilerParams(dimension_semantics=("arbitrary",) * n_axes, vmem_limit_bytes=VMEM_LIMIT)


def _const_spec(shape):
    nd = len(shape)
    return pl.BlockSpec(shape, lambda *_: (0,) * nd, pipeline_mode=pl.Buffered(1))


def _ada_body(c_ref, w_ref, b_ref, o_ref):
    c = c_ref[...]
    ca = c * _sigmoid(c)
    o_ref[...] = jnp.dot(ca.astype(BF16), w_ref[...].astype(BF16), preferred_element_type=F32) + b_ref[...]


def _ada(c_all, ada_w, ada_b):
    rows = c_all.shape[0]
    tn = 1536
    return pl.pallas_call(
        _ada_body,
        grid=(DEPTH, 6 * D_MODEL // tn),
        in_specs=[pl.BlockSpec((rows, D_MODEL), lambda l, n: (0, 0)),
                  pl.BlockSpec((None, D_MODEL, tn), lambda l, n: (l, 0, n)),
                  pl.BlockSpec((None, 1, tn), lambda l, n: (l, 0, n))],
        out_specs=pl.BlockSpec((None, rows, tn), lambda l, n: (l, 0, n)),
        out_shape=jax.ShapeDtypeStruct((DEPTH, rows, 6 * D_MODEL), F32),
        compiler_params=_cparams(2),
        name="ada",
    )(c_all, ada_w, ada_b.reshape(DEPTH, 1, 6 * D_MODEL))


def _ab_body(x_ref, mod_ref, ng_ref, ap_ref, bp_ref, win_ref, aw_ref, bw_ref, bs_ref, wout_ref,
             x1_ref, an_ref, bn_ref, ea, eb, s2, s4, s8, *, TM, RS, NT, pos0):
    j = pl.program_id(1)
    HA = _ru8((A_CONV - 1) * RS)
    L1, L2, L3, HB = _pool_levels(RS)
    E = HB + TM
    na, nbp = (A_CONV - 1) * RS, B_PREV * RS

    @pl.when(j == 0)
    def _():
        _slab_put(ea, HA - na, ap_ref[...])
        if HB > nbp:
            _slab_put(eb, 0, jnp.zeros((HB - nbp, B_WIDTH), F32))
        _slab_put(eb, HB - nbp, bp_ref[...])

    x = x_ref[...]
    sh1, sc1, g1 = mod_ref[0], mod_ref[1], mod_ref[2]
    h = _rms(x, ng_ref[0:1, :]) * (1.0 + sc1) + sh1
    proj = jnp.dot(h.astype(BF16), win_ref[...], preferred_element_type=F32)
    hh, bg = proj[:, 0:A_WIDTH], proj[:, A_WIDTH:2 * A_WIDTH]
    cg, u = proj[:, 2 * A_WIDTH:3 * A_WIDTH], proj[:, 3 * A_WIDTH:]

    v = cg * hh
    _slab_put(ea, HA, v)
    z = (aw_ref[2:3, :] * v + aw_ref[1:2, :] * _slab_get(ea, HA - RS, TM)
         + aw_ref[0:1, :] * _slab_get(ea, HA - 2 * RS, TM))
    ya = bg * z
    a_last = _slab_get(ea, HA + TM - na, na)
    an_ref[...] = a_last
    if NT > 1:
        _slab_put(ea, HA - na, a_last)

    _slab_put(eb, HB, u)
    _slab_put(s2, L1, _slab_get(eb, L1, E - L1) + _slab_get(eb, L1 - RS, E - L1))
    _slab_put(s4, L2, _slab_get(s2, L2, E - L2, 1) + _slab_get(s2, L2 - 2 * RS, E - L2, 1))
    _slab_put(s8, L3, _slab_get(s4, L3, E - L3, 1) + _slab_get(s4, L3 - 4 * RS, E - L3, 1))
    wsum = (s2[0, HB:E, :], s4[0, HB:E, :], s8[0, HB:E, :], s8[1, HB:E, :] + s8[1, HB - 8 * RS:E - 8 * RS, :])
    b_last = _slab_get(eb, E - nbp, nbp)
    bn_ref[...] = b_last
    if NT > 1:
        _slab_put(eb, HB - nbp, b_last)
    row = lax.broadcasted_iota(jnp.int32, (TM, B_GROUP), 0) + j * TM
    pos1 = lax.shift_right_logical(row, int(math.log2(RS))) + (pos0 + 1)
    ybs = []
    for g, win in enumerate(B_WINDOWS):
        cnt = jnp.minimum(pos1, win).astype(F32)
        pooled = wsum[g] / cnt - u[:, g * B_GROUP:(g + 1) * B_GROUP]
        ybs.append(jnp.dot(pooled.astype(BF16), bw_ref[g], preferred_element_type=F32))
    yb = jnp.concatenate(ybs, axis=-1) * bs_ref[...]

    ycat = jnp.concatenate([ya, yb], axis=-1).astype(BF16)
    y = jnp.dot(ycat, wout_ref[...], preferred_element_type=F32)
    x1_ref[...] = x + g1 * _rms(y, ng_ref[1:2, :])


def _mixer_ab(x, mod, ng, a_prev, b_prev, w_in, a_w, b_w, b_scale, w_out, *, TM, RS, pos0):
    NB, R, _ = x.shape
    NT = R // TM
    MR = mod.shape[2]
    _, _, _, HB = _pool_levels(RS)
    HA = _ru8((A_CONV - 1) * RS)
    E = HB + TM
    na, nbp = (A_CONV - 1) * RS, B_PREV * RS
    tile = lambda c: pl.BlockSpec((None, TM, c), lambda n, j: (n, j, 0))
    per_n = lambda r, c: pl.BlockSpec((None, r, c), lambda n, j: (n, 0, 0))
    return pl.pallas_call(
        functools.partial(_ab_body, TM=TM, RS=RS, NT=NT, pos0=pos0),
        grid=(NB, NT),
        in_specs=[tile(D_MODEL),
                  pl.BlockSpec((None, 6, MR, D_MODEL), lambda n, j: (n, 0, 0, 0)),
                  _const_spec((4, D_MODEL)),
                  per_n(na, A_WIDTH), per_n(nbp, B_WIDTH),
                  _const_spec(w_in.shape), _const_spec(a_w.shape), _const_spec(b_w.shape),
                  _const_spec(b_scale.shape), _const_spec(w_out.shape)],
        out_specs=[tile(D_MODEL), per_n(na, A_WIDTH), per_n(nbp, B_WIDTH)],
        out_shape=[jax.ShapeDtypeStruct((NB, R, D_MODEL), F32),
                   jax.ShapeDtypeStruct((NB, na, A_WIDTH), F32),
                   jax.ShapeDtypeStruct((NB, nbp, B_WIDTH), F32)],
        scratch_shapes=[pltpu.VMEM((4, HA + TM, LANES), F32), pltpu.VMEM((4, E, LANES), F32),
                        pltpu.VMEM((4, E, LANES), F32), pltpu.VMEM((3, E, LANES), F32),
                        pltpu.VMEM((2, E, LANES), F32)],
        compiler_params=_cparams(2),
        name="mixer_ab",
    )(x, mod, ng, a_prev, b_prev, w_in, a_w, b_w, b_scale, w_out)


def _ffn_body(x_ref, mod_ref, ng_ref, fp_ref, wup_ref, cw_ref, cb_ref, wdn_ref,
              y_ref, fn_ref, halo, ext, act, *, TM, RS, NT):
    j = pl.program_id(1)
    nf = (FFN_CONV - 1) * RS
    HF = _ru8(nf)

    @pl.when(j == 0)
    def _():
        halo[...] = fp_ref[...]

    x = x_ref[...]
    sh2, sc2, g2 = mod_ref[3], mod_ref[4], mod_ref[5]
    hb = (_rms(x, ng_ref[2:3, :]) * (1.0 + sc2) + sh2).astype(BF16)

    def conv_part(col, buf):
        up = jnp.dot(hb, wup_ref[:, col:col + FF_CHUNK], preferred_element_type=F32)
        _slab_put(buf, HF - nf, halo[:, col:col + FF_CHUNK])
        _slab_put(buf, HF, up)
        out = (cw_ref[2:3, col:col + FF_CHUNK] * up
               + cw_ref[1:2, col:col + FF_CHUNK] * _slab_get(buf, HF - RS, TM)
               + cw_ref[0:1, col:col + FF_CHUNK] * _slab_get(buf, HF - 2 * RS, TM)
               + cb_ref[:, col:col + FF_CHUNK])
        halo[:, col:col + FF_CHUNK] = _slab_get(buf, HF + TM - nf, nf)
        return out

    for c in range(D_FF // FF_CHUNK):
        a = conv_part(c * FF_CHUNK, ext.at[2 * (c % 2)])
        g = conv_part(D_FF + c * FF_CHUNK, ext.at[2 * (c % 2) + 1])
        act[:, c * FF_CHUNK:(c + 1) * FF_CHUNK] = (a * (g * _sigmoid(g))).astype(BF16)
    fn_ref[...] = halo[...]
    y = jnp.dot(act[...], wdn_ref[...], preferred_element_type=F32)
    y_ref[...] = x + g2 * _rms(y, ng_ref[3:4, :])


def _conv_ffn(x, mod, ng, f_prev, w_up, conv_w, conv_b, w_down, *, TM, RS):
    NB, R, _ = x.shape
    NT = R // TM
    MR = mod.shape[2]
    nf = (FFN_CONV - 1) * RS
    tile = pl.BlockSpec((None, TM, D_MODEL), lambda n, j: (n, j, 0))
    per_n = pl.BlockSpec((None, nf, 2 * D_FF), lambda n, j: (n, 0, 0))
    return pl.pallas_call(
        functools.partial(_ffn_body, TM=TM, RS=RS, NT=NT),
        grid=(NB, NT),
        in_specs=[tile,
                  pl.BlockSpec((None, 6, MR, D_MODEL), lambda n, j: (n, 0, 0, 0)),
                  _const_spec((4, D_MODEL)),
                  per_n,
                  _const_spec(w_up.shape), _const_spec(conv_w.shape), _const_spec(conv_b.shape),
                  _const_spec(w_down.shape)],
        out_specs=[tile, per_n],
        out_shape=[jax.ShapeDtypeStruct((NB, R, D_MODEL), F32),
                   jax.ShapeDtypeStruct((NB, nf, 2 * D_FF), F32)],
        scratch_shapes=[pltpu.VMEM((nf, 2 * D_FF), F32),
                        pltpu.VMEM((4, FF_CHUNK // LANES, _ru8(nf) + TM, LANES), F32),
                        pltpu.VMEM((TM, D_FF), BF16)],
        compiler_params=_cparams(2),
        name="conv_ffn",
    )(x, mod, ng, f_prev, w_up, conv_w, conv_b, w_down)


def _cd_in_body(*refs, TM, RS, NT, permute):
    if permute:
        (x_ref, mod_ref, ng_ref, dp_ref, win_ref, dw_ref, db_ref, lg_ref, lb_ref, perm_ref,
         q0_ref, q1_ref, q2_ref, kv0_ref, kv1_ref, kv2_ref, kvf_ref, yd_ref, dn_ref, ed) = refs
    else:
        (x_ref, mod_ref, ng_ref, dp_ref, win_ref, dw_ref, db_ref, lg_ref, lb_ref,
         q0_ref, q1_ref, q2_ref, kv0_ref, kv1_ref, kv2_ref, kvf_ref, yd_ref, dn_ref, ed) = refs
    q_refs, kv_refs = (q0_ref, q1_ref, q2_ref), (kv0_ref, kv1_ref, kv2_ref)
    j = pl.program_id(1)
    nd = (D_CONV - 1) * RS
    HD = _ru8(nd)

    @pl.when(j == 0)
    def _():
        _slab_put(ed, HD - nd, dp_ref[...])

    x = x_ref[...]
    sh1, sc1 = mod_ref[0], mod_ref[1]
    h = _rms(x, ng_ref[0:1, :]) * (1.0 + sc1) + sh1
    proj = jnp.dot(h.astype(BF16), win_ref[...], preferred_element_type=F32)
    qs = (proj[:, 0:768] * ATTN_SCALE).astype(BF16)
    ks = proj[:, 768:1536].astype(BF16)
    vs = proj[:, 1536:2304].astype(BF16)
    for g in range(len(C_PAIRS)):
        cols = slice(g * C_GW, (g + 1) * C_GW)
        xg = jnp.concatenate([qs[:, cols], ks[:, cols], vs[:, cols]], axis=-1)
        if permute and C_PAIRS[g][1] > 1:
            xg = jnp.dot(perm_ref[g - 1], xg, preferred_element_type=F32).astype(BF16)
        q_refs[g][...] = xg[:, 0:C_GW]
        kv_refs[g][...] = xg[:, C_GW:]
    kvf_ref[...] = proj[:, 768:2304]
    dv, dg = proj[:, C_QKV:C_QKV + D_WIDTH], proj[:, C_QKV + D_WIDTH:]
    _slab_put(ed, HD, dv * _sigmoid(dg))

    db, lg, lb = db_ref[...], lg_ref[...], lb_ref[...]
    tiles = D_ROWS // SUBLANES
    for c in range(TM // D_ROWS):
        accs = []
        for s in range(D_WIDTH // LANES):
            acc = jnp.zeros((tiles, SUBLANES, LANES), F32)
            for kk in range(D_CONV):
                off = HD - (D_CONV - 1 - kk) * RS + c * D_ROWS
                tap = ed[s, off:off + D_ROWS, :].reshape(tiles, SUBLANES, LANES)
                acc = acc + dw_ref[kk, :, s * LANES:(s + 1) * LANES] * tap
            accs.append(acc.reshape(D_ROWS, LANES))
        zc = jnp.concatenate(accs, axis=-1) + db
        mu = jnp.mean(zc, axis=-1, keepdims=True)
        zc = zc - mu
        var = jnp.mean(zc * zc, axis=-1, keepdims=True)
        yl = zc * lax.rsqrt(var + EPS) * lg + lb
        yd_ref[c * D_ROWS:(c + 1) * D_ROWS, :] = (yl * _sigmoid(yl)).astype(BF16)

    d_last = _slab_get(ed, HD + TM - nd, nd)
    dn_ref[...] = d_last
    if NT > 1:
        _slab_put(ed, HD - nd, d_last)


def _residue_perm(tm, dil):
    p = np.zeros((tm, tm), np.float32)
    a, r = np.meshgrid(np.arange(tm // dil), np.arange(dil), indexing="ij")
    p[(r * (tm // dil) + a).ravel(), (dil * a + r).ravel()] = 1.0
    return p


def _cd_in(x, mod, ng, d_prev, w_in, d_w, d_b, ln_g, ln_b, *, TM, RS, tail_rows, permute):
    NB, R, _ = x.shape
    NT = R // TM
    MR = mod.shape[2]
    nd = (D_CONV - 1) * RS
    n_tail = tail_rows // TM
    perm_in, perm_spec = [], []
    if permute:
        perm_in = [jnp.asarray(np.stack([_residue_perm(TM, dil) for _, dil in C_PAIRS[1:]]), BF16)]
        perm_spec = [_const_spec((len(C_PAIRS) - 1, TM, TM))]
    tile = lambda c: pl.BlockSpec((None, TM, c), lambda n, j: (n, j, 0))
    per_n = pl.BlockSpec((None, nd, D_WIDTH), lambda n, j: (n, 0, 0))
    tail = pl.BlockSpec((None, TM, 1536), lambda n, j: (n, jnp.maximum(j - (NT - n_tail), 0), 0))
    return pl.pallas_call(
        functools.partial(_cd_in_body, TM=TM, RS=RS, NT=NT, permute=permute),
        grid=(NB, NT),
        in_specs=[tile(D_MODEL),
                  pl.BlockSpec((None, 6, MR, D_MODEL), lambda n, j: (n, 0, 0, 0)),
                  _const_spec((4, D_MODEL)),
                  per_n,
                  _const_spec(w_in.shape), _const_spec(d_w.shape), _const_spec(d_b.shape),
                  _const_spec(ln_g.shape), _const_spec(ln_b.shape)] + perm_spec,
        out_specs=[tile(C_GW)] * 3 + [tile(2 * C_GW)] * 3 + [tail, tile(D_WIDTH), per_n],
        out_shape=[jax.ShapeDtypeStruct((NB, R, C_GW), BF16)] * 3
                  + [jax.ShapeDtypeStruct((NB, R, 2 * C_GW), BF16)] * 3
                  + [jax.ShapeDtypeStruct((NB, tail_rows, 1536), F32),
                     jax.ShapeDtypeStruct((NB, R, D_WIDTH), BF16),
                     jax.ShapeDtypeStruct((NB, nd, D_WIDTH), F32)],
        scratch_shapes=[pltpu.VMEM((D_WIDTH // LANES, _ru8(nd) + TM, LANES), F32)],
        compiler_params=_cparams(2),
        name="cd_in",
    )(x, mod, ng, d_prev, w_in, d_w, d_b, ln_g, ln_b, *perm_in)


def _head_of_lane(shape):
    return lax.shift_right_logical(lax.broadcasted_iota(jnp.int32, shape, 1), 6)


def _attn_blocks(blocks):
    lane_head = _head_of_lane((Q_BLOCK, C_GW))
    logits = []
    for q, kvp, kvc, _ in blocks:
        zero = jnp.zeros_like(q)
        qs = jnp.concatenate([jnp.where(lane_head == h, q, zero) for h in range(C_HPG)], axis=0)
        kcat = jnp.concatenate([kvp[:, 0:C_GW], kvc[:, 0:C_GW]], axis=0)
        logits.append(lax.dot_general(qs, kcat, (((1,), (1,)), ((), ())), preferred_element_type=F32))
    outs = []
    for (q, kvp, kvc, bias), lg in zip(blocks, logits):
        vcat = jnp.concatenate([kvp[:, C_GW:], kvc[:, C_GW:]], axis=0)
        lg = lg + bias
        m = jnp.max(lg, axis=-1, keepdims=True)
        p = jnp.exp(lg - m)
        s = jnp.sum(p, axis=-1, keepdims=True)
        pv = jnp.dot(p.astype(BF16), vcat, preferred_element_type=F32) * (1.0 / s)
        lse = m + jnp.log(s)
        o = jnp.zeros((Q_BLOCK, C_GW), F32)
        l = jnp.zeros((Q_BLOCK, C_GW), F32)
        for h in range(C_HPG):
            sel = lane_head == h
            o = jnp.where(sel, pv[h * Q_BLOCK:(h + 1) * Q_BLOCK, :], o)
            l = jnp.where(sel, lse[h * Q_BLOCK:(h + 1) * Q_BLOCK, :], l)
        outs.append((o, l))
    return outs


def _blocks_per_trip(count):
    return max(u for u in (4, 3, 2, 1) if count % u == 0)


def _attn_prompt_body(q0, q1, q2, kv0, kv1, kv2, p0, p1, p2, bias_ref, yc_ref, osc, lsc, *, TM):
    qs_, kvs, prevs = (q0, q1, q2), (kv0, kv1, kv2), (p0, p1, p2)
    var = jnp.minimum(pl.program_id(1), 1)
    n_blocks = ATT_ROWS // Q_BLOCK

    def put(g, start, n, stride, o, l, r0):
        rows = pl.ds(start, n, stride=stride) if stride > 1 else pl.ds(start, n)
        for s in range(C_GW // LANES):
            osc[g, s, rows, :] = o[r0:r0 + n, s * LANES:(s + 1) * LANES]
            lsc[g, s, rows, :] = l[r0:r0 + n, s * LANES:(s + 1) * LANES]

    for g, (_, dil) in enumerate(C_PAIRS):
        q_ref, kv_ref, p_ref = qs_[g], kvs[g], prevs[g]
        cls = TM // dil
        if cls >= Q_BLOCK:
            prev_off = Q_BLOCK if dil == 1 else TM
            n_first = prev_off // Q_BLOCK
            per_tile = TM // Q_BLOCK

            def token_start(idx, dil=dil, per_tile=per_tile):
                return idx * Q_BLOCK if dil == 1 else (idx // per_tile) * TM + idx % per_tile

            first = []
            for idx in range(n_first):
                rows = slice(idx * Q_BLOCK, (idx + 1) * Q_BLOCK)
                first.append((q_ref[rows, :], p_ref[rows, :], kv_ref[rows, :], bias_ref[g, var]))
            for idx, (o, l) in enumerate(_attn_blocks(first)):
                put(g, token_start(idx), Q_BLOCK, dil, o, l, 0)

            per_trip = _blocks_per_trip(n_blocks - n_first)

            def body(trip, carry, g=g, dil=dil, q_ref=q_ref, kv_ref=kv_ref, prev_off=prev_off,
                     token_start=token_start, n_first=n_first, per_trip=per_trip):
                idxs = [n_first + trip * per_trip + u for u in range(per_trip)]
                blocks = []
                for idx in idxs:
                    st = pl.multiple_of(idx * Q_BLOCK, Q_BLOCK)
                    blocks.append((q_ref[pl.ds(st, Q_BLOCK), :], kv_ref[pl.ds(st - prev_off, Q_BLOCK), :],
                                   kv_ref[pl.ds(st, Q_BLOCK), :], bias_ref[g, 1]))
                for idx, (o, l) in zip(idxs, _attn_blocks(blocks)):
                    put(g, token_start(idx), Q_BLOCK, dil, o, l, 0)
                return carry

            lax.fori_loop(0, (n_blocks - n_first) // per_trip, body, 0)
        else:
            tiles = Q_BLOCK // cls
            per_trip = _blocks_per_trip(dil)

            def body(trip, carry, g=g, dil=dil, q_ref=q_ref, kv_ref=kv_ref, p_ref=p_ref, cls=cls, tiles=tiles,
                     per_trip=per_trip):
                rs = [trip * per_trip + u for u in range(per_trip)]
                blocks = []
                for r in rs:
                    st = pl.multiple_of(r * cls, cls)
                    gather = lambda ref, st=st: jnp.concatenate(
                        [ref[pl.ds(c * TM + st, cls), :] for c in range(tiles)], axis=0)
                    blocks.append((gather(q_ref), gather(p_ref), gather(kv_ref), bias_ref[g, var]))
                for r, (o, l) in zip(rs, _attn_blocks(blocks)):
                    for c in range(tiles):
                        put(g, c * TM + r, cls, dil, o, l, c * cls)
                return carry

            lax.fori_loop(0, dil // per_trip, body, 0)

    def merge(ch, carry):
        st = pl.multiple_of(ch * Q_BLOCK, Q_BLOCK)
        get = lambda ref, g: jnp.concatenate([ref[g, s, pl.ds(st, Q_BLOCK), :] for s in range(C_GW // LANES)], axis=-1)
        ls = [get(lsc, g) for g in range(len(C_PAIRS))]
        mm = jnp.maximum(jnp.maximum(ls[0], ls[1]), ls[2])
        es = [jnp.exp(l - mm) for l in ls]
        den = es[0] + es[1] + es[2]
        yc = (es[0] / den) * get(osc, 0) + (es[1] / den) * get(osc, 1) + (es[2] / den) * get(osc, 2)
        yc_ref[pl.ds(st, Q_BLOCK), :] = yc.astype(BF16)
        return carry

    lax.fori_loop(0, n_blocks, merge, 0)


def _attn_prompt(qs, kvs, bias, *, TM):
    N, S, _ = qs[0].shape
    assert S % ATT_ROWS == 0 and ATT_ROWS == Q_BLOCK * C_PAIRS[-1][1] and ATT_ROWS % TM == 0
    cur = lambda c: pl.BlockSpec((None, ATT_ROWS, c), lambda n, i: (n, i, 0))
    prev_rows = [Q_BLOCK if dil == 1 else (TM if TM // dil >= Q_BLOCK else ATT_ROWS) for _, dil in C_PAIRS]
    prev = [pl.BlockSpec((None, pr, 2 * C_GW), lambda n, i, k=ATT_ROWS // pr: (n, jnp.maximum(i * k - 1, 0), 0))
            for pr in prev_rows]
    slabs = C_GW // LANES
    return pl.pallas_call(
        functools.partial(_attn_prompt_body, TM=TM),
        grid=(N, S // ATT_ROWS),
        in_specs=[cur(C_GW)] * 3 + [cur(2 * C_GW)] * 3 + prev + [_const_spec(bias.shape)],
        out_specs=cur(C_GW),
        out_shape=jax.ShapeDtypeStruct((N, S, C_GW), BF16),
        scratch_shapes=[pltpu.VMEM((len(C_PAIRS), slabs, ATT_ROWS, LANES), F32)] * 2,
        compiler_params=_cparams(2),
        name="attn_prompt",
    )(*qs, *kvs, *kvs, bias)


SROWS = 32


def _attn_sample_body(q_ref, kvn_ref, c0_ref, c1_ref, c2_ref, bc0_ref, bc1_ref, bc2_ref, bn_ref,
                      yc_ref, n0_ref, n1_ref, n2_ref):
    caches = (c0_ref, c1_ref, c2_ref)
    bcs = (bc0_ref, bc1_ref, bc2_ref)
    news = (n0_ref, n1_ref, n2_ref)
    lane_head = _head_of_lane((SROWS, C_GW))
    row_head = lax.shift_right_logical(lax.broadcasted_iota(jnp.int32, (SROWS, C_GW), 0), 3)
    own = lane_head == row_head
    lane128 = lax.broadcasted_iota(jnp.int32, (SROWS, 128), 1)
    kvn = kvn_ref[...]
    r0 = 8 - DEC_SEQ
    outs, lses = [], []
    for g, (win, dil) in enumerate(C_PAIRS):
        wb = win
        qg = q_ref[:, g * C_GW:(g + 1) * C_GW]
        qs = jnp.where(own, qg, jnp.zeros_like(qg))
        cache = caches[g]
        kt = cache[0:C_GW, :].astype(BF16)
        vt = cache[C_GW:2 * C_GW, :].astype(BF16)
        kn = kvn[:, g * C_GW:(g + 1) * C_GW]
        vn = kvn[:, 768 + g * C_GW:768 + (g + 1) * C_GW]
        lc = jnp.dot(qs, kt, preferred_element_type=F32) + bcs[g][...]
        qf = qs.astype(F32)
        ln = bn_ref[g]
        for c in range(DEC_SEQ):
            d = jnp.sum(qf * kn[r0 + c:r0 + c + 1, :], axis=-1, keepdims=True)
            ln = ln + jnp.where(lane128 == c, d, 0.0)
        m = jnp.maximum(jnp.max(lc, axis=-1, keepdims=True), jnp.max(ln, axis=-1, keepdims=True))
        pc = jnp.exp(lc - m)
        pn = jnp.exp(ln - m)
        s = jnp.sum(pc, axis=-1, keepdims=True) + jnp.sum(pn, axis=-1, keepdims=True)
        pv = lax.dot_general(pc.astype(BF16), vt, (((1,), (1,)), ((), ())), preferred_element_type=F32)
        for c in range(DEC_SEQ):
            pcol = jnp.sum(jnp.where(lane128 == c, pn, 0.0), axis=-1, keepdims=True)
            pv = pv + pcol * vn[r0 + c:r0 + c + 1, :]
        outs.append(pv / s)
        lses.append(m + jnp.log(s))
        rolled = pltpu.roll(cache[...], wb - DEC_SEQ, 1)
        new_rows = jnp.concatenate([jnp.zeros((128 - 8, 2 * C_GW), F32), jnp.concatenate([kn, vn], axis=-1)], axis=0)
        new_cols = new_rows.T
        lane_t = lax.broadcasted_iota(jnp.int32, (2 * C_GW, 128), 1)
        if wb > 128:
            news[g][:, 0:wb - 128] = rolled[:, 0:wb - 128]
        news[g][:, wb - 128:wb] = jnp.where(lane_t >= 128 - DEC_SEQ, new_cols, rolled[:, wb - 128:wb])
    mm = jnp.maximum(jnp.maximum(lses[0], lses[1]), lses[2])
    es = [jnp.exp(l - mm) for l in lses]
    den = es[0] + es[1] + es[2]
    y = (es[0] / den) * outs[0] + (es[1] / den) * outs[1] + (es[2] / den) * outs[2]
    y = jnp.where(own, y, 0.0)
    yc_ref[...] = y[0:8, :] + y[8:16, :] + y[16:24, :] + y[24:32, :]


def _attn_sample(q_rep, kvn, caches, bias_c, bias_n):
    NBt = q_rep.shape[0]
    per_b = lambda r, c: pl.BlockSpec((None, r, c), lambda b: (b, 0, 0))
    wbs = [w for w, _ in C_PAIRS]
    return pl.pallas_call(
        _attn_sample_body,
        grid=(NBt,),
        in_specs=[per_b(SROWS, 768), per_b(8, 1536)] + [per_b(512, w) for w in wbs]
                 + [_const_spec((SROWS, w)) for w in wbs] + [_const_spec((3, SROWS, 128))],
        out_specs=[per_b(8, C_GW)] + [per_b(512, w) for w in wbs],
        out_shape=[jax.ShapeDtypeStruct((NBt, 8, C_GW), F32)]
                  + [jax.ShapeDtypeStruct((NBt, 512, w), F32) for w in wbs],
        compiler_params=_cparams(1),
        name="attn_sample",
    )(q_rep, kvn, *caches, *bias_c, bias_n)


def _cd_out_body(x_ref, mod_ref, ng_ref, yc_ref, yd_ref, wout_ref, x1_ref):
    x = x_ref[...]
    g1 = mod_ref[2]
    y = (jnp.dot(yc_ref[...].astype(BF16), wout_ref[0:C_GW, :], preferred_element_type=F32)
         + jnp.dot(yd_ref[...], wout_ref[C_GW:, :], preferred_element_type=F32))
    x1_ref[...] = x + g1 * _rms(y, ng_ref[1:2, :])


def _cd_out(x, mod, ng, yc, yd, w_out, *, TM):
    NB, R, _ = x.shape
    NT = R // TM
    MR = mod.shape[2]
    tile = lambda c: pl.BlockSpec((None, TM, c), lambda n, j: (n, j, 0))
    return pl.pallas_call(
        _cd_out_body,
        grid=(NB, NT),
        in_specs=[tile(D_MODEL),
                  pl.BlockSpec((None, 6, MR, D_MODEL), lambda n, j: (n, 0, 0, 0)),
                  _const_spec((4, D_MODEL)),
                  tile(C_GW), tile(D_WIDTH), _const_spec(w_out.shape)],
        out_specs=tile(D_MODEL),
        out_shape=jax.ShapeDtypeStruct((NB, R, D_MODEL), F32),
        compiler_params=_cparams(2),
        name="cd_out",
    )(x, mod, ng, yc, yd, w_out)


def _t5_bucket(dist):
    dist = np.asarray(dist)
    max_exact = N_BUCKETS // 2
    large = max_exact + (np.log(np.maximum(dist, max_exact) / max_exact) / np.log(MAX_DISTANCE / max_exact)
                         * (N_BUCKETS - max_exact)).astype(np.int32)
    large = np.minimum(large, N_BUCKETS - 1)
    return np.where(dist < max_exact, dist, large).astype(np.int32)


def _group_bias(rel_bias, g, dil):
    buckets = _t5_bucket(dil * np.arange(C_TAPS + 1))
    return rel_bias[buckets][:, g * C_HPG:(g + 1) * C_HPG].T


def _toeplitz_body(c_ref, o_ref):
    keep = lax.broadcasted_iota(jnp.int32, (Q_BLOCK, 2 * Q_BLOCK), 1) >= Q_BLOCK
    for h in range(C_HPG):
        taps = jnp.broadcast_to(c_ref[h:h + 1, :], (Q_BLOCK, 2 * Q_BLOCK))
        t = pltpu.roll(taps, 0, 1, stride=1, stride_axis=0)
        o_ref[1, h * Q_BLOCK:(h + 1) * Q_BLOCK, :] = t
        o_ref[0, h * Q_BLOCK:(h + 1) * Q_BLOCK, :] = jnp.where(keep, t, NEG)


def _prompt_bias(bias_gs):
    c = jnp.stack([jnp.concatenate([b[:, ::-1].astype(F32), jnp.full((C_HPG, Q_BLOCK - 1), NEG, F32)], axis=1)
                   for b in bias_gs])
    return pl.pallas_call(
        _toeplitz_body,
        grid=(len(bias_gs),),
        in_specs=[pl.BlockSpec((None, C_HPG, 2 * Q_BLOCK), lambda g: (g, 0, 0))],
        out_specs=pl.BlockSpec((None, 2, C_HPG * Q_BLOCK, 2 * Q_BLOCK), lambda g: (g, 0, 0, 0)),
        out_shape=jax.ShapeDtypeStruct((len(bias_gs), 2, C_HPG * Q_BLOCK, 2 * Q_BLOCK), F32),
        compiler_params=_cparams(1),
        name="attn_bias",
    )(c)


def _sample_bias(bias_g, wb, dil):
    z = jnp.stack([bias_g.astype(F32)] + [jnp.full(bias_g.shape, NEG, F32)] * (dil - 1), axis=-1)
    z = z.reshape(C_HPG, (C_TAPS + 1) * dil)
    n = wb + DEC_SEQ
    z = z[:, :n] if z.shape[1] >= n else jnp.pad(z, ((0, 0), (0, n - z.shape[1])), constant_values=NEG)
    fl = z[:, ::-1]
    bc = jnp.stack([fl[:, DEC_SEQ - 1 - t:DEC_SEQ - 1 - t + wb] for t in range(DEC_SEQ)], axis=1)
    bc = jnp.pad(bc, ((0, 0), (0, 8 - DEC_SEQ), (0, 0))).reshape(SROWS, wb)
    bn = jnp.stack([jnp.pad(z[:, :t + 1][:, ::-1], ((0, 0), (0, DEC_SEQ - 1 - t)), constant_values=NEG)
                    for t in range(DEC_SEQ)], axis=1)
    bn = jnp.pad(bn, ((0, 0), (0, 8 - DEC_SEQ), (0, 128 - DEC_SEQ)), constant_values=NEG).reshape(SROWS, 128)
    return bc, bn


def _time_major(s):
    b, k, c = s.shape
    return s.transpose(1, 0, 2).reshape(1, k * b, c)


def _batch_major(s, b):
    _, r, c = s.shape
    return s.reshape(r // b, b, c).transpose(1, 0, 2)


def _stack(x, mod, st, w, *, TM, TM_FFN, RS, pos0):
    NB = x.shape[0]
    new = {}
    ng = w["norm_g"]
    x, new["a"], new["b"] = _mixer_ab(x, mod[0], ng[0], st["a"], st["b"], w["ab_w_in"], w["a_conv_w"],
                                      w["b_w_grp"], w["b_scale"], w["ab_w_out"], TM=TM, RS=RS, pos0=pos0)
    x, new["f0"] = _conv_ffn(x, mod[0], ng[0], st["f0"], w["ffn_w_up"][0], w["ffn_conv_w"][0],
                             w["ffn_conv_b"][0], w["ffn_w_down"][0], TM=TM_FFN, RS=RS)
    tail_rows = min(C_PAIRS[-1][0], x.shape[1])
    prompt = RS == 1
    q0, q1, q2, kv0, kv1, kv2, kvf, yd, new["d"] = _cd_in(
        x, mod[1], ng[1], st["d"], w["cd_w_in"], w["d_conv_w"], w["d_conv_b"], w["d_ln_g"], w["d_ln_b"],
        TM=TM, RS=RS, tail_rows=tail_rows, permute=prompt)
    if prompt:
        yc = _attn_prompt([q0, q1, q2], [kv0, kv1, kv2], _prompt_bias(w["bias_g"]), TM=TM)
        for g, (win, dil) in enumerate(C_PAIRS):
            wb = min(win, x.shape[1])
            kt = kvf[:, tail_rows - wb:, g * C_GW:(g + 1) * C_GW]
            vt = kvf[:, tail_rows - wb:, 768 + g * C_GW:768 + (g + 1) * C_GW]
            new["c%d" % g] = jnp.stack([kt, vt], axis=2).reshape(NB, wb, 2, C_HPG, C_HEAD_DIM)
    else:
        B = RS
        qb = _batch_major(jnp.concatenate([q0, q1, q2], axis=-1), B)
        q_rep = jnp.pad(jnp.broadcast_to(qb[:, None], (B, C_HPG, DEC_SEQ, 768)),
                        ((0, 0), (0, 0), (0, 8 - DEC_SEQ), (0, 0))).reshape(B, SROWS, 768)
        kvn = jnp.pad(_batch_major(kvf, B), ((0, 0), (8 - DEC_SEQ, 0), (0, 0)))
        bias = [_sample_bias(w["bias_g"][g], win, dil) for g, (win, dil) in enumerate(C_PAIRS)]
        yc, n0, n1, n2 = _attn_sample(q_rep, kvn, st["c"], [b[0] for b in bias],
                                      jnp.stack([b[1] for b in bias]))
        for g, n in enumerate((n0, n1, n2)):
            new["c%d" % g] = n.reshape(B, 2, C_HPG, C_HEAD_DIM, n.shape[2]).transpose(0, 4, 1, 2, 3)
        yc = _time_major(yc[:, :DEC_SEQ])
    x = _cd_out(x, mod[1], ng[1], yc, yd, w["cd_w_out"], TM=TM)
    x, new["f1"] = _conv_ffn(x, mod[1], ng[1], st["f1"], w["ffn_w_up"][1], w["ffn_conv_w"][1],
                             w["ffn_conv_b"][1], w["ffn_w_down"][1], TM=TM_FFN, RS=RS)
    return x, new


def kernel(x_prompt, x_sample, state_a_conv, state_b_pool, cache_c_win128, cache_c_win512, cache_c_win2048,
           state_d_conv, state_ffn_conv, c_prompt, c_sample, ada_w, ada_b, norm_g, rel_bias, ab_w_in, a_conv_w,
           b_w_grp, b_scale, ab_w_out, cd_w_in, d_conv_w, d_conv_b, d_ln_g, d_ln_b, cd_w_out, ffn_w_up,
           ffn_conv_w, ffn_conv_b, ffn_w_down):
    B, T = DEC_BATCH, DEC_SEQ
    w = dict(norm_g=norm_g,
             ab_w_in=ab_w_in[0].astype(BF16), a_conv_w=a_conv_w[0], b_w_grp=b_w_grp[0].astype(BF16),
             b_scale=b_scale, ab_w_out=ab_w_out[0].astype(BF16),
             cd_w_in=cd_w_in[0].astype(BF16),
             d_conv_w=jnp.broadcast_to(d_conv_w[0][:, None, :], (D_CONV, SUBLANES, D_WIDTH)),
             d_conv_b=d_conv_b, d_ln_g=d_ln_g,
             d_ln_b=d_ln_b, cd_w_out=cd_w_out[0].astype(BF16),
             ffn_w_up=ffn_w_up.astype(BF16), ffn_conv_w=ffn_conv_w, ffn_conv_b=ffn_conv_b[:, None, :],
             ffn_w_down=ffn_w_down.astype(BF16),
             bias_g=[_group_bias(rel_bias, g, dil) for g, (_, dil) in enumerate(C_PAIRS)])

    mod = _ada(jnp.concatenate([c_prompt, c_sample], axis=0), ada_w, ada_b)
    mod_p = mod[:, :BATCH].reshape(DEPTH, BATCH, 6, 1, D_MODEL)
    mod_s = mod[:, BATCH:].reshape(DEPTH, B, 6, D_MODEL).transpose(0, 2, 1, 3)
    mod_s = jnp.broadcast_to(mod_s[:, :, None], (DEPTH, 6, T, B, D_MODEL)).reshape(DEPTH, 1, 6, T * B, D_MODEL)

    zeros = lambda k, c: jnp.zeros((BATCH, k, c), F32)
    st_p = dict(a=zeros(A_CONV - 1, A_WIDTH), b=zeros(B_PREV, B_WIDTH), d=zeros(D_CONV - 1, D_WIDTH),
                f0=zeros(FFN_CONV - 1, 2 * D_FF), f1=zeros(FFN_CONV - 1, 2 * D_FF))
    y_p, np_ = _stack(x_prompt, mod_p, st_p, w, TM=TM_PROMPT, TM_FFN=TM_FFN_PROMPT, RS=1, pos0=0)

    st_s = dict(a=_time_major(state_a_conv[0]), b=_time_major(state_b_pool[0]), d=_time_major(state_d_conv[0]),
                f0=_time_major(state_ffn_conv[0]), f1=_time_major(state_ffn_conv[1]),
                c=[c[0].transpose(0, 2, 3, 4, 1).reshape(B, 512, c.shape[2])
                   for c in (cache_c_win128, cache_c_win512, cache_c_win2048)])
    y_s, ns = _stack(_time_major(x_sample), mod_s, st_s, w, TM=T * B, TM_FFN=T * B, RS=B, pos0=PAST_LEN)

    bm = lambda s: _batch_major(s, B)
    return (y_p, bm(y_s),
            np_["a"][None], bm(ns["a"])[None], np_["b"][None], bm(ns["b"])[None],
            np_["c0"][None], ns["c0"][None], np_["c1"][None], ns["c1"][None], np_["c2"][None], ns["c2"][None],
            np_["d"][None], bm(ns["d"])[None],
            jnp.stack([np_["f0"], np_["f1"]]), jnp.stack([bm(ns["f0"]), bm(ns["f1"])]))
```

```python
import functools
import math

import numpy as np
import jax
import jax.numpy as jnp
from jax import lax
from jax.experimental import pallas as pl
from jax.experimental.pallas import tpu as pltpu

D_MODEL = 1024
BATCH = 4
SEQ = 4096
DEPTH = 2
DEC_BATCH = 32
DEC_SEQ = 4
PAST_LEN = 8192
EPS = 1e-6
A_WIDTH = 512
A_CONV = 3
B_WIDTH = 512
B_WINDOWS = (2, 4, 8, 16)
B_GROUP = 128
B_PREV = 15
C_PAIRS = ((128, 1), (512, 4), (2048, 16))
C_HPG = 4
C_HEAD_DIM = 64
C_HEADS = 12
C_QKV = 2304
C_GW = C_HPG * C_HEAD_DIM
C_TAPS = 128
ATTN_SCALE = C_HEAD_DIM ** -0.5
LOG2E = math.log2(math.e)
Q_BLOCK = 128
N_BUCKETS = 32
MAX_DISTANCE = 2048
D_WIDTH = 512
D_CONV = 31
D_FF = 2816
FFN_CONV = 3

SUBLANES = 8
LANES = 128
VMEM_LIMIT = 56 * 1024 * 1024
NEG = -1e30
TM_PROMPT = 512
TM_FFN_PROMPT = 1024
ATT_ROWS = 2048
FF_CHUNK = 256
D_ROWS = 32

F32 = jnp.float32
BF16 = jnp.bfloat16


def _ru8(n):
    return -(-n // SUBLANES) * SUBLANES


def _pool_levels(rs):
    l1 = _ru8(rs)
    l2 = _ru8(l1 + 2 * rs)
    l3 = _ru8(l2 + 4 * rs)
    l4 = _ru8(l3 + 8 * rs)
    return l1, l2, l3, l4


def _slab_put(ref, r0, val):
    n = val.shape[0]
    for s in range(val.shape[1] // LANES):
        ref[s, r0:r0 + n, :] = val[:, s * LANES:(s + 1) * LANES]


def _slab_get(ref, r0, n, s0=0, ns=None):
    ns = ref.shape[0] - s0 if ns is None else ns
    return jnp.concatenate([ref[s, r0:r0 + n, :] for s in range(s0, s0 + ns)], axis=-1)


def _rms(x, g):
    return x * lax.rsqrt(jnp.mean(x * x, axis=-1, keepdims=True) + EPS) * g


def _sigmoid(x):
    return 1.0 / (1.0 + jnp.exp(-x))


def _cparams(n_axes):
    return pltpu.CompilerParams(dimension_semantics=("arbitrary",) * n_axes, vmem_limit_bytes=VMEM_LIMIT)


def _const_spec(shape):
    nd = len(shape)
    return pl.BlockSpec(shape, lambda *_: (0,) * nd, pipeline_mode=pl.Buffered(1))


def _ada_body(c_ref, w_ref, b_ref, o_ref):
    c = c_ref[...]
    ca = c * _sigmoid(c)
    o_ref[...] = jnp.dot(ca.astype(BF16), w_ref[...].astype(BF16), preferred_element_type=F32) + b_ref[...]


def _ada(c_all, ada_w, ada_b):
    rows = c_all.shape[0]
    tn = 1536
    return pl.pallas_call(
        _ada_body,
        grid=(DEPTH, 6 * D_MODEL // tn),
        in_specs=[pl.BlockSpec((rows, D_MODEL), lambda l, n: (0, 0)),
                  pl.BlockSpec((None, D_MODEL, tn), lambda l, n: (l, 0, n)),
                  pl.BlockSpec((None, 1, tn), lambda l, n: (l, 0, n))],
        out_specs=pl.BlockSpec((None, rows, tn), lambda l, n: (l, 0, n)),
        out_shape=jax.ShapeDtypeStruct((DEPTH, rows, 6 * D_MODEL), F32),
        compiler_params=_cparams(2),
        name="ada",
    )(c_all, ada_w, ada_b.reshape(DEPTH, 1, 6 * D_MODEL))


def _ab_body(x_ref, mod_ref, ng_ref, ap_ref, bp_ref, win_ref, aw_ref, bw_ref, bs_ref, wout_ref,
             x1_ref, an_ref, bn_ref, ea, eb, s2, s4, s8, *, TM, RS, NT, pos0):
    j = pl.program_id(1)
    HA = _ru8((A_CONV - 1) * RS)
    L1, L2, L3, HB = _pool_levels(RS)
    E = HB + TM
    na, nbp = (A_CONV - 1) * RS, B_PREV * RS

    @pl.when(j == 0)
    def _():
        _slab_put(ea, HA - na, ap_ref[...])
        if HB > nbp:
            _slab_put(eb, 0, jnp.zeros((HB - nbp, B_WIDTH), F32))
        _slab_put(eb, HB - nbp, bp_ref[...])

    x = x_ref[...]
    sh1, sc1, g1 = mod_ref[0], mod_ref[1], mod_ref[2]
    h = _rms(x, ng_ref[0:1, :]) * (1.0 + sc1) + sh1
    proj = jnp.dot(h.astype(BF16), win_ref[...], preferred_element_type=F32)
    hh, bg = proj[:, 0:A_WIDTH], proj[:, A_WIDTH:2 * A_WIDTH]
    cg, u = proj[:, 2 * A_WIDTH:3 * A_WIDTH], proj[:, 3 * A_WIDTH:]

    v = cg * hh
    _slab_put(ea, HA, v)
    z = (aw_ref[2:3, :] * v + aw_ref[1:2, :] * _slab_get(ea, HA - RS, TM)
         + aw_ref[0:1, :] * _slab_get(ea, HA - 2 * RS, TM))
    ya = bg * z
    a_last = _slab_get(ea, HA + TM - na, na)
    an_ref[...] = a_last
    if NT > 1:
        _slab_put(ea, HA - na, a_last)

    _slab_put(eb, HB, u)
    _slab_put(s2, L1, _slab_get(eb, L1, E - L1) + _slab_get(eb, L1 - RS, E - L1))
    _slab_put(s4, L2, _slab_get(s2, L2, E - L2, 1) + _slab_get(s2, L2 - 2 * RS, E - L2, 1))
    _slab_put(s8, L3, _slab_get(s4, L3, E - L3, 1) + _slab_get(s4, L3 - 4 * RS, E - L3, 1))
    wsum = (s2[0, HB:E, :], s4[0, HB:E, :], s8[0, HB:E, :], s8[1, HB:E, :] + s8[1, HB - 8 * RS:E - 8 * RS, :])
    b_last = _slab_get(eb, E - nbp, nbp)
    bn_ref[...] = b_last
    if NT > 1:
        _slab_put(eb, HB - nbp, b_last)
    row = lax.broadcasted_iota(jnp.int32, (TM, B_GROUP), 0) + j * TM
    pos1 = lax.shift_right_logical(row, int(math.log2(RS))) + (pos0 + 1)
    ybs = []
    for g, win in enumerate(B_WINDOWS):
        cnt = jnp.minimum(pos1, win).astype(F32)
        pooled = wsum[g] / cnt - u[:, g * B_GROUP:(g + 1) * B_GROUP]
        ybs.append(jnp.dot(pooled.astype(BF16), bw_ref[g], preferred_element_type=F32))
    yb = jnp.concatenate(ybs, axis=-1) * bs_ref[...]

    ycat = jnp.concatenate([ya, yb], axis=-1).astype(BF16)
    y = jnp.dot(ycat, wout_ref[...], preferred_element_type=F32)
    x1_ref[...] = x + g1 * _rms(y, ng_ref[1:2, :])


def _mixer_ab(x, mod, ng, a_prev, b_prev, w_in, a_w, b_w, b_scale, w_out, *, TM, RS, pos0):
    NB, R, _ = x.shape
    NT = R // TM
    MR = mod.shape[2]
    _, _, _, HB = _pool_levels(RS)
    HA = _ru8((A_CONV - 1) * RS)
    E = HB + TM
    na, nbp = (A_CONV - 1) * RS, B_PREV * RS
    tile = lambda c: pl.BlockSpec((None, TM, c), lambda n, j: (n, j, 0))
    per_n = lambda r, c: pl.BlockSpec((None, r, c), lambda n, j: (n, 0, 0))
    return pl.pallas_call(
        functools.partial(_ab_body, TM=TM, RS=RS, NT=NT, pos0=pos0),
        grid=(NB, NT),
        in_specs=[tile(D_MODEL),
                  pl.BlockSpec((None, 6, MR, D_MODEL), lambda n, j: (n, 0, 0, 0)),
                  _const_spec((4, D_MODEL)),
                  per_n(na, A_WIDTH), per_n(nbp, B_WIDTH),
                  _const_spec(w_in.shape), _const_spec(a_w.shape), _const_spec(b_w.shape),
                  _const_spec(b_scale.shape), _const_spec(w_out.shape)],
        out_specs=[tile(D_MODEL), per_n(na, A_WIDTH), per_n(nbp, B_WIDTH)],
        out_shape=[jax.ShapeDtypeStruct((NB, R, D_MODEL), F32),
                   jax.ShapeDtypeStruct((NB, na, A_WIDTH), F32),
                   jax.ShapeDtypeStruct((NB, nbp, B_WIDTH), F32)],
        scratch_shapes=[pltpu.VMEM((4, HA + TM, LANES), F32), pltpu.VMEM((4, E, LANES), F32),
                        pltpu.VMEM((4, E, LANES), F32), pltpu.VMEM((3, E, LANES), F32),
                        pltpu.VMEM((2, E, LANES), F32)],
        compiler_params=_cparams(2),
        name="mixer_ab",
    )(x, mod, ng, a_prev, b_prev, w_in, a_w, b_w, b_scale, w_out)


def _ffn_body(*refs, TM, RS, NT, mixer_out):
    if mixer_out:
        (x_ref, mod_ref, ng_ref, fp_ref, wup_ref, cw_ref, cb_ref, wdn_ref, yc_ref, yd_ref, wout_ref,
         y_ref, fn_ref, halo, ext, act) = refs
    else:
        (x_ref, mod_ref, ng_ref, fp_ref, wup_ref, cw_ref, cb_ref, wdn_ref,
         y_ref, fn_ref, halo, ext, act) = refs
    j = pl.program_id(1)
    nf = (FFN_CONV - 1) * RS
    HF = _ru8(nf)

    @pl.when(j == 0)
    def _():
        halo[...] = fp_ref[...]

    x = x_ref[...]
    if mixer_out:
        ym = (jnp.dot(yc_ref[...].astype(BF16), wout_ref[0:C_GW, :], preferred_element_type=F32)
              + jnp.dot(yd_ref[...], wout_ref[C_GW:, :], preferred_element_type=F32))
        x = x + mod_ref[2] * _rms(ym, ng_ref[1:2, :])
    sh2, sc2, g2 = mod_ref[3], mod_ref[4], mod_ref[5]
    hb = (_rms(x, ng_ref[2:3, :]) * (1.0 + sc2) + sh2).astype(BF16)

    def conv_part(col, buf):
        up = jnp.dot(hb, wup_ref[:, col:col + FF_CHUNK], preferred_element_type=F32)
        _slab_put(buf, HF - nf, halo[:, col:col + FF_CHUNK])
        _slab_put(buf, HF, up)
        out = (cw_ref[2:3, col:col + FF_CHUNK] * up
               + cw_ref[1:2, col:col + FF_CHUNK] * _slab_get(buf, HF - RS, TM)
               + cw_ref[0:1, col:col + FF_CHUNK] * _slab_get(buf, HF - 2 * RS, TM)
               + cb_ref[:, col:col + FF_CHUNK])
        halo[:, col:col + FF_CHUNK] = _slab_get(buf, HF + TM - nf, nf)
        return out

    for c in range(D_FF // FF_CHUNK):
        a = conv_part(c * FF_CHUNK, ext.at[2 * (c % 2)])
        g = conv_part(D_FF + c * FF_CHUNK, ext.at[2 * (c % 2) + 1])
        act[:, c * FF_CHUNK:(c + 1) * FF_CHUNK] = (a * (g * _sigmoid(g))).astype(BF16)
    fn_ref[...] = halo[...]
    y = jnp.dot(act[...], wdn_ref[...], preferred_element_type=F32)
    y_ref[...] = x + g2 * _rms(y, ng_ref[3:4, :])


def _layer_spec(shape, layer):
    nd = len(shape) - 1
    return pl.BlockSpec((None,) + tuple(shape[1:]), lambda *_: (layer,) + (0,) * nd, pipeline_mode=pl.Buffered(1))


def _conv_ffn(x, mod, ng, f_prev, w_up, conv_w, conv_b, w_down, *, layer, TM, RS, mixer_out=None):
    NB, R, _ = x.shape
    NT = R // TM
    MR = mod.shape[2]
    nf = (FFN_CONV - 1) * RS
    tile_c = lambda c: pl.BlockSpec((None, TM, c), lambda n, j: (n, j, 0))
    tile = tile_c(D_MODEL)
    per_n = pl.BlockSpec((None, nf, 2 * D_FF), lambda n, j: (n, 0, 0))
    mix_in, mix_specs = [], []
    if mixer_out is not None:
        mix_in = list(mixer_out)
        mix_specs = [tile_c(C_GW), tile_c(D_WIDTH), _const_spec(mixer_out[2].shape)]
    return pl.pallas_call(
        functools.partial(_ffn_body, TM=TM, RS=RS, NT=NT, mixer_out=mixer_out is not None),
        grid=(NB, NT),
        in_specs=[tile,
                  pl.BlockSpec((None, 6, MR, D_MODEL), lambda n, j: (n, 0, 0, 0)),
                  _const_spec((4, D_MODEL)),
                  per_n,
                  _layer_spec(w_up.shape, layer), _layer_spec(conv_w.shape, layer),
                  _layer_spec(conv_b.shape, layer), _layer_spec(w_down.shape, layer)] + mix_specs,
        out_specs=[tile, per_n],
        out_shape=[jax.ShapeDtypeStruct((NB, R, D_MODEL), F32),
                   jax.ShapeDtypeStruct((NB, nf, 2 * D_FF), F32)],
        scratch_shapes=[pltpu.VMEM((nf, 2 * D_FF), F32),
                        pltpu.VMEM((4, FF_CHUNK // LANES, _ru8(nf) + TM, LANES), F32),
                        pltpu.VMEM((TM, D_FF), BF16)],
        compiler_params=_cparams(2),
        name="conv_ffn",
    )(x, mod, ng, f_prev, w_up, conv_w, conv_b, w_down, *mix_in)


def _cd_in_body(*refs, TM, RS, NT, prompt_attn):
    if prompt_attn:
        (x_ref, mod_ref, ng_ref, dp_ref, win_ref, dw_ref, db_ref, lg_ref, lb_ref, perm_ref,
         q0_ref, q1_ref, q2_ref, kv0_ref, kv1_ref, kv2_ref, kvf_ref, yd_ref, dn_ref, ed) = refs
    else:
        (x_ref, mod_ref, ng_ref, dp_ref, win_ref, dw_ref, db_ref, lg_ref, lb_ref,
         q0_ref, q1_ref, q2_ref, kv0_ref, kv1_ref, kv2_ref, kvf_ref, yd_ref, dn_ref, ed) = refs
    q_refs, kv_refs = (q0_ref, q1_ref, q2_ref), (kv0_ref, kv1_ref, kv2_ref)
    j = pl.program_id(1)
    nd = (D_CONV - 1) * RS
    HD = _ru8(nd)

    @pl.when(j == 0)
    def _():
        _slab_put(ed, HD - nd, dp_ref[...])

    x = x_ref[...]
    sh1, sc1 = mod_ref[0], mod_ref[1]
    h = _rms(x, ng_ref[0:1, :]) * (1.0 + sc1) + sh1
    proj = jnp.dot(h.astype(BF16), win_ref[...], preferred_element_type=F32)
    qs = (proj[:, 0:768] * (ATTN_SCALE * LOG2E if prompt_attn else ATTN_SCALE)).astype(BF16)
    ks = proj[:, 768:1536].astype(BF16)
    vs = proj[:, 1536:2304].astype(BF16)
    for g in range(len(C_PAIRS)):
        cols = slice(g * C_GW, (g + 1) * C_GW)
        xg = jnp.concatenate([qs[:, cols], ks[:, cols], vs[:, cols]], axis=-1)
        if prompt_attn and C_PAIRS[g][1] > 1:
            xg = jnp.dot(perm_ref[g - 1], xg, preferred_element_type=F32).astype(BF16)
        q_refs[g][...] = xg[:, 0:C_GW]
        kv_refs[g][...] = xg[:, C_GW:]
    kvf_ref[...] = proj[:, 768:2304]
    dv, dg = proj[:, C_QKV:C_QKV + D_WIDTH], proj[:, C_QKV + D_WIDTH:]
    _slab_put(ed, HD, dv * _sigmoid(dg))

    db, lg, lb = db_ref[...], lg_ref[...], lb_ref[...]
    tiles = D_ROWS // SUBLANES
    for c in range(TM // D_ROWS):
        accs = []
        for s in range(D_WIDTH // LANES):
            acc = jnp.zeros((tiles, SUBLANES, LANES), F32)
            for kk in range(D_CONV):
                off = HD - (D_CONV - 1 - kk) * RS + c * D_ROWS
                tap = ed[s, off:off + D_ROWS, :].reshape(tiles, SUBLANES, LANES)
                acc = acc + dw_ref[kk, :, s * LANES:(s + 1) * LANES] * tap
            accs.append(acc.reshape(D_ROWS, LANES))
        zc = jnp.concatenate(accs, axis=-1) + db
        mu = jnp.mean(zc, axis=-1, keepdims=True)
        zc = zc - mu
        var = jnp.mean(zc * zc, axis=-1, keepdims=True)
        yl = zc * lax.rsqrt(var + EPS) * lg + lb
        yd_ref[c * D_ROWS:(c + 1) * D_ROWS, :] = (yl * _sigmoid(yl)).astype(BF16)

    d_last = _slab_get(ed, HD + TM - nd, nd)
    dn_ref[...] = d_last
    if NT > 1:
        _slab_put(ed, HD - nd, d_last)


def _residue_perm(tm, dil):
    p = np.zeros((tm, tm), np.float32)
    a, r = np.meshgrid(np.arange(tm // dil), np.arange(dil), indexing="ij")
    p[(r * (tm // dil) + a).ravel(), (dil * a + r).ravel()] = 1.0
    return p


def _cd_in(x, mod, ng, d_prev, w_in, d_w, d_b, ln_g, ln_b, *, TM, RS, tail_rows, prompt_attn):
    NB, R, _ = x.shape
    NT = R // TM
    MR = mod.shape[2]
    nd = (D_CONV - 1) * RS
    n_tail = tail_rows // TM
    perm_in, perm_spec = [], []
    if prompt_attn:
        perm_in = [jnp.asarray(np.stack([_residue_perm(TM, dil) for _, dil in C_PAIRS[1:]]), BF16)]
        perm_spec = [_const_spec((len(C_PAIRS) - 1, TM, TM))]
    tile = lambda c: pl.BlockSpec((None, TM, c), lambda n, j: (n, j, 0))
    per_n = pl.BlockSpec((None, nd, D_WIDTH), lambda n, j: (n, 0, 0))
    tail = pl.BlockSpec((None, TM, 1536), lambda n, j: (n, jnp.maximum(j - (NT - n_tail), 0), 0))
    return pl.pallas_call(
        functools.partial(_cd_in_body, TM=TM, RS=RS, NT=NT, prompt_attn=prompt_attn),
        grid=(NB, NT),
        in_specs=[tile(D_MODEL),
                  pl.BlockSpec((None, 6, MR, D_MODEL), lambda n, j: (n, 0, 0, 0)),
                  _const_spec((4, D_MODEL)),
                  per_n,
                  _const_spec(w_in.shape), _const_spec(d_w.shape), _const_spec(d_b.shape),
                  _const_spec(ln_g.shape), _const_spec(ln_b.shape)] + perm_spec,
        out_specs=[tile(C_GW)] * 3 + [tile(2 * C_GW)] * 3 + [tail, tile(D_WIDTH), per_n],
        out_shape=[jax.ShapeDtypeStruct((NB, R, C_GW), BF16)] * 3
                  + [jax.ShapeDtypeStruct((NB, R, 2 * C_GW), BF16)] * 3
                  + [jax.ShapeDtypeStruct((NB, tail_rows, 1536), F32),
                     jax.ShapeDtypeStruct((NB, R, D_WIDTH), BF16),
                     jax.ShapeDtypeStruct((NB, nd, D_WIDTH), F32)],
        scratch_shapes=[pltpu.VMEM((D_WIDTH // LANES, _ru8(nd) + TM, LANES), F32)],
        compiler_params=_cparams(2),
        name="cd_in",
    )(x, mod, ng, d_prev, w_in, d_w, d_b, ln_g, ln_b, *perm_in)


def _head_of_lane(shape):
    return lax.shift_right_logical(lax.broadcasted_iota(jnp.int32, shape, 1), 6)


def _attn_blocks(blocks):
    lane_head = _head_of_lane((Q_BLOCK, C_GW))
    logits = []
    for q, kvp, kvc, _ in blocks:
        zero = jnp.zeros_like(q)
        qs = jnp.concatenate([jnp.where(lane_head == h, q, zero) for h in range(C_HPG)], axis=0)
        kcat = jnp.concatenate([kvp[:, 0:C_GW], kvc[:, 0:C_GW]], axis=0)
        logits.append(lax.dot_general(qs, kcat, (((1,), (1,)), ((), ())), preferred_element_type=F32))
    outs = []
    for (q, kvp, kvc, bias), lg in zip(blocks, logits):
        vcat = jnp.concatenate([kvp[:, C_GW:], kvc[:, C_GW:]], axis=0)
        lg = lg + bias
        m = jnp.max(lg, axis=-1, keepdims=True)
        p = jnp.exp2(lg - m)
        s = jnp.sum(p, axis=-1, keepdims=True)
        pv = jnp.dot(p.astype(BF16), vcat, preferred_element_type=F32) * (1.0 / s)
        lse = m + jnp.log2(s)
        o = jnp.zeros((Q_BLOCK, C_GW), F32)
        l = jnp.zeros((Q_BLOCK, C_GW), F32)
        for h in range(C_HPG):
            sel = lane_head == h
            o = jnp.where(sel, pv[h * Q_BLOCK:(h + 1) * Q_BLOCK, :], o)
            l = jnp.where(sel, lse[h * Q_BLOCK:(h + 1) * Q_BLOCK, :], l)
        outs.append((o, l))
    return outs


def _blocks_per_trip(count):
    return max(u for u in (8, 6, 5, 4, 3, 2, 1) if count % u == 0)


def _attn_prompt_body(q0, q1, q2, kv0, kv1, kv2, p0, p1, p2, bias_ref, yc_ref, osc, lsc, *, TM):
    qs_, kvs, prevs = (q0, q1, q2), (kv0, kv1, kv2), (p0, p1, p2)
    var = jnp.minimum(pl.program_id(1), 1)
    n_blocks = ATT_ROWS // Q_BLOCK

    def put(g, start, n, stride, o, l, r0):
        rows = pl.ds(start, n, stride=stride) if stride > 1 else pl.ds(start, n)
        for s in range(C_GW // LANES):
            osc[g, s, rows, :] = o[r0:r0 + n, s * LANES:(s + 1) * LANES]
            lsc[g, s, rows, :] = l[r0:r0 + n, s * LANES:(s + 1) * LANES]

    for g, (_, dil) in enumerate(C_PAIRS):
        q_ref, kv_ref, p_ref = qs_[g], kvs[g], prevs[g]
        cls = TM // dil
        if cls >= Q_BLOCK:
            prev_off = Q_BLOCK if dil == 1 else TM
            n_first = prev_off // Q_BLOCK
            per_tile = TM // Q_BLOCK

            def token_start(idx, dil=dil, per_tile=per_tile):
                return idx * Q_BLOCK if dil == 1 else (idx // per_tile) * TM + idx % per_tile

            first = []
            for idx in range(n_first):
                rows = slice(idx * Q_BLOCK, (idx + 1) * Q_BLOCK)
                first.append((q_ref[rows, :], p_ref[rows, :], kv_ref[rows, :], bias_ref[g, var]))
            for idx, (o, l) in enumerate(_attn_blocks(first)):
                put(g, token_start(idx), Q_BLOCK, dil, o, l, 0)

            per_trip = _blocks_per_trip(n_blocks - n_first)

            def body(trip, carry, g=g, dil=dil, q_ref=q_ref, kv_ref=kv_ref, prev_off=prev_off,
                     token_start=token_start, n_first=n_first, per_trip=per_trip):
                idxs = [n_first + trip * per_trip + u for u in range(per_trip)]
                blocks = []
                for idx in idxs:
                    st = pl.multiple_of(idx * Q_BLOCK, Q_BLOCK)
                    blocks.append((q_ref[pl.ds(st, Q_BLOCK), :], kv_ref[pl.ds(st - prev_off, Q_BLOCK), :],
                                   kv_ref[pl.ds(st, Q_BLOCK), :], bias_ref[g, 1]))
                for idx, (o, l) in zip(idxs, _attn_blocks(blocks)):
                    put(g, token_start(idx), Q_BLOCK, dil, o, l, 0)
                return carry

            lax.fori_loop(0, (n_blocks - n_first) // per_trip, body, 0)
        else:
            tiles = Q_BLOCK // cls
            per_trip = _blocks_per_trip(dil)

            def body(trip, carry, g=g, dil=dil, q_ref=q_ref, kv_ref=kv_ref, p_ref=p_ref, cls=cls, tiles=tiles,
                     per_trip=per_trip):
                rs = [trip * per_trip + u for u in range(per_trip)]
                blocks = []
                for r in rs:
                    st = pl.multiple_of(r * cls, cls)
                    gather = lambda ref, st=st: jnp.concatenate(
                        [ref[pl.ds(c * TM + st, cls), :] for c in range(tiles)], axis=0)
                    blocks.append((gather(q_ref), gather(p_ref), gather(kv_ref), bias_ref[g, var]))
                for r, (o, l) in zip(rs, _attn_blocks(blocks)):
                    for c in range(tiles):
                        put(g, c * TM + r, cls, dil, o, l, c * cls)
                return carry

            lax.fori_loop(0, dil // per_trip, body, 0)

    def merge(ch, carry):
        st = pl.multiple_of(ch * Q_BLOCK, Q_BLOCK)
        get = lambda ref, g: jnp.concatenate([ref[g, s, pl.ds(st, Q_BLOCK), :] for s in range(C_GW // LANES)], axis=-1)
        ls = [get(lsc, g) for g in range(len(C_PAIRS))]
        mm = jnp.maximum(jnp.maximum(ls[0], ls[1]), ls[2])
        es = [jnp.exp2(l - mm) for l in ls]
        den = es[0] + es[1] + es[2]
        yc = (es[0] / den) * get(osc, 0) + (es[1] / den) * get(osc, 1) + (es[2] / den) * get(osc, 2)
        yc_ref[pl.ds(st, Q_BLOCK), :] = yc.astype(BF16)
        return carry

    lax.fori_loop(0, n_blocks, merge, 0)


def _attn_prompt(qs, kvs, bias, *, TM):
    N, S, _ = qs[0].shape
    assert S % ATT_ROWS == 0 and ATT_ROWS == Q_BLOCK * C_PAIRS[-1][1] and ATT_ROWS % TM == 0
    cur = lambda c: pl.BlockSpec((None, ATT_ROWS, c), lambda n, i: (n, i, 0))
    prev_rows = [Q_BLOCK if dil == 1 else (TM if TM // dil >= Q_BLOCK else ATT_ROWS) for _, dil in C_PAIRS]
    prev = [pl.BlockSpec((None, pr, 2 * C_GW), lambda n, i, k=ATT_ROWS // pr: (n, jnp.maximum(i * k - 1, 0), 0))
            for pr in prev_rows]
    slabs = C_GW // LANES
    return pl.pallas_call(
        functools.partial(_attn_prompt_body, TM=TM),
        grid=(N, S // ATT_ROWS),
        in_specs=[cur(C_GW)] * 3 + [cur(2 * C_GW)] * 3 + prev + [_const_spec(bias.shape)],
        out_specs=cur(C_GW),
        out_shape=jax.ShapeDtypeStruct((N, S, C_GW), BF16),
        scratch_shapes=[pltpu.VMEM((len(C_PAIRS), slabs, ATT_ROWS, LANES), F32)] * 2,
        compiler_params=_cparams(2),
        name="attn_prompt",
    )(*qs, *kvs, *kvs, bias)


SROWS = 32


def _attn_sample_body(q_ref, kvn_ref, c0_ref, c1_ref, c2_ref, bc0_ref, bc1_ref, bc2_ref, bn_ref,
                      yc_ref, n0_ref, n1_ref, n2_ref):
    caches = (c0_ref, c1_ref, c2_ref)
    bcs = (bc0_ref, bc1_ref, bc2_ref)
    news = (n0_ref, n1_ref, n2_ref)
    lane_head = _head_of_lane((SROWS, C_GW))
    row_head = lax.shift_right_logical(lax.broadcasted_iota(jnp.int32, (SROWS, C_GW), 0), 3)
    own = lane_head == row_head
    lane128 = lax.broadcasted_iota(jnp.int32, (SROWS, 128), 1)
    kvn = kvn_ref[...]
    r0 = 8 - DEC_SEQ
    outs, lses = [], []
    for g, (win, dil) in enumerate(C_PAIRS):
        wb = win
        qg = q_ref[:, g * C_GW:(g + 1) * C_GW]
        qs = jnp.where(own, qg, jnp.zeros_like(qg))
        cache = caches[g]
        kt = cache[0:C_GW, :].astype(BF16)
        vt = cache[C_GW:2 * C_GW, :].astype(BF16)
        kn = kvn[:, g * C_GW:(g + 1) * C_GW]
        vn = kvn[:, 768 + g * C_GW:768 + (g + 1) * C_GW]
        lc = jnp.dot(qs, kt, preferred_element_type=F32) + bcs[g][...]
        qf = qs.astype(F32)
        ln = bn_ref[g]
        for c in range(DEC_SEQ):
            d = jnp.sum(qf * kn[r0 + c:r0 + c + 1, :], axis=-1, keepdims=True)
            ln = ln + jnp.where(lane128 == c, d, 0.0)
        m = jnp.maximum(jnp.max(lc, axis=-1, keepdims=True), jnp.max(ln, axis=-1, keepdims=True))
        pc = jnp.exp(lc - m)
        pn = jnp.exp(ln - m)
        s = jnp.sum(pc, axis=-1, keepdims=True) + jnp.sum(pn, axis=-1, keepdims=True)
        pv = lax.dot_general(pc.astype(BF16), vt, (((1,), (1,)), ((), ())), preferred_element_type=F32)
        for c in range(DEC_SEQ):
            pcol = jnp.sum(jnp.where(lane128 == c, pn, 0.0), axis=-1, keepdims=True)
            pv = pv + pcol * vn[r0 + c:r0 + c + 1, :]
        outs.append(pv / s)
        lses.append(m + jnp.log(s))
        rolled = pltpu.roll(cache[...], wb - DEC_SEQ, 1)
        new_rows = jnp.concatenate([jnp.zeros((128 - 8, 2 * C_GW), F32), jnp.concatenate([kn, vn], axis=-1)], axis=0)
        new_cols = new_rows.T
        lane_t = lax.broadcasted_iota(jnp.int32, (2 * C_GW, 128), 1)
        if wb > 128:
            news[g][:, 0:wb - 128] = rolled[:, 0:wb - 128]
        news[g][:, wb - 128:wb] = jnp.where(lane_t >= 128 - DEC_SEQ, new_cols, rolled[:, wb - 128:wb])
    mm = jnp.maximum(jnp.maximum(lses[0], lses[1]), lses[2])
    es = [jnp.exp(l - mm) for l in lses]
    den = es[0] + es[1] + es[2]
    y = (es[0] / den) * outs[0] + (es[1] / den) * outs[1] + (es[2] / den) * outs[2]
    y = jnp.where(own, y, 0.0)
    yc_ref[...] = y[0:8, :] + y[8:16, :] + y[16:24, :] + y[24:32, :]


def _attn_sample(q_rep, kvn, caches, bias_c, bias_n):
    NBt = q_rep.shape[0]
    per_b = lambda r, c: pl.BlockSpec((None, r, c), lambda b: (b, 0, 0))
    wbs = [w for w, _ in C_PAIRS]
    return pl.pallas_call(
        _attn_sample_body,
        grid=(NBt,),
        in_specs=[per_b(SROWS, 768), per_b(8, 1536)] + [per_b(512, w) for w in wbs]
                 + [_const_spec((SROWS, w)) for w in wbs] + [_const_spec((3, SROWS, 128))],
        out_specs=[per_b(8, C_GW)] + [per_b(512, w) for w in wbs],
        out_shape=[jax.ShapeDtypeStruct((NBt, 8, C_GW), F32)]
                  + [jax.ShapeDtypeStruct((NBt, 512, w), F32) for w in wbs],
        compiler_params=_cparams(1),
        name="attn_sample",
    )(q_rep, kvn, *caches, *bias_c, bias_n)


def _t5_bucket(dist):
    dist = np.asarray(dist)
    max_exact = N_BUCKETS // 2
    large = max_exact + (np.log(np.maximum(dist, max_exact) / max_exact) / np.log(MAX_DISTANCE / max_exact)
                         * (N_BUCKETS - max_exact)).astype(np.int32)
    large = np.minimum(large, N_BUCKETS - 1)
    return np.where(dist < max_exact, dist, large).astype(np.int32)


def _group_bias(rel_bias, g, dil):
    buckets = _t5_bucket(dil * np.arange(C_TAPS + 1))
    return rel_bias[buckets][:, g * C_HPG:(g + 1) * C_HPG].T


def _toeplitz_body(c_ref, o_ref):
    keep = lax.broadcasted_iota(jnp.int32, (Q_BLOCK, 2 * Q_BLOCK), 1) >= Q_BLOCK
    for h in range(C_HPG):
        taps = jnp.broadcast_to(c_ref[h:h + 1, :], (Q_BLOCK, 2 * Q_BLOCK))
        t = pltpu.roll(taps, 0, 1, stride=1, stride_axis=0)
        o_ref[1, h * Q_BLOCK:(h + 1) * Q_BLOCK, :] = t
        o_ref[0, h * Q_BLOCK:(h + 1) * Q_BLOCK, :] = jnp.where(keep, t, NEG)


def _prompt_bias(bias_gs):
    c = jnp.stack([jnp.concatenate([b[:, ::-1].astype(F32) * LOG2E, jnp.full((C_HPG, Q_BLOCK - 1), NEG, F32)], axis=1)
                   for b in bias_gs])
    return pl.pallas_call(
        _toeplitz_body,
        grid=(len(bias_gs),),
        in_specs=[pl.BlockSpec((None, C_HPG, 2 * Q_BLOCK), lambda g: (g, 0, 0))],
        out_specs=pl.BlockSpec((None, 2, C_HPG * Q_BLOCK, 2 * Q_BLOCK), lambda g: (g, 0, 0, 0)),
        out_shape=jax.ShapeDtypeStruct((len(bias_gs), 2, C_HPG * Q_BLOCK, 2 * Q_BLOCK), F32),
        compiler_params=_cparams(1),
        name="attn_bias",
    )(c)


def _sample_bias(bias_g, wb, dil):
    z = jnp.stack([bias_g.astype(F32)] + [jnp.full(bias_g.shape, NEG, F32)] * (dil - 1), axis=-1)
    z = z.reshape(C_HPG, (C_TAPS + 1) * dil)
    n = wb + DEC_SEQ
    z = z[:, :n] if z.shape[1] >= n else jnp.pad(z, ((0, 0), (0, n - z.shape[1])), constant_values=NEG)
    fl = z[:, ::-1]
    bc = jnp.stack([fl[:, DEC_SEQ - 1 - t:DEC_SEQ - 1 - t + wb] for t in range(DEC_SEQ)], axis=1)
    bc = jnp.pad(bc, ((0, 0), (0, 8 - DEC_SEQ), (0, 0))).reshape(SROWS, wb)
    bn = jnp.stack([jnp.pad(z[:, :t + 1][:, ::-1], ((0, 0), (0, DEC_SEQ - 1 - t)), constant_values=NEG)
                    for t in range(DEC_SEQ)], axis=1)
    bn = jnp.pad(bn, ((0, 0), (0, 8 - DEC_SEQ), (0, 128 - DEC_SEQ)), constant_values=NEG).reshape(SROWS, 128)
    return bc, bn


def _time_major(s):
    b, k, c = s.shape
    return s.transpose(1, 0, 2).reshape(1, k * b, c)


def _batch_major(s, b):
    _, r, c = s.shape
    return s.reshape(r // b, b, c).transpose(1, 0, 2)


def _stack(x, mod, st, w, *, TM, TM_FFN, RS, pos0):
    NB = x.shape[0]
    new = {}
    ng = w["norm_g"]
    x, new["a"], new["b"] = _mixer_ab(x, mod[0], ng[0], st["a"], st["b"], w["ab_w_in"], w["a_conv_w"],
                                      w["b_w_grp"], w["b_scale"], w["ab_w_out"], TM=TM, RS=RS, pos0=pos0)
    x, new["f0"] = _conv_ffn(x, mod[0], ng[0], st["f0"], w["ffn_w_up"], w["ffn_conv_w"],
                             w["ffn_conv_b"], w["ffn_w_down"], layer=0, TM=TM_FFN, RS=RS)
    tail_rows = min(C_PAIRS[-1][0], x.shape[1])
    prompt = RS == 1
    q0, q1, q2, kv0, kv1, kv2, kvf, yd, new["d"] = _cd_in(
        x, mod[1], ng[1], st["d"], w["cd_w_in"], w["d_conv_w"], w["d_conv_b"], w["d_ln_g"], w["d_ln_b"],
        TM=TM, RS=RS, tail_rows=tail_rows, prompt_attn=prompt)
    if prompt:
        yc = _attn_prompt([q0, q1, q2], [kv0, kv1, kv2], _prompt_bias(w["bias_g"]), TM=TM)
        for g, (win, dil) in enumerate(C_PAIRS):
            wb = min(win, x.shape[1])
            kt = kvf[:, tail_rows - wb:, g * C_GW:(g + 1) * C_GW]
            vt = kvf[:, tail_rows - wb:, 768 + g * C_GW:768 + (g + 1) * C_GW]
            new["c%d" % g] = jnp.stack([kt, vt], axis=2).reshape(NB, wb, 2, C_HPG, C_HEAD_DIM)
    else:
        B = RS
        qb = _batch_major(jnp.concatenate([q0, q1, q2], axis=-1), B)
        q_rep = jnp.pad(jnp.broadcast_to(qb[:, None], (B, C_HPG, DEC_SEQ, 768)),
                        ((0, 0), (0, 0), (0, 8 - DEC_SEQ), (0, 0))).reshape(B, SROWS, 768)
        kvn = jnp.pad(_batch_major(kvf, B), ((0, 0), (8 - DEC_SEQ, 0), (0, 0)))
        bias = [_sample_bias(w["bias_g"][g], win, dil) for g, (win, dil) in enumerate(C_PAIRS)]
        yc, n0, n1, n2 = _attn_sample(q_rep, kvn, st["c"], [b[0] for b in bias],
                                      jnp.stack([b[1] for b in bias]))
        for g, n in enumerate((n0, n1, n2)):
            new["c%d" % g] = n.reshape(B, 2, C_HPG, C_HEAD_DIM, n.shape[2]).transpose(0, 4, 1, 2, 3)
        yc = _time_major(yc[:, :DEC_SEQ])
    x, new["f1"] = _conv_ffn(x, mod[1], ng[1], st["f1"], w["ffn_w_up"], w["ffn_conv_w"],
                             w["ffn_conv_b"], w["ffn_w_down"], layer=1, TM=TM_FFN, RS=RS,
                             mixer_out=(yc, yd, w["cd_w_out"]))
    return x, new


def kernel(x_prompt, x_sample, state_a_conv, state_b_pool, cache_c_win128, cache_c_win512, cache_c_win2048,
           state_d_conv, state_ffn_conv, c_prompt, c_sample, ada_w, ada_b, norm_g, rel_bias, ab_w_in, a_conv_w,
           b_w_grp, b_scale, ab_w_out, cd_w_in, d_conv_w, d_conv_b, d_ln_g, d_ln_b, cd_w_out, ffn_w_up,
           ffn_conv_w, ffn_conv_b, ffn_w_down):
    B, T = DEC_BATCH, DEC_SEQ
    w = dict(norm_g=norm_g,
             ab_w_in=ab_w_in[0].astype(BF16), a_conv_w=a_conv_w[0], b_w_grp=b_w_grp[0].astype(BF16),
             b_scale=b_scale, ab_w_out=ab_w_out[0].astype(BF16),
             cd_w_in=cd_w_in[0].astype(BF16),
             d_conv_w=jnp.broadcast_to(d_conv_w[0][:, None, :], (D_CONV, SUBLANES, D_WIDTH)),
             d_conv_b=d_conv_b, d_ln_g=d_ln_g,
             d_ln_b=d_ln_b, cd_w_out=cd_w_out[0].astype(BF16),
             ffn_w_up=ffn_w_up.astype(BF16), ffn_conv_w=ffn_conv_w, ffn_conv_b=ffn_conv_b[:, None, :],
             ffn_w_down=ffn_w_down.astype(BF16),
             bias_g=[_group_bias(rel_bias, g, dil) for g, (_, dil) in enumerate(C_PAIRS)])

    mod = _ada(jnp.concatenate([c_prompt, c_sample], axis=0), ada_w, ada_b)
    mod_p = mod[:, :BATCH].reshape(DEPTH, BATCH, 6, 1, D_MODEL)
    mod_s = mod[:, BATCH:].reshape(DEPTH, B, 6, D_MODEL).transpose(0, 2, 1, 3)
    mod_s = jnp.broadcast_to(mod_s[:, :, None], (DEPTH, 6, T, B, D_MODEL)).reshape(DEPTH, 1, 6, T * B, D_MODEL)

    zeros = lambda k, c: jnp.zeros((BATCH, k, c), F32)
    st_p = dict(a=zeros(A_CONV - 1, A_WIDTH), b=zeros(B_PREV, B_WIDTH), d=zeros(D_CONV - 1, D_WIDTH),
                f0=zeros(FFN_CONV - 1, 2 * D_FF), f1=zeros(FFN_CONV - 1, 2 * D_FF))
    y_p, np_ = _stack(x_prompt, mod_p, st_p, w, TM=TM_PROMPT, TM_FFN=TM_FFN_PROMPT, RS=1, pos0=0)

    st_s = dict(a=_time_major(state_a_conv[0]), b=_time_major(state_b_pool[0]), d=_time_major(state_d_conv[0]),
                f0=_time_major(state_ffn_conv[0]), f1=_time_major(state_ffn_conv[1]),
                c=[c[0].transpose(0, 2, 3, 4, 1).reshape(B, 512, c.shape[2])
                   for c in (cache_c_win128, cache_c_win512, cache_c_win2048)])
    y_s, ns = _stack(_time_major(x_sample), mod_s, st_s, w, TM=T * B, TM_FFN=T * B, RS=B, pos0=PAST_LEN)

    bm = lambda s: _batch_major(s, B)
    return (y_p, bm(y_s),
            np_["a"][None], bm(ns["a"])[None], np_["b"][None], bm(ns["b"])[None],
            np_["c0"][None], ns["c0"][None], np_["c1"][None], ns["c1"][None], np_["c2"][None], ns["c2"][None],
            np_["d"][None], bm(ns["d"])[None],
            jnp.stack([np_["f0"], np_["f1"]]), jnp.stack([bm(ns["f0"]), bm(ns["f1"])]))
```

```python
import functools
import math

import numpy as np
import jax
import jax.numpy as jnp
from jax import lax
from jax.experimental import pallas as pl
from jax.experimental.pallas import tpu as pltpu

D_MODEL = 1024
BATCH = 4
SEQ = 4096
DEPTH = 2
DEC_BATCH = 32
DEC_SEQ = 4
PAST_LEN = 8192
EPS = 1e-6
A_WIDTH = 512
A_CONV = 3
B_WIDTH = 512
B_WINDOWS = (2, 4, 8, 16)
B_GROUP = 128
B_PREV = 15
C_PAIRS = ((128, 1), (512, 4), (2048, 16))
C_HPG = 4
C_HEAD_DIM = 64
C_HEADS = 12
C_QKV = 2304
C_GW = C_HPG * C_HEAD_DIM
C_TAPS = 128
ATTN_SCALE = C_HEAD_DIM ** -0.5
LOG2E = math.log2(math.e)
Q_BLOCK = 128
N_BUCKETS = 32
MAX_DISTANCE = 2048
D_WIDTH = 512
D_CONV = 31
D_FF = 2816
FFN_CONV = 3

SUBLANES = 8
LANES = 128
VMEM_LIMIT = 56 * 1024 * 1024
NEG = -1e30
TM_PROMPT = 512
TM_FFN_PROMPT = 1024
ATT_ROWS = 2048
FF_CHUNK = 256
D_ROWS = 32

F32 = jnp.float32
BF16 = jnp.bfloat16


def _ru8(n):
    return -(-n // SUBLANES) * SUBLANES


def _pool_levels(rs):
    l1 = _ru8(rs)
    l2 = _ru8(l1 + 2 * rs)
    l3 = _ru8(l2 + 4 * rs)
    l4 = _ru8(l3 + 8 * rs)
    return l1, l2, l3, l4


def _slab_put(ref, r0, val):
    n = val.shape[0]
    for s in range(val.shape[1] // LANES):
        ref[s, r0:r0 + n, :] = val[:, s * LANES:(s + 1) * LANES]


def _slab_get(ref, r0, n, s0=0, ns=None):
    ns = ref.shape[0] - s0 if ns is None else ns
    return jnp.concatenate([ref[s, r0:r0 + n, :] for s in range(s0, s0 + ns)], axis=-1)


def _rms(x, g):
    return x * lax.rsqrt(jnp.mean(x * x, axis=-1, keepdims=True) + EPS) * g


def _sigmoid(x):
    return 1.0 / (1.0 + jnp.exp(-x))


def _cparams(n_axes):
    return pltpu.CompilerParams(dimension_semantics=("arbitrary",) * n_axes, vmem_limit_bytes=VMEM_LIMIT)


def _const_spec(shape):
    nd = len(shape)
    return pl.BlockSpec(shape, lambda *_: (0,) * nd, pipeline_mode=pl.Buffered(1))


def _ada_body(c_ref, w_ref, b_ref, o_ref):
    c = c_ref[...]
    ca = c * _sigmoid(c)
    o_ref[...] = jnp.dot(ca.astype(BF16), w_ref[...].astype(BF16), preferred_element_type=F32) + b_ref[...]


def _ada(c_all, ada_w, ada_b):
    rows = c_all.shape[0]
    tn = 1536
    return pl.pallas_call(
        _ada_body,
        grid=(DEPTH, 6 * D_MODEL // tn),
        in_specs=[pl.BlockSpec((rows, D_MODEL), lambda l, n: (0, 0)),
                  pl.BlockSpec((None, D_MODEL, tn), lambda l, n: (l, 0, n)),
                  pl.BlockSpec((None, 1, tn), lambda l, n: (l, 0, n))],
        out_specs=pl.BlockSpec((None, rows, tn), lambda l, n: (l, 0, n)),
        out_shape=jax.ShapeDtypeStruct((DEPTH, rows, 6 * D_MODEL), F32),
        compiler_params=_cparams(2),
        name="ada",
    )(c_all, ada_w, ada_b.reshape(DEPTH, 1, 6 * D_MODEL))


def _ab_body(x_ref, mod_ref, ng_ref, ap_ref, bp_ref, win_ref, aw_ref, bw_ref, bs_ref, wout_ref,
             x1_ref, an_ref, bn_ref, ea, eb, s2, s4, s8, *, TM, RS, NT, pos0):
    j = pl.program_id(1)
    HA = _ru8((A_CONV - 1) * RS)
    L1, L2, L3, HB = _pool_levels(RS)
    E = HB + TM
    na, nbp = (A_CONV - 1) * RS, B_PREV * RS

    @pl.when(j == 0)
    def _():
        _slab_put(ea, HA - na, ap_ref[...])
        if HB > nbp:
            _slab_put(eb, 0, jnp.zeros((HB - nbp, B_WIDTH), F32))
        _slab_put(eb, HB - nbp, bp_ref[...])

    x = x_ref[...]
    sh1, sc1, g1 = mod_ref[0], mod_ref[1], mod_ref[2]
    h = _rms(x, ng_ref[0:1, :]) * (1.0 + sc1) + sh1
    proj = jnp.dot(h.astype(BF16), win_ref[...], preferred_element_type=F32)
    hh, bg = proj[:, 0:A_WIDTH], proj[:, A_WIDTH:2 * A_WIDTH]
    cg, u = proj[:, 2 * A_WIDTH:3 * A_WIDTH], proj[:, 3 * A_WIDTH:]

    v = cg * hh
    _slab_put(ea, HA, v)
    z = (aw_ref[2:3, :] * v + aw_ref[1:2, :] * _slab_get(ea, HA - RS, TM)
         + aw_ref[0:1, :] * _slab_get(ea, HA - 2 * RS, TM))
    ya = bg * z
    a_last = _slab_get(ea, HA + TM - na, na)
    an_ref[...] = a_last
    if NT > 1:
        _slab_put(ea, HA - na, a_last)

    _slab_put(eb, HB, u)
    _slab_put(s2, L1, _slab_get(eb, L1, E - L1) + _slab_get(eb, L1 - RS, E - L1))
    _slab_put(s4, L2, _slab_get(s2, L2, E - L2, 1) + _slab_get(s2, L2 - 2 * RS, E - L2, 1))
    _slab_put(s8, L3, _slab_get(s4, L3, E - L3, 1) + _slab_get(s4, L3 - 4 * RS, E - L3, 1))
    wsum = (s2[0, HB:E, :], s4[0, HB:E, :], s8[0, HB:E, :], s8[1, HB:E, :] + s8[1, HB - 8 * RS:E - 8 * RS, :])
    b_last = _slab_get(eb, E - nbp, nbp)
    bn_ref[...] = b_last
    if NT > 1:
        _slab_put(eb, HB - nbp, b_last)
    row = lax.broadcasted_iota(jnp.int32, (TM, B_GROUP), 0) + j * TM
    pos1 = lax.shift_right_logical(row, int(math.log2(RS))) + (pos0 + 1)
    ybs = []
    for g, win in enumerate(B_WINDOWS):
        cnt = jnp.minimum(pos1, win).astype(F32)
        pooled = wsum[g] / cnt - u[:, g * B_GROUP:(g + 1) * B_GROUP]
        ybs.append(jnp.dot(pooled.astype(BF16), bw_ref[g], preferred_element_type=F32))
    yb = jnp.concatenate(ybs, axis=-1) * bs_ref[...]

    ycat = jnp.concatenate([ya, yb], axis=-1).astype(BF16)
    y = jnp.dot(ycat, wout_ref[...], preferred_element_type=F32)
    x1_ref[...] = x + g1 * _rms(y, ng_ref[1:2, :])


def _mixer_ab(x, mod, ng, a_prev, b_prev, w_in, a_w, b_w, b_scale, w_out, *, TM, RS, pos0):
    NB, R, _ = x.shape
    NT = R // TM
    MR = mod.shape[2]
    _, _, _, HB = _pool_levels(RS)
    HA = _ru8((A_CONV - 1) * RS)
    E = HB + TM
    na, nbp = (A_CONV - 1) * RS, B_PREV * RS
    tile = lambda c: pl.BlockSpec((None, TM, c), lambda n, j: (n, j, 0))
    per_n = lambda r, c: pl.BlockSpec((None, r, c), lambda n, j: (n, 0, 0))
    return pl.pallas_call(
        functools.partial(_ab_body, TM=TM, RS=RS, NT=NT, pos0=pos0),
        grid=(NB, NT),
        in_specs=[tile(D_MODEL),
                  pl.BlockSpec((None, 6, MR, D_MODEL), lambda n, j: (n, 0, 0, 0)),
                  _const_spec((4, D_MODEL)),
                  per_n(na, A_WIDTH), per_n(nbp, B_WIDTH),
                  _const_spec(w_in.shape), _const_spec(a_w.shape), _const_spec(b_w.shape),
                  _const_spec(b_scale.shape), _const_spec(w_out.shape)],
        out_specs=[tile(D_MODEL), per_n(na, A_WIDTH), per_n(nbp, B_WIDTH)],
        out_shape=[jax.ShapeDtypeStruct((NB, R, D_MODEL), F32),
                   jax.ShapeDtypeStruct((NB, na, A_WIDTH), F32),
                   jax.ShapeDtypeStruct((NB, nbp, B_WIDTH), F32)],
        scratch_shapes=[pltpu.VMEM((4, HA + TM, LANES), F32), pltpu.VMEM((4, E, LANES), F32),
                        pltpu.VMEM((4, E, LANES), F32), pltpu.VMEM((3, E, LANES), F32),
                        pltpu.VMEM((2, E, LANES), F32)],
        compiler_params=_cparams(2),
        name="mixer_ab",
    )(x, mod, ng, a_prev, b_prev, w_in, a_w, b_w, b_scale, w_out)


def _ffn_body(*refs, TM, RS, NT, mixer_out):
    if mixer_out:
        (x_ref, mod_ref, ng_ref, fp_ref, wup_ref, cw_ref, cb_ref, wdn_ref, yc_ref, yd_ref, wout_ref,
         y_ref, fn_ref, halo, ext, act) = refs
    else:
        (x_ref, mod_ref, ng_ref, fp_ref, wup_ref, cw_ref, cb_ref, wdn_ref,
         y_ref, fn_ref, halo, ext, act) = refs
    j = pl.program_id(1)
    nf = (FFN_CONV - 1) * RS
    HF = _ru8(nf)

    @pl.when(j == 0)
    def _():
        halo[...] = fp_ref[...]

    x = x_ref[...]
    if mixer_out:
        ym = (jnp.dot(yc_ref[...].astype(BF16), wout_ref[0:C_GW, :], preferred_element_type=F32)
              + jnp.dot(yd_ref[...], wout_ref[C_GW:, :], preferred_element_type=F32))
        x = x + mod_ref[2] * _rms(ym, ng_ref[1:2, :])
    sh2, sc2, g2 = mod_ref[3], mod_ref[4], mod_ref[5]
    hb = (_rms(x, ng_ref[2:3, :]) * (1.0 + sc2) + sh2).astype(BF16)

    def conv_part(col, buf):
        up = jnp.dot(hb, wup_ref[:, col:col + FF_CHUNK], preferred_element_type=F32)
        _slab_put(buf, HF - nf, halo[:, col:col + FF_CHUNK])
        _slab_put(buf, HF, up)
        out = (cw_ref[2:3, col:col + FF_CHUNK] * up
               + cw_ref[1:2, col:col + FF_CHUNK] * _slab_get(buf, HF - RS, TM)
               + cw_ref[0:1, col:col + FF_CHUNK] * _slab_get(buf, HF - 2 * RS, TM)
               + cb_ref[:, col:col + FF_CHUNK])
        halo[:, col:col + FF_CHUNK] = _slab_get(buf, HF + TM - nf, nf)
        return out

    for c in range(D_FF // FF_CHUNK):
        a = conv_part(c * FF_CHUNK, ext.at[2 * (c % 2)])
        g = conv_part(D_FF + c * FF_CHUNK, ext.at[2 * (c % 2) + 1])
        act[:, c * FF_CHUNK:(c + 1) * FF_CHUNK] = (a * (g * _sigmoid(g))).astype(BF16)
    fn_ref[...] = halo[...]
    y = jnp.dot(act[...], wdn_ref[...], preferred_element_type=F32)
    y_ref[...] = x + g2 * _rms(y, ng_ref[3:4, :])


def _layer_spec(shape, layer):
    nd = len(shape) - 1
    return pl.BlockSpec((None,) + tuple(shape[1:]), lambda *_: (layer,) + (0,) * nd, pipeline_mode=pl.Buffered(1))


def _conv_ffn(x, mod, ng, f_prev, w_up, conv_w, conv_b, w_down, *, layer, TM, RS, mixer_out=None):
    NB, R, _ = x.shape
    NT = R // TM
    MR = mod.shape[2]
    nf = (FFN_CONV - 1) * RS
    tile_c = lambda c: pl.BlockSpec((None, TM, c), lambda n, j: (n, j, 0))
    tile = tile_c(D_MODEL)
    per_n = pl.BlockSpec((None, nf, 2 * D_FF), lambda n, j: (n, 0, 0))
    mix_in, mix_specs = [], []
    if mixer_out is not None:
        mix_in = list(mixer_out)
        mix_specs = [tile_c(C_GW), tile_c(D_WIDTH), _const_spec(mixer_out[2].shape)]
    return pl.pallas_call(
        functools.partial(_ffn_body, TM=TM, RS=RS, NT=NT, mixer_out=mixer_out is not None),
        grid=(NB, NT),
        in_specs=[tile,
                  pl.BlockSpec((None, 6, MR, D_MODEL), lambda n, j: (n, 0, 0, 0)),
                  _const_spec((4, D_MODEL)),
                  per_n,
                  _layer_spec(w_up.shape, layer), _layer_spec(conv_w.shape, layer),
                  _layer_spec(conv_b.shape, layer), _layer_spec(w_down.shape, layer)] + mix_specs,
        out_specs=[tile, per_n],
        out_shape=[jax.ShapeDtypeStruct((NB, R, D_MODEL), F32),
                   jax.ShapeDtypeStruct((NB, nf, 2 * D_FF), F32)],
        scratch_shapes=[pltpu.VMEM((nf, 2 * D_FF), F32),
                        pltpu.VMEM((4, FF_CHUNK // LANES, _ru8(nf) + TM, LANES), F32),
                        pltpu.VMEM((TM, D_FF), BF16)],
        compiler_params=_cparams(2),
        name="conv_ffn",
    )(x, mod, ng, f_prev, w_up, conv_w, conv_b, w_down, *mix_in)


def _cd_in_body(*refs, TM, RS, NT, prompt_attn):
    if prompt_attn:
        (x_ref, mod_ref, ng_ref, dp_ref, win_ref, dw_ref, db_ref, lg_ref, lb_ref, perm_ref,
         q0_ref, q1_ref, q2_ref, kv0_ref, kv1_ref, kv2_ref, c0_ref, c1_ref, c2_ref, yd_ref, dn_ref,
         ed, kvf_ref) = refs
        cache_refs = (c0_ref, c1_ref, c2_ref)
    else:
        (x_ref, mod_ref, ng_ref, dp_ref, win_ref, dw_ref, db_ref, lg_ref, lb_ref,
         q0_ref, q1_ref, q2_ref, kv0_ref, kv1_ref, kv2_ref, kvf_ref, yd_ref, dn_ref, ed) = refs
    q_refs, kv_refs = (q0_ref, q1_ref, q2_ref), (kv0_ref, kv1_ref, kv2_ref)
    j = pl.program_id(1)
    nd = (D_CONV - 1) * RS
    HD = _ru8(nd)

    @pl.when(j == 0)
    def _():
        _slab_put(ed, HD - nd, dp_ref[...])

    x = x_ref[...]
    sh1, sc1 = mod_ref[0], mod_ref[1]
    h = _rms(x, ng_ref[0:1, :]) * (1.0 + sc1) + sh1
    proj = jnp.dot(h.astype(BF16), win_ref[...], preferred_element_type=F32)
    qs = (proj[:, 0:768] * (ATTN_SCALE * LOG2E if prompt_attn else ATTN_SCALE)).astype(BF16)
    ks = proj[:, 768:1536].astype(BF16)
    vs = proj[:, 1536:2304].astype(BF16)
    for g in range(len(C_PAIRS)):
        cols = slice(g * C_GW, (g + 1) * C_GW)
        xg = jnp.concatenate([qs[:, cols], ks[:, cols], vs[:, cols]], axis=-1)
        if prompt_attn and C_PAIRS[g][1] > 1:
            xg = jnp.dot(perm_ref[g - 1], xg, preferred_element_type=F32).astype(BF16)
        q_refs[g][...] = xg[:, 0:C_GW]
        kv_refs[g][...] = xg[:, C_GW:]
    kvf_ref[...] = proj[:, 768:2304]
    dv, dg = proj[:, C_QKV:C_QKV + D_WIDTH], proj[:, C_QKV + D_WIDTH:]
    _slab_put(ed, HD, dv * _sigmoid(dg))

    db, lg, lb = db_ref[...], lg_ref[...], lb_ref[...]
    tiles = D_ROWS // SUBLANES
    for c in range(TM // D_ROWS):
        accs = []
        for s in range(D_WIDTH // LANES):
            acc = jnp.zeros((tiles, SUBLANES, LANES), F32)
            for kk in range(D_CONV):
                off = HD - (D_CONV - 1 - kk) * RS + c * D_ROWS
                tap = ed[s, off:off + D_ROWS, :].reshape(tiles, SUBLANES, LANES)
                acc = acc + dw_ref[kk, :, s * LANES:(s + 1) * LANES] * tap
            accs.append(acc.reshape(D_ROWS, LANES))
        zc = jnp.concatenate(accs, axis=-1) + db
        mu = jnp.mean(zc, axis=-1, keepdims=True)
        zc = zc - mu
        var = jnp.mean(zc * zc, axis=-1, keepdims=True)
        yl = zc * lax.rsqrt(var + EPS) * lg + lb
        yd_ref[c * D_ROWS:(c + 1) * D_ROWS, :] = (yl * _sigmoid(yl)).astype(BF16)

    d_last = _slab_get(ed, HD + TM - nd, nd)
    dn_ref[...] = d_last
    if NT > 1:
        _slab_put(ed, HD - nd, d_last)

    if prompt_attn:
        for g, (win, _) in enumerate(C_PAIRS):
            cols = min(win, TM)

            @pl.when(j >= NT - max(win // TM, 1))
            def _(g=g, cols=cols):
                kt = kvf_ref[:, g * C_GW:(g + 1) * C_GW].T
                vt = kvf_ref[:, 768 + g * C_GW:768 + (g + 1) * C_GW].T
                cache_refs[g][0:C_GW, :] = kt[:, TM - cols:]
                cache_refs[g][C_GW:, :] = vt[:, TM - cols:]


def _residue_perm(tm, dil):
    p = np.zeros((tm, tm), np.float32)
    a, r = np.meshgrid(np.arange(tm // dil), np.arange(dil), indexing="ij")
    p[(r * (tm // dil) + a).ravel(), (dil * a + r).ravel()] = 1.0
    return p


def _cd_in(x, mod, ng, d_prev, w_in, d_w, d_b, ln_g, ln_b, *, TM, RS, prompt_attn):
    NB, R, _ = x.shape
    NT = R // TM
    MR = mod.shape[2]
    nd = (D_CONV - 1) * RS
    tile = lambda c: pl.BlockSpec((None, TM, c), lambda n, j: (n, j, 0))
    per_n = pl.BlockSpec((None, nd, D_WIDTH), lambda n, j: (n, 0, 0))
    perm_in, perm_spec = [], []
    if prompt_attn:
        perm_in = [jnp.asarray(np.stack([_residue_perm(TM, dil) for _, dil in C_PAIRS[1:]]), BF16)]
        perm_spec = [_const_spec((len(C_PAIRS) - 1, TM, TM))]
        assert all(win <= R and (win % TM == 0 or TM % win == 0) for win, _ in C_PAIRS)
        kv_specs = [pl.BlockSpec((None, 2 * C_GW, min(win, TM)),
                                 lambda n, j, first=NT - max(win // TM, 1): (n, 0, jnp.maximum(j - first, 0)))
                    for win, _ in C_PAIRS]
        kv_shapes = [jax.ShapeDtypeStruct((NB, 2 * C_GW, win), F32) for win, _ in C_PAIRS]
        kv_scratch = [pltpu.VMEM((TM, 1536), F32)]
    else:
        kv_specs = [tile(1536)]
        kv_shapes = [jax.ShapeDtypeStruct((NB, R, 1536), F32)]
        kv_scratch = []
    return pl.pallas_call(
        functools.partial(_cd_in_body, TM=TM, RS=RS, NT=NT, prompt_attn=prompt_attn),
        grid=(NB, NT),
        in_specs=[tile(D_MODEL),
                  pl.BlockSpec((None, 6, MR, D_MODEL), lambda n, j: (n, 0, 0, 0)),
                  _const_spec((4, D_MODEL)),
                  per_n,
                  _const_spec(w_in.shape), _const_spec(d_w.shape), _const_spec(d_b.shape),
                  _const_spec(ln_g.shape), _const_spec(ln_b.shape)] + perm_spec,
        out_specs=[tile(C_GW)] * 3 + [tile(2 * C_GW)] * 3 + kv_specs + [tile(D_WIDTH), per_n],
        out_shape=[jax.ShapeDtypeStruct((NB, R, C_GW), BF16)] * 3
                  + [jax.ShapeDtypeStruct((NB, R, 2 * C_GW), BF16)] * 3
                  + kv_shapes
                  + [jax.ShapeDtypeStruct((NB, R, D_WIDTH), BF16),
                     jax.ShapeDtypeStruct((NB, nd, D_WIDTH), F32)],
        scratch_shapes=[pltpu.VMEM((D_WIDTH // LANES, _ru8(nd) + TM, LANES), F32)] + kv_scratch,
        compiler_params=_cparams(2),
        name="cd_in",
    )(x, mod, ng, d_prev, w_in, d_w, d_b, ln_g, ln_b, *perm_in)


def _head_of_lane(shape):
    return lax.shift_right_logical(lax.broadcasted_iota(jnp.int32, shape, 1), 6)


def _attn_blocks(blocks):
    lane_head = _head_of_lane((Q_BLOCK, C_GW))
    logits = []
    for q, kvp, kvc, _ in blocks:
        zero = jnp.zeros_like(q)
        qs = jnp.concatenate([jnp.where(lane_head == h, q, zero) for h in range(C_HPG)], axis=0)
        kcat = jnp.concatenate([kvp[:, 0:C_GW], kvc[:, 0:C_GW]], axis=0)
        logits.append(lax.dot_general(qs, kcat, (((1,), (1,)), ((), ())), preferred_element_type=F32))
    outs = []
    for (q, kvp, kvc, bias), lg in zip(blocks, logits):
        vcat = jnp.concatenate([kvp[:, C_GW:], kvc[:, C_GW:]], axis=0)
        lg = lg + bias
        m = jnp.max(lg, axis=-1, keepdims=True)
        p = jnp.exp2(lg - m)
        s = jnp.sum(p, axis=-1, keepdims=True)
        pv = jnp.dot(p.astype(BF16), vcat, preferred_element_type=F32) * (1.0 / s)
        lse = m + jnp.log2(s)
        o = jnp.zeros((Q_BLOCK, C_GW), F32)
        l = jnp.zeros((Q_BLOCK, C_GW), F32)
        for h in range(C_HPG):
            sel = lane_head == h
            o = jnp.where(sel, pv[h * Q_BLOCK:(h + 1) * Q_BLOCK, :], o)
            l = jnp.where(sel, lse[h * Q_BLOCK:(h + 1) * Q_BLOCK, :], l)
        outs.append((o, l))
    return outs


def _blocks_per_trip(count):
    return max(u for u in (8, 6, 5, 4, 3, 2, 1) if count % u == 0)


def _attn_prompt_body(q0, q1, q2, kv0, kv1, kv2, p0, p1, p2, bias_ref, yc_ref, osc, lsc, *, TM):
    qs_, kvs, prevs = (q0, q1, q2), (kv0, kv1, kv2), (p0, p1, p2)
    var = jnp.minimum(pl.program_id(1), 1)
    n_blocks = ATT_ROWS // Q_BLOCK

    def put(g, start, n, stride, o, l, r0):
        rows = pl.ds(start, n, stride=stride) if stride > 1 else pl.ds(start, n)
        for s in range(C_GW // LANES):
            osc[g, s, rows, :] = o[r0:r0 + n, s * LANES:(s + 1) * LANES]
            lsc[g, s, rows, :] = l[r0:r0 + n, s * LANES:(s + 1) * LANES]

    for g, (_, dil) in enumerate(C_PAIRS):
        q_ref, kv_ref, p_ref = qs_[g], kvs[g], prevs[g]
        cls = TM // dil
        if cls >= Q_BLOCK:
            prev_off = Q_BLOCK if dil == 1 else TM
            n_first = prev_off // Q_BLOCK
            per_tile = TM // Q_BLOCK

            def token_start(idx, dil=dil, per_tile=per_tile):
                return idx * Q_BLOCK if dil == 1 else (idx // per_tile) * TM + idx % per_tile

            first = []
            for idx in range(n_first):
                rows = slice(idx * Q_BLOCK, (idx + 1) * Q_BLOCK)
                first.append((q_ref[rows, :], p_ref[rows, :], kv_ref[rows, :], bias_ref[g, var]))
            for idx, (o, l) in enumerate(_attn_blocks(first)):
                put(g, token_start(idx), Q_BLOCK, dil, o, l, 0)

            per_trip = _blocks_per_trip(n_blocks - n_first)

            def body(trip, carry, g=g, dil=dil, q_ref=q_ref, kv_ref=kv_ref, prev_off=prev_off,
                     token_start=token_start, n_first=n_first, per_trip=per_trip):
                idxs = [n_first + trip * per_trip + u for u in range(per_trip)]
                blocks = []
                for idx in idxs:
                    st = pl.multiple_of(idx * Q_BLOCK, Q_BLOCK)
                    blocks.append((q_ref[pl.ds(st, Q_BLOCK), :], kv_ref[pl.ds(st - prev_off, Q_BLOCK), :],
                                   kv_ref[pl.ds(st, Q_BLOCK), :], bias_ref[g, 1]))
                for idx, (o, l) in zip(idxs, _attn_blocks(blocks)):
                    put(g, token_start(idx), Q_BLOCK, dil, o, l, 0)
                return carry

            lax.fori_loop(0, (n_blocks - n_first) // per_trip, body, 0)
        else:
            tiles = Q_BLOCK // cls
            per_trip = _blocks_per_trip(dil)

            def body(trip, carry, g=g, dil=dil, q_ref=q_ref, kv_ref=kv_ref, p_ref=p_ref, cls=cls, tiles=tiles,
                     per_trip=per_trip):
                rs = [trip * per_trip + u for u in range(per_trip)]
                blocks = []
                for r in rs:
                    st = pl.multiple_of(r * cls, cls)
                    gather = lambda ref, st=st: jnp.concatenate(
                        [ref[pl.ds(c * TM + st, cls), :] for c in range(tiles)], axis=0)
                    blocks.append((gather(q_ref), gather(p_ref), gather(kv_ref), bias_ref[g, var]))
                for r, (o, l) in zip(rs, _attn_blocks(blocks)):
                    for c in range(tiles):
                        put(g, c * TM + r, cls, dil, o, l, c * cls)
                return carry

            lax.fori_loop(0, dil // per_trip, body, 0)

    def merge(ch, carry):
        st = pl.multiple_of(ch * Q_BLOCK, Q_BLOCK)
        get = lambda ref, g: jnp.concatenate([ref[g, s, pl.ds(st, Q_BLOCK), :] for s in range(C_GW // LANES)], axis=-1)
        ls = [get(lsc, g) for g in range(len(C_PAIRS))]
        mm = jnp.maximum(jnp.maximum(ls[0], ls[1]), ls[2])
        es = [jnp.exp2(l - mm) for l in ls]
        den = es[0] + es[1] + es[2]
        yc = (es[0] / den) * get(osc, 0) + (es[1] / den) * get(osc, 1) + (es[2] / den) * get(osc, 2)
        yc_ref[pl.ds(st, Q_BLOCK), :] = yc.astype(BF16)
        return carry

    lax.fori_loop(0, n_blocks, merge, 0)


def _attn_prompt(qs, kvs, bias, *, TM):
    N, S, _ = qs[0].shape
    assert S % ATT_ROWS == 0 and ATT_ROWS == Q_BLOCK * C_PAIRS[-1][1] and ATT_ROWS % TM == 0
    cur = lambda c: pl.BlockSpec((None, ATT_ROWS, c), lambda n, i: (n, i, 0))
    prev_rows = [Q_BLOCK if dil == 1 else (TM if TM // dil >= Q_BLOCK else ATT_ROWS) for _, dil in C_PAIRS]
    prev = [pl.BlockSpec((None, pr, 2 * C_GW), lambda n, i, k=ATT_ROWS // pr: (n, jnp.maximum(i * k - 1, 0), 0))
            for pr in prev_rows]
    slabs = C_GW // LANES
    return pl.pallas_call(
        functools.partial(_attn_prompt_body, TM=TM),
        grid=(N, S // ATT_ROWS),
        in_specs=[cur(C_GW)] * 3 + [cur(2 * C_GW)] * 3 + prev + [_const_spec(bias.shape)],
        out_specs=cur(C_GW),
        out_shape=jax.ShapeDtypeStruct((N, S, C_GW), BF16),
        scratch_shapes=[pltpu.VMEM((len(C_PAIRS), slabs, ATT_ROWS, LANES), F32)] * 2,
        compiler_params=_cparams(2),
        name="attn_prompt",
    )(*qs, *kvs, *kvs, bias)


SROWS = 32


def _attn_sample_body(q_ref, kvn_ref, c0_ref, c1_ref, c2_ref, bc0_ref, bc1_ref, bc2_ref, bn_ref,
                      yc_ref, n0_ref, n1_ref, n2_ref):
    caches = (c0_ref, c1_ref, c2_ref)
    bcs = (bc0_ref, bc1_ref, bc2_ref)
    news = (n0_ref, n1_ref, n2_ref)
    lane_head = _head_of_lane((SROWS, C_GW))
    row_head = lax.shift_right_logical(lax.broadcasted_iota(jnp.int32, (SROWS, C_GW), 0), 3)
    own = lane_head == row_head
    lane128 = lax.broadcasted_iota(jnp.int32, (SROWS, 128), 1)
    kvn = kvn_ref[...]
    r0 = 8 - DEC_SEQ
    outs, lses = [], []
    for g, (win, dil) in enumerate(C_PAIRS):
        wb = win
        qg = q_ref[:, g * C_GW:(g + 1) * C_GW]
        qs = jnp.where(own, qg, jnp.zeros_like(qg))
        cache = caches[g]
        kt = cache[0:C_GW, :].astype(BF16)
        vt = cache[C_GW:2 * C_GW, :].astype(BF16)
        kn = kvn[:, g * C_GW:(g + 1) * C_GW]
        vn = kvn[:, 768 + g * C_GW:768 + (g + 1) * C_GW]
        lc = jnp.dot(qs, kt, preferred_element_type=F32) + bcs[g][...]
        qf = qs.astype(F32)
        ln = bn_ref[g]
        for c in range(DEC_SEQ):
            d = jnp.sum(qf * kn[r0 + c:r0 + c + 1, :], axis=-1, keepdims=True)
            ln = ln + jnp.where(lane128 == c, d, 0.0)
        m = jnp.maximum(jnp.max(lc, axis=-1, keepdims=True), jnp.max(ln, axis=-1, keepdims=True))
        pc = jnp.exp(lc - m)
        pn = jnp.exp(ln - m)
        s = jnp.sum(pc, axis=-1, keepdims=True) + jnp.sum(pn, axis=-1, keepdims=True)
        pv = lax.dot_general(pc.astype(BF16), vt, (((1,), (1,)), ((), ())), preferred_element_type=F32)
        for c in range(DEC_SEQ):
            pcol = jnp.sum(jnp.where(lane128 == c, pn, 0.0), axis=-1, keepdims=True)
            pv = pv + pcol * vn[r0 + c:r0 + c + 1, :]
        outs.append(pv / s)
        lses.append(m + jnp.log(s))
        rolled = pltpu.roll(cache[...], wb - DEC_SEQ, 1)
        new_rows = jnp.concatenate([jnp.zeros((128 - 8, 2 * C_GW), F32), jnp.concatenate([kn, vn], axis=-1)], axis=0)
        new_cols = new_rows.T
        lane_t = lax.broadcasted_iota(jnp.int32, (2 * C_GW, 128), 1)
        if wb > 128:
            news[g][:, 0:wb - 128] = rolled[:, 0:wb - 128]
        news[g][:, wb - 128:wb] = jnp.where(lane_t >= 128 - DEC_SEQ, new_cols, rolled[:, wb - 128:wb])
    mm = jnp.maximum(jnp.maximum(lses[0], lses[1]), lses[2])
    es = [jnp.exp(l - mm) for l in lses]
    den = es[0] + es[1] + es[2]
    y = (es[0] / den) * outs[0] + (es[1] / den) * outs[1] + (es[2] / den) * outs[2]
    y = jnp.where(own, y, 0.0)
    yc_ref[...] = y[0:8, :] + y[8:16, :] + y[16:24, :] + y[24:32, :]


def _attn_sample(q_rep, kvn, caches, bias_c, bias_n):
    NBt = q_rep.shape[0]
    per_b = lambda r, c: pl.BlockSpec((None, r, c), lambda b: (b, 0, 0))
    wbs = [w for w, _ in C_PAIRS]
    return pl.pallas_call(
        _attn_sample_body,
        grid=(NBt,),
        in_specs=[per_b(SROWS, 768), per_b(8, 1536)] + [per_b(512, w) for w in wbs]
                 + [_const_spec((SROWS, w)) for w in wbs] + [_const_spec((3, SROWS, 128))],
        out_specs=[per_b(8, C_GW)] + [per_b(512, w) for w in wbs],
        out_shape=[jax.ShapeDtypeStruct((NBt, 8, C_GW), F32)]
                  + [jax.ShapeDtypeStruct((NBt, 512, w), F32) for w in wbs],
        compiler_params=_cparams(1),
        name="attn_sample",
    )(q_rep, kvn, *caches, *bias_c, bias_n)


def _t5_bucket(dist):
    dist = np.asarray(dist)
    max_exact = N_BUCKETS // 2
    large = max_exact + (np.log(np.maximum(dist, max_exact) / max_exact) / np.log(MAX_DISTANCE / max_exact)
                         * (N_BUCKETS - max_exact)).astype(np.int32)
    large = np.minimum(large, N_BUCKETS - 1)
    return np.where(dist < max_exact, dist, large).astype(np.int32)


def _group_bias(rel_bias, g, dil):
    buckets = _t5_bucket(dil * np.arange(C_TAPS + 1))
    return rel_bias[buckets][:, g * C_HPG:(g + 1) * C_HPG].T


def _toeplitz_body(c_ref, o_ref):
    keep = lax.broadcasted_iota(jnp.int32, (Q_BLOCK, 2 * Q_BLOCK), 1) >= Q_BLOCK
    for h in range(C_HPG):
        taps = jnp.broadcast_to(c_ref[h:h + 1, :], (Q_BLOCK, 2 * Q_BLOCK))
        t = pltpu.roll(taps, 0, 1, stride=1, stride_axis=0)
        o_ref[1, h * Q_BLOCK:(h + 1) * Q_BLOCK, :] = t
        o_ref[0, h * Q_BLOCK:(h + 1) * Q_BLOCK, :] = jnp.where(keep, t, NEG)


def _prompt_bias(bias_gs):
    c = jnp.stack([jnp.concatenate([b[:, ::-1].astype(F32) * LOG2E, jnp.full((C_HPG, Q_BLOCK - 1), NEG, F32)], axis=1)
                   for b in bias_gs])
    return pl.pallas_call(
        _toeplitz_body,
        grid=(len(bias_gs),),
        in_specs=[pl.BlockSpec((None, C_HPG, 2 * Q_BLOCK), lambda g: (g, 0, 0))],
        out_specs=pl.BlockSpec((None, 2, C_HPG * Q_BLOCK, 2 * Q_BLOCK), lambda g: (g, 0, 0, 0)),
        out_shape=jax.ShapeDtypeStruct((len(bias_gs), 2, C_HPG * Q_BLOCK, 2 * Q_BLOCK), F32),
        compiler_params=_cparams(1),
        name="attn_bias",
    )(c)


def _sample_bias(bias_g, wb, dil):
    z = jnp.stack([bias_g.astype(F32)] + [jnp.full(bias_g.shape, NEG, F32)] * (dil - 1), axis=-1)
    z = z.reshape(C_HPG, (C_TAPS + 1) * dil)
    n = wb + DEC_SEQ
    z = z[:, :n] if z.shape[1] >= n else jnp.pad(z, ((0, 0), (0, n - z.shape[1])), constant_values=NEG)
    fl = z[:, ::-1]
    bc = jnp.stack([fl[:, DEC_SEQ - 1 - t:DEC_SEQ - 1 - t + wb] for t in range(DEC_SEQ)], axis=1)
    bc = jnp.pad(bc, ((0, 0), (0, 8 - DEC_SEQ), (0, 0))).reshape(SROWS, wb)
    bn = jnp.stack([jnp.pad(z[:, :t + 1][:, ::-1], ((0, 0), (0, DEC_SEQ - 1 - t)), constant_values=NEG)
                    for t in range(DEC_SEQ)], axis=1)
    bn = jnp.pad(bn, ((0, 0), (0, 8 - DEC_SEQ), (0, 128 - DEC_SEQ)), constant_values=NEG).reshape(SROWS, 128)
    return bc, bn


def _time_major(s):
    b, k, c = s.shape
    return s.transpose(1, 0, 2).reshape(1, k * b, c)


def _batch_major(s, b):
    _, r, c = s.shape
    return s.reshape(r // b, b, c).transpose(1, 0, 2)


def _stack(x, mod, st, w, *, TM, TM_FFN, RS, pos0):
    NB = x.shape[0]
    new = {}
    ng = w["norm_g"]
    x, new["a"], new["b"] = _mixer_ab(x, mod[0], ng[0], st["a"], st["b"], w["ab_w_in"], w["a_conv_w"],
                                      w["b_w_grp"], w["b_scale"], w["ab_w_out"], TM=TM, RS=RS, pos0=pos0)
    x, new["f0"] = _conv_ffn(x, mod[0], ng[0], st["f0"], w["ffn_w_up"], w["ffn_conv_w"],
                             w["ffn_conv_b"], w["ffn_w_down"], layer=0, TM=TM_FFN, RS=RS)
    prompt = RS == 1
    outs = _cd_in(x, mod[1], ng[1], st["d"], w["cd_w_in"], w["d_conv_w"], w["d_conv_b"], w["d_ln_g"], w["d_ln_b"],
                  TM=TM, RS=RS, prompt_attn=prompt)
    qs, kvs, yd, new["d"] = outs[0:3], outs[3:6], outs[-2], outs[-1]
    cache_layout = lambda n: n.reshape(n.shape[0], 2, C_HPG, C_HEAD_DIM, n.shape[2]).transpose(0, 4, 1, 2, 3)
    if prompt:
        yc = _attn_prompt(qs, kvs, _prompt_bias(w["bias_g"]), TM=TM)
        for g in range(len(C_PAIRS)):
            new["c%d" % g] = cache_layout(outs[6 + g])
    else:
        B = RS
        kvf = outs[6]
        qb = _batch_major(jnp.concatenate(qs, axis=-1), B)
        q_rep = jnp.pad(jnp.broadcast_to(qb[:, None], (B, C_HPG, DEC_SEQ, 768)),
                        ((0, 0), (0, 0), (0, 8 - DEC_SEQ), (0, 0))).reshape(B, SROWS, 768)
        kvn = jnp.pad(_batch_major(kvf, B), ((0, 0), (8 - DEC_SEQ, 0), (0, 0)))
        bias = [_sample_bias(w["bias_g"][g], win, dil) for g, (win, dil) in enumerate(C_PAIRS)]
        yc, n0, n1, n2 = _attn_sample(q_rep, kvn, st["c"], [b[0] for b in bias],
                                      jnp.stack([b[1] for b in bias]))
        for g, n in enumerate((n0, n1, n2)):
            new["c%d" % g] = cache_layout(n)
        yc = _time_major(yc[:, :DEC_SEQ])
    x, new["f1"] = _conv_ffn(x, mod[1], ng[1], st["f1"], w["ffn_w_up"], w["ffn_conv_w"],
                             w["ffn_conv_b"], w["ffn_w_down"], layer=1, TM=TM_FFN, RS=RS,
                             mixer_out=(yc, yd, w["cd_w_out"]))
    return x, new


def kernel(x_prompt, x_sample, state_a_conv, state_b_pool, cache_c_win128, cache_c_win512, cache_c_win2048,
           state_d_conv, state_ffn_conv, c_prompt, c_sample, ada_w, ada_b, norm_g, rel_bias, ab_w_in, a_conv_w,
           b_w_grp, b_scale, ab_w_out, cd_w_in, d_conv_w, d_conv_b, d_ln_g, d_ln_b, cd_w_out, ffn_w_up,
           ffn_conv_w, ffn_conv_b, ffn_w_down):
    B, T = DEC_BATCH, DEC_SEQ
    w = dict(norm_g=norm_g,
             ab_w_in=ab_w_in[0].astype(BF16), a_conv_w=a_conv_w[0], b_w_grp=b_w_grp[0].astype(BF16),
             b_scale=b_scale, ab_w_out=ab_w_out[0].astype(BF16),
             cd_w_in=cd_w_in[0].astype(BF16),
             d_conv_w=jnp.broadcast_to(d_conv_w[0][:, None, :], (D_CONV, SUBLANES, D_WIDTH)),
             d_conv_b=d_conv_b, d_ln_g=d_ln_g,
             d_ln_b=d_ln_b, cd_w_out=cd_w_out[0].astype(BF16),
             ffn_w_up=ffn_w_up.astype(BF16), ffn_conv_w=ffn_conv_w, ffn_conv_b=ffn_conv_b[:, None, :],
             ffn_w_down=ffn_w_down.astype(BF16),
             bias_g=[_group_bias(rel_bias, g, dil) for g, (_, dil) in enumerate(C_PAIRS)])

    mod = _ada(jnp.concatenate([c_prompt, c_sample], axis=0), ada_w, ada_b)
    mod_p = mod[:, :BATCH].reshape(DEPTH, BATCH, 6, 1, D_MODEL)
    mod_s = mod[:, BATCH:].reshape(DEPTH, B, 6, D_MODEL).transpose(0, 2, 1, 3)
    mod_s = jnp.broadcast_to(mod_s[:, :, None], (DEPTH, 6, T, B, D_MODEL)).reshape(DEPTH, 1, 6, T * B, D_MODEL)

    zeros = lambda k, c: jnp.zeros((BATCH, k, c), F32)
    st_p = dict(a=zeros(A_CONV - 1, A_WIDTH), b=zeros(B_PREV, B_WIDTH), d=zeros(D_CONV - 1, D_WIDTH),
                f0=zeros(FFN_CONV - 1, 2 * D_FF), f1=zeros(FFN_CONV - 1, 2 * D_FF))
    y_p, np_ = _stack(x_prompt, mod_p, st_p, w, TM=TM_PROMPT, TM_FFN=TM_FFN_PROMPT, RS=1, pos0=0)

    st_s = dict(a=_time_major(state_a_conv[0]), b=_time_major(state_b_pool[0]), d=_time_major(state_d_conv[0]),
                f0=_time_major(state_ffn_conv[0]), f1=_time_major(state_ffn_conv[1]),
                c=[c[0].transpose(0, 2, 3, 4, 1).reshape(B, 512, c.shape[2])
                   for c in (cache_c_win128, cache_c_win512, cache_c_win2048)])
    y_s, ns = _stack(_time_major(x_sample), mod_s, st_s, w, TM=T * B, TM_FFN=T * B, RS=B, pos0=PAST_LEN)

    bm = lambda s: _batch_major(s, B)
    return (y_p, bm(y_s),
            np_["a"][None], bm(ns["a"])[None], np_["b"][None], bm(ns["b"])[None],
            np_["c0"][None], ns["c0"][None], np_["c1"][None], ns["c1"][None], np_["c2"][None], ns["c2"][None],
            np_["d"][None], bm(ns["d"])[None],
            jnp.stack([np_["f0"], np_["f1"]]), jnp.stack([bm(ns["f0"]), bm(ns["f1"])]))
```

```python
import functools
import math

import numpy as np
import jax
import jax.numpy as jnp
from jax import lax
from jax.experimental import pallas as pl
from jax.experimental.pallas import tpu as pltpu

D_MODEL = 1024
BATCH = 4
SEQ = 4096
DEPTH = 2
DEC_BATCH = 32
DEC_SEQ = 4
PAST_LEN = 8192
EPS = 1e-6
A_WIDTH = 512
A_CONV = 3
B_WIDTH = 512
B_WINDOWS = (2, 4, 8, 16)
B_GROUP = 128
B_PREV = 15
C_PAIRS = ((128, 1), (512, 4), (2048, 16))
C_HPG = 4
C_HEAD_DIM = 64
C_HEADS = 12
C_QKV = 2304
C_GW = C_HPG * C_HEAD_DIM
C_TAPS = 128
ATTN_SCALE = C_HEAD_DIM ** -0.5
LOG2E = math.log2(math.e)
Q_BLOCK = 128
N_BUCKETS = 32
MAX_DISTANCE = 2048
D_WIDTH = 512
D_CONV = 31
D_FF = 2816
FFN_CONV = 3

SUBLANES = 8
LANES = 128
VMEM_LIMIT = 56 * 1024 * 1024
NEG = -1e30
TM_PROMPT = 512
TM_FFN_PROMPT = 1024
ATT_ROWS = 2048
FF_CHUNK = 256
D_ROWS = 32

F32 = jnp.float32
BF16 = jnp.bfloat16


def _ru8(n):
    return -(-n // SUBLANES) * SUBLANES


def _pool_levels(rs):
    l1 = _ru8(rs)
    l2 = _ru8(l1 + 2 * rs)
    l3 = _ru8(l2 + 4 * rs)
    l4 = _ru8(l3 + 8 * rs)
    return l1, l2, l3, l4


def _slab_put(ref, r0, val):
    n = val.shape[0]
    for s in range(val.shape[1] // LANES):
        ref[s, r0:r0 + n, :] = val[:, s * LANES:(s + 1) * LANES]


def _slab_get(ref, r0, n, s0=0, ns=None):
    ns = ref.shape[0] - s0 if ns is None else ns
    return jnp.concatenate([ref[s, r0:r0 + n, :] for s in range(s0, s0 + ns)], axis=-1)


def _rms(x, g):
    return x * lax.rsqrt(jnp.mean(x * x, axis=-1, keepdims=True) + EPS) * g


def _sigmoid(x):
    return 1.0 / (1.0 + jnp.exp(-x))


def _cparams(n_axes):
    return pltpu.CompilerParams(dimension_semantics=("arbitrary",) * n_axes, vmem_limit_bytes=VMEM_LIMIT)


def _const_spec(shape):
    nd = len(shape)
    return pl.BlockSpec(shape, lambda *_: (0,) * nd, pipeline_mode=pl.Buffered(1))


def _ada_body(c_ref, w_ref, b_ref, o_ref):
    c = c_ref[...]
    ca = c * _sigmoid(c)
    o_ref[...] = jnp.dot(ca.astype(BF16), w_ref[...].astype(BF16), preferred_element_type=F32) + b_ref[...]


def _ada(c_all, ada_w, ada_b):
    rows = c_all.shape[0]
    tn = 1536
    return pl.pallas_call(
        _ada_body,
        grid=(DEPTH, 6 * D_MODEL // tn),
        in_specs=[pl.BlockSpec((rows, D_MODEL), lambda l, n: (0, 0)),
                  pl.BlockSpec((None, D_MODEL, tn), lambda l, n: (l, 0, n)),
                  pl.BlockSpec((None, 1, tn), lambda l, n: (l, 0, n))],
        out_specs=pl.BlockSpec((None, rows, tn), lambda l, n: (l, 0, n)),
        out_shape=jax.ShapeDtypeStruct((DEPTH, rows, 6 * D_MODEL), F32),
        compiler_params=_cparams(2),
        name="ada",
    )(c_all, ada_w, ada_b.reshape(DEPTH, 1, 6 * D_MODEL))


def _ab_body(x_ref, mod_ref, ng_ref, ap_ref, bp_ref, win_ref, aw_ref, bw_ref, bs_ref, wout_ref,
             x1_ref, an_ref, bn_ref, ea, eb, s2, s4, s8, *, TM, RS, NT, pos0):
    j = pl.program_id(1)
    HA = _ru8((A_CONV - 1) * RS)
    L1, L2, L3, HB = _pool_levels(RS)
    E = HB + TM
    na, nbp = (A_CONV - 1) * RS, B_PREV * RS

    @pl.when(j == 0)
    def _():
        _slab_put(ea, HA - na, ap_ref[...])
        if HB > nbp:
            _slab_put(eb, 0, jnp.zeros((HB - nbp, B_WIDTH), F32))
        _slab_put(eb, HB - nbp, bp_ref[...])

    x = x_ref[...]
    sh1, sc1, g1 = mod_ref[0], mod_ref[1], mod_ref[2]
    h = _rms(x, ng_ref[0:1, :]) * (1.0 + sc1) + sh1
    proj = jnp.dot(h.astype(BF16), win_ref[...], preferred_element_type=F32)
    hh, bg = proj[:, 0:A_WIDTH], proj[:, A_WIDTH:2 * A_WIDTH]
    cg, u = proj[:, 2 * A_WIDTH:3 * A_WIDTH], proj[:, 3 * A_WIDTH:]

    v = cg * hh
    _slab_put(ea, HA, v)
    z = (aw_ref[2:3, :] * v + aw_ref[1:2, :] * _slab_get(ea, HA - RS, TM)
         + aw_ref[0:1, :] * _slab_get(ea, HA - 2 * RS, TM))
    ya = bg * z
    a_last = _slab_get(ea, HA + TM - na, na)
    an_ref[...] = a_last
    if NT > 1:
        _slab_put(ea, HA - na, a_last)

    _slab_put(eb, HB, u)
    _slab_put(s2, L1, _slab_get(eb, L1, E - L1) + _slab_get(eb, L1 - RS, E - L1))
    _slab_put(s4, L2, _slab_get(s2, L2, E - L2, 1) + _slab_get(s2, L2 - 2 * RS, E - L2, 1))
    _slab_put(s8, L3, _slab_get(s4, L3, E - L3, 1) + _slab_get(s4, L3 - 4 * RS, E - L3, 1))
    wsum = (s2[0, HB:E, :], s4[0, HB:E, :], s8[0, HB:E, :], s8[1, HB:E, :] + s8[1, HB - 8 * RS:E - 8 * RS, :])
    b_last = _slab_get(eb, E - nbp, nbp)
    bn_ref[...] = b_last
    if NT > 1:
        _slab_put(eb, HB - nbp, b_last)
    row = lax.broadcasted_iota(jnp.int32, (TM, B_GROUP), 0) + j * TM
    pos1 = lax.shift_right_logical(row, int(math.log2(RS))) + (pos0 + 1)
    ybs = []
    for g, win in enumerate(B_WINDOWS):
        cnt = jnp.minimum(pos1, win).astype(F32)
        pooled = wsum[g] / cnt - u[:, g * B_GROUP:(g + 1) * B_GROUP]
        ybs.append(jnp.dot(pooled.astype(BF16), bw_ref[g], preferred_element_type=F32))
    yb = jnp.concatenate(ybs, axis=-1) * bs_ref[...]

    ycat = jnp.concatenate([ya, yb], axis=-1).astype(BF16)
    y = jnp.dot(ycat, wout_ref[...], preferred_element_type=F32)
    x1_ref[...] = x + g1 * _rms(y, ng_ref[1:2, :])


def _mixer_ab(x, mod, ng, a_prev, b_prev, w_in, a_w, b_w, b_scale, w_out, *, TM, RS, pos0):
    NB, R, _ = x.shape
    NT = R // TM
    MR = mod.shape[2]
    _, _, _, HB = _pool_levels(RS)
    HA = _ru8((A_CONV - 1) * RS)
    E = HB + TM
    na, nbp = (A_CONV - 1) * RS, B_PREV * RS
    tile = lambda c: pl.BlockSpec((None, TM, c), lambda n, j: (n, j, 0))
    per_n = lambda r, c: pl.BlockSpec((None, r, c), lambda n, j: (n, 0, 0))
    return pl.pallas_call(
        functools.partial(_ab_body, TM=TM, RS=RS, NT=NT, pos0=pos0),
        grid=(NB, NT),
        in_specs=[tile(D_MODEL),
                  pl.BlockSpec((None, 6, MR, D_MODEL), lambda n, j: (n, 0, 0, 0)),
                  _const_spec((4, D_MODEL)),
                  per_n(na, A_WIDTH), per_n(nbp, B_WIDTH),
                  _const_spec(w_in.shape), _const_spec(a_w.shape), _const_spec(b_w.shape),
                  _const_spec(b_scale.shape), _const_spec(w_out.shape)],
        out_specs=[tile(D_MODEL), per_n(na, A_WIDTH), per_n(nbp, B_WIDTH)],
        out_shape=[jax.ShapeDtypeStruct((NB, R, D_MODEL), F32),
                   jax.ShapeDtypeStruct((NB, na, A_WIDTH), F32),
                   jax.ShapeDtypeStruct((NB, nbp, B_WIDTH), F32)],
        scratch_shapes=[pltpu.VMEM((4, HA + TM, LANES), F32), pltpu.VMEM((4, E, LANES), F32),
                        pltpu.VMEM((4, E, LANES), F32), pltpu.VMEM((3, E, LANES), F32),
                        pltpu.VMEM((2, E, LANES), F32)],
        compiler_params=_cparams(2),
        name="mixer_ab",
    )(x, mod, ng, a_prev, b_prev, w_in, a_w, b_w, b_scale, w_out)


def _ffn_body(*refs, TM, RS, NT, mixer_out):
    if mixer_out:
        (x_ref, mod_ref, ng_ref, fp_ref, wup_ref, cw_ref, cb_ref, wdn_ref, yc_ref, yd_ref, wout_ref,
         y_ref, fn_ref, halo, ext, act) = refs
    else:
        (x_ref, mod_ref, ng_ref, fp_ref, wup_ref, cw_ref, cb_ref, wdn_ref,
         y_ref, fn_ref, halo, ext, act) = refs
    j = pl.program_id(1)
    nf = (FFN_CONV - 1) * RS
    HF = _ru8(nf)

    @pl.when(j == 0)
    def _():
        halo[...] = fp_ref[...]

    x = x_ref[...]
    if mixer_out:
        ym = (jnp.dot(yc_ref[...].astype(BF16), wout_ref[0:C_GW, :], preferred_element_type=F32)
              + jnp.dot(yd_ref[...], wout_ref[C_GW:, :], preferred_element_type=F32))
        x = x + mod_ref[2] * _rms(ym, ng_ref[1:2, :])
    sh2, sc2, g2 = mod_ref[3], mod_ref[4], mod_ref[5]
    hb = (_rms(x, ng_ref[2:3, :]) * (1.0 + sc2) + sh2).astype(BF16)

    def conv_part(col, buf):
        up = jnp.dot(hb, wup_ref[:, col:col + FF_CHUNK], preferred_element_type=F32)
        _slab_put(buf, HF - nf, halo[:, col:col + FF_CHUNK])
        _slab_put(buf, HF, up)
        out = (cw_ref[2:3, col:col + FF_CHUNK] * up
               + cw_ref[1:2, col:col + FF_CHUNK] * _slab_get(buf, HF - RS, TM)
               + cw_ref[0:1, col:col + FF_CHUNK] * _slab_get(buf, HF - 2 * RS, TM)
               + cb_ref[:, col:col + FF_CHUNK])
        halo[:, col:col + FF_CHUNK] = _slab_get(buf, HF + TM - nf, nf)
        return out

    for c in range(D_FF // FF_CHUNK):
        a = conv_part(c * FF_CHUNK, ext.at[2 * (c % 2)])
        g = conv_part(D_FF + c * FF_CHUNK, ext.at[2 * (c % 2) + 1])
        act[:, c * FF_CHUNK:(c + 1) * FF_CHUNK] = (a * (g * _sigmoid(g))).astype(BF16)
    fn_ref[...] = halo[...]
    y = jnp.dot(act[...], wdn_ref[...], preferred_element_type=F32)
    y_ref[...] = x + g2 * _rms(y, ng_ref[3:4, :])


def _layer_spec(shape, layer):
    nd = len(shape) - 1
    return pl.BlockSpec((None,) + tuple(shape[1:]), lambda *_: (layer,) + (0,) * nd, pipeline_mode=pl.Buffered(1))


def _conv_ffn(x, mod, ng, f_prev, w_up, conv_w, conv_b, w_down, *, layer, TM, RS, mixer_out=None):
    NB, R, _ = x.shape
    NT = R // TM
    MR = mod.shape[2]
    nf = (FFN_CONV - 1) * RS
    tile_c = lambda c: pl.BlockSpec((None, TM, c), lambda n, j: (n, j, 0))
    tile = tile_c(D_MODEL)
    per_n = pl.BlockSpec((None, nf, 2 * D_FF), lambda n, j: (n, 0, 0))
    mix_in, mix_specs = [], []
    if mixer_out is not None:
        mix_in = list(mixer_out)
        mix_specs = [tile_c(C_GW), tile_c(D_WIDTH), _const_spec(mixer_out[2].shape)]
    return pl.pallas_call(
        functools.partial(_ffn_body, TM=TM, RS=RS, NT=NT, mixer_out=mixer_out is not None),
        grid=(NB, NT),
        in_specs=[tile,
                  pl.BlockSpec((None, 6, MR, D_MODEL), lambda n, j: (n, 0, 0, 0)),
                  _const_spec((4, D_MODEL)),
                  per_n,
                  _layer_spec(w_up.shape, layer), _layer_spec(conv_w.shape, layer),
                  _layer_spec(conv_b.shape, layer), _layer_spec(w_down.shape, layer)] + mix_specs,
        out_specs=[tile, per_n],
        out_shape=[jax.ShapeDtypeStruct((NB, R, D_MODEL), F32),
                   jax.ShapeDtypeStruct((NB, nf, 2 * D_FF), F32)],
        scratch_shapes=[pltpu.VMEM((nf, 2 * D_FF), F32),
                        pltpu.VMEM((4, FF_CHUNK // LANES, _ru8(nf) + TM, LANES), F32),
                        pltpu.VMEM((TM, D_FF), BF16)],
        compiler_params=_cparams(2),
        name="conv_ffn",
    )(x, mod, ng, f_prev, w_up, conv_w, conv_b, w_down, *mix_in)


def _cd_in_body(*refs, TM, RS, NT, prompt_attn):
    if prompt_attn:
        (x_ref, mod_ref, ng_ref, dp_ref, win_ref, dw_ref, db_ref, lg_ref, lb_ref, perm_ref,
         q0_ref, q1_ref, q2_ref, kv0_ref, kv1_ref, kv2_ref, c0_ref, c1_ref, c2_ref, yd_ref, dn_ref,
         ed, kvf_ref) = refs
        cache_refs = (c0_ref, c1_ref, c2_ref)
    else:
        (x_ref, mod_ref, ng_ref, dp_ref, win_ref, dw_ref, db_ref, lg_ref, lb_ref,
         q0_ref, q1_ref, q2_ref, kv0_ref, kv1_ref, kv2_ref, kvf_ref, yd_ref, dn_ref, ed) = refs
    q_refs, kv_refs = (q0_ref, q1_ref, q2_ref), (kv0_ref, kv1_ref, kv2_ref)
    j = pl.program_id(1)
    nd = (D_CONV - 1) * RS
    HD = _ru8(nd)

    @pl.when(j == 0)
    def _():
        _slab_put(ed, HD - nd, dp_ref[...])

    x = x_ref[...]
    sh1, sc1 = mod_ref[0], mod_ref[1]
    h = _rms(x, ng_ref[0:1, :]) * (1.0 + sc1) + sh1
    proj = jnp.dot(h.astype(BF16), win_ref[...], preferred_element_type=F32)
    qs = (proj[:, 0:768] * (ATTN_SCALE * LOG2E if prompt_attn else ATTN_SCALE)).astype(BF16)
    ks = proj[:, 768:1536].astype(BF16)
    vs = proj[:, 1536:2304].astype(BF16)
    for g in range(len(C_PAIRS)):
        cols = slice(g * C_GW, (g + 1) * C_GW)
        xg = jnp.concatenate([qs[:, cols], ks[:, cols], vs[:, cols]], axis=-1)
        if prompt_attn and C_PAIRS[g][1] > 1:
            xg = jnp.dot(perm_ref[g - 1], xg, preferred_element_type=F32).astype(BF16)
        q_refs[g][...] = xg[:, 0:C_GW]
        kv_refs[g][...] = xg[:, C_GW:]
    kvf_ref[...] = proj[:, 768:2304]
    dv, dg = proj[:, C_QKV:C_QKV + D_WIDTH], proj[:, C_QKV + D_WIDTH:]
    _slab_put(ed, HD, dv * _sigmoid(dg))

    db, lg, lb = db_ref[...], lg_ref[...], lb_ref[...]
    tiles = D_ROWS // SUBLANES
    for c in range(TM // D_ROWS):
        accs = []
        for s in range(D_WIDTH // LANES):
            acc = jnp.zeros((tiles, SUBLANES, LANES), F32)
            for kk in range(D_CONV):
                off = HD - (D_CONV - 1 - kk) * RS + c * D_ROWS
                tap = ed[s, off:off + D_ROWS, :].reshape(tiles, SUBLANES, LANES)
                acc = acc + dw_ref[kk, :, s * LANES:(s + 1) * LANES] * tap
            accs.append(acc.reshape(D_ROWS, LANES))
        zc = jnp.concatenate(accs, axis=-1) + db
        mu = jnp.mean(zc, axis=-1, keepdims=True)
        zc = zc - mu
        var = jnp.mean(zc * zc, axis=-1, keepdims=True)
        yl = zc * lax.rsqrt(var + EPS) * lg + lb
        yd_ref[c * D_ROWS:(c + 1) * D_ROWS, :] = (yl * _sigmoid(yl)).astype(BF16)

    d_last = _slab_get(ed, HD + TM - nd, nd)
    dn_ref[...] = d_last
    if NT > 1:
        _slab_put(ed, HD - nd, d_last)

    if prompt_attn:
        for g, (win, _) in enumerate(C_PAIRS):
            cols = min(win, TM)

            @pl.when(j >= NT - max(win // TM, 1))
            def _(g=g, cols=cols):
                kt = kvf_ref[:, g * C_GW:(g + 1) * C_GW].T
                vt = kvf_ref[:, 768 + g * C_GW:768 + (g + 1) * C_GW].T
                cache_refs[g][0:C_GW, :] = kt[:, TM - cols:]
                cache_refs[g][C_GW:, :] = vt[:, TM - cols:]


def _residue_perm(tm, dil):
    p = np.zeros((tm, tm), np.float32)
    a, r = np.meshgrid(np.arange(tm // dil), np.arange(dil), indexing="ij")
    p[(r * (tm // dil) + a).ravel(), (dil * a + r).ravel()] = 1.0
    return p


def _cd_in(x, mod, ng, d_prev, w_in, d_w, d_b, ln_g, ln_b, *, TM, RS, prompt_attn):
    NB, R, _ = x.shape
    NT = R // TM
    MR = mod.shape[2]
    nd = (D_CONV - 1) * RS
    tile = lambda c: pl.BlockSpec((None, TM, c), lambda n, j: (n, j, 0))
    per_n = pl.BlockSpec((None, nd, D_WIDTH), lambda n, j: (n, 0, 0))
    perm_in, perm_spec = [], []
    if prompt_attn:
        perm_in = [jnp.asarray(np.stack([_residue_perm(TM, dil) for _, dil in C_PAIRS[1:]]), BF16)]
        perm_spec = [_const_spec((len(C_PAIRS) - 1, TM, TM))]
        assert all(win <= R and (win % TM == 0 or TM % win == 0) for win, _ in C_PAIRS)
        kv_specs = [pl.BlockSpec((None, 2 * C_GW, min(win, TM)),
                                 lambda n, j, first=NT - max(win // TM, 1): (n, 0, jnp.maximum(j - first, 0)))
                    for win, _ in C_PAIRS]
        kv_shapes = [jax.ShapeDtypeStruct((NB, 2 * C_GW, win), F32) for win, _ in C_PAIRS]
        kv_scratch = [pltpu.VMEM((TM, 1536), F32)]
    else:
        kv_specs = [tile(1536)]
        kv_shapes = [jax.ShapeDtypeStruct((NB, R, 1536), F32)]
        kv_scratch = []
    return pl.pallas_call(
        functools.partial(_cd_in_body, TM=TM, RS=RS, NT=NT, prompt_attn=prompt_attn),
        grid=(NB, NT),
        in_specs=[tile(D_MODEL),
                  pl.BlockSpec((None, 6, MR, D_MODEL), lambda n, j: (n, 0, 0, 0)),
                  _const_spec((4, D_MODEL)),
                  per_n,
                  _const_spec(w_in.shape), _const_spec(d_w.shape), _const_spec(d_b.shape),
                  _const_spec(ln_g.shape), _const_spec(ln_b.shape)] + perm_spec,
        out_specs=[tile(C_GW)] * 3 + [tile(2 * C_GW)] * 3 + kv_specs + [tile(D_WIDTH), per_n],
        out_shape=[jax.ShapeDtypeStruct((NB, R, C_GW), BF16)] * 3
                  + [jax.ShapeDtypeStruct((NB, R, 2 * C_GW), BF16)] * 3
                  + kv_shapes
                  + [jax.ShapeDtypeStruct((NB, R, D_WIDTH), BF16),
                     jax.ShapeDtypeStruct((NB, nd, D_WIDTH), F32)],
        scratch_shapes=[pltpu.VMEM((D_WIDTH // LANES, _ru8(nd) + TM, LANES), F32)] + kv_scratch,
        compiler_params=_cparams(2),
        name="cd_in",
    )(x, mod, ng, d_prev, w_in, d_w, d_b, ln_g, ln_b, *perm_in)


def _head_of_lane(shape):
    return lax.shift_right_logical(lax.broadcasted_iota(jnp.int32, shape, 1), 6)


def _attn_blocks(blocks):
    lane_head = _head_of_lane((Q_BLOCK, C_GW))
    first_head = lax.broadcasted_iota(jnp.int32, (Q_BLOCK, LANES), 1) < C_HEAD_DIM
    logits = []
    for q, kvp, kvc, _ in blocks:
        zero = jnp.zeros_like(q)
        qs = jnp.concatenate([jnp.where(lane_head == h, q, zero) for h in range(C_HPG)], axis=0)
        kcat = jnp.concatenate([kvp[:, 0:C_GW], kvc[:, 0:C_GW]], axis=0)
        logits.append(lax.dot_general(qs, kcat, (((1,), (1,)), ((), ())), preferred_element_type=F32))
    logits = [lg + bias for lg, (_, _, _, bias) in zip(logits, blocks)]
    ms = [jnp.max(lg, axis=-1, keepdims=True) for lg in logits]
    ps = [jnp.exp2(lg - m) for lg, m in zip(logits, ms)]
    ss = [jnp.sum(p, axis=-1, keepdims=True) for p in ps]
    pvs = []
    for (_, kvp, kvc, _), p in zip(blocks, ps):
        pb = p.astype(BF16)
        vlo = jnp.concatenate([kvp[:, C_GW:C_GW + LANES], kvc[:, C_GW:C_GW + LANES]], axis=0)
        vhi = jnp.concatenate([kvp[:, C_GW + LANES:], kvc[:, C_GW + LANES:]], axis=0)
        pvs.append((jnp.dot(pb[0:2 * Q_BLOCK], vlo, preferred_element_type=F32),
                    jnp.dot(pb[2 * Q_BLOCK:], vhi, preferred_element_type=F32)))
    outs = []
    for (lo, hi), m, s in zip(pvs, ms, ss):
        r = 1.0 / s
        lse = m + jnp.log2(s)
        pick = lambda a, b: jnp.where(first_head, a, b)
        rows = lambda x, h: x[h * Q_BLOCK:(h + 1) * Q_BLOCK]
        o = jnp.concatenate([pick(lo[0:Q_BLOCK] * rows(r, 0), lo[Q_BLOCK:] * rows(r, 1)),
                             pick(hi[0:Q_BLOCK] * rows(r, 2), hi[Q_BLOCK:] * rows(r, 3))], axis=-1)
        l = jnp.concatenate([pick(rows(lse, 0), rows(lse, 1)), pick(rows(lse, 2), rows(lse, 3))], axis=-1)
        outs.append((o, l))
    return outs


def _blocks_per_trip(count):
    return max(u for u in (5, 4, 3, 2, 1) if count % u == 0)


def _attn_prompt_body(q0, q1, q2, kv0, kv1, kv2, p0, p1, p2, bias_ref, yc_ref, osc, lsc, *, TM):
    qs_, kvs, prevs = (q0, q1, q2), (kv0, kv1, kv2), (p0, p1, p2)
    var = jnp.minimum(pl.program_id(1), 1)
    n_blocks = ATT_ROWS // Q_BLOCK

    def put(g, start, n, stride, o, l, r0):
        rows = pl.ds(start, n, stride=stride) if stride > 1 else pl.ds(start, n)
        for s in range(C_GW // LANES):
            osc[g, s, rows, :] = o[r0:r0 + n, s * LANES:(s + 1) * LANES]
            lsc[g, s, rows, :] = l[r0:r0 + n, s * LANES:(s + 1) * LANES]

    for g, (_, dil) in enumerate(C_PAIRS):
        q_ref, kv_ref, p_ref = qs_[g], kvs[g], prevs[g]
        cls = TM // dil
        if cls >= Q_BLOCK:
            prev_off = Q_BLOCK if dil == 1 else TM
            n_first = prev_off // Q_BLOCK
            per_tile = TM // Q_BLOCK

            def token_start(idx, dil=dil, per_tile=per_tile):
                return idx * Q_BLOCK if dil == 1 else (idx // per_tile) * TM + idx % per_tile

            first = []
            for idx in range(n_first):
                rows = slice(idx * Q_BLOCK, (idx + 1) * Q_BLOCK)
                first.append((q_ref[rows, :], p_ref[rows, :], kv_ref[rows, :], bias_ref[g, var]))
            for idx, (o, l) in enumerate(_attn_blocks(first)):
                put(g, token_start(idx), Q_BLOCK, dil, o, l, 0)

            per_trip = _blocks_per_trip(n_blocks - n_first)

            def body(trip, carry, g=g, dil=dil, q_ref=q_ref, kv_ref=kv_ref, prev_off=prev_off,
                     token_start=token_start, n_first=n_first, per_trip=per_trip):
                idxs = [n_first + trip * per_trip + u for u in range(per_trip)]
                blocks = []
                for idx in idxs:
                    st = pl.multiple_of(idx * Q_BLOCK, Q_BLOCK)
                    blocks.append((q_ref[pl.ds(st, Q_BLOCK), :], kv_ref[pl.ds(st - prev_off, Q_BLOCK), :],
                                   kv_ref[pl.ds(st, Q_BLOCK), :], bias_ref[g, 1]))
                for idx, (o, l) in zip(idxs, _attn_blocks(blocks)):
                    put(g, token_start(idx), Q_BLOCK, dil, o, l, 0)
                return carry

            lax.fori_loop(0, (n_blocks - n_first) // per_trip, body, 0)
        else:
            tiles = Q_BLOCK // cls
            per_trip = _blocks_per_trip(dil)

            def body(trip, carry, g=g, dil=dil, q_ref=q_ref, kv_ref=kv_ref, p_ref=p_ref, cls=cls, tiles=tiles,
                     per_trip=per_trip):
                rs = [trip * per_trip + u for u in range(per_trip)]
                blocks = []
                for r in rs:
                    st = pl.multiple_of(r * cls, cls)
                    gather = lambda ref, st=st: jnp.concatenate(
                        [ref[pl.ds(c * TM + st, cls), :] for c in range(tiles)], axis=0)
                    blocks.append((gather(q_ref), gather(p_ref), gather(kv_ref), bias_ref[g, var]))
                for r, (o, l) in zip(rs, _attn_blocks(blocks)):
                    for c in range(tiles):
                        put(g, c * TM + r, cls, dil, o, l, c * cls)
                return carry

            lax.fori_loop(0, dil // per_trip, body, 0)

    def merge(ch, carry):
        st = pl.multiple_of(ch * Q_BLOCK, Q_BLOCK)
        get = lambda ref, g: jnp.concatenate([ref[g, s, pl.ds(st, Q_BLOCK), :] for s in range(C_GW // LANES)], axis=-1)
        ls = [get(lsc, g) for g in range(len(C_PAIRS))]
        mm = jnp.maximum(jnp.maximum(ls[0], ls[1]), ls[2])
        es = [jnp.exp2(l - mm) for l in ls]
        den = es[0] + es[1] + es[2]
        yc = (es[0] / den) * get(osc, 0) + (es[1] / den) * get(osc, 1) + (es[2] / den) * get(osc, 2)
        yc_ref[pl.ds(st, Q_BLOCK), :] = yc.astype(BF16)
        return carry

    lax.fori_loop(0, n_blocks, merge, 0)


def _attn_prompt(qs, kvs, bias, *, TM):
    N, S, _ = qs[0].shape
    assert S % ATT_ROWS == 0 and ATT_ROWS == Q_BLOCK * C_PAIRS[-1][1] and ATT_ROWS % TM == 0
    cur = lambda c: pl.BlockSpec((None, ATT_ROWS, c), lambda n, i: (n, i, 0))
    prev_rows = [Q_BLOCK if dil == 1 else (TM if TM // dil >= Q_BLOCK else ATT_ROWS) for _, dil in C_PAIRS]
    prev = [pl.BlockSpec((None, pr, 2 * C_GW), lambda n, i, k=ATT_ROWS // pr: (n, jnp.maximum(i * k - 1, 0), 0))
            for pr in prev_rows]
    slabs = C_GW // LANES
    return pl.pallas_call(
        functools.partial(_attn_prompt_body, TM=TM),
        grid=(N, S // ATT_ROWS),
        in_specs=[cur(C_GW)] * 3 + [cur(2 * C_GW)] * 3 + prev + [_const_spec(bias.shape)],
        out_specs=cur(C_GW),
        out_shape=jax.ShapeDtypeStruct((N, S, C_GW), BF16),
        scratch_shapes=[pltpu.VMEM((len(C_PAIRS), slabs, ATT_ROWS, LANES), F32)] * 2,
        compiler_params=_cparams(2),
        name="attn_prompt",
    )(*qs, *kvs, *kvs, bias)


SROWS = 32


def _attn_sample_body(q_ref, kvn_ref, c0_ref, c1_ref, c2_ref, bc0_ref, bc1_ref, bc2_ref, bn_ref,
                      yc_ref, n0_ref, n1_ref, n2_ref):
    caches = (c0_ref, c1_ref, c2_ref)
    bcs = (bc0_ref, bc1_ref, bc2_ref)
    news = (n0_ref, n1_ref, n2_ref)
    lane_head = _head_of_lane((SROWS, C_GW))
    row_head = lax.shift_right_logical(lax.broadcasted_iota(jnp.int32, (SROWS, C_GW), 0), 3)
    own = lane_head == row_head
    lane128 = lax.broadcasted_iota(jnp.int32, (SROWS, 128), 1)
    kvn = kvn_ref[...]
    r0 = 8 - DEC_SEQ
    outs, lses = [], []
    for g, (win, dil) in enumerate(C_PAIRS):
        wb = win
        qg = q_ref[:, g * C_GW:(g + 1) * C_GW]
        qs = jnp.where(own, qg, jnp.zeros_like(qg))
        cache = caches[g]
        kt = cache[0:C_GW, :].astype(BF16)
        vt = cache[C_GW:2 * C_GW, :].astype(BF16)
        kn = kvn[:, g * C_GW:(g + 1) * C_GW]
        vn = kvn[:, 768 + g * C_GW:768 + (g + 1) * C_GW]
        lc = jnp.dot(qs, kt, preferred_element_type=F32) + bcs[g][...]
        qf = qs.astype(F32)
        ln = bn_ref[g]
        for c in range(DEC_SEQ):
            d = jnp.sum(qf * kn[r0 + c:r0 + c + 1, :], axis=-1, keepdims=True)
            ln = ln + jnp.where(lane128 == c, d, 0.0)
        m = jnp.maximum(jnp.max(lc, axis=-1, keepdims=True), jnp.max(ln, axis=-1, keepdims=True))
        pc = jnp.exp(lc - m)
        pn = jnp.exp(ln - m)
        s = jnp.sum(pc, axis=-1, keepdims=True) + jnp.sum(pn, axis=-1, keepdims=True)
        pv = lax.dot_general(pc.astype(BF16), vt, (((1,), (1,)), ((), ())), preferred_element_type=F32)
        for c in range(DEC_SEQ):
            pcol = jnp.sum(jnp.where(lane128 == c, pn, 0.0), axis=-1, keepdims=True)
            pv = pv + pcol * vn[r0 + c:r0 + c + 1, :]
        outs.append(pv / s)
        lses.append(m + jnp.log(s))
        rolled = pltpu.roll(cache[...], wb - DEC_SEQ, 1)
        new_rows = jnp.concatenate([jnp.zeros((128 - 8, 2 * C_GW), F32), jnp.concatenate([kn, vn], axis=-1)], axis=0)
        new_cols = new_rows.T
        lane_t = lax.broadcasted_iota(jnp.int32, (2 * C_GW, 128), 1)
        if wb > 128:
            news[g][:, 0:wb - 128] = rolled[:, 0:wb - 128]
        news[g][:, wb - 128:wb] = jnp.where(lane_t >= 128 - DEC_SEQ, new_cols, rolled[:, wb - 128:wb])
    mm = jnp.maximum(jnp.maximum(lses[0], lses[1]), lses[2])
    es = [jnp.exp(l - mm) for l in lses]
    den = es[0] + es[1] + es[2]
    y = (es[0] / den) * outs[0] + (es[1] / den) * outs[1] + (es[2] / den) * outs[2]
    y = jnp.where(own, y, 0.0)
    yc_ref[...] = y[0:8, :] + y[8:16, :] + y[16:24, :] + y[24:32, :]


def _attn_sample(q_rep, kvn, caches, bias_c, bias_n):
    NBt = q_rep.shape[0]
    per_b = lambda r, c: pl.BlockSpec((None, r, c), lambda b: (b, 0, 0))
    wbs = [w for w, _ in C_PAIRS]
    return pl.pallas_call(
        _attn_sample_body,
        grid=(NBt,),
        in_specs=[per_b(SROWS, 768), per_b(8, 1536)] + [per_b(512, w) for w in wbs]
                 + [_const_spec((SROWS, w)) for w in wbs] + [_const_spec((3, SROWS, 128))],
        out_specs=[per_b(8, C_GW)] + [per_b(512, w) for w in wbs],
        out_shape=[jax.ShapeDtypeStruct((NBt, 8, C_GW), F32)]
                  + [jax.ShapeDtypeStruct((NBt, 512, w), F32) for w in wbs],
        compiler_params=_cparams(1),
        name="attn_sample",
    )(q_rep, kvn, *caches, *bias_c, bias_n)


def _t5_bucket(dist):
    dist = np.asarray(dist)
    max_exact = N_BUCKETS // 2
    large = max_exact + (np.log(np.maximum(dist, max_exact) / max_exact) / np.log(MAX_DISTANCE / max_exact)
                         * (N_BUCKETS - max_exact)).astype(np.int32)
    large = np.minimum(large, N_BUCKETS - 1)
    return np.where(dist < max_exact, dist, large).astype(np.int32)


def _group_bias(rel_bias, g, dil):
    buckets = _t5_bucket(dil * np.arange(C_TAPS + 1))
    return rel_bias[buckets][:, g * C_HPG:(g + 1) * C_HPG].T


def _toeplitz_body(c_ref, o_ref):
    keep = lax.broadcasted_iota(jnp.int32, (Q_BLOCK, 2 * Q_BLOCK), 1) >= Q_BLOCK
    for h in range(C_HPG):
        taps = jnp.broadcast_to(c_ref[h:h + 1, :], (Q_BLOCK, 2 * Q_BLOCK))
        t = pltpu.roll(taps, 0, 1, stride=1, stride_axis=0)
        o_ref[1, h * Q_BLOCK:(h + 1) * Q_BLOCK, :] = t
        o_ref[0, h * Q_BLOCK:(h + 1) * Q_BLOCK, :] = jnp.where(keep, t, NEG)


def _prompt_bias(bias_gs):
    c = jnp.stack([jnp.concatenate([b[:, ::-1].astype(F32) * LOG2E, jnp.full((C_HPG, Q_BLOCK - 1), NEG, F32)], axis=1)
                   for b in bias_gs])
    return pl.pallas_call(
        _toeplitz_body,
        grid=(len(bias_gs),),
        in_specs=[pl.BlockSpec((None, C_HPG, 2 * Q_BLOCK), lambda g: (g, 0, 0))],
        out_specs=pl.BlockSpec((None, 2, C_HPG * Q_BLOCK, 2 * Q_BLOCK), lambda g: (g, 0, 0, 0)),
        out_shape=jax.ShapeDtypeStruct((len(bias_gs), 2, C_HPG * Q_BLOCK, 2 * Q_BLOCK), F32),
        compiler_params=_cparams(1),
        name="attn_bias",
    )(c)


def _sample_bias(bias_g, wb, dil):
    z = jnp.stack([bias_g.astype(F32)] + [jnp.full(bias_g.shape, NEG, F32)] * (dil - 1), axis=-1)
    z = z.reshape(C_HPG, (C_TAPS + 1) * dil)
    n = wb + DEC_SEQ
    z = z[:, :n] if z.shape[1] >= n else jnp.pad(z, ((0, 0), (0, n - z.shape[1])), constant_values=NEG)
    fl = jnp.pad(z, ((0, 0), (DEC_SEQ - 1, 0)), constant_values=NEG)[:, ::-1]
    ext = jnp.stack([fl[:, DEC_SEQ - 1 - t:DEC_SEQ - 1 - t + n] for t in range(DEC_SEQ)], axis=1)
    bc = jnp.pad(ext[:, :, :wb], ((0, 0), (0, 8 - DEC_SEQ), (0, 0))).reshape(SROWS, wb)
    bn = jnp.pad(ext[:, :, wb:], ((0, 0), (0, 8 - DEC_SEQ), (0, 128 - DEC_SEQ)),
                 constant_values=NEG).reshape(SROWS, 128)
    return bc, bn


def _time_major(s):
    b, k, c = s.shape
    return s.transpose(1, 0, 2).reshape(1, k * b, c)


def _batch_major(s, b):
    _, r, c = s.shape
    return s.reshape(r // b, b, c).transpose(1, 0, 2)


def _stack(x, mod, st, w, *, TM, TM_FFN, RS, pos0):
    NB = x.shape[0]
    new = {}
    ng = w["norm_g"]
    x, new["a"], new["b"] = _mixer_ab(x, mod[0], ng[0], st["a"], st["b"], w["ab_w_in"], w["a_conv_w"],
                                      w["b_w_grp"], w["b_scale"], w["ab_w_out"], TM=TM, RS=RS, pos0=pos0)
    x, new["f0"] = _conv_ffn(x, mod[0], ng[0], st["f0"], w["ffn_w_up"], w["ffn_conv_w"],
                             w["ffn_conv_b"], w["ffn_w_down"], layer=0, TM=TM_FFN, RS=RS)
    prompt = RS == 1
    outs = _cd_in(x, mod[1], ng[1], st["d"], w["cd_w_in"], w["d_conv_w"], w["d_conv_b"], w["d_ln_g"], w["d_ln_b"],
                  TM=TM, RS=RS, prompt_attn=prompt)
    qs, kvs, yd, new["d"] = outs[0:3], outs[3:6], outs[-2], outs[-1]
    cache_layout = lambda n: n.reshape(n.shape[0], 2, C_HPG, C_HEAD_DIM, n.shape[2]).transpose(0, 4, 1, 2, 3)
    if prompt:
        yc = _attn_prompt(qs, kvs, _prompt_bias(w["bias_g"]), TM=TM)
        for g in range(len(C_PAIRS)):
            new["c%d" % g] = cache_layout(outs[6 + g])
    else:
        B = RS
        kvf = outs[6]
        qb = _batch_major(jnp.concatenate(qs, axis=-1), B)
        q_rep = jnp.pad(jnp.broadcast_to(qb[:, None], (B, C_HPG, DEC_SEQ, 768)),
                        ((0, 0), (0, 0), (0, 8 - DEC_SEQ), (0, 0))).reshape(B, SROWS, 768)
        kvn = jnp.pad(_batch_major(kvf, B), ((0, 0), (8 - DEC_SEQ, 0), (0, 0)))
        bias = [_sample_bias(w["bias_g"][g], win, dil) for g, (win, dil) in enumerate(C_PAIRS)]
        yc, n0, n1, n2 = _attn_sample(q_rep, kvn, st["c"], [b[0] for b in bias],
                                      jnp.stack([b[1] for b in bias]))
        for g, n in enumerate((n0, n1, n2)):
            new["c%d" % g] = cache_layout(n)
        yc = _time_major(yc[:, :DEC_SEQ])
    x, new["f1"] = _conv_ffn(x, mod[1], ng[1], st["f1"], w["ffn_w_up"], w["ffn_conv_w"],
                             w["ffn_conv_b"], w["ffn_w_down"], layer=1, TM=TM_FFN, RS=RS,
                             mixer_out=(yc, yd, w["cd_w_out"]))
    return x, new


def kernel(x_prompt, x_sample, state_a_conv, state_b_pool, cache_c_win128, cache_c_win512, cache_c_win2048,
           state_d_conv, state_ffn_conv, c_prompt, c_sample, ada_w, ada_b, norm_g, rel_bias, ab_w_in, a_conv_w,
           b_w_grp, b_scale, ab_w_out, cd_w_in, d_conv_w, d_conv_b, d_ln_g, d_ln_b, cd_w_out, ffn_w_up,
           ffn_conv_w, ffn_conv_b, ffn_w_down):
    B, T = DEC_BATCH, DEC_SEQ
    w = dict(norm_g=norm_g,
             ab_w_in=ab_w_in[0].astype(BF16), a_conv_w=a_conv_w[0], b_w_grp=b_w_grp[0].astype(BF16),
             b_scale=b_scale, ab_w_out=ab_w_out[0].astype(BF16),
             cd_w_in=cd_w_in[0].astype(BF16),
             d_conv_w=jnp.broadcast_to(d_conv_w[0][:, None, :], (D_CONV, SUBLANES, D_WIDTH)),
             d_conv_b=d_conv_b, d_ln_g=d_ln_g,
             d_ln_b=d_ln_b, cd_w_out=cd_w_out[0].astype(BF16),
             ffn_w_up=ffn_w_up.astype(BF16), ffn_conv_w=ffn_conv_w, ffn_conv_b=ffn_conv_b[:, None, :],
             ffn_w_down=ffn_w_down.astype(BF16),
             bias_g=[_group_bias(rel_bias, g, dil) for g, (_, dil) in enumerate(C_PAIRS)])

    mod = _ada(jnp.concatenate([c_prompt, c_sample], axis=0), ada_w, ada_b)
    mod_p = mod[:, :BATCH].reshape(DEPTH, BATCH, 6, 1, D_MODEL)
    mod_s = mod[:, BATCH:].reshape(DEPTH, B, 6, D_MODEL).transpose(0, 2, 1, 3)
    mod_s = jnp.broadcast_to(mod_s[:, :, None], (DEPTH, 6, T, B, D_MODEL)).reshape(DEPTH, 1, 6, T * B, D_MODEL)

    zeros = lambda k, c: jnp.zeros((BATCH, k, c), F32)
    st_p = dict(a=zeros(A_CONV - 1, A_WIDTH), b=zeros(B_PREV, B_WIDTH), d=zeros(D_CONV - 1, D_WIDTH),
                f0=zeros(FFN_CONV - 1, 2 * D_FF), f1=zeros(FFN_CONV - 1, 2 * D_FF))
    y_p, np_ = _stack(x_prompt, mod_p, st_p, w, TM=TM_PROMPT, TM_FFN=TM_FFN_PROMPT, RS=1, pos0=0)

    st_s = dict(a=_time_major(state_a_conv[0]), b=_time_major(state_b_pool[0]), d=_time_major(state_d_conv[0]),
                f0=_time_major(state_ffn_conv[0]), f1=_time_major(state_ffn_conv[1]),
                c=[c[0].transpose(0, 2, 3, 4, 1).reshape(B, 512, c.shape[2])
                   for c in (cache_c_win128, cache_c_win512, cache_c_win2048)])
    y_s, ns = _stack(_time_major(x_sample), mod_s, st_s, w, TM=T * B, TM_FFN=T * B, RS=B, pos0=PAST_LEN)

    bm = lambda s: _batch_major(s, B)
    return (y_p, bm(y_s),
            np_["a"][None], bm(ns["a"])[None], np_["b"][None], bm(ns["b"])[None],
            np_["c0"][None], ns["c0"][None], np_["c1"][None], ns["c1"][None], np_["c2"][None], ns["c2"][None],
            np_["d"][None], bm(ns["d"])[None],
            jnp.stack([np_["f0"], np_["f1"]]), jnp.stack([bm(ns["f0"]), bm(ns["f1"])]))
```

```python
import functools
import math

import numpy as np
import jax
import jax.numpy as jnp
from jax import lax
from jax.experimental import pallas as pl
from jax.experimental.pallas import tpu as pltpu

D_MODEL = 1024
BATCH = 4
SEQ = 4096
DEPTH = 2
DEC_BATCH = 32
DEC_SEQ = 4
PAST_LEN = 8192
EPS = 1e-6
A_WIDTH = 512
A_CONV = 3
B_WIDTH = 512
B_WINDOWS = (2, 4, 8, 16)
B_GROUP = 128
B_PREV = 15
C_PAIRS = ((128, 1), (512, 4), (2048, 16))
C_HPG = 4
C_HEAD_DIM = 64
C_HEADS = 12
C_QKV = 2304
C_GW = C_HPG * C_HEAD_DIM
C_TAPS = 128
ATTN_SCALE = C_HEAD_DIM ** -0.5
LOG2E = math.log2(math.e)
Q_BLOCK = 128
N_BUCKETS = 32
MAX_DISTANCE = 2048
D_WIDTH = 512
D_CONV = 31
D_FF = 2816
FFN_CONV = 3

SUBLANES = 8
LANES = 128
VMEM_LIMIT = 56 * 1024 * 1024
NEG = -1e30
TM_PROMPT = 512
TM_WIDE_PROMPT = 1024
ATT_ROWS = 2048
FF_CHUNK = 256
D_ROWS = 32

F32 = jnp.float32
BF16 = jnp.bfloat16


def _ru8(n):
    return -(-n // SUBLANES) * SUBLANES


def _pool_levels(rs):
    l1 = _ru8(rs)
    l2 = _ru8(l1 + 2 * rs)
    l3 = _ru8(l2 + 4 * rs)
    l4 = _ru8(l3 + 8 * rs)
    return l1, l2, l3, l4


def _slab_put(ref, r0, val):
    n = val.shape[0]
    for s in range(val.shape[1] // LANES):
        ref[s, r0:r0 + n, :] = val[:, s * LANES:(s + 1) * LANES]


def _slab_get(ref, r0, n, s0=0, ns=None):
    ns = ref.shape[0] - s0 if ns is None else ns
    return jnp.concatenate([ref[s, r0:r0 + n, :] for s in range(s0, s0 + ns)], axis=-1)


def _rms(x, g):
    return x * lax.rsqrt(jnp.mean(x * x, axis=-1, keepdims=True) + EPS) * g


def _sigmoid(x):
    return 1.0 / (1.0 + jnp.exp(-x))


def _cparams(n_axes):
    return pltpu.CompilerParams(dimension_semantics=("arbitrary",) * n_axes, vmem_limit_bytes=VMEM_LIMIT)


def _const_spec(shape):
    nd = len(shape)
    return pl.BlockSpec(shape, lambda *_: (0,) * nd, pipeline_mode=pl.Buffered(1))


def _ada_body(c_ref, w_ref, b_ref, o_ref):
    c = c_ref[...]
    ca = c * _sigmoid(c)
    o_ref[...] = jnp.dot(ca.astype(BF16), w_ref[...].astype(BF16), preferred_element_type=F32) + b_ref[...]


def _ada(c_all, ada_w, ada_b):
    rows = c_all.shape[0]
    tn = 1536
    return pl.pallas_call(
        _ada_body,
        grid=(DEPTH, 6 * D_MODEL // tn),
        in_specs=[pl.BlockSpec((rows, D_MODEL), lambda l, n: (0, 0)),
                  pl.BlockSpec((None, D_MODEL, tn), lambda l, n: (l, 0, n)),
                  pl.BlockSpec((None, 1, tn), lambda l, n: (l, 0, n))],
        out_specs=pl.BlockSpec((None, rows, tn), lambda l, n: (l, 0, n)),
        out_shape=jax.ShapeDtypeStruct((DEPTH, rows, 6 * D_MODEL), F32),
        compiler_params=_cparams(2),
        name="ada",
    )(c_all, ada_w, ada_b.reshape(DEPTH, 1, 6 * D_MODEL))


def _ab_body(x_ref, mod_ref, ng_ref, ap_ref, bp_ref, win_ref, aw_ref, bw_ref, bs_ref, wout_ref,
             x1_ref, an_ref, bn_ref, ea, eb, s2, s4, s8, *, TM, RS, NT, pos0):
    j = pl.program_id(1)
    HA = _ru8((A_CONV - 1) * RS)
    L1, L2, L3, HB = _pool_levels(RS)
    E = HB + TM
    na, nbp = (A_CONV - 1) * RS, B_PREV * RS

    @pl.when(j == 0)
    def _():
        _slab_put(ea, HA - na, ap_ref[...])
        if HB > nbp:
            _slab_put(eb, 0, jnp.zeros((HB - nbp, B_WIDTH), F32))
        _slab_put(eb, HB - nbp, bp_ref[...])

    x = x_ref[...]
    sh1, sc1, g1 = mod_ref[0], mod_ref[1], mod_ref[2]
    h = _rms(x, ng_ref[0:1, :]) * (1.0 + sc1) + sh1
    proj = jnp.dot(h.astype(BF16), win_ref[...], preferred_element_type=F32)
    hh, bg = proj[:, 0:A_WIDTH], proj[:, A_WIDTH:2 * A_WIDTH]
    cg, u = proj[:, 2 * A_WIDTH:3 * A_WIDTH], proj[:, 3 * A_WIDTH:]

    v = cg * hh
    _slab_put(ea, HA, v)
    z = (aw_ref[2:3, :] * v + aw_ref[1:2, :] * _slab_get(ea, HA - RS, TM)
         + aw_ref[0:1, :] * _slab_get(ea, HA - 2 * RS, TM))
    ya = bg * z
    a_last = _slab_get(ea, HA + TM - na, na)
    an_ref[...] = a_last
    if NT > 1:
        _slab_put(ea, HA - na, a_last)

    _slab_put(eb, HB, u)
    _slab_put(s2, L1, _slab_get(eb, L1, E - L1) + _slab_get(eb, L1 - RS, E - L1))
    _slab_put(s4, L2, _slab_get(s2, L2, E - L2, 1) + _slab_get(s2, L2 - 2 * RS, E - L2, 1))
    _slab_put(s8, L3, _slab_get(s4, L3, E - L3, 1) + _slab_get(s4, L3 - 4 * RS, E - L3, 1))
    wsum = (s2[0, HB:E, :], s4[0, HB:E, :], s8[0, HB:E, :], s8[1, HB:E, :] + s8[1, HB - 8 * RS:E - 8 * RS, :])
    b_last = _slab_get(eb, E - nbp, nbp)
    bn_ref[...] = b_last
    if NT > 1:
        _slab_put(eb, HB - nbp, b_last)
    row = lax.broadcasted_iota(jnp.int32, (TM, B_GROUP), 0) + j * TM
    pos1 = lax.shift_right_logical(row, int(math.log2(RS))) + (pos0 + 1)
    ybs = []
    for g, win in enumerate(B_WINDOWS):
        cnt = jnp.minimum(pos1, win).astype(F32)
        pooled = wsum[g] / cnt - u[:, g * B_GROUP:(g + 1) * B_GROUP]
        ybs.append(jnp.dot(pooled.astype(BF16), bw_ref[g], preferred_element_type=F32))
    yb = jnp.concatenate(ybs, axis=-1) * bs_ref[...]

    ycat = jnp.concatenate([ya, yb], axis=-1).astype(BF16)
    y = jnp.dot(ycat, wout_ref[...], preferred_element_type=F32)
    x1_ref[...] = x + g1 * _rms(y, ng_ref[1:2, :])


def _mixer_ab(x, mod, ng, a_prev, b_prev, w_in, a_w, b_w, b_scale, w_out, *, TM, RS, pos0):
    NB, R, _ = x.shape
    NT = R // TM
    MR = mod.shape[2]
    _, _, _, HB = _pool_levels(RS)
    HA = _ru8((A_CONV - 1) * RS)
    E = HB + TM
    na, nbp = (A_CONV - 1) * RS, B_PREV * RS
    tile = lambda c: pl.BlockSpec((None, TM, c), lambda n, j: (n, j, 0))
    per_n = lambda r, c: pl.BlockSpec((None, r, c), lambda n, j: (n, 0, 0))
    return pl.pallas_call(
        functools.partial(_ab_body, TM=TM, RS=RS, NT=NT, pos0=pos0),
        grid=(NB, NT),
        in_specs=[tile(D_MODEL),
                  pl.BlockSpec((None, 6, MR, D_MODEL), lambda n, j: (n, 0, 0, 0)),
                  _const_spec((4, D_MODEL)),
                  per_n(na, A_WIDTH), per_n(nbp, B_WIDTH),
                  _const_spec(w_in.shape), _const_spec(a_w.shape), _const_spec(b_w.shape),
                  _const_spec(b_scale.shape), _const_spec(w_out.shape)],
        out_specs=[tile(D_MODEL), per_n(na, A_WIDTH), per_n(nbp, B_WIDTH)],
        out_shape=[jax.ShapeDtypeStruct((NB, R, D_MODEL), F32),
                   jax.ShapeDtypeStruct((NB, na, A_WIDTH), F32),
                   jax.ShapeDtypeStruct((NB, nbp, B_WIDTH), F32)],
        scratch_shapes=[pltpu.VMEM((4, HA + TM, LANES), F32), pltpu.VMEM((4, E, LANES), F32),
                        pltpu.VMEM((4, E, LANES), F32), pltpu.VMEM((3, E, LANES), F32),
                        pltpu.VMEM((2, E, LANES), F32)],
        compiler_params=_cparams(2),
        name="mixer_ab",
    )(x, mod, ng, a_prev, b_prev, w_in, a_w, b_w, b_scale, w_out)


def _ffn_body(*refs, TM, RS, NT, mixer_out):
    if mixer_out:
        (x_ref, mod_ref, ng_ref, fp_ref, wup_ref, cw_ref, cb_ref, wdn_ref, yc_ref, yd_ref, wout_ref,
         y_ref, fn_ref, halo, ext, act) = refs
    else:
        (x_ref, mod_ref, ng_ref, fp_ref, wup_ref, cw_ref, cb_ref, wdn_ref,
         y_ref, fn_ref, halo, ext, act) = refs
    j = pl.program_id(1)
    nf = (FFN_CONV - 1) * RS
    HF = _ru8(nf)

    @pl.when(j == 0)
    def _():
        halo[...] = fp_ref[...]

    x = x_ref[...]
    if mixer_out:
        ym = (jnp.dot(yc_ref[...].astype(BF16), wout_ref[0:C_GW, :], preferred_element_type=F32)
              + jnp.dot(yd_ref[...], wout_ref[C_GW:, :], preferred_element_type=F32))
        x = x + mod_ref[2] * _rms(ym, ng_ref[1:2, :])
    sh2, sc2, g2 = mod_ref[3], mod_ref[4], mod_ref[5]
    hb = (_rms(x, ng_ref[2:3, :]) * (1.0 + sc2) + sh2).astype(BF16)

    def conv_part(col, buf):
        up = jnp.dot(hb, wup_ref[:, col:col + FF_CHUNK], preferred_element_type=F32)
        _slab_put(buf, HF - nf, halo[:, col:col + FF_CHUNK])
        _slab_put(buf, HF, up)
        out = (cw_ref[2:3, col:col + FF_CHUNK] * up
               + cw_ref[1:2, col:col + FF_CHUNK] * _slab_get(buf, HF - RS, TM)
               + cw_ref[0:1, col:col + FF_CHUNK] * _slab_get(buf, HF - 2 * RS, TM)
               + cb_ref[:, col:col + FF_CHUNK])
        halo[:, col:col + FF_CHUNK] = _slab_get(buf, HF + TM - nf, nf)
        return out

    for c in range(D_FF // FF_CHUNK):
        a = conv_part(c * FF_CHUNK, ext.at[2 * (c % 2)])
        g = conv_part(D_FF + c * FF_CHUNK, ext.at[2 * (c % 2) + 1])
        act[:, c * FF_CHUNK:(c + 1) * FF_CHUNK] = (a * (g * _sigmoid(g))).astype(BF16)
    fn_ref[...] = halo[...]
    y = jnp.dot(act[...], wdn_ref[...], preferred_element_type=F32)
    y_ref[...] = x + g2 * _rms(y, ng_ref[3:4, :])


def _layer_spec(shape, layer):
    nd = len(shape) - 1
    return pl.BlockSpec((None,) + tuple(shape[1:]), lambda *_: (layer,) + (0,) * nd, pipeline_mode=pl.Buffered(1))


def _conv_ffn(x, mod, ng, f_prev, w_up, conv_w, conv_b, w_down, *, layer, TM, RS, mixer_out=None):
    NB, R, _ = x.shape
    NT = R // TM
    MR = mod.shape[2]
    nf = (FFN_CONV - 1) * RS
    tile_c = lambda c: pl.BlockSpec((None, TM, c), lambda n, j: (n, j, 0))
    tile = tile_c(D_MODEL)
    per_n = pl.BlockSpec((None, nf, 2 * D_FF), lambda n, j: (n, 0, 0))
    mix_in, mix_specs = [], []
    if mixer_out is not None:
        mix_in = list(mixer_out)
        mix_specs = [tile_c(C_GW), tile_c(D_WIDTH), _const_spec(mixer_out[2].shape)]
    return pl.pallas_call(
        functools.partial(_ffn_body, TM=TM, RS=RS, NT=NT, mixer_out=mixer_out is not None),
        grid=(NB, NT),
        in_specs=[tile,
                  pl.BlockSpec((None, 6, MR, D_MODEL), lambda n, j: (n, 0, 0, 0)),
                  _const_spec((4, D_MODEL)),
                  per_n,
                  _layer_spec(w_up.shape, layer), _layer_spec(conv_w.shape, layer),
                  _layer_spec(conv_b.shape, layer), _layer_spec(w_down.shape, layer)] + mix_specs,
        out_specs=[tile, per_n],
        out_shape=[jax.ShapeDtypeStruct((NB, R, D_MODEL), F32),
                   jax.ShapeDtypeStruct((NB, nf, 2 * D_FF), F32)],
        scratch_shapes=[pltpu.VMEM((nf, 2 * D_FF), F32),
                        pltpu.VMEM((4, FF_CHUNK // LANES, _ru8(nf) + TM, LANES), F32),
                        pltpu.VMEM((TM, D_FF), BF16)],
        compiler_params=_cparams(2),
        name="conv_ffn",
    )(x, mod, ng, f_prev, w_up, conv_w, conv_b, w_down, *mix_in)


def _cd_in_body(*refs, TM, RS, NT, prompt_attn):
    if prompt_attn:
        (x_ref, mod_ref, ng_ref, dp_ref, win_ref, dw_ref, db_ref, lg_ref, lb_ref, perm_ref,
         q0_ref, q1_ref, q2_ref, kv0_ref, kv1_ref, kv2_ref, c0_ref, c1_ref, c2_ref, yd_ref, dn_ref,
         ed, kvf_ref) = refs
        cache_refs = (c0_ref, c1_ref, c2_ref)
    else:
        (x_ref, mod_ref, ng_ref, dp_ref, win_ref, dw_ref, db_ref, lg_ref, lb_ref,
         q0_ref, q1_ref, q2_ref, kv0_ref, kv1_ref, kv2_ref, kvf_ref, yd_ref, dn_ref, ed) = refs
    q_refs, kv_refs = (q0_ref, q1_ref, q2_ref), (kv0_ref, kv1_ref, kv2_ref)
    j = pl.program_id(1)
    nd = (D_CONV - 1) * RS
    HD = _ru8(nd)

    @pl.when(j == 0)
    def _():
        _slab_put(ed, HD - nd, dp_ref[...])

    x = x_ref[...]
    sh1, sc1 = mod_ref[0], mod_ref[1]
    h = _rms(x, ng_ref[0:1, :]) * (1.0 + sc1) + sh1
    proj = jnp.dot(h.astype(BF16), win_ref[...], preferred_element_type=F32)
    qs = (proj[:, 0:768] * (ATTN_SCALE * LOG2E if prompt_attn else ATTN_SCALE)).astype(BF16)
    ks = proj[:, 768:1536].astype(BF16)
    vs = proj[:, 1536:2304].astype(BF16)
    for g in range(len(C_PAIRS)):
        cols = slice(g * C_GW, (g + 1) * C_GW)
        xg = jnp.concatenate([qs[:, cols], ks[:, cols], vs[:, cols]], axis=-1)
        if prompt_attn and C_PAIRS[g][1] > 1:
            xg = jnp.dot(perm_ref[g - 1], xg, preferred_element_type=F32).astype(BF16)
        q_refs[g][...] = xg[:, 0:C_GW]
        kv_refs[g][...] = xg[:, C_GW:]
    kvf_ref[...] = proj[:, 768:2304]
    dv, dg = proj[:, C_QKV:C_QKV + D_WIDTH], proj[:, C_QKV + D_WIDTH:]
    _slab_put(ed, HD, dv * _sigmoid(dg))

    db, lg, lb = db_ref[...], lg_ref[...], lb_ref[...]
    tiles = D_ROWS // SUBLANES
    for c in range(TM // D_ROWS):
        accs = []
        for s in range(D_WIDTH // LANES):
            acc = jnp.zeros((tiles, SUBLANES, LANES), F32)
            for kk in range(D_CONV):
                off = HD - (D_CONV - 1 - kk) * RS + c * D_ROWS
                tap = ed[s, off:off + D_ROWS, :].reshape(tiles, SUBLANES, LANES)
                acc = acc + dw_ref[kk, :, s * LANES:(s + 1) * LANES] * tap
            accs.append(acc.reshape(D_ROWS, LANES))
        zc = jnp.concatenate(accs, axis=-1) + db
        mu = jnp.mean(zc, axis=-1, keepdims=True)
        zc = zc - mu
        var = jnp.mean(zc * zc, axis=-1, keepdims=True)
        yl = zc * lax.rsqrt(var + EPS) * lg + lb
        yd_ref[c * D_ROWS:(c + 1) * D_ROWS, :] = (yl * _sigmoid(yl)).astype(BF16)

    d_last = _slab_get(ed, HD + TM - nd, nd)
    dn_ref[...] = d_last
    if NT > 1:
        _slab_put(ed, HD - nd, d_last)

    if prompt_attn:
        for g, (win, _) in enumerate(C_PAIRS):
            cols = min(win, TM)

            @pl.when(j >= NT - max(win // TM, 1))
            def _(g=g, cols=cols):
                kt = kvf_ref[:, g * C_GW:(g + 1) * C_GW].T
                vt = kvf_ref[:, 768 + g * C_GW:768 + (g + 1) * C_GW].T
                cache_refs[g][0:C_GW, :] = kt[:, TM - cols:]
                cache_refs[g][C_GW:, :] = vt[:, TM - cols:]


def _residue_perm(tm, dil):
    p = np.zeros((tm, tm), np.float32)
    a, r = np.meshgrid(np.arange(tm // dil), np.arange(dil), indexing="ij")
    p[(r * (tm // dil) + a).ravel(), (dil * a + r).ravel()] = 1.0
    return p


def _cd_in(x, mod, ng, d_prev, w_in, d_w, d_b, ln_g, ln_b, *, TM, RS, prompt_attn):
    NB, R, _ = x.shape
    NT = R // TM
    MR = mod.shape[2]
    nd = (D_CONV - 1) * RS
    tile = lambda c: pl.BlockSpec((None, TM, c), lambda n, j: (n, j, 0))
    per_n = pl.BlockSpec((None, nd, D_WIDTH), lambda n, j: (n, 0, 0))
    perm_in, perm_spec = [], []
    if prompt_attn:
        perm_in = [jnp.asarray(np.stack([_residue_perm(TM, dil) for _, dil in C_PAIRS[1:]]), BF16)]
        perm_spec = [_const_spec((len(C_PAIRS) - 1, TM, TM))]
        assert all(win <= R and (win % TM == 0 or TM % win == 0) for win, _ in C_PAIRS)
        kv_specs = [pl.BlockSpec((None, 2 * C_GW, min(win, TM)),
                                 lambda n, j, first=NT - max(win // TM, 1): (n, 0, jnp.maximum(j - first, 0)))
                    for win, _ in C_PAIRS]
        kv_shapes = [jax.ShapeDtypeStruct((NB, 2 * C_GW, win), F32) for win, _ in C_PAIRS]
        kv_scratch = [pltpu.VMEM((TM, 1536), F32)]
    else:
        kv_specs = [tile(1536)]
        kv_shapes = [jax.ShapeDtypeStruct((NB, R, 1536), F32)]
        kv_scratch = []
    return pl.pallas_call(
        functools.partial(_cd_in_body, TM=TM, RS=RS, NT=NT, prompt_attn=prompt_attn),
        grid=(NB, NT),
        in_specs=[tile(D_MODEL),
                  pl.BlockSpec((None, 6, MR, D_MODEL), lambda n, j: (n, 0, 0, 0)),
                  _const_spec((4, D_MODEL)),
                  per_n,
                  _const_spec(w_in.shape), _const_spec(d_w.shape), _const_spec(d_b.shape),
                  _const_spec(ln_g.shape), _const_spec(ln_b.shape)] + perm_spec,
        out_specs=[tile(C_GW)] * 3 + [tile(2 * C_GW)] * 3 + kv_specs + [tile(D_WIDTH), per_n],
        out_shape=[jax.ShapeDtypeStruct((NB, R, C_GW), BF16)] * 3
                  + [jax.ShapeDtypeStruct((NB, R, 2 * C_GW), BF16)] * 3
                  + kv_shapes
                  + [jax.ShapeDtypeStruct((NB, R, D_WIDTH), BF16),
                     jax.ShapeDtypeStruct((NB, nd, D_WIDTH), F32)],
        scratch_shapes=[pltpu.VMEM((D_WIDTH // LANES, _ru8(nd) + TM, LANES), F32)] + kv_scratch,
        compiler_params=_cparams(2),
        name="cd_in",
    )(x, mod, ng, d_prev, w_in, d_w, d_b, ln_g, ln_b, *perm_in)


def _head_of_lane(shape):
    return lax.shift_right_logical(lax.broadcasted_iota(jnp.int32, shape, 1), 6)


def _attn_blocks(blocks):
    lane_head = _head_of_lane((Q_BLOCK, C_GW))
    first_head = lax.broadcasted_iota(jnp.int32, (Q_BLOCK, LANES), 1) < C_HEAD_DIM
    logits = []
    for q, kvp, kvc, _ in blocks:
        zero = jnp.zeros_like(q)
        qs = jnp.concatenate([jnp.where(lane_head == h, q, zero) for h in range(C_HPG)], axis=0)
        kcat = jnp.concatenate([kvp[:, 0:C_GW], kvc[:, 0:C_GW]], axis=0)
        logits.append(lax.dot_general(qs, kcat, (((1,), (1,)), ((), ())), preferred_element_type=F32))
    logits = [lg + bias for lg, (_, _, _, bias) in zip(logits, blocks)]
    ms = [jnp.max(lg, axis=-1, keepdims=True) for lg in logits]
    ps = [jnp.exp2(lg - m) for lg, m in zip(logits, ms)]
    ss = [jnp.sum(p, axis=-1, keepdims=True) for p in ps]
    pvs = []
    for (_, kvp, kvc, _), p in zip(blocks, ps):
        pb = p.astype(BF16)
        vlo = jnp.concatenate([kvp[:, C_GW:C_GW + LANES], kvc[:, C_GW:C_GW + LANES]], axis=0)
        vhi = jnp.concatenate([kvp[:, C_GW + LANES:], kvc[:, C_GW + LANES:]], axis=0)
        pvs.append((jnp.dot(pb[0:2 * Q_BLOCK], vlo, preferred_element_type=F32),
                    jnp.dot(pb[2 * Q_BLOCK:], vhi, preferred_element_type=F32)))
    outs = []
    for (lo, hi), m, s in zip(pvs, ms, ss):
        r = 1.0 / s
        lse = m + jnp.log2(s)
        pick = lambda a, b: jnp.where(first_head, a, b)
        rows = lambda x, h: x[h * Q_BLOCK:(h + 1) * Q_BLOCK]
        o = jnp.concatenate([pick(lo[0:Q_BLOCK] * rows(r, 0), lo[Q_BLOCK:] * rows(r, 1)),
                             pick(hi[0:Q_BLOCK] * rows(r, 2), hi[Q_BLOCK:] * rows(r, 3))], axis=-1)
        l = jnp.concatenate([pick(rows(lse, 0), rows(lse, 1)), pick(rows(lse, 2), rows(lse, 3))], axis=-1)
        outs.append((o, l))
    return outs


def _blocks_per_trip(count):
    return max(u for u in (5, 4, 3, 2, 1) if count % u == 0)


def _attn_prompt_body(q0, q1, q2, kv0, kv1, kv2, p0, p1, p2, bias_ref, yc_ref, osc, lsc, *, TM):
    qs_, kvs, prevs = (q0, q1, q2), (kv0, kv1, kv2), (p0, p1, p2)
    var = jnp.minimum(pl.program_id(1), 1)
    n_blocks = ATT_ROWS // Q_BLOCK

    def put(g, start, n, stride, o, l, r0):
        rows = pl.ds(start, n, stride=stride) if stride > 1 else pl.ds(start, n)
        for s in range(C_GW // LANES):
            osc[g, s, rows, :] = o[r0:r0 + n, s * LANES:(s + 1) * LANES]
            lsc[g, s, rows, :] = l[r0:r0 + n, s * LANES:(s + 1) * LANES]

    for g, (_, dil) in enumerate(C_PAIRS):
        q_ref, kv_ref, p_ref = qs_[g], kvs[g], prevs[g]
        cls = TM // dil
        if cls >= Q_BLOCK:
            prev_off = Q_BLOCK if dil == 1 else TM
            n_first = prev_off // Q_BLOCK
            per_tile = TM // Q_BLOCK

            def token_start(idx, dil=dil, per_tile=per_tile):
                return idx * Q_BLOCK if dil == 1 else (idx // per_tile) * TM + idx % per_tile

            first = []
            for idx in range(n_first):
                rows = slice(idx * Q_BLOCK, (idx + 1) * Q_BLOCK)
                first.append((q_ref[rows, :], p_ref[rows, :], kv_ref[rows, :], bias_ref[g, var]))
            for idx, (o, l) in enumerate(_attn_blocks(first)):
                put(g, token_start(idx), Q_BLOCK, dil, o, l, 0)

            per_trip = _blocks_per_trip(n_blocks - n_first)

            def body(trip, carry, g=g, dil=dil, q_ref=q_ref, kv_ref=kv_ref, prev_off=prev_off,
                     token_start=token_start, n_first=n_first, per_trip=per_trip):
                idxs = [n_first + trip * per_trip + u for u in range(per_trip)]
                blocks = []
                for idx in idxs:
                    st = pl.multiple_of(idx * Q_BLOCK, Q_BLOCK)
                    blocks.append((q_ref[pl.ds(st, Q_BLOCK), :], kv_ref[pl.ds(st - prev_off, Q_BLOCK), :],
                                   kv_ref[pl.ds(st, Q_BLOCK), :], bias_ref[g, 1]))
                for idx, (o, l) in zip(idxs, _attn_blocks(blocks)):
                    put(g, token_start(idx), Q_BLOCK, dil, o, l, 0)
                return carry

            lax.fori_loop(0, (n_blocks - n_first) // per_trip, body, 0)
        else:
            tiles = Q_BLOCK // cls
            per_trip = _blocks_per_trip(dil)

            def body(trip, carry, g=g, dil=dil, q_ref=q_ref, kv_ref=kv_ref, p_ref=p_ref, cls=cls, tiles=tiles,
                     per_trip=per_trip):
                rs = [trip * per_trip + u for u in range(per_trip)]
                blocks = []
                for r in rs:
                    st = pl.multiple_of(r * cls, cls)
                    gather = lambda ref, st=st: jnp.concatenate(
                        [ref[pl.ds(c * TM + st, cls), :] for c in range(tiles)], axis=0)
                    blocks.append((gather(q_ref), gather(p_ref), gather(kv_ref), bias_ref[g, var]))
                for r, (o, l) in zip(rs, _attn_blocks(blocks)):
                    for c in range(tiles):
                        put(g, c * TM + r, cls, dil, o, l, c * cls)
                return carry

            lax.fori_loop(0, dil // per_trip, body, 0)

    def merge(ch, carry):
        st = pl.multiple_of(ch * Q_BLOCK, Q_BLOCK)
        get = lambda ref, g: jnp.concatenate([ref[g, s, pl.ds(st, Q_BLOCK), :] for s in range(C_GW // LANES)], axis=-1)
        ls = [get(lsc, g) for g in range(len(C_PAIRS))]
        mm = jnp.maximum(jnp.maximum(ls[0], ls[1]), ls[2])
        es = [jnp.exp2(l - mm) for l in ls]
        den = es[0] + es[1] + es[2]
        yc = (es[0] / den) * get(osc, 0) + (es[1] / den) * get(osc, 1) + (es[2] / den) * get(osc, 2)
        yc_ref[pl.ds(st, Q_BLOCK), :] = yc.astype(BF16)
        return carry

    lax.fori_loop(0, n_blocks, merge, 0)


def _attn_prompt(qs, kvs, bias, *, TM):
    N, S, _ = qs[0].shape
    assert S % ATT_ROWS == 0 and ATT_ROWS == Q_BLOCK * C_PAIRS[-1][1] and ATT_ROWS % TM == 0
    cur = lambda c: pl.BlockSpec((None, ATT_ROWS, c), lambda n, i: (n, i, 0))
    prev_rows = [Q_BLOCK if dil == 1 else (TM if TM // dil >= Q_BLOCK else ATT_ROWS) for _, dil in C_PAIRS]
    prev = [pl.BlockSpec((None, pr, 2 * C_GW), lambda n, i, k=ATT_ROWS // pr: (n, jnp.maximum(i * k - 1, 0), 0))
            for pr in prev_rows]
    slabs = C_GW // LANES
    return pl.pallas_call(
        functools.partial(_attn_prompt_body, TM=TM),
        grid=(N, S // ATT_ROWS),
        in_specs=[cur(C_GW)] * 3 + [cur(2 * C_GW)] * 3 + prev + [_const_spec(bias.shape)],
        out_specs=cur(C_GW),
        out_shape=jax.ShapeDtypeStruct((N, S, C_GW), BF16),
        scratch_shapes=[pltpu.VMEM((len(C_PAIRS), slabs, ATT_ROWS, LANES), F32)] * 2,
        compiler_params=_cparams(2),
        name="attn_prompt",
    )(*qs, *kvs, *kvs, bias)


SROWS = 32


def _attn_sample_body(q_ref, kvn_ref, c0_ref, c1_ref, c2_ref, bc0_ref, bc1_ref, bc2_ref, bn_ref,
                      yc_ref, n0_ref, n1_ref, n2_ref):
    caches = (c0_ref, c1_ref, c2_ref)
    bcs = (bc0_ref, bc1_ref, bc2_ref)
    news = (n0_ref, n1_ref, n2_ref)
    lane_head = _head_of_lane((SROWS, C_GW))
    row_head = lax.shift_right_logical(lax.broadcasted_iota(jnp.int32, (SROWS, C_GW), 0), 3)
    own = lane_head == row_head
    lane128 = lax.broadcasted_iota(jnp.int32, (SROWS, 128), 1)
    kvn = kvn_ref[...]
    r0 = 8 - DEC_SEQ
    outs, lses = [], []
    for g, (win, dil) in enumerate(C_PAIRS):
        wb = win
        qg = q_ref[:, g * C_GW:(g + 1) * C_GW]
        qs = jnp.where(own, qg, jnp.zeros_like(qg))
        cache = caches[g]
        kt = cache[0:C_GW, :].astype(BF16)
        vt = cache[C_GW:2 * C_GW, :].astype(BF16)
        kn = kvn[:, g * C_GW:(g + 1) * C_GW]
        vn = kvn[:, 768 + g * C_GW:768 + (g + 1) * C_GW]
        lc = jnp.dot(qs, kt, preferred_element_type=F32) + bcs[g][...]
        qf = qs.astype(F32)
        ln = bn_ref[g]
        for c in range(DEC_SEQ):
            d = jnp.sum(qf * kn[r0 + c:r0 + c + 1, :], axis=-1, keepdims=True)
            ln = ln + jnp.where(lane128 == c, d, 0.0)
        m = jnp.maximum(jnp.max(lc, axis=-1, keepdims=True), jnp.max(ln, axis=-1, keepdims=True))
        pc = jnp.exp(lc - m)
        pn = jnp.exp(ln - m)
        s = jnp.sum(pc, axis=-1, keepdims=True) + jnp.sum(pn, axis=-1, keepdims=True)
        pv = lax.dot_general(pc.astype(BF16), vt, (((1,), (1,)), ((), ())), preferred_element_type=F32)
        for c in range(DEC_SEQ):
            pcol = jnp.sum(jnp.where(lane128 == c, pn, 0.0), axis=-1, keepdims=True)
            pv = pv + pcol * vn[r0 + c:r0 + c + 1, :]
        outs.append(pv / s)
        lses.append(m + jnp.log(s))
        rolled = pltpu.roll(cache[...], wb - DEC_SEQ, 1)
        new_rows = jnp.concatenate([jnp.zeros((128 - 8, 2 * C_GW), F32), jnp.concatenate([kn, vn], axis=-1)], axis=0)
        new_cols = new_rows.T
        lane_t = lax.broadcasted_iota(jnp.int32, (2 * C_GW, 128), 1)
        if wb > 128:
            news[g][:, 0:wb - 128] = rolled[:, 0:wb - 128]
        news[g][:, wb - 128:wb] = jnp.where(lane_t >= 128 - DEC_SEQ, new_cols, rolled[:, wb - 128:wb])
    mm = jnp.maximum(jnp.maximum(lses[0], lses[1]), lses[2])
    es = [jnp.exp(l - mm) for l in lses]
    den = es[0] + es[1] + es[2]
    y = (es[0] / den) * outs[0] + (es[1] / den) * outs[1] + (es[2] / den) * outs[2]
    y = jnp.where(own, y, 0.0)
    yc_ref[...] = y[0:8, :] + y[8:16, :] + y[16:24, :] + y[24:32, :]


def _attn_sample(q_rep, kvn, caches, bias_c, bias_n):
    NBt = q_rep.shape[0]
    per_b = lambda r, c: pl.BlockSpec((None, r, c), lambda b: (b, 0, 0))
    wbs = [w for w, _ in C_PAIRS]
    return pl.pallas_call(
        _attn_sample_body,
        grid=(NBt,),
        in_specs=[per_b(SROWS, 768), per_b(8, 1536)] + [per_b(512, w) for w in wbs]
                 + [_const_spec((SROWS, w)) for w in wbs] + [_const_spec((3, SROWS, 128))],
        out_specs=[per_b(8, C_GW)] + [per_b(512, w) for w in wbs],
        out_shape=[jax.ShapeDtypeStruct((NBt, 8, C_GW), F32)]
                  + [jax.ShapeDtypeStruct((NBt, 512, w), F32) for w in wbs],
        compiler_params=_cparams(1),
        name="attn_sample",
    )(q_rep, kvn, *caches, *bias_c, bias_n)


def _t5_bucket(dist):
    dist = np.asarray(dist)
    max_exact = N_BUCKETS // 2
    large = max_exact + (np.log(np.maximum(dist, max_exact) / max_exact) / np.log(MAX_DISTANCE / max_exact)
                         * (N_BUCKETS - max_exact)).astype(np.int32)
    large = np.minimum(large, N_BUCKETS - 1)
    return np.where(dist < max_exact, dist, large).astype(np.int32)


def _group_bias(rel_bias, g, dil):
    buckets = _t5_bucket(dil * np.arange(C_TAPS + 1))
    return rel_bias[buckets][:, g * C_HPG:(g + 1) * C_HPG].T


def _toeplitz_body(c_ref, o_ref):
    keep = lax.broadcasted_iota(jnp.int32, (Q_BLOCK, 2 * Q_BLOCK), 1) >= Q_BLOCK
    for h in range(C_HPG):
        taps = jnp.broadcast_to(c_ref[h:h + 1, :], (Q_BLOCK, 2 * Q_BLOCK))
        t = pltpu.roll(taps, 0, 1, stride=1, stride_axis=0)
        o_ref[1, h * Q_BLOCK:(h + 1) * Q_BLOCK, :] = t
        o_ref[0, h * Q_BLOCK:(h + 1) * Q_BLOCK, :] = jnp.where(keep, t, NEG)


def _prompt_bias(bias_gs):
    c = jnp.stack([jnp.concatenate([b[:, ::-1].astype(F32) * LOG2E, jnp.full((C_HPG, Q_BLOCK - 1), NEG, F32)], axis=1)
                   for b in bias_gs])
    return pl.pallas_call(
        _toeplitz_body,
        grid=(len(bias_gs),),
        in_specs=[pl.BlockSpec((None, C_HPG, 2 * Q_BLOCK), lambda g: (g, 0, 0))],
        out_specs=pl.BlockSpec((None, 2, C_HPG * Q_BLOCK, 2 * Q_BLOCK), lambda g: (g, 0, 0, 0)),
        out_shape=jax.ShapeDtypeStruct((len(bias_gs), 2, C_HPG * Q_BLOCK, 2 * Q_BLOCK), F32),
        compiler_params=_cparams(1),
        name="attn_bias",
    )(c)


def _sample_bias(bias_g, wb, dil):
    n = wb + DEC_SEQ
    t = np.arange(8)[:, None]
    d = wb + t - np.arange(n)[None, :]
    valid = (t < DEC_SEQ) & (d >= 0) & (d % dil == 0) & (d // dil <= C_TAPS)
    place = np.zeros((8, C_TAPS + 1, n), np.float32)
    ti, ii = np.nonzero(valid)
    place[ti, (d // dil)[ti, ii], ii] = 1.0
    ext = jnp.einsum("hj,tji->hti", bias_g.astype(F32), place, precision=lax.Precision.HIGHEST)
    fill = np.where((t >= DEC_SEQ) & (np.arange(n)[None, :] < wb), 0.0, NEG).astype(np.float32)
    ext = jnp.where(valid[None], ext, fill[None])
    bc = ext[:, :, :wb].reshape(SROWS, wb)
    bn = jnp.pad(ext[:, :, wb:], ((0, 0), (0, 0), (0, 128 - DEC_SEQ)), constant_values=NEG).reshape(SROWS, 128)
    return bc, bn


def _time_major(s):
    b, k, c = s.shape
    return s.transpose(1, 0, 2).reshape(1, k * b, c)


def _batch_major(s, b):
    _, r, c = s.shape
    return s.reshape(r // b, b, c).transpose(1, 0, 2)


def _stack(x, mod, st, w, *, TM, TM_WIDE, RS, pos0):
    NB = x.shape[0]
    new = {}
    ng = w["norm_g"]
    x, new["a"], new["b"] = _mixer_ab(x, mod[0], ng[0], st["a"], st["b"], w["ab_w_in"], w["a_conv_w"],
                                      w["b_w_grp"], w["b_scale"], w["ab_w_out"], TM=TM_WIDE, RS=RS, pos0=pos0)
    x, new["f0"] = _conv_ffn(x, mod[0], ng[0], st["f0"], w["ffn_w_up"], w["ffn_conv_w"],
                             w["ffn_conv_b"], w["ffn_w_down"], layer=0, TM=TM_WIDE, RS=RS)
    prompt = RS == 1
    outs = _cd_in(x, mod[1], ng[1], st["d"], w["cd_w_in"], w["d_conv_w"], w["d_conv_b"], w["d_ln_g"], w["d_ln_b"],
                  TM=TM, RS=RS, prompt_attn=prompt)
    qs, kvs, yd, new["d"] = outs[0:3], outs[3:6], outs[-2], outs[-1]
    cache_layout = lambda n: n.reshape(n.shape[0], 2, C_HPG, C_HEAD_DIM, n.shape[2]).transpose(0, 4, 1, 2, 3)
    if prompt:
        yc = _attn_prompt(qs, kvs, _prompt_bias(w["bias_g"]), TM=TM)
        for g in range(len(C_PAIRS)):
            new["c%d" % g] = cache_layout(outs[6 + g])
    else:
        B = RS
        kvf = outs[6]
        qb = _batch_major(jnp.concatenate(qs, axis=-1), B)
        q_rep = jnp.pad(jnp.broadcast_to(qb[:, None], (B, C_HPG, DEC_SEQ, 768)),
                        ((0, 0), (0, 0), (0, 8 - DEC_SEQ), (0, 0))).reshape(B, SROWS, 768)
        kvn = jnp.pad(_batch_major(kvf, B), ((0, 0), (8 - DEC_SEQ, 0), (0, 0)))
        bias = [_sample_bias(w["bias_g"][g], win, dil) for g, (win, dil) in enumerate(C_PAIRS)]
        yc, n0, n1, n2 = _attn_sample(q_rep, kvn, st["c"], [b[0] for b in bias],
                                      jnp.stack([b[1] for b in bias]))
        for g, n in enumerate((n0, n1, n2)):
            new["c%d" % g] = cache_layout(n)
        yc = _time_major(yc[:, :DEC_SEQ])
    x, new["f1"] = _conv_ffn(x, mod[1], ng[1], st["f1"], w["ffn_w_up"], w["ffn_conv_w"],
                             w["ffn_conv_b"], w["ffn_w_down"], layer=1, TM=TM_WIDE, RS=RS,
                             mixer_out=(yc, yd, w["cd_w_out"]))
    return x, new


def kernel(x_prompt, x_sample, state_a_conv, state_b_pool, cache_c_win128, cache_c_win512, cache_c_win2048,
           state_d_conv, state_ffn_conv, c_prompt, c_sample, ada_w, ada_b, norm_g, rel_bias, ab_w_in, a_conv_w,
           b_w_grp, b_scale, ab_w_out, cd_w_in, d_conv_w, d_conv_b, d_ln_g, d_ln_b, cd_w_out, ffn_w_up,
           ffn_conv_w, ffn_conv_b, ffn_w_down):
    B, T = DEC_BATCH, DEC_SEQ
    w = dict(norm_g=norm_g,
             ab_w_in=ab_w_in[0].astype(BF16), a_conv_w=a_conv_w[0], b_w_grp=b_w_grp[0].astype(BF16),
             b_scale=b_scale, ab_w_out=ab_w_out[0].astype(BF16),
             cd_w_in=cd_w_in[0].astype(BF16),
             d_conv_w=jnp.broadcast_to(d_conv_w[0][:, None, :], (D_CONV, SUBLANES, D_WIDTH)),
             d_conv_b=d_conv_b, d_ln_g=d_ln_g,
             d_ln_b=d_ln_b, cd_w_out=cd_w_out[0].astype(BF16),
             ffn_w_up=ffn_w_up.astype(BF16), ffn_conv_w=ffn_conv_w, ffn_conv_b=ffn_conv_b[:, None, :],
             ffn_w_down=ffn_w_down.astype(BF16),
             bias_g=[_group_bias(rel_bias, g, dil) for g, (_, dil) in enumerate(C_PAIRS)])

    mod = _ada(jnp.concatenate([c_prompt, c_sample], axis=0), ada_w, ada_b)
    mod_p = mod[:, :BATCH].reshape(DEPTH, BATCH, 6, 1, D_MODEL)
    mod_s = mod[:, BATCH:].reshape(DEPTH, B, 6, D_MODEL).transpose(0, 2, 1, 3)
    mod_s = jnp.broadcast_to(mod_s[:, :, None], (DEPTH, 6, T, B, D_MODEL)).reshape(DEPTH, 1, 6, T * B, D_MODEL)

    zeros = lambda k, c: jnp.zeros((BATCH, k, c), F32)
    st_p = dict(a=zeros(A_CONV - 1, A_WIDTH), b=zeros(B_PREV, B_WIDTH), d=zeros(D_CONV - 1, D_WIDTH),
                f0=zeros(FFN_CONV - 1, 2 * D_FF), f1=zeros(FFN_CONV - 1, 2 * D_FF))
    y_p, np_ = _stack(x_prompt, mod_p, st_p, w, TM=TM_PROMPT, TM_WIDE=TM_WIDE_PROMPT, RS=1, pos0=0)

    st_s = dict(a=_time_major(state_a_conv[0]), b=_time_major(state_b_pool[0]), d=_time_major(state_d_conv[0]),
                f0=_time_major(state_ffn_conv[0]), f1=_time_major(state_ffn_conv[1]),
                c=[c[0].transpose(0, 2, 3, 4, 1).reshape(B, 512, c.shape[2])
                   for c in (cache_c_win128, cache_c_win512, cache_c_win2048)])
    y_s, ns = _stack(_time_major(x_sample), mod_s, st_s, w, TM=T * B, TM_WIDE=T * B, RS=B, pos0=PAST_LEN)

    bm = lambda s: _batch_major(s, B)
    return (y_p, bm(y_s),
            np_["a"][None], bm(ns["a"])[None], np_["b"][None], bm(ns["b"])[None],
            np_["c0"][None], ns["c0"][None], np_["c1"][None], ns["c1"][None], np_["c2"][None], ns["c2"][None],
            np_["d"][None], bm(ns["d"])[None],
            jnp.stack([np_["f0"], np_["f1"]]), jnp.stack([bm(ns["f0"]), bm(ns["f1"])]))
```

```python
import functools
import math

import numpy as np
import jax
import jax.numpy as jnp
from jax import lax
from jax.experimental import pallas as pl
from jax.experimental.pallas import tpu as pltpu

D_MODEL = 1024
BATCH = 4
SEQ = 4096
DEPTH = 2
DEC_BATCH = 32
DEC_SEQ = 4
PAST_LEN = 8192
EPS = 1e-6
A_WIDTH = 512
A_CONV = 3
B_WIDTH = 512
B_WINDOWS = (2, 4, 8, 16)
B_GROUP = 128
B_PREV = 15
C_PAIRS = ((128, 1), (512, 4), (2048, 16))
C_HPG = 4
C_HEAD_DIM = 64
C_HEADS = 12
C_QKV = 2304
C_GW = C_HPG * C_HEAD_DIM
C_TAPS = 128
ATTN_SCALE = C_HEAD_DIM ** -0.5
LOG2E = math.log2(math.e)
Q_BLOCK = 128
N_BUCKETS = 32
MAX_DISTANCE = 2048
D_WIDTH = 512
D_CONV = 31
D_FF = 2816
FFN_CONV = 3

SUBLANES = 8
LANES = 128
VMEM_LIMIT = 56 * 1024 * 1024
NEG = -1e30
TM_PROMPT = 512
TM_WIDE_PROMPT = 1024
ATT_ROWS = 2048
FF_CHUNK = 256
D_ROWS = 32

F32 = jnp.float32
BF16 = jnp.bfloat16


def _ru8(n):
    return -(-n // SUBLANES) * SUBLANES


def _pool_levels(rs):
    l1 = _ru8(rs)
    l2 = _ru8(l1 + 2 * rs)
    l3 = _ru8(l2 + 4 * rs)
    l4 = _ru8(l3 + 8 * rs)
    return l1, l2, l3, l4


def _slab_put(ref, r0, val):
    n = val.shape[0]
    for s in range(val.shape[1] // LANES):
        ref[s, r0:r0 + n, :] = val[:, s * LANES:(s + 1) * LANES]


def _slab_get(ref, r0, n, s0=0, ns=None):
    ns = ref.shape[0] - s0 if ns is None else ns
    return jnp.concatenate([ref[s, r0:r0 + n, :] for s in range(s0, s0 + ns)], axis=-1)


def _rms(x, g):
    return x * lax.rsqrt(jnp.mean(x * x, axis=-1, keepdims=True) + EPS) * g


def _sigmoid(x):
    return 1.0 / (1.0 + jnp.exp(-x))


def _cparams(n_axes):
    return pltpu.CompilerParams(dimension_semantics=("arbitrary",) * n_axes, vmem_limit_bytes=VMEM_LIMIT)


def _const_spec(shape):
    nd = len(shape)
    return pl.BlockSpec(shape, lambda *_: (0,) * nd, pipeline_mode=pl.Buffered(1))


def _ada_body(c_ref, w_ref, b_ref, o_ref):
    c = c_ref[...]
    ca = c * _sigmoid(c)
    o_ref[...] = jnp.dot(ca.astype(BF16), w_ref[...].astype(BF16), preferred_element_type=F32) + b_ref[...]


def _ada(c_all, ada_w, ada_b):
    rows = c_all.shape[0]
    tn = 1536
    return pl.pallas_call(
        _ada_body,
        grid=(DEPTH, 6 * D_MODEL // tn),
        in_specs=[pl.BlockSpec((rows, D_MODEL), lambda l, n: (0, 0)),
                  pl.BlockSpec((None, D_MODEL, tn), lambda l, n: (l, 0, n)),
                  pl.BlockSpec((None, 1, tn), lambda l, n: (l, 0, n))],
        out_specs=pl.BlockSpec((None, rows, tn), lambda l, n: (l, 0, n)),
        out_shape=jax.ShapeDtypeStruct((DEPTH, rows, 6 * D_MODEL), F32),
        compiler_params=_cparams(2),
        name="ada",
    )(c_all, ada_w, ada_b.reshape(DEPTH, 1, 6 * D_MODEL))


def _ab_body(*refs, TM, RS, NT, pos0, n_cast):
    (x_ref, mod_ref, ng_ref, ap_ref, bp_ref, win_ref, aw_ref, bw_ref, bs_ref, wout_ref) = refs[:10]
    cast_in = refs[10:10 + n_cast]
    x1_ref, an_ref, bn_ref = refs[10 + n_cast:13 + n_cast]
    cast_out = refs[13 + n_cast:13 + 2 * n_cast]
    ea, eb, s2, s4, s8 = refs[13 + 2 * n_cast:]
    for src, dst in zip(cast_in, cast_out):
        dst[...] = src[...].astype(BF16)
    j = pl.program_id(1)
    HA = _ru8((A_CONV - 1) * RS)
    L1, L2, L3, HB = _pool_levels(RS)
    E = HB + TM
    na, nbp = (A_CONV - 1) * RS, B_PREV * RS

    @pl.when(j == 0)
    def _():
        _slab_put(ea, HA - na, ap_ref[...])
        if HB > nbp:
            _slab_put(eb, 0, jnp.zeros((HB - nbp, B_WIDTH), F32))
        _slab_put(eb, HB - nbp, bp_ref[...])

    x = x_ref[...]
    sh1, sc1, g1 = mod_ref[0], mod_ref[1], mod_ref[2]
    h = _rms(x, ng_ref[0:1, :]) * (1.0 + sc1) + sh1
    proj = jnp.dot(h.astype(BF16), win_ref[...], preferred_element_type=F32)
    hh, bg = proj[:, 0:A_WIDTH], proj[:, A_WIDTH:2 * A_WIDTH]
    cg, u = proj[:, 2 * A_WIDTH:3 * A_WIDTH], proj[:, 3 * A_WIDTH:]

    v = cg * hh
    _slab_put(ea, HA, v)
    z = (aw_ref[2:3, :] * v + aw_ref[1:2, :] * _slab_get(ea, HA - RS, TM)
         + aw_ref[0:1, :] * _slab_get(ea, HA - 2 * RS, TM))
    ya = bg * z
    a_last = _slab_get(ea, HA + TM - na, na)
    an_ref[...] = a_last
    if NT > 1:
        _slab_put(ea, HA - na, a_last)

    _slab_put(eb, HB, u)
    _slab_put(s2, L1, _slab_get(eb, L1, E - L1) + _slab_get(eb, L1 - RS, E - L1))
    _slab_put(s4, L2, _slab_get(s2, L2, E - L2, 1) + _slab_get(s2, L2 - 2 * RS, E - L2, 1))
    _slab_put(s8, L3, _slab_get(s4, L3, E - L3, 1) + _slab_get(s4, L3 - 4 * RS, E - L3, 1))
    wsum = (s2[0, HB:E, :], s4[0, HB:E, :], s8[0, HB:E, :], s8[1, HB:E, :] + s8[1, HB - 8 * RS:E - 8 * RS, :])
    b_last = _slab_get(eb, E - nbp, nbp)
    bn_ref[...] = b_last
    if NT > 1:
        _slab_put(eb, HB - nbp, b_last)
    row = lax.broadcasted_iota(jnp.int32, (TM, B_GROUP), 0) + j * TM
    pos1 = lax.shift_right_logical(row, int(math.log2(RS))) + (pos0 + 1)
    ybs = []
    for g, win in enumerate(B_WINDOWS):
        cnt = jnp.minimum(pos1, win).astype(F32)
        pooled = wsum[g] / cnt - u[:, g * B_GROUP:(g + 1) * B_GROUP]
        ybs.append(jnp.dot(pooled.astype(BF16), bw_ref[g], preferred_element_type=F32))
    yb = jnp.concatenate(ybs, axis=-1) * bs_ref[...]

    ycat = jnp.concatenate([ya, yb], axis=-1).astype(BF16)
    y = jnp.dot(ycat, wout_ref[...], preferred_element_type=F32)
    x1_ref[...] = x + g1 * _rms(y, ng_ref[1:2, :])


def _mixer_ab(x, mod, ng, a_prev, b_prev, w_in, a_w, b_w, b_scale, w_out, *, TM, RS, pos0, cast=()):
    NB, R, _ = x.shape
    NT = R // TM
    MR = mod.shape[2]
    steps = NB * NT
    cast_specs, cast_shapes = [], []
    for a in cast:
        rows = a.shape[1] // steps
        assert rows * steps == a.shape[1] and rows % (2 * SUBLANES) == 0
        cast_specs.append(pl.BlockSpec((a.shape[0], rows, a.shape[2]), lambda n, j: (0, n * NT + j, 0)))
        cast_shapes.append(jax.ShapeDtypeStruct(a.shape, BF16))
    _, _, _, HB = _pool_levels(RS)
    HA = _ru8((A_CONV - 1) * RS)
    E = HB + TM
    na, nbp = (A_CONV - 1) * RS, B_PREV * RS
    tile = lambda c: pl.BlockSpec((None, TM, c), lambda n, j: (n, j, 0))
    per_n = lambda r, c: pl.BlockSpec((None, r, c), lambda n, j: (n, 0, 0))
    outs = pl.pallas_call(
        functools.partial(_ab_body, TM=TM, RS=RS, NT=NT, pos0=pos0, n_cast=len(cast)),
        grid=(NB, NT),
        in_specs=[tile(D_MODEL),
                  pl.BlockSpec((None, 6, MR, D_MODEL), lambda n, j: (n, 0, 0, 0)),
                  _const_spec((4, D_MODEL)),
                  per_n(na, A_WIDTH), per_n(nbp, B_WIDTH),
                  _const_spec(w_in.shape), _const_spec(a_w.shape), _const_spec(b_w.shape),
                  _const_spec(b_scale.shape), _const_spec(w_out.shape)] + cast_specs,
        out_specs=[tile(D_MODEL), per_n(na, A_WIDTH), per_n(nbp, B_WIDTH)] + cast_specs,
        out_shape=[jax.ShapeDtypeStruct((NB, R, D_MODEL), F32),
                   jax.ShapeDtypeStruct((NB, na, A_WIDTH), F32),
                   jax.ShapeDtypeStruct((NB, nbp, B_WIDTH), F32)] + cast_shapes,
        scratch_shapes=[pltpu.VMEM((4, HA + TM, LANES), F32), pltpu.VMEM((4, E, LANES), F32),
                        pltpu.VMEM((4, E, LANES), F32), pltpu.VMEM((3, E, LANES), F32),
                        pltpu.VMEM((2, E, LANES), F32)],
        compiler_params=_cparams(2),
        name="mixer_ab",
    )(x, mod, ng, a_prev, b_prev, w_in, a_w, b_w, b_scale, w_out, *cast)
    return outs[0], outs[1], outs[2], tuple(outs[3:])


def _ffn_body(*refs, TM, RS, NT, mixer_out):
    if mixer_out:
        (x_ref, mod_ref, ng_ref, fp_ref, wup_ref, cw_ref, cb_ref, wdn_ref, yc_ref, yd_ref, wout_ref,
         y_ref, fn_ref, halo, ext, act) = refs
    else:
        (x_ref, mod_ref, ng_ref, fp_ref, wup_ref, cw_ref, cb_ref, wdn_ref,
         y_ref, fn_ref, halo, ext, act) = refs
    j = pl.program_id(1)
    nf = (FFN_CONV - 1) * RS
    HF = _ru8(nf)

    @pl.when(j == 0)
    def _():
        halo[...] = fp_ref[...]

    x = x_ref[...]
    if mixer_out:
        ym = (jnp.dot(yc_ref[...].astype(BF16), wout_ref[0:C_GW, :], preferred_element_type=F32)
              + jnp.dot(yd_ref[...], wout_ref[C_GW:, :], preferred_element_type=F32))
        x = x + mod_ref[2] * _rms(ym, ng_ref[1:2, :])
    sh2, sc2, g2 = mod_ref[3], mod_ref[4], mod_ref[5]
    hb = (_rms(x, ng_ref[2:3, :]) * (1.0 + sc2) + sh2).astype(BF16)

    def conv_part(col, buf):
        up = jnp.dot(hb, wup_ref[:, col:col + FF_CHUNK], preferred_element_type=F32)
        _slab_put(buf, HF - nf, halo[:, col:col + FF_CHUNK])
        _slab_put(buf, HF, up)
        out = (cw_ref[2:3, col:col + FF_CHUNK] * up
               + cw_ref[1:2, col:col + FF_CHUNK] * _slab_get(buf, HF - RS, TM)
               + cw_ref[0:1, col:col + FF_CHUNK] * _slab_get(buf, HF - 2 * RS, TM)
               + cb_ref[:, col:col + FF_CHUNK])
        halo[:, col:col + FF_CHUNK] = _slab_get(buf, HF + TM - nf, nf)
        return out

    for c in range(D_FF // FF_CHUNK):
        a = conv_part(c * FF_CHUNK, ext.at[2 * (c % 2)])
        g = conv_part(D_FF + c * FF_CHUNK, ext.at[2 * (c % 2) + 1])
        act[:, c * FF_CHUNK:(c + 1) * FF_CHUNK] = (a * (g * _sigmoid(g))).astype(BF16)
    fn_ref[...] = halo[...]
    y = jnp.dot(act[...], wdn_ref[...], preferred_element_type=F32)
    y_ref[...] = x + g2 * _rms(y, ng_ref[3:4, :])


def _layer_spec(shape, layer):
    nd = len(shape) - 1
    return pl.BlockSpec((None,) + tuple(shape[1:]), lambda *_: (layer,) + (0,) * nd, pipeline_mode=pl.Buffered(1))


def _conv_ffn(x, mod, ng, f_prev, w_up, conv_w, conv_b, w_down, *, layer, TM, RS, mixer_out=None):
    NB, R, _ = x.shape
    NT = R // TM
    MR = mod.shape[2]
    nf = (FFN_CONV - 1) * RS
    tile_c = lambda c: pl.BlockSpec((None, TM, c), lambda n, j: (n, j, 0))
    tile = tile_c(D_MODEL)
    per_n = pl.BlockSpec((None, nf, 2 * D_FF), lambda n, j: (n, 0, 0))
    mix_in, mix_specs = [], []
    if mixer_out is not None:
        mix_in = list(mixer_out)
        mix_specs = [tile_c(C_GW), tile_c(D_WIDTH), _const_spec(mixer_out[2].shape)]
    return pl.pallas_call(
        functools.partial(_ffn_body, TM=TM, RS=RS, NT=NT, mixer_out=mixer_out is not None),
        grid=(NB, NT),
        in_specs=[tile,
                  pl.BlockSpec((None, 6, MR, D_MODEL), lambda n, j: (n, 0, 0, 0)),
                  _const_spec((4, D_MODEL)),
                  per_n,
                  _layer_spec(w_up.shape, layer), _layer_spec(conv_w.shape, layer),
                  _layer_spec(conv_b.shape, layer), _layer_spec(w_down.shape, layer)] + mix_specs,
        out_specs=[tile, per_n],
        out_shape=[jax.ShapeDtypeStruct((NB, R, D_MODEL), F32),
                   jax.ShapeDtypeStruct((NB, nf, 2 * D_FF), F32)],
        scratch_shapes=[pltpu.VMEM((nf, 2 * D_FF), F32),
                        pltpu.VMEM((4, FF_CHUNK // LANES, _ru8(nf) + TM, LANES), F32),
                        pltpu.VMEM((TM, D_FF), BF16)],
        compiler_params=_cparams(2),
        name="conv_ffn",
    )(x, mod, ng, f_prev, w_up, conv_w, conv_b, w_down, *mix_in)


def _cd_in_body(*refs, TM, RS, NT, prompt_attn):
    if prompt_attn:
        (x_ref, mod_ref, ng_ref, dp_ref, win_ref, dw_ref, db_ref, lg_ref, lb_ref, perm_ref,
         q0_ref, q1_ref, q2_ref, kv0_ref, kv1_ref, kv2_ref, c0_ref, c1_ref, c2_ref, yd_ref, dn_ref,
         ed, kvf_ref) = refs
        cache_refs = (c0_ref, c1_ref, c2_ref)
    else:
        (x_ref, mod_ref, ng_ref, dp_ref, win_ref, dw_ref, db_ref, lg_ref, lb_ref,
         q0_ref, q1_ref, q2_ref, kv0_ref, kv1_ref, kv2_ref, kvf_ref, yd_ref, dn_ref, ed) = refs
    q_refs, kv_refs = (q0_ref, q1_ref, q2_ref), (kv0_ref, kv1_ref, kv2_ref)
    j = pl.program_id(1)
    nd = (D_CONV - 1) * RS
    HD = _ru8(nd)

    @pl.when(j == 0)
    def _():
        _slab_put(ed, HD - nd, dp_ref[...])

    x = x_ref[...]
    sh1, sc1 = mod_ref[0], mod_ref[1]
    h = _rms(x, ng_ref[0:1, :]) * (1.0 + sc1) + sh1
    proj = jnp.dot(h.astype(BF16), win_ref[...], preferred_element_type=F32)
    qs = (proj[:, 0:768] * (ATTN_SCALE * LOG2E if prompt_attn else ATTN_SCALE)).astype(BF16)
    ks = proj[:, 768:1536].astype(BF16)
    vs = proj[:, 1536:2304].astype(BF16)
    for g in range(len(C_PAIRS)):
        cols = slice(g * C_GW, (g + 1) * C_GW)
        xg = jnp.concatenate([qs[:, cols], ks[:, cols], vs[:, cols]], axis=-1)
        if prompt_attn and C_PAIRS[g][1] > 1:
            xg = jnp.dot(perm_ref[g - 1], xg, preferred_element_type=F32).astype(BF16)
        q_refs[g][...] = xg[:, 0:C_GW]
        kv_refs[g][...] = xg[:, C_GW:]
    kvf_ref[...] = proj[:, 768:2304]
    dv, dg = proj[:, C_QKV:C_QKV + D_WIDTH], proj[:, C_QKV + D_WIDTH:]
    _slab_put(ed, HD, dv * _sigmoid(dg))

    db, lg, lb = db_ref[...], lg_ref[...], lb_ref[...]
    tiles = D_ROWS // SUBLANES
    for c in range(TM // D_ROWS):
        accs = []
        for s in range(D_WIDTH // LANES):
            acc = jnp.zeros((tiles, SUBLANES, LANES), F32)
            for kk in range(D_CONV):
                off = HD - (D_CONV - 1 - kk) * RS + c * D_ROWS
                tap = ed[s, off:off + D_ROWS, :].reshape(tiles, SUBLANES, LANES)
                acc = acc + dw_ref[kk, :, s * LANES:(s + 1) * LANES] * tap
            accs.append(acc.reshape(D_ROWS, LANES))
        zc = jnp.concatenate(accs, axis=-1) + db
        mu = jnp.mean(zc, axis=-1, keepdims=True)
        zc = zc - mu
        var = jnp.mean(zc * zc, axis=-1, keepdims=True)
        yl = zc * lax.rsqrt(var + EPS) * lg + lb
        yd_ref[c * D_ROWS:(c + 1) * D_ROWS, :] = (yl * _sigmoid(yl)).astype(BF16)

    d_last = _slab_get(ed, HD + TM - nd, nd)
    dn_ref[...] = d_last
    if NT > 1:
        _slab_put(ed, HD - nd, d_last)

    if prompt_attn:
        for g, (win, _) in enumerate(C_PAIRS):
            cols = min(win, TM)

            @pl.when(j >= NT - max(win // TM, 1))
            def _(g=g, cols=cols):
                kt = kvf_ref[:, g * C_GW:(g + 1) * C_GW].T
                vt = kvf_ref[:, 768 + g * C_GW:768 + (g + 1) * C_GW].T
                cache_refs[g][0:C_GW, :] = kt[:, TM - cols:]
                cache_refs[g][C_GW:, :] = vt[:, TM - cols:]


def _residue_perm(tm, dil):
    p = np.zeros((tm, tm), np.float32)
    a, r = np.meshgrid(np.arange(tm // dil), np.arange(dil), indexing="ij")
    p[(r * (tm // dil) + a).ravel(), (dil * a + r).ravel()] = 1.0
    return p


def _cd_in(x, mod, ng, d_prev, w_in, d_w, d_b, ln_g, ln_b, *, TM, RS, prompt_attn):
    NB, R, _ = x.shape
    NT = R // TM
    MR = mod.shape[2]
    nd = (D_CONV - 1) * RS
    tile = lambda c: pl.BlockSpec((None, TM, c), lambda n, j: (n, j, 0))
    per_n = pl.BlockSpec((None, nd, D_WIDTH), lambda n, j: (n, 0, 0))
    perm_in, perm_spec = [], []
    if prompt_attn:
        perm_in = [jnp.asarray(np.stack([_residue_perm(TM, dil) for _, dil in C_PAIRS[1:]]), BF16)]
        perm_spec = [_const_spec((len(C_PAIRS) - 1, TM, TM))]
        assert all(win <= R and (win % TM == 0 or TM % win == 0) for win, _ in C_PAIRS)
        kv_specs = [pl.BlockSpec((None, 2 * C_GW, min(win, TM)),
                                 lambda n, j, first=NT - max(win // TM, 1): (n, 0, jnp.maximum(j - first, 0)))
                    for win, _ in C_PAIRS]
        kv_shapes = [jax.ShapeDtypeStruct((NB, 2 * C_GW, win), F32) for win, _ in C_PAIRS]
        kv_scratch = [pltpu.VMEM((TM, 1536), F32)]
    else:
        kv_specs = [tile(1536)]
        kv_shapes = [jax.ShapeDtypeStruct((NB, R, 1536), F32)]
        kv_scratch = []
    return pl.pallas_call(
        functools.partial(_cd_in_body, TM=TM, RS=RS, NT=NT, prompt_attn=prompt_attn),
        grid=(NB, NT),
        in_specs=[tile(D_MODEL),
                  pl.BlockSpec((None, 6, MR, D_MODEL), lambda n, j: (n, 0, 0, 0)),
                  _const_spec((4, D_MODEL)),
                  per_n,
                  _const_spec(w_in.shape), _const_spec(d_w.shape), _const_spec(d_b.shape),
                  _const_spec(ln_g.shape), _const_spec(ln_b.shape)] + perm_spec,
        out_specs=[tile(C_GW)] * 3 + [tile(2 * C_GW)] * 3 + kv_specs + [tile(D_WIDTH), per_n],
        out_shape=[jax.ShapeDtypeStruct((NB, R, C_GW), BF16)] * 3
                  + [jax.ShapeDtypeStruct((NB, R, 2 * C_GW), BF16)] * 3
                  + kv_shapes
                  + [jax.ShapeDtypeStruct((NB, R, D_WIDTH), BF16),
                     jax.ShapeDtypeStruct((NB, nd, D_WIDTH), F32)],
        scratch_shapes=[pltpu.VMEM((D_WIDTH // LANES, _ru8(nd) + TM, LANES), F32)] + kv_scratch,
        compiler_params=_cparams(2),
        name="cd_in",
    )(x, mod, ng, d_prev, w_in, d_w, d_b, ln_g, ln_b, *perm_in)


def _head_of_lane(shape):
    return lax.shift_right_logical(lax.broadcasted_iota(jnp.int32, shape, 1), 6)


def _attn_blocks(blocks):
    lane_head = _head_of_lane((Q_BLOCK, C_GW))
    first_head = lax.broadcasted_iota(jnp.int32, (Q_BLOCK, LANES), 1) < C_HEAD_DIM
    logits = []
    for q, kvp, kvc, _ in blocks:
        zero = jnp.zeros_like(q)
        qs = jnp.concatenate([jnp.where(lane_head == h, q, zero) for h in range(C_HPG)], axis=0)
        kcat = jnp.concatenate([kvp[:, 0:C_GW], kvc[:, 0:C_GW]], axis=0)
        logits.append(lax.dot_general(qs, kcat, (((1,), (1,)), ((), ())), preferred_element_type=F32))
    logits = [lg + bias for lg, (_, _, _, bias) in zip(logits, blocks)]
    ms = [jnp.max(lg, axis=-1, keepdims=True) for lg in logits]
    ps = [jnp.exp2(lg - m) for lg, m in zip(logits, ms)]
    ss = [jnp.sum(p, axis=-1, keepdims=True) for p in ps]
    pvs = []
    for (_, kvp, kvc, _), p in zip(blocks, ps):
        pb = p.astype(BF16)
        vlo = jnp.concatenate([kvp[:, C_GW:C_GW + LANES], kvc[:, C_GW:C_GW + LANES]], axis=0)
        vhi = jnp.concatenate([kvp[:, C_GW + LANES:], kvc[:, C_GW + LANES:]], axis=0)
        pvs.append((jnp.dot(pb[0:2 * Q_BLOCK], vlo, preferred_element_type=F32),
                    jnp.dot(pb[2 * Q_BLOCK:], vhi, preferred_element_type=F32)))
    outs = []
    for (lo, hi), m, s in zip(pvs, ms, ss):
        r = 1.0 / s
        lse = m + jnp.log2(s)
        pick = lambda a, b: jnp.where(first_head, a, b)
        rows = lambda x, h: x[h * Q_BLOCK:(h + 1) * Q_BLOCK]
        o = jnp.concatenate([pick(lo[0:Q_BLOCK] * rows(r, 0), lo[Q_BLOCK:] * rows(r, 1)),
                             pick(hi[0:Q_BLOCK] * rows(r, 2), hi[Q_BLOCK:] * rows(r, 3))], axis=-1)
        l = jnp.concatenate([pick(rows(lse, 0), rows(lse, 1)), pick(rows(lse, 2), rows(lse, 3))], axis=-1)
        outs.append((o, l))
    return outs


def _blocks_per_trip(count):
    return max(u for u in (5, 4, 3, 2, 1) if count % u == 0)


def _attn_prompt_body(q0, q1, q2, kv0, kv1, kv2, p0, p1, p2, bias_ref, yc_ref, osc, lsc, *, TM):
    qs_, kvs, prevs = (q0, q1, q2), (kv0, kv1, kv2), (p0, p1, p2)
    var = jnp.minimum(pl.program_id(1), 1)
    n_blocks = ATT_ROWS // Q_BLOCK

    def put(g, start, n, stride, o, l, r0):
        rows = pl.ds(start, n, stride=stride) if stride > 1 else pl.ds(start, n)
        for s in range(C_GW // LANES):
            osc[g, s, rows, :] = o[r0:r0 + n, s * LANES:(s + 1) * LANES]
            lsc[g, s, rows, :] = l[r0:r0 + n, s * LANES:(s + 1) * LANES]

    for g, (_, dil) in enumerate(C_PAIRS):
        q_ref, kv_ref, p_ref = qs_[g], kvs[g], prevs[g]
        cls = TM // dil
        if cls >= Q_BLOCK:
            prev_off = Q_BLOCK if dil == 1 else TM
            n_first = prev_off // Q_BLOCK
            per_tile = TM // Q_BLOCK

            def token_start(idx, dil=dil, per_tile=per_tile):
                return idx * Q_BLOCK if dil == 1 else (idx // per_tile) * TM + idx % per_tile

            first = []
            for idx in range(n_first):
                rows = slice(idx * Q_BLOCK, (idx + 1) * Q_BLOCK)
                first.append((q_ref[rows, :], p_ref[rows, :], kv_ref[rows, :], bias_ref[g, var]))
            for idx, (o, l) in enumerate(_attn_blocks(first)):
                put(g, token_start(idx), Q_BLOCK, dil, o, l, 0)

            per_trip = _blocks_per_trip(n_blocks - n_first)

            def body(trip, carry, g=g, dil=dil, q_ref=q_ref, kv_ref=kv_ref, prev_off=prev_off,
                     token_start=token_start, n_first=n_first, per_trip=per_trip):
                idxs = [n_first + trip * per_trip + u for u in range(per_trip)]
                blocks = []
                for idx in idxs:
                    st = pl.multiple_of(idx * Q_BLOCK, Q_BLOCK)
                    blocks.append((q_ref[pl.ds(st, Q_BLOCK), :], kv_ref[pl.ds(st - prev_off, Q_BLOCK), :],
                                   kv_ref[pl.ds(st, Q_BLOCK), :], bias_ref[g, 1]))
                for idx, (o, l) in zip(idxs, _attn_blocks(blocks)):
                    put(g, token_start(idx), Q_BLOCK, dil, o, l, 0)
                return carry

            lax.fori_loop(0, (n_blocks - n_first) // per_trip, body, 0)
        else:
            tiles = Q_BLOCK // cls
            per_trip = _blocks_per_trip(dil)

            def body(trip, carry, g=g, dil=dil, q_ref=q_ref, kv_ref=kv_ref, p_ref=p_ref, cls=cls, tiles=tiles,
                     per_trip=per_trip):
                rs = [trip * per_trip + u for u in range(per_trip)]
                blocks = []
                for r in rs:
                    st = pl.multiple_of(r * cls, cls)
                    gather = lambda ref, st=st: jnp.concatenate(
                        [ref[pl.ds(c * TM + st, cls), :] for c in range(tiles)], axis=0)
                    blocks.append((gather(q_ref), gather(p_ref), gather(kv_ref), bias_ref[g, var]))
                for r, (o, l) in zip(rs, _attn_blocks(blocks)):
                    for c in range(tiles):
                        put(g, c * TM + r, cls, dil, o, l, c * cls)
                return carry

            lax.fori_loop(0, dil // per_trip, body, 0)

    def merge(ch, carry):
        st = pl.multiple_of(ch * Q_BLOCK, Q_BLOCK)
        get = lambda ref, g: jnp.concatenate([ref[g, s, pl.ds(st, Q_BLOCK), :] for s in range(C_GW // LANES)], axis=-1)
        ls = [get(lsc, g) for g in range(len(C_PAIRS))]
        mm = jnp.maximum(jnp.maximum(ls[0], ls[1]), ls[2])
        es = [jnp.exp2(l - mm) for l in ls]
        den = es[0] + es[1] + es[2]
        yc = (es[0] / den) * get(osc, 0) + (es[1] / den) * get(osc, 1) + (es[2] / den) * get(osc, 2)
        yc_ref[pl.ds(st, Q_BLOCK), :] = yc.astype(BF16)
        return carry

    lax.fori_loop(0, n_blocks, merge, 0)


def _attn_prompt(qs, kvs, bias, *, TM):
    N, S, _ = qs[0].shape
    assert S % ATT_ROWS == 0 and ATT_ROWS == Q_BLOCK * C_PAIRS[-1][1] and ATT_ROWS % TM == 0
    cur = lambda c: pl.BlockSpec((None, ATT_ROWS, c), lambda n, i: (n, i, 0))
    prev_rows = [Q_BLOCK if dil == 1 else (TM if TM // dil >= Q_BLOCK else ATT_ROWS) for _, dil in C_PAIRS]
    prev = [pl.BlockSpec((None, pr, 2 * C_GW), lambda n, i, k=ATT_ROWS // pr: (n, jnp.maximum(i * k - 1, 0), 0))
            for pr in prev_rows]
    slabs = C_GW // LANES
    return pl.pallas_call(
        functools.partial(_attn_prompt_body, TM=TM),
        grid=(N, S // ATT_ROWS),
        in_specs=[cur(C_GW)] * 3 + [cur(2 * C_GW)] * 3 + prev + [_const_spec(bias.shape)],
        out_specs=cur(C_GW),
        out_shape=jax.ShapeDtypeStruct((N, S, C_GW), BF16),
        scratch_shapes=[pltpu.VMEM((len(C_PAIRS), slabs, ATT_ROWS, LANES), F32)] * 2,
        compiler_params=_cparams(2),
        name="attn_prompt",
    )(*qs, *kvs, *kvs, bias)


SROWS = 32


def _attn_sample_body(q_ref, kvn_ref, c0_ref, c1_ref, c2_ref, bc0_ref, bc1_ref, bc2_ref, bn_ref,
                      yc_ref, n0_ref, n1_ref, n2_ref):
    caches = (c0_ref, c1_ref, c2_ref)
    bcs = (bc0_ref, bc1_ref, bc2_ref)
    news = (n0_ref, n1_ref, n2_ref)
    lane_head = _head_of_lane((SROWS, C_GW))
    row_head = lax.shift_right_logical(lax.broadcasted_iota(jnp.int32, (SROWS, C_GW), 0), 3)
    own = lane_head == row_head
    lane128 = lax.broadcasted_iota(jnp.int32, (SROWS, 128), 1)
    kvn = kvn_ref[...]
    r0 = 8 - DEC_SEQ
    outs, lses = [], []
    for g, (win, dil) in enumerate(C_PAIRS):
        wb = win
        qg = q_ref[:, g * C_GW:(g + 1) * C_GW]
        qs = jnp.where(own, qg, jnp.zeros_like(qg))
        cache = caches[g]
        kt = cache[0:C_GW, :].astype(BF16)
        vt = cache[C_GW:2 * C_GW, :].astype(BF16)
        kn = kvn[:, g * C_GW:(g + 1) * C_GW]
        vn = kvn[:, 768 + g * C_GW:768 + (g + 1) * C_GW]
        lc = jnp.dot(qs, kt, preferred_element_type=F32) + bcs[g][...]
        qf = qs.astype(F32)
        ln = bn_ref[g]
        for c in range(DEC_SEQ):
            d = jnp.sum(qf * kn[r0 + c:r0 + c + 1, :], axis=-1, keepdims=True)
            ln = ln + jnp.where(lane128 == c, d, 0.0)
        m = jnp.maximum(jnp.max(lc, axis=-1, keepdims=True), jnp.max(ln, axis=-1, keepdims=True))
        pc = jnp.exp(lc - m)
        pn = jnp.exp(ln - m)
        s = jnp.sum(pc, axis=-1, keepdims=True) + jnp.sum(pn, axis=-1, keepdims=True)
        pv = lax.dot_general(pc.astype(BF16), vt, (((1,), (1,)), ((), ())), preferred_element_type=F32)
        for c in range(DEC_SEQ):
            pcol = jnp.sum(jnp.where(lane128 == c, pn, 0.0), axis=-1, keepdims=True)
            pv = pv + pcol * vn[r0 + c:r0 + c + 1, :]
        outs.append(pv / s)
        lses.append(m + jnp.log(s))
        rolled = pltpu.roll(cache[...], wb - DEC_SEQ, 1)
        new_rows = jnp.concatenate([jnp.zeros((128 - 8, 2 * C_GW), F32), jnp.concatenate([kn, vn], axis=-1)], axis=0)
        new_cols = new_rows.T
        lane_t = lax.broadcasted_iota(jnp.int32, (2 * C_GW, 128), 1)
        if wb > 128:
            news[g][:, 0:wb - 128] = rolled[:, 0:wb - 128]
        news[g][:, wb - 128:wb] = jnp.where(lane_t >= 128 - DEC_SEQ, new_cols, rolled[:, wb - 128:wb])
    mm = jnp.maximum(jnp.maximum(lses[0], lses[1]), lses[2])
    es = [jnp.exp(l - mm) for l in lses]
    den = es[0] + es[1] + es[2]
    y = (es[0] / den) * outs[0] + (es[1] / den) * outs[1] + (es[2] / den) * outs[2]
    y = jnp.where(own, y, 0.0)
    yc_ref[...] = y[0:8, :] + y[8:16, :] + y[16:24, :] + y[24:32, :]


def _attn_sample(q_rep, kvn, caches, bias_c, bias_n):
    NBt = q_rep.shape[0]
    per_b = lambda r, c: pl.BlockSpec((None, r, c), lambda b: (b, 0, 0))
    wbs = [w for w, _ in C_PAIRS]
    return pl.pallas_call(
        _attn_sample_body,
        grid=(NBt,),
        in_specs=[per_b(SROWS, 768), per_b(8, 1536)] + [per_b(512, w) for w in wbs]
                 + [_const_spec((SROWS, w)) for w in wbs] + [_const_spec((3, SROWS, 128))],
        out_specs=[per_b(8, C_GW)] + [per_b(512, w) for w in wbs],
        out_shape=[jax.ShapeDtypeStruct((NBt, 8, C_GW), F32)]
                  + [jax.ShapeDtypeStruct((NBt, 512, w), F32) for w in wbs],
        compiler_params=_cparams(1),
        name="attn_sample",
    )(q_rep, kvn, *caches, *bias_c, bias_n)


def _t5_bucket(dist):
    dist = np.asarray(dist)
    max_exact = N_BUCKETS // 2
    large = max_exact + (np.log(np.maximum(dist, max_exact) / max_exact) / np.log(MAX_DISTANCE / max_exact)
                         * (N_BUCKETS - max_exact)).astype(np.int32)
    large = np.minimum(large, N_BUCKETS - 1)
    return np.where(dist < max_exact, dist, large).astype(np.int32)


def _group_bias(rel_bias, g, dil):
    buckets = _t5_bucket(dil * np.arange(C_TAPS + 1))
    return rel_bias[buckets][:, g * C_HPG:(g + 1) * C_HPG].T


def _toeplitz_body(c_ref, o_ref):
    keep = lax.broadcasted_iota(jnp.int32, (Q_BLOCK, 2 * Q_BLOCK), 1) >= Q_BLOCK
    for h in range(C_HPG):
        taps = jnp.broadcast_to(c_ref[h:h + 1, :], (Q_BLOCK, 2 * Q_BLOCK))
        t = pltpu.roll(taps, 0, 1, stride=1, stride_axis=0)
        o_ref[1, h * Q_BLOCK:(h + 1) * Q_BLOCK, :] = t
        o_ref[0, h * Q_BLOCK:(h + 1) * Q_BLOCK, :] = jnp.where(keep, t, NEG)


def _prompt_bias(bias_gs):
    c = jnp.stack([jnp.concatenate([b[:, ::-1].astype(F32) * LOG2E, jnp.full((C_HPG, Q_BLOCK - 1), NEG, F32)], axis=1)
                   for b in bias_gs])
    return pl.pallas_call(
        _toeplitz_body,
        grid=(len(bias_gs),),
        in_specs=[pl.BlockSpec((None, C_HPG, 2 * Q_BLOCK), lambda g: (g, 0, 0))],
        out_specs=pl.BlockSpec((None, 2, C_HPG * Q_BLOCK, 2 * Q_BLOCK), lambda g: (g, 0, 0, 0)),
        out_shape=jax.ShapeDtypeStruct((len(bias_gs), 2, C_HPG * Q_BLOCK, 2 * Q_BLOCK), F32),
        compiler_params=_cparams(1),
        name="attn_bias",
    )(c)


def _sample_bias(bias_g, wb, dil):
    n = wb + DEC_SEQ
    t = np.arange(8)[:, None]
    d = wb + t - np.arange(n)[None, :]
    valid = (t < DEC_SEQ) & (d >= 0) & (d % dil == 0) & (d // dil <= C_TAPS)
    place = np.zeros((8, C_TAPS + 1, n), np.float32)
    ti, ii = np.nonzero(valid)
    place[ti, (d // dil)[ti, ii], ii] = 1.0
    ext = jnp.einsum("hj,tji->hti", bias_g.astype(F32), place, precision=lax.Precision.HIGHEST)
    fill = np.where((t >= DEC_SEQ) & (np.arange(n)[None, :] < wb), 0.0, NEG).astype(np.float32)
    ext = jnp.where(valid[None], ext, fill[None])
    bc = ext[:, :, :wb].reshape(SROWS, wb)
    bn = jnp.pad(ext[:, :, wb:], ((0, 0), (0, 0), (0, 128 - DEC_SEQ)), constant_values=NEG).reshape(SROWS, 128)
    return bc, bn


def _time_major(s):
    b, k, c = s.shape
    return s.transpose(1, 0, 2).reshape(1, k * b, c)


def _batch_major(s, b):
    _, r, c = s.shape
    return s.reshape(r // b, b, c).transpose(1, 0, 2)


def _stack(x, mod, st, w, *, TM, TM_WIDE, RS, pos0):
    new = {}
    ng = w["norm_g"]
    cast_keys = [k for k in ("ffn_w_up", "ffn_w_down", "cd_w_in", "cd_w_out") if w[k].dtype != BF16]
    x, new["a"], new["b"], cast_out = _mixer_ab(
        x, mod[0], ng[0], st["a"], st["b"], w["ab_w_in"], w["a_conv_w"], w["b_w_grp"], w["b_scale"], w["ab_w_out"],
        TM=TM_WIDE, RS=RS, pos0=pos0, cast=tuple(w[k] for k in cast_keys))
    w = dict(w, **dict(zip(cast_keys, cast_out)))
    x, new["f0"] = _conv_ffn(x, mod[0], ng[0], st["f0"], w["ffn_w_up"], w["ffn_conv_w"],
                             w["ffn_conv_b"], w["ffn_w_down"], layer=0, TM=TM_WIDE, RS=RS)
    prompt = RS == 1
    outs = _cd_in(x, mod[1], ng[1], st["d"], w["cd_w_in"][0], w["d_conv_w"], w["d_conv_b"], w["d_ln_g"], w["d_ln_b"],
                  TM=TM, RS=RS, prompt_attn=prompt)
    qs, kvs, yd, new["d"] = outs[0:3], outs[3:6], outs[-2], outs[-1]
    cache_layout = lambda n: n.reshape(n.shape[0], 2, C_HPG, C_HEAD_DIM, n.shape[2]).transpose(0, 4, 1, 2, 3)
    if prompt:
        yc = _attn_prompt(qs, kvs, _prompt_bias(w["bias_g"]), TM=TM)
        for g in range(len(C_PAIRS)):
            new["c%d" % g] = cache_layout(outs[6 + g])
    else:
        B = RS
        kvf = outs[6]
        qb = _batch_major(jnp.concatenate(qs, axis=-1), B)
        q_rep = jnp.pad(jnp.broadcast_to(qb[:, None], (B, C_HPG, DEC_SEQ, 768)),
                        ((0, 0), (0, 0), (0, 8 - DEC_SEQ), (0, 0))).reshape(B, SROWS, 768)
        kvn = jnp.pad(_batch_major(kvf, B), ((0, 0), (8 - DEC_SEQ, 0), (0, 0)))
        bias = [_sample_bias(w["bias_g"][g], win, dil) for g, (win, dil) in enumerate(C_PAIRS)]
        yc, n0, n1, n2 = _attn_sample(q_rep, kvn, st["c"], [b[0] for b in bias],
                                      jnp.stack([b[1] for b in bias]))
        for g, n in enumerate((n0, n1, n2)):
            new["c%d" % g] = cache_layout(n)
        yc = _time_major(yc[:, :DEC_SEQ])
    x, new["f1"] = _conv_ffn(x, mod[1], ng[1], st["f1"], w["ffn_w_up"], w["ffn_conv_w"],
                             w["ffn_conv_b"], w["ffn_w_down"], layer=1, TM=TM_WIDE, RS=RS,
                             mixer_out=(yc, yd, w["cd_w_out"][0]))
    return x, new, w


def kernel(x_prompt, x_sample, state_a_conv, state_b_pool, cache_c_win128, cache_c_win512, cache_c_win2048,
           state_d_conv, state_ffn_conv, c_prompt, c_sample, ada_w, ada_b, norm_g, rel_bias, ab_w_in, a_conv_w,
           b_w_grp, b_scale, ab_w_out, cd_w_in, d_conv_w, d_conv_b, d_ln_g, d_ln_b, cd_w_out, ffn_w_up,
           ffn_conv_w, ffn_conv_b, ffn_w_down):
    B, T = DEC_BATCH, DEC_SEQ
    w = dict(norm_g=norm_g,
             ab_w_in=ab_w_in[0].astype(BF16), a_conv_w=a_conv_w[0], b_w_grp=b_w_grp[0].astype(BF16),
             b_scale=b_scale, ab_w_out=ab_w_out[0].astype(BF16),
             cd_w_in=cd_w_in, cd_w_out=cd_w_out,
             d_conv_w=jnp.broadcast_to(d_conv_w[0][:, None, :], (D_CONV, SUBLANES, D_WIDTH)),
             d_conv_b=d_conv_b, d_ln_g=d_ln_g, d_ln_b=d_ln_b,
             ffn_w_up=ffn_w_up, ffn_conv_w=ffn_conv_w, ffn_conv_b=ffn_conv_b[:, None, :], ffn_w_down=ffn_w_down,
             bias_g=[_group_bias(rel_bias, g, dil) for g, (_, dil) in enumerate(C_PAIRS)])

    mod = _ada(jnp.concatenate([c_prompt, c_sample], axis=0), ada_w, ada_b)
    mod_p = mod[:, :BATCH].reshape(DEPTH, BATCH, 6, 1, D_MODEL)
    mod_s = mod[:, BATCH:].reshape(DEPTH, B, 6, D_MODEL).transpose(0, 2, 1, 3)
    mod_s = jnp.broadcast_to(mod_s[:, :, None], (DEPTH, 6, T, B, D_MODEL)).reshape(DEPTH, 1, 6, T * B, D_MODEL)

    zeros = lambda k, c: jnp.zeros((BATCH, k, c), F32)
    st_p = dict(a=zeros(A_CONV - 1, A_WIDTH), b=zeros(B_PREV, B_WIDTH), d=zeros(D_CONV - 1, D_WIDTH),
                f0=zeros(FFN_CONV - 1, 2 * D_FF), f1=zeros(FFN_CONV - 1, 2 * D_FF))
    y_p, np_, w = _stack(x_prompt, mod_p, st_p, w, TM=TM_PROMPT, TM_WIDE=TM_WIDE_PROMPT, RS=1, pos0=0)

    st_s = dict(a=_time_major(state_a_conv[0]), b=_time_major(state_b_pool[0]), d=_time_major(state_d_conv[0]),
                f0=_time_major(state_ffn_conv[0]), f1=_time_major(state_ffn_conv[1]),
                c=[c[0].transpose(0, 2, 3, 4, 1).reshape(B, 512, c.shape[2])
                   for c in (cache_c_win128, cache_c_win512, cache_c_win2048)])
    y_s, ns, _ = _stack(_time_major(x_sample), mod_s, st_s, w, TM=T * B, TM_WIDE=T * B, RS=B, pos0=PAST_LEN)

    bm = lambda s: _batch_major(s, B)
    return (y_p, bm(y_s),
            np_["a"][None], bm(ns["a"])[None], np_["b"][None], bm(ns["b"])[None],
            np_["c0"][None], ns["c0"][None], np_["c1"][None], ns["c1"][None], np_["c2"][None], ns["c2"][None],
            np_["d"][None], bm(ns["d"])[None],
            jnp.stack([np_["f0"], np_["f1"]]), jnp.stack([bm(ns["f0"]), bm(ns["f1"])]))
```

```python
import functools
import math

import numpy as np
import jax
import jax.numpy as jnp
from jax import lax
from jax.experimental import pallas as pl
from jax.experimental.pallas import tpu as pltpu

D_MODEL = 1024
BATCH = 4
SEQ = 4096
DEPTH = 2
DEC_BATCH = 32
DEC_SEQ = 4
PAST_LEN = 8192
EPS = 1e-6
A_WIDTH = 512
A_CONV = 3
B_WIDTH = 512
B_WINDOWS = (2, 4, 8, 16)
B_GROUP = 128
B_PREV = 15
C_PAIRS = ((128, 1), (512, 4), (2048, 16))
C_HPG = 4
C_HEAD_DIM = 64
C_HEADS = 12
C_Q = C_HEADS * C_HEAD_DIM
C_QKV = 3 * C_Q
C_GW = C_HPG * C_HEAD_DIM
C_TAPS = 128
ATTN_SCALE = C_HEAD_DIM ** -0.5
LOG2E = math.log2(math.e)
Q_BLOCK = 128
N_BUCKETS = 32
MAX_DISTANCE = 2048
D_WIDTH = 512
D_CONV = 31
D_FF = 2816
FFN_CONV = 3

SUBLANES = 8
LANES = 128
VMEM_LIMIT = 56 * 1024 * 1024
NEG = -1e30
TM_PROMPT = 512
TM_WIDE_PROMPT = 1024
ATT_ROWS = 2048
FF_CHUNK = 256
D_ROWS = 32

F32 = jnp.float32
BF16 = jnp.bfloat16


def _ru8(n):
    return -(-n // SUBLANES) * SUBLANES


def _pool_levels(rs):
    l1 = _ru8(rs)
    l2 = _ru8(l1 + 2 * rs)
    l3 = _ru8(l2 + 4 * rs)
    l4 = _ru8(l3 + 8 * rs)
    return l1, l2, l3, l4


def _slab_put(ref, r0, val):
    n = val.shape[0]
    for s in range(val.shape[1] // LANES):
        ref[s, r0:r0 + n, :] = val[:, s * LANES:(s + 1) * LANES]


def _slab_get(ref, r0, n, s0=0, ns=None):
    ns = ref.shape[0] - s0 if ns is None else ns
    return jnp.concatenate([ref[s, r0:r0 + n, :] for s in range(s0, s0 + ns)], axis=-1)


def _rms(x, g):
    return x * lax.rsqrt(jnp.mean(x * x, axis=-1, keepdims=True) + EPS) * g


def _sigmoid(x):
    return 1.0 / (1.0 + jnp.exp(-x))


def _cparams(n_axes):
    return pltpu.CompilerParams(dimension_semantics=("arbitrary",) * n_axes, vmem_limit_bytes=VMEM_LIMIT)


def _const_spec(shape):
    nd = len(shape)
    return pl.BlockSpec(shape, lambda *_: (0,) * nd, pipeline_mode=pl.Buffered(1))


def _ada_body(c_ref, w_ref, b_ref, o_ref):
    c = c_ref[...]
    ca = c * _sigmoid(c)
    o_ref[...] = jnp.dot(ca.astype(BF16), w_ref[...].astype(BF16), preferred_element_type=F32) + b_ref[...]


def _ada(c_all, ada_w, ada_b):
    rows = c_all.shape[0]
    tn = 1536
    return pl.pallas_call(
        _ada_body,
        grid=(DEPTH, 6 * D_MODEL // tn),
        in_specs=[pl.BlockSpec((rows, D_MODEL), lambda l, n: (0, 0)),
                  pl.BlockSpec((None, D_MODEL, tn), lambda l, n: (l, 0, n)),
                  pl.BlockSpec((None, 1, tn), lambda l, n: (l, 0, n))],
        out_specs=pl.BlockSpec((None, rows, tn), lambda l, n: (l, 0, n)),
        out_shape=jax.ShapeDtypeStruct((DEPTH, rows, 6 * D_MODEL), F32),
        compiler_params=_cparams(2),
        name="ada",
    )(c_all, ada_w, ada_b.reshape(DEPTH, 1, 6 * D_MODEL))


def _ab_body(*refs, TM, RS, NT, pos0, n_cast):
    (x_ref, mod_ref, ng_ref, ap_ref, bp_ref, win_ref, aw_ref, bw_ref, bs_ref, wout_ref) = refs[:10]
    cast_in = refs[10:10 + n_cast]
    x1_ref, an_ref, bn_ref = refs[10 + n_cast:13 + n_cast]
    cast_out = refs[13 + n_cast:13 + 2 * n_cast]
    ea, eb, s2, s4, s8 = refs[13 + 2 * n_cast:]
    for src, dst in zip(cast_in, cast_out):
        dst[...] = src[...].astype(BF16)
    j = pl.program_id(1)
    HA = _ru8((A_CONV - 1) * RS)
    L1, L2, L3, HB = _pool_levels(RS)
    E = HB + TM
    na, nbp = (A_CONV - 1) * RS, B_PREV * RS

    @pl.when(j == 0)
    def _():
        _slab_put(ea, HA - na, ap_ref[...])
        if HB > nbp:
            _slab_put(eb, 0, jnp.zeros((HB - nbp, B_WIDTH), F32))
        _slab_put(eb, HB - nbp, bp_ref[...])

    x = x_ref[...]
    sh1, sc1, g1 = mod_ref[0], mod_ref[1], mod_ref[2]
    h = _rms(x, ng_ref[0:1, :]) * (1.0 + sc1) + sh1
    proj = jnp.dot(h.astype(BF16), win_ref[...], preferred_element_type=F32)
    hh, bg = proj[:, 0:A_WIDTH], proj[:, A_WIDTH:2 * A_WIDTH]
    cg, u = proj[:, 2 * A_WIDTH:3 * A_WIDTH], proj[:, 3 * A_WIDTH:]

    v = cg * hh
    _slab_put(ea, HA, v)
    z = (aw_ref[2:3, :] * v + aw_ref[1:2, :] * _slab_get(ea, HA - RS, TM)
         + aw_ref[0:1, :] * _slab_get(ea, HA - 2 * RS, TM))
    ya = bg * z
    a_last = _slab_get(ea, HA + TM - na, na)
    an_ref[...] = a_last
    if NT > 1:
        _slab_put(ea, HA - na, a_last)

    _slab_put(eb, HB, u)
    _slab_put(s2, L1, _slab_get(eb, L1, E - L1) + _slab_get(eb, L1 - RS, E - L1))
    _slab_put(s4, L2, _slab_get(s2, L2, E - L2, 1) + _slab_get(s2, L2 - 2 * RS, E - L2, 1))
    _slab_put(s8, L3, _slab_get(s4, L3, E - L3, 1) + _slab_get(s4, L3 - 4 * RS, E - L3, 1))
    wsum = (s2[0, HB:E, :], s4[0, HB:E, :], s8[0, HB:E, :], s8[1, HB:E, :] + s8[1, HB - 8 * RS:E - 8 * RS, :])
    b_last = _slab_get(eb, E - nbp, nbp)
    bn_ref[...] = b_last
    if NT > 1:
        _slab_put(eb, HB - nbp, b_last)
    row = lax.broadcasted_iota(jnp.int32, (TM, B_GROUP), 0) + j * TM
    pos1 = lax.shift_right_logical(row, int(math.log2(RS))) + (pos0 + 1)
    ybs = []
    for g, win in enumerate(B_WINDOWS):
        cnt = jnp.minimum(pos1, win).astype(F32)
        pooled = wsum[g] / cnt - u[:, g * B_GROUP:(g + 1) * B_GROUP]
        ybs.append(jnp.dot(pooled.astype(BF16), bw_ref[g], preferred_element_type=F32))
    yb = jnp.concatenate(ybs, axis=-1) * bs_ref[...]

    ycat = jnp.concatenate([ya, yb], axis=-1).astype(BF16)
    y = jnp.dot(ycat, wout_ref[...], preferred_element_type=F32)
    x1_ref[...] = x + g1 * _rms(y, ng_ref[1:2, :])


def _mixer_ab(x, mod, ng, a_prev, b_prev, w_in, a_w, b_w, b_scale, w_out, *, TM, RS, pos0, cast=()):
    NB, R, _ = x.shape
    NT = R // TM
    MR = mod.shape[2]
    steps = NB * NT
    cast_specs, cast_shapes = [], []
    for a in cast:
        rows = a.shape[1] // steps
        assert rows * steps == a.shape[1] and rows % (2 * SUBLANES) == 0
        cast_specs.append(pl.BlockSpec((a.shape[0], rows, a.shape[2]), lambda n, j: (0, n * NT + j, 0)))
        cast_shapes.append(jax.ShapeDtypeStruct(a.shape, BF16))
    _, _, _, HB = _pool_levels(RS)
    HA = _ru8((A_CONV - 1) * RS)
    E = HB + TM
    na, nbp = (A_CONV - 1) * RS, B_PREV * RS
    tile = lambda c: pl.BlockSpec((None, TM, c), lambda n, j: (n, j, 0))
    per_n = lambda r, c: pl.BlockSpec((None, r, c), lambda n, j: (n, 0, 0))
    outs = pl.pallas_call(
        functools.partial(_ab_body, TM=TM, RS=RS, NT=NT, pos0=pos0, n_cast=len(cast)),
        grid=(NB, NT),
        in_specs=[tile(D_MODEL),
                  pl.BlockSpec((None, 6, MR, D_MODEL), lambda n, j: (n, 0, 0, 0)),
                  _const_spec((4, D_MODEL)),
                  per_n(na, A_WIDTH), per_n(nbp, B_WIDTH),
                  _const_spec(w_in.shape), _const_spec(a_w.shape), _const_spec(b_w.shape),
                  _const_spec(b_scale.shape), _const_spec(w_out.shape)] + cast_specs,
        out_specs=[tile(D_MODEL), per_n(na, A_WIDTH), per_n(nbp, B_WIDTH)] + cast_specs,
        out_shape=[jax.ShapeDtypeStruct((NB, R, D_MODEL), F32),
                   jax.ShapeDtypeStruct((NB, na, A_WIDTH), F32),
                   jax.ShapeDtypeStruct((NB, nbp, B_WIDTH), F32)] + cast_shapes,
        scratch_shapes=[pltpu.VMEM((4, HA + TM, LANES), F32), pltpu.VMEM((4, E, LANES), F32),
                        pltpu.VMEM((4, E, LANES), F32), pltpu.VMEM((3, E, LANES), F32),
                        pltpu.VMEM((2, E, LANES), F32)],
        compiler_params=_cparams(2),
        name="mixer_ab",
    )(x, mod, ng, a_prev, b_prev, w_in, a_w, b_w, b_scale, w_out, *cast)
    return outs[0], outs[1], outs[2], tuple(outs[3:])


def _ffn_body(*refs, TM, RS, NT, mixer_out):
    if mixer_out:
        (x_ref, mod_ref, ng_ref, fp_ref, wup_ref, cw_ref, cb_ref, wdn_ref, yc_ref, yd_ref, wout_ref,
         y_ref, fn_ref, halo, ext, act) = refs
    else:
        (x_ref, mod_ref, ng_ref, fp_ref, wup_ref, cw_ref, cb_ref, wdn_ref,
         y_ref, fn_ref, halo, ext, act) = refs
    j = pl.program_id(1)
    nf = (FFN_CONV - 1) * RS
    HF = _ru8(nf)

    @pl.when(j == 0)
    def _():
        halo[...] = fp_ref[...]

    x = x_ref[...]
    if mixer_out:
        ym = (jnp.dot(yc_ref[...].astype(BF16), wout_ref[0:C_GW, :], preferred_element_type=F32)
              + jnp.dot(yd_ref[...], wout_ref[C_GW:, :], preferred_element_type=F32))
        x = x + mod_ref[2] * _rms(ym, ng_ref[1:2, :])
    sh2, sc2, g2 = mod_ref[3], mod_ref[4], mod_ref[5]
    hb = (_rms(x, ng_ref[2:3, :]) * (1.0 + sc2) + sh2).astype(BF16)

    def conv_part(col, buf):
        up = jnp.dot(hb, wup_ref[:, col:col + FF_CHUNK], preferred_element_type=F32)
        _slab_put(buf, HF - nf, halo[:, col:col + FF_CHUNK])
        _slab_put(buf, HF, up)
        out = (cw_ref[2:3, col:col + FF_CHUNK] * up
               + cw_ref[1:2, col:col + FF_CHUNK] * _slab_get(buf, HF - RS, TM)
               + cw_ref[0:1, col:col + FF_CHUNK] * _slab_get(buf, HF - 2 * RS, TM)
               + cb_ref[:, col:col + FF_CHUNK])
        halo[:, col:col + FF_CHUNK] = _slab_get(buf, HF + TM - nf, nf)
        return out

    for c in range(D_FF // FF_CHUNK):
        a = conv_part(c * FF_CHUNK, ext.at[2 * (c % 2)])
        g = conv_part(D_FF + c * FF_CHUNK, ext.at[2 * (c % 2) + 1])
        act[:, c * FF_CHUNK:(c + 1) * FF_CHUNK] = (a * (g * _sigmoid(g))).astype(BF16)
    fn_ref[...] = halo[...]
    y = jnp.dot(act[...], wdn_ref[...], preferred_element_type=F32)
    y_ref[...] = x + g2 * _rms(y, ng_ref[3:4, :])


def _layer_spec(shape, layer):
    nd = len(shape) - 1
    return pl.BlockSpec((None,) + tuple(shape[1:]), lambda *_: (layer,) + (0,) * nd, pipeline_mode=pl.Buffered(1))


def _conv_ffn(x, mod, ng, f_prev, w_up, conv_w, conv_b, w_down, *, layer, TM, RS, mixer_out=None):
    NB, R, _ = x.shape
    NT = R // TM
    MR = mod.shape[2]
    nf = (FFN_CONV - 1) * RS
    tile_c = lambda c: pl.BlockSpec((None, TM, c), lambda n, j: (n, j, 0))
    tile = tile_c(D_MODEL)
    per_n = pl.BlockSpec((None, nf, 2 * D_FF), lambda n, j: (n, 0, 0))
    mix_in, mix_specs = [], []
    if mixer_out is not None:
        mix_in = list(mixer_out)
        mix_specs = [tile_c(C_GW), tile_c(D_WIDTH), _const_spec(mixer_out[2].shape)]
    return pl.pallas_call(
        functools.partial(_ffn_body, TM=TM, RS=RS, NT=NT, mixer_out=mixer_out is not None),
        grid=(NB, NT),
        in_specs=[tile,
                  pl.BlockSpec((None, 6, MR, D_MODEL), lambda n, j: (n, 0, 0, 0)),
                  _const_spec((4, D_MODEL)),
                  per_n,
                  _layer_spec(w_up.shape, layer), _layer_spec(conv_w.shape, layer),
                  _layer_spec(conv_b.shape, layer), _layer_spec(w_down.shape, layer)] + mix_specs,
        out_specs=[tile, per_n],
        out_shape=[jax.ShapeDtypeStruct((NB, R, D_MODEL), F32),
                   jax.ShapeDtypeStruct((NB, nf, 2 * D_FF), F32)],
        scratch_shapes=[pltpu.VMEM((nf, 2 * D_FF), F32),
                        pltpu.VMEM((4, FF_CHUNK // LANES, _ru8(nf) + TM, LANES), F32),
                        pltpu.VMEM((TM, D_FF), BF16)],
        compiler_params=_cparams(2),
        name="conv_ffn",
    )(x, mod, ng, f_prev, w_up, conv_w, conv_b, w_down, *mix_in)


def _cd_in_body(*refs, TM, RS, NT, prompt_attn):
    if prompt_attn:
        (x_ref, mod_ref, ng_ref, dp_ref, win_ref, dw_ref, db_ref, lg_ref, lb_ref, perm_ref,
         q0_ref, q1_ref, q2_ref, kv0_ref, kv1_ref, kv2_ref, c0_ref, c1_ref, c2_ref, yd_ref, dn_ref,
         ed, kvf_ref) = refs
        cache_refs = (c0_ref, c1_ref, c2_ref)
    else:
        (x_ref, mod_ref, ng_ref, dp_ref, win_ref, dw_ref, db_ref, lg_ref, lb_ref,
         q0_ref, q1_ref, q2_ref, kv0_ref, kv1_ref, kv2_ref, kvf_ref, yd_ref, dn_ref, ed) = refs
    q_refs, kv_refs = (q0_ref, q1_ref, q2_ref), (kv0_ref, kv1_ref, kv2_ref)
    j = pl.program_id(1)
    nd = (D_CONV - 1) * RS
    HD = _ru8(nd)

    @pl.when(j == 0)
    def _():
        _slab_put(ed, HD - nd, dp_ref[...])

    x = x_ref[...]
    sh1, sc1 = mod_ref[0], mod_ref[1]
    h = _rms(x, ng_ref[0:1, :]) * (1.0 + sc1) + sh1
    proj = jnp.dot(h.astype(BF16), win_ref[...], preferred_element_type=F32)
    qs = (proj[:, 0:C_Q] * (ATTN_SCALE * LOG2E if prompt_attn else ATTN_SCALE)).astype(BF16)
    ks = proj[:, C_Q:2 * C_Q].astype(BF16)
    vs = proj[:, 2 * C_Q:C_QKV].astype(BF16)
    for g in range(len(C_PAIRS)):
        cols = slice(g * C_GW, (g + 1) * C_GW)
        xg = jnp.concatenate([qs[:, cols], ks[:, cols], vs[:, cols]], axis=-1)
        if prompt_attn and C_PAIRS[g][1] > 1:
            xg = jnp.dot(perm_ref[g - 1], xg, preferred_element_type=F32).astype(BF16)
        q_refs[g][...] = xg[:, 0:C_GW]
        kv_refs[g][...] = xg[:, C_GW:]
    kvf_ref[...] = proj[:, C_Q:C_QKV]
    dv, dg = proj[:, C_QKV:C_QKV + D_WIDTH], proj[:, C_QKV + D_WIDTH:]
    _slab_put(ed, HD, dv * _sigmoid(dg))

    db, lg, lb = db_ref[...], lg_ref[...], lb_ref[...]
    tiles = D_ROWS // SUBLANES
    for c in range(TM // D_ROWS):
        accs = []
        for s in range(D_WIDTH // LANES):
            acc = jnp.zeros((tiles, SUBLANES, LANES), F32)
            for kk in range(D_CONV):
                off = HD - (D_CONV - 1 - kk) * RS + c * D_ROWS
                tap = ed[s, off:off + D_ROWS, :].reshape(tiles, SUBLANES, LANES)
                acc = acc + dw_ref[kk, :, s * LANES:(s + 1) * LANES] * tap
            accs.append(acc.reshape(D_ROWS, LANES))
        zc = jnp.concatenate(accs, axis=-1) + db
        mu = jnp.mean(zc, axis=-1, keepdims=True)
        zc = zc - mu
        var = jnp.mean(zc * zc, axis=-1, keepdims=True)
        yl = zc * lax.rsqrt(var + EPS) * lg + lb
        yd_ref[c * D_ROWS:(c + 1) * D_ROWS, :] = (yl * _sigmoid(yl)).astype(BF16)

    d_last = _slab_get(ed, HD + TM - nd, nd)
    dn_ref[...] = d_last
    if NT > 1:
        _slab_put(ed, HD - nd, d_last)

    if prompt_attn:
        for g, (win, _) in enumerate(C_PAIRS):
            cols = min(win, TM)

            @pl.when(j >= NT - max(win // TM, 1))
            def _(g=g, cols=cols):
                kt = kvf_ref[:, g * C_GW:(g + 1) * C_GW].T
                vt = kvf_ref[:, C_Q + g * C_GW:C_Q + (g + 1) * C_GW].T
                cache_refs[g][0:C_GW, :] = kt[:, TM - cols:]
                cache_refs[g][C_GW:, :] = vt[:, TM - cols:]


def _residue_perm(tm, dil):
    p = np.zeros((tm, tm), np.float32)
    a, r = np.meshgrid(np.arange(tm // dil), np.arange(dil), indexing="ij")
    p[(r * (tm // dil) + a).ravel(), (dil * a + r).ravel()] = 1.0
    return p


def _cd_in(x, mod, ng, d_prev, w_in, d_w, d_b, ln_g, ln_b, *, TM, RS, prompt_attn):
    NB, R, _ = x.shape
    NT = R // TM
    MR = mod.shape[2]
    nd = (D_CONV - 1) * RS
    tile = lambda c: pl.BlockSpec((None, TM, c), lambda n, j: (n, j, 0))
    per_n = pl.BlockSpec((None, nd, D_WIDTH), lambda n, j: (n, 0, 0))
    perm_in, perm_spec = [], []
    if prompt_attn:
        perm_in = [jnp.asarray(np.stack([_residue_perm(TM, dil) for _, dil in C_PAIRS[1:]]), BF16)]
        perm_spec = [_const_spec((len(C_PAIRS) - 1, TM, TM))]
        assert all(win <= R and (win % TM == 0 or TM % win == 0) for win, _ in C_PAIRS)
        kv_specs = [pl.BlockSpec((None, 2 * C_GW, min(win, TM)),
                                 lambda n, j, first=NT - max(win // TM, 1): (n, 0, jnp.maximum(j - first, 0)))
                    for win, _ in C_PAIRS]
        kv_shapes = [jax.ShapeDtypeStruct((NB, 2 * C_GW, win), F32) for win, _ in C_PAIRS]
        kv_scratch = [pltpu.VMEM((TM, 2 * C_Q), F32)]
    else:
        kv_specs = [tile(2 * C_Q)]
        kv_shapes = [jax.ShapeDtypeStruct((NB, R, 2 * C_Q), F32)]
        kv_scratch = []
    return pl.pallas_call(
        functools.partial(_cd_in_body, TM=TM, RS=RS, NT=NT, prompt_attn=prompt_attn),
        grid=(NB, NT),
        in_specs=[tile(D_MODEL),
                  pl.BlockSpec((None, 6, MR, D_MODEL), lambda n, j: (n, 0, 0, 0)),
                  _const_spec((4, D_MODEL)),
                  per_n,
                  _const_spec(w_in.shape), _const_spec(d_w.shape), _const_spec(d_b.shape),
                  _const_spec(ln_g.shape), _const_spec(ln_b.shape)] + perm_spec,
        out_specs=[tile(C_GW)] * 3 + [tile(2 * C_GW)] * 3 + kv_specs + [tile(D_WIDTH), per_n],
        out_shape=[jax.ShapeDtypeStruct((NB, R, C_GW), BF16)] * 3
                  + [jax.ShapeDtypeStruct((NB, R, 2 * C_GW), BF16)] * 3
                  + kv_shapes
                  + [jax.ShapeDtypeStruct((NB, R, D_WIDTH), BF16),
                     jax.ShapeDtypeStruct((NB, nd, D_WIDTH), F32)],
        scratch_shapes=[pltpu.VMEM((D_WIDTH // LANES, _ru8(nd) + TM, LANES), F32)] + kv_scratch,
        compiler_params=_cparams(2),
        name="cd_in",
    )(x, mod, ng, d_prev, w_in, d_w, d_b, ln_g, ln_b, *perm_in)


def _head_of_lane(shape):
    return lax.shift_right_logical(lax.broadcasted_iota(jnp.int32, shape, 1), 6)


def _attn_blocks(blocks):
    lane_head = _head_of_lane((Q_BLOCK, C_GW))
    first_head = lax.broadcasted_iota(jnp.int32, (Q_BLOCK, LANES), 1) < C_HEAD_DIM
    logits = []
    for q, kvp, kvc, _ in blocks:
        zero = jnp.zeros_like(q)
        qs = jnp.concatenate([jnp.where(lane_head == h, q, zero) for h in range(C_HPG)], axis=0)
        kcat = jnp.concatenate([kvp[:, 0:C_GW], kvc[:, 0:C_GW]], axis=0)
        logits.append(lax.dot_general(qs, kcat, (((1,), (1,)), ((), ())), preferred_element_type=F32))
    logits = [lg + bias for lg, (_, _, _, bias) in zip(logits, blocks)]
    ms = [jnp.max(lg, axis=-1, keepdims=True) for lg in logits]
    ps = [jnp.exp2(lg - m) for lg, m in zip(logits, ms)]
    ss = [jnp.sum(p, axis=-1, keepdims=True) for p in ps]
    pvs = []
    for (_, kvp, kvc, _), p in zip(blocks, ps):
        pb = p.astype(BF16)
        vlo = jnp.concatenate([kvp[:, C_GW:C_GW + LANES], kvc[:, C_GW:C_GW + LANES]], axis=0)
        vhi = jnp.concatenate([kvp[:, C_GW + LANES:], kvc[:, C_GW + LANES:]], axis=0)
        pvs.append((jnp.dot(pb[0:2 * Q_BLOCK], vlo, preferred_element_type=F32),
                    jnp.dot(pb[2 * Q_BLOCK:], vhi, preferred_element_type=F32)))
    outs = []
    for (lo, hi), m, s in zip(pvs, ms, ss):
        r = 1.0 / s
        lse = m + jnp.log2(s)
        pick = lambda a, b: jnp.where(first_head, a, b)
        rows = lambda x, h: x[h * Q_BLOCK:(h + 1) * Q_BLOCK]
        o = jnp.concatenate([pick(lo[0:Q_BLOCK] * rows(r, 0), lo[Q_BLOCK:] * rows(r, 1)),
                             pick(hi[0:Q_BLOCK] * rows(r, 2), hi[Q_BLOCK:] * rows(r, 3))], axis=-1)
        l = jnp.concatenate([pick(rows(lse, 0), rows(lse, 1)), pick(rows(lse, 2), rows(lse, 3))], axis=-1)
        outs.append((o, l))
    return outs


def _blocks_per_trip(count):
    return max(u for u in (5, 4, 3, 2, 1) if count % u == 0)


def _attn_prompt_body(q0, q1, q2, kv0, kv1, kv2, p0, p1, p2, bias_ref, yc_ref, osc, lsc, *, TM):
    qs_, kvs, prevs = (q0, q1, q2), (kv0, kv1, kv2), (p0, p1, p2)
    var = jnp.minimum(pl.program_id(1), 1)
    n_blocks = ATT_ROWS // Q_BLOCK

    def put(g, start, n, stride, o, l, r0):
        rows = pl.ds(start, n, stride=stride) if stride > 1 else pl.ds(start, n)
        for s in range(C_GW // LANES):
            osc[g, s, rows, :] = o[r0:r0 + n, s * LANES:(s + 1) * LANES]
            lsc[g, s, rows, :] = l[r0:r0 + n, s * LANES:(s + 1) * LANES]

    for g, (_, dil) in enumerate(C_PAIRS):
        q_ref, kv_ref, p_ref = qs_[g], kvs[g], prevs[g]
        cls = TM // dil
        if cls >= Q_BLOCK:
            prev_off = Q_BLOCK if dil == 1 else TM
            n_first = prev_off // Q_BLOCK
            per_tile = TM // Q_BLOCK

            def token_start(idx, dil=dil, per_tile=per_tile):
                return idx * Q_BLOCK if dil == 1 else (idx // per_tile) * TM + idx % per_tile

            first = []
            for idx in range(n_first):
                rows = slice(idx * Q_BLOCK, (idx + 1) * Q_BLOCK)
                first.append((q_ref[rows, :], p_ref[rows, :], kv_ref[rows, :], bias_ref[g, var]))
            for idx, (o, l) in enumerate(_attn_blocks(first)):
                put(g, token_start(idx), Q_BLOCK, dil, o, l, 0)

            per_trip = _blocks_per_trip(n_blocks - n_first)

            def body(trip, carry, g=g, dil=dil, q_ref=q_ref, kv_ref=kv_ref, prev_off=prev_off,
                     token_start=token_start, n_first=n_first, per_trip=per_trip):
                idxs = [n_first + trip * per_trip + u for u in range(per_trip)]
                blocks = []
                for idx in idxs:
                    st = pl.multiple_of(idx * Q_BLOCK, Q_BLOCK)
                    blocks.append((q_ref[pl.ds(st, Q_BLOCK), :], kv_ref[pl.ds(st - prev_off, Q_BLOCK), :],
                                   kv_ref[pl.ds(st, Q_BLOCK), :], bias_ref[g, 1]))
                for idx, (o, l) in zip(idxs, _attn_blocks(blocks)):
                    put(g, token_start(idx), Q_BLOCK, dil, o, l, 0)
                return carry

            lax.fori_loop(0, (n_blocks - n_first) // per_trip, body, 0)
        else:
            tiles = Q_BLOCK // cls
            per_trip = _blocks_per_trip(dil)

            def body(trip, carry, g=g, dil=dil, q_ref=q_ref, kv_ref=kv_ref, p_ref=p_ref, cls=cls, tiles=tiles,
                     per_trip=per_trip):
                rs = [trip * per_trip + u for u in range(per_trip)]
                blocks = []
                for r in rs:
                    st = pl.multiple_of(r * cls, cls)
                    gather = lambda ref, st=st: jnp.concatenate(
                        [ref[pl.ds(c * TM + st, cls), :] for c in range(tiles)], axis=0)
                    blocks.append((gather(q_ref), gather(p_ref), gather(kv_ref), bias_ref[g, var]))
                for r, (o, l) in zip(rs, _attn_blocks(blocks)):
                    for c in range(tiles):
                        put(g, c * TM + r, cls, dil, o, l, c * cls)
                return carry

            lax.fori_loop(0, dil // per_trip, body, 0)

    def merge(ch, carry):
        st = pl.multiple_of(ch * Q_BLOCK, Q_BLOCK)
        get = lambda ref, g: jnp.concatenate([ref[g, s, pl.ds(st, Q_BLOCK), :] for s in range(C_GW // LANES)], axis=-1)
        ls = [get(lsc, g) for g in range(len(C_PAIRS))]
        mm = jnp.maximum(jnp.maximum(ls[0], ls[1]), ls[2])
        es = [jnp.exp2(l - mm) for l in ls]
        den = es[0] + es[1] + es[2]
        yc = (es[0] / den) * get(osc, 0) + (es[1] / den) * get(osc, 1) + (es[2] / den) * get(osc, 2)
        yc_ref[pl.ds(st, Q_BLOCK), :] = yc.astype(BF16)
        return carry

    lax.fori_loop(0, n_blocks, merge, 0)


def _attn_prompt(qs, kvs, bias, *, TM):
    N, S, _ = qs[0].shape
    assert S % ATT_ROWS == 0 and ATT_ROWS == Q_BLOCK * C_PAIRS[-1][1] and ATT_ROWS % TM == 0
    cur = lambda c: pl.BlockSpec((None, ATT_ROWS, c), lambda n, i: (n, i, 0))
    prev_rows = [Q_BLOCK if dil == 1 else (TM if TM // dil >= Q_BLOCK else ATT_ROWS) for _, dil in C_PAIRS]
    prev = [pl.BlockSpec((None, pr, 2 * C_GW), lambda n, i, k=ATT_ROWS // pr: (n, jnp.maximum(i * k - 1, 0), 0))
            for pr in prev_rows]
    slabs = C_GW // LANES
    return pl.pallas_call(
        functools.partial(_attn_prompt_body, TM=TM),
        grid=(N, S // ATT_ROWS),
        in_specs=[cur(C_GW)] * 3 + [cur(2 * C_GW)] * 3 + prev + [_const_spec(bias.shape)],
        out_specs=cur(C_GW),
        out_shape=jax.ShapeDtypeStruct((N, S, C_GW), BF16),
        scratch_shapes=[pltpu.VMEM((len(C_PAIRS), slabs, ATT_ROWS, LANES), F32)] * 2,
        compiler_params=_cparams(2),
        name="attn_prompt",
    )(*qs, *kvs, *kvs, bias)


SROWS = 32


def _attn_sample_body(q_ref, kvn_ref, c0_ref, c1_ref, c2_ref, bc0_ref, bc1_ref, bc2_ref, bn_ref,
                      yc_ref, n0_ref, n1_ref, n2_ref):
    caches = (c0_ref, c1_ref, c2_ref)
    bcs = (bc0_ref, bc1_ref, bc2_ref)
    news = (n0_ref, n1_ref, n2_ref)
    lane_head = _head_of_lane((SROWS, C_GW))
    row_head = lax.shift_right_logical(lax.broadcasted_iota(jnp.int32, (SROWS, C_GW), 0), 3)
    own = lane_head == row_head
    lane128 = lax.broadcasted_iota(jnp.int32, (SROWS, LANES), 1)
    kvn = kvn_ref[...]
    r0 = 8 - DEC_SEQ
    outs, lses = [], []
    for g, (win, dil) in enumerate(C_PAIRS):
        wb = win
        qg = q_ref[:, g * C_GW:(g + 1) * C_GW]
        qs = jnp.where(own, qg, jnp.zeros_like(qg))
        cache = caches[g]
        kt = cache[0:C_GW, :].astype(BF16)
        vt = cache[C_GW:2 * C_GW, :].astype(BF16)
        kn = kvn[:, g * C_GW:(g + 1) * C_GW]
        vn = kvn[:, C_Q + g * C_GW:C_Q + (g + 1) * C_GW]
        lc = jnp.dot(qs, kt, preferred_element_type=F32) + bcs[g][...]
        qf = qs.astype(F32)
        ln = bn_ref[g]
        for c in range(DEC_SEQ):
            d = jnp.sum(qf * kn[r0 + c:r0 + c + 1, :], axis=-1, keepdims=True)
            ln = ln + jnp.where(lane128 == c, d, 0.0)
        m = jnp.maximum(jnp.max(lc, axis=-1, keepdims=True), jnp.max(ln, axis=-1, keepdims=True))
        pc = jnp.exp(lc - m)
        pn = jnp.exp(ln - m)
        s = jnp.sum(pc, axis=-1, keepdims=True) + jnp.sum(pn, axis=-1, keepdims=True)
        pv = lax.dot_general(pc.astype(BF16), vt, (((1,), (1,)), ((), ())), preferred_element_type=F32)
        for c in range(DEC_SEQ):
            pcol = jnp.sum(jnp.where(lane128 == c, pn, 0.0), axis=-1, keepdims=True)
            pv = pv + pcol * vn[r0 + c:r0 + c + 1, :]
        outs.append(pv / s)
        lses.append(m + jnp.log(s))
        rolled = pltpu.roll(cache[...], wb - DEC_SEQ, 1)
        new_rows = jnp.concatenate([jnp.zeros((LANES - SUBLANES, 2 * C_GW), F32),
                                    jnp.concatenate([kn, vn], axis=-1)], axis=0)
        new_cols = new_rows.T
        lane_t = lax.broadcasted_iota(jnp.int32, (2 * C_GW, LANES), 1)
        if wb > LANES:
            news[g][:, 0:wb - LANES] = rolled[:, 0:wb - LANES]
        news[g][:, wb - LANES:wb] = jnp.where(lane_t >= LANES - DEC_SEQ, new_cols, rolled[:, wb - LANES:wb])
    mm = jnp.maximum(jnp.maximum(lses[0], lses[1]), lses[2])
    es = [jnp.exp(l - mm) for l in lses]
    den = es[0] + es[1] + es[2]
    y = (es[0] / den) * outs[0] + (es[1] / den) * outs[1] + (es[2] / den) * outs[2]
    y = jnp.where(own, y, 0.0)
    yc_ref[...] = y[0:8, :] + y[8:16, :] + y[16:24, :] + y[24:32, :]


def _attn_sample(q_rep, kvn, caches, bias_c, bias_n):
    NBt = q_rep.shape[0]
    per_b = lambda r, c: pl.BlockSpec((None, r, c), lambda b: (b, 0, 0))
    wbs = [w for w, _ in C_PAIRS]
    return pl.pallas_call(
        _attn_sample_body,
        grid=(NBt,),
        in_specs=[per_b(SROWS, C_Q), per_b(8, 2 * C_Q)] + [per_b(2 * C_GW, w) for w in wbs]
                 + [_const_spec((SROWS, w)) for w in wbs] + [_const_spec((len(wbs), SROWS, LANES))],
        out_specs=[per_b(8, C_GW)] + [per_b(2 * C_GW, w) for w in wbs],
        out_shape=[jax.ShapeDtypeStruct((NBt, 8, C_GW), F32)]
                  + [jax.ShapeDtypeStruct((NBt, 2 * C_GW, w), F32) for w in wbs],
        compiler_params=_cparams(1),
        name="attn_sample",
    )(q_rep, kvn, *caches, *bias_c, bias_n)


def _t5_bucket(dist):
    dist = np.asarray(dist)
    max_exact = N_BUCKETS // 2
    large = max_exact + (np.log(np.maximum(dist, max_exact) / max_exact) / np.log(MAX_DISTANCE / max_exact)
                         * (N_BUCKETS - max_exact)).astype(np.int32)
    large = np.minimum(large, N_BUCKETS - 1)
    return np.where(dist < max_exact, dist, large).astype(np.int32)


def _group_bias(rel_bias, g, dil):
    buckets = _t5_bucket(dil * np.arange(C_TAPS + 1))
    return rel_bias[buckets][:, g * C_HPG:(g + 1) * C_HPG].T


def _toeplitz_body(c_ref, o_ref):
    keep = lax.broadcasted_iota(jnp.int32, (Q_BLOCK, 2 * Q_BLOCK), 1) >= Q_BLOCK
    for h in range(C_HPG):
        taps = jnp.broadcast_to(c_ref[h:h + 1, :], (Q_BLOCK, 2 * Q_BLOCK))
        t = pltpu.roll(taps, 0, 1, stride=1, stride_axis=0)
        o_ref[1, h * Q_BLOCK:(h + 1) * Q_BLOCK, :] = t
        o_ref[0, h * Q_BLOCK:(h + 1) * Q_BLOCK, :] = jnp.where(keep, t, NEG)


def _prompt_bias(bias_gs):
    c = jnp.stack([jnp.concatenate([b[:, ::-1].astype(F32) * LOG2E, jnp.full((C_HPG, Q_BLOCK - 1), NEG, F32)], axis=1)
                   for b in bias_gs])
    return pl.pallas_call(
        _toeplitz_body,
        grid=(len(bias_gs),),
        in_specs=[pl.BlockSpec((None, C_HPG, 2 * Q_BLOCK), lambda g: (g, 0, 0))],
        out_specs=pl.BlockSpec((None, 2, C_HPG * Q_BLOCK, 2 * Q_BLOCK), lambda g: (g, 0, 0, 0)),
        out_shape=jax.ShapeDtypeStruct((len(bias_gs), 2, C_HPG * Q_BLOCK, 2 * Q_BLOCK), F32),
        compiler_params=_cparams(1),
        name="attn_bias",
    )(c)


def _sample_bias(bias_g, wb, dil):
    n = wb + DEC_SEQ
    t = np.arange(8)[:, None]
    d = wb + t - np.arange(n)[None, :]
    valid = (t < DEC_SEQ) & (d >= 0) & (d % dil == 0) & (d // dil <= C_TAPS)
    place = np.zeros((8, C_TAPS + 1, n), np.float32)
    ti, ii = np.nonzero(valid)
    place[ti, (d // dil)[ti, ii], ii] = 1.0
    ext = jnp.einsum("hj,tji->hti", bias_g.astype(F32), place, precision=lax.Precision.HIGHEST)
    fill = np.where((t >= DEC_SEQ) & (np.arange(n)[None, :] < wb), 0.0, NEG).astype(np.float32)
    ext = jnp.where(valid[None], ext, fill[None])
    bc = ext[:, :, :wb].reshape(SROWS, wb)
    bn = jnp.pad(ext[:, :, wb:], ((0, 0), (0, 0), (0, LANES - DEC_SEQ)), constant_values=NEG).reshape(SROWS, LANES)
    return bc, bn


def _time_major(s):
    b, k, c = s.shape
    return s.transpose(1, 0, 2).reshape(1, k * b, c)


def _batch_major(s, b):
    _, r, c = s.shape
    return s.reshape(r // b, b, c).transpose(1, 0, 2)


def _stack(x, mod, st, w, *, TM, TM_WIDE, RS, pos0):
    new = {}
    ng = w["norm_g"]
    cast_keys = [k for k in ("ffn_w_up", "ffn_w_down", "cd_w_in", "cd_w_out") if w[k].dtype != BF16]
    x, new["a"], new["b"], cast_out = _mixer_ab(
        x, mod[0], ng[0], st["a"], st["b"], w["ab_w_in"], w["a_conv_w"], w["b_w_grp"], w["b_scale"], w["ab_w_out"],
        TM=TM_WIDE, RS=RS, pos0=pos0, cast=tuple(w[k] for k in cast_keys))
    w = dict(w, **dict(zip(cast_keys, cast_out)))
    x, new["f0"] = _conv_ffn(x, mod[0], ng[0], st["f0"], w["ffn_w_up"], w["ffn_conv_w"],
                             w["ffn_conv_b"], w["ffn_w_down"], layer=0, TM=TM_WIDE, RS=RS)
    prompt = RS == 1
    outs = _cd_in(x, mod[1], ng[1], st["d"], w["cd_w_in"][0], w["d_conv_w"], w["d_conv_b"], w["d_ln_g"], w["d_ln_b"],
                  TM=TM, RS=RS, prompt_attn=prompt)
    qs, kvs, yd, new["d"] = outs[0:3], outs[3:6], outs[-2], outs[-1]
    cache_layout = lambda n: n.reshape(n.shape[0], 2, C_HPG, C_HEAD_DIM, n.shape[2]).transpose(0, 4, 1, 2, 3)
    if prompt:
        yc = _attn_prompt(qs, kvs, _prompt_bias(w["bias_g"]), TM=TM)
        for g in range(len(C_PAIRS)):
            new["c%d" % g] = cache_layout(outs[6 + g])
    else:
        B = RS
        kvf = outs[6]
        qb = _batch_major(jnp.concatenate(qs, axis=-1), B)
        q_rep = jnp.pad(jnp.broadcast_to(qb[:, None], (B, C_HPG, DEC_SEQ, C_Q)),
                        ((0, 0), (0, 0), (0, 8 - DEC_SEQ), (0, 0))).reshape(B, SROWS, C_Q)
        kvn = jnp.pad(_batch_major(kvf, B), ((0, 0), (8 - DEC_SEQ, 0), (0, 0)))
        bias = [_sample_bias(w["bias_g"][g], win, dil) for g, (win, dil) in enumerate(C_PAIRS)]
        yc, n0, n1, n2 = _attn_sample(q_rep, kvn, st["c"], [b[0] for b in bias],
                                      jnp.stack([b[1] for b in bias]))
        for g, n in enumerate((n0, n1, n2)):
            new["c%d" % g] = cache_layout(n)
        yc = _time_major(yc[:, :DEC_SEQ])
    x, new["f1"] = _conv_ffn(x, mod[1], ng[1], st["f1"], w["ffn_w_up"], w["ffn_conv_w"],
                             w["ffn_conv_b"], w["ffn_w_down"], layer=1, TM=TM_WIDE, RS=RS,
                             mixer_out=(yc, yd, w["cd_w_out"][0]))
    return x, new, w


def kernel(x_prompt, x_sample, state_a_conv, state_b_pool, cache_c_win128, cache_c_win512, cache_c_win2048,
           state_d_conv, state_ffn_conv, c_prompt, c_sample, ada_w, ada_b, norm_g, rel_bias, ab_w_in, a_conv_w,
           b_w_grp, b_scale, ab_w_out, cd_w_in, d_conv_w, d_conv_b, d_ln_g, d_ln_b, cd_w_out, ffn_w_up,
           ffn_conv_w, ffn_conv_b, ffn_w_down):
    B, T = DEC_BATCH, DEC_SEQ
    w = dict(norm_g=norm_g,
             ab_w_in=ab_w_in[0].astype(BF16), a_conv_w=a_conv_w[0], b_w_grp=b_w_grp[0].astype(BF16),
             b_scale=b_scale, ab_w_out=ab_w_out[0].astype(BF16),
             cd_w_in=cd_w_in, cd_w_out=cd_w_out,
             d_conv_w=jnp.broadcast_to(d_conv_w[0][:, None, :], (D_CONV, SUBLANES, D_WIDTH)),
             d_conv_b=d_conv_b, d_ln_g=d_ln_g, d_ln_b=d_ln_b,
             ffn_w_up=ffn_w_up, ffn_conv_w=ffn_conv_w, ffn_conv_b=ffn_conv_b[:, None, :], ffn_w_down=ffn_w_down,
             bias_g=[_group_bias(rel_bias, g, dil) for g, (_, dil) in enumerate(C_PAIRS)])

    mod = _ada(jnp.concatenate([c_prompt, c_sample], axis=0), ada_w, ada_b)
    mod_p = mod[:, :BATCH].reshape(DEPTH, BATCH, 6, 1, D_MODEL)
    mod_s = mod[:, BATCH:].reshape(DEPTH, B, 6, D_MODEL).transpose(0, 2, 1, 3)
    mod_s = jnp.broadcast_to(mod_s[:, :, None], (DEPTH, 6, T, B, D_MODEL)).reshape(DEPTH, 1, 6, T * B, D_MODEL)

    zeros = lambda k, c: jnp.zeros((BATCH, k, c), F32)
    st_p = dict(a=zeros(A_CONV - 1, A_WIDTH), b=zeros(B_PREV, B_WIDTH), d=zeros(D_CONV - 1, D_WIDTH),
                f0=zeros(FFN_CONV - 1, 2 * D_FF), f1=zeros(FFN_CONV - 1, 2 * D_FF))
    y_p, np_, w = _stack(x_prompt, mod_p, st_p, w, TM=TM_PROMPT, TM_WIDE=TM_WIDE_PROMPT, RS=1, pos0=0)

    st_s = dict(a=_time_major(state_a_conv[0]), b=_time_major(state_b_pool[0]), d=_time_major(state_d_conv[0]),
                f0=_time_major(state_ffn_conv[0]), f1=_time_major(state_ffn_conv[1]),
                c=[c[0].transpose(0, 2, 3, 4, 1).reshape(B, 2 * C_GW, c.shape[2])
                   for c in (cache_c_win128, cache_c_win512, cache_c_win2048)])
    y_s, ns, _ = _stack(_time_major(x_sample), mod_s, st_s, w, TM=T * B, TM_WIDE=T * B, RS=B, pos0=PAST_LEN)

    bm = lambda s: _batch_major(s, B)
    return (y_p, bm(y_s),
            np_["a"][None], bm(ns["a"])[None], np_["b"][None], bm(ns["b"])[None],
            np_["c0"][None], ns["c0"][None], np_["c1"][None], ns["c1"][None], np_["c2"][None], ns["c2"][None],
            np_["d"][None], bm(ns["d"])[None],
            jnp.stack([np_["f0"], np_["f1"]]), jnp.stack([bm(ns["f0"]), bm(ns["f1"])]))
```

```python
import functools
import math

import numpy as np
import jax
import jax.numpy as jnp
from jax import lax
from jax.experimental import pallas as pl
from jax.experimental.pallas import tpu as pltpu

D_MODEL = 1024
BATCH = 4
SEQ = 4096
DEPTH = 2
DEC_BATCH = 32
DEC_SEQ = 4
PAST_LEN = 8192
EPS = 1e-6
A_WIDTH = 512
A_CONV = 3
B_WIDTH = 512
B_WINDOWS = (2, 4, 8, 16)
B_GROUP = 128
B_PREV = 15
C_PAIRS = ((128, 1), (512, 4), (2048, 16))
C_HPG = 4
C_HEAD_DIM = 64
C_HEADS = 12
C_Q = C_HEADS * C_HEAD_DIM
C_QKV = 3 * C_Q
C_GW = C_HPG * C_HEAD_DIM
C_TAPS = 128
ATTN_SCALE = C_HEAD_DIM ** -0.5
LOG2E = math.log2(math.e)
Q_BLOCK = 128
N_BUCKETS = 32
MAX_DISTANCE = 2048
D_WIDTH = 512
D_CONV = 31
D_FF = 2816
FFN_CONV = 3

SUBLANES = 8
LANES = 128
VMEM_LIMIT = 56 * 1024 * 1024
NEG = -1e30
TM_PROMPT = 512
TM_WIDE_PROMPT = 1024
ATT_ROWS = 2048
FF_CHUNK = 256
D_ROWS = 32

F32 = jnp.float32
BF16 = jnp.bfloat16


def _ru8(n):
    return -(-n // SUBLANES) * SUBLANES


def _pool_levels(rs):
    l1 = _ru8(rs)
    l2 = _ru8(l1 + 2 * rs)
    l3 = _ru8(l2 + 4 * rs)
    l4 = _ru8(l3 + 8 * rs)
    return l1, l2, l3, l4


def _slab_put(ref, r0, val):
    n = val.shape[0]
    for s in range(val.shape[1] // LANES):
        ref[s, r0:r0 + n, :] = val[:, s * LANES:(s + 1) * LANES]


def _slab_get(ref, r0, n, s0=0, ns=None):
    ns = ref.shape[0] - s0 if ns is None else ns
    return jnp.concatenate([ref[s, r0:r0 + n, :] for s in range(s0, s0 + ns)], axis=-1)


def _rms(x, g):
    return x * lax.rsqrt(jnp.mean(x * x, axis=-1, keepdims=True) + EPS) * g


def _sigmoid(x):
    return 1.0 / (1.0 + jnp.exp(-x))


def _cparams(n_axes):
    return pltpu.CompilerParams(dimension_semantics=("arbitrary",) * n_axes, vmem_limit_bytes=VMEM_LIMIT)


def _const_spec(shape):
    nd = len(shape)
    return pl.BlockSpec(shape, lambda *_: (0,) * nd, pipeline_mode=pl.Buffered(1))


def _ada_body(c_ref, w_ref, b_ref, o_ref):
    c = c_ref[...]
    ca = c * _sigmoid(c)
    o_ref[...] = jnp.dot(ca.astype(BF16), w_ref[...].astype(BF16), preferred_element_type=F32) + b_ref[...]


def _ada(c_all, ada_w, ada_b):
    rows = c_all.shape[0]
    tn = 1536
    return pl.pallas_call(
        _ada_body,
        grid=(DEPTH, 6 * D_MODEL // tn),
        in_specs=[pl.BlockSpec((rows, D_MODEL), lambda l, n: (0, 0)),
                  pl.BlockSpec((None, D_MODEL, tn), lambda l, n: (l, 0, n)),
                  pl.BlockSpec((None, 1, tn), lambda l, n: (l, 0, n))],
        out_specs=pl.BlockSpec((None, rows, tn), lambda l, n: (l, 0, n)),
        out_shape=jax.ShapeDtypeStruct((DEPTH, rows, 6 * D_MODEL), F32),
        compiler_params=_cparams(2),
        name="ada",
    )(c_all, ada_w, ada_b.reshape(DEPTH, 1, 6 * D_MODEL))


def _ab_body(*refs, TM, RS, NT, pos0, n_cast):
    (x_ref, mod_ref, ng_ref, ap_ref, bp_ref, win_ref, aw_ref, bw_ref, bs_ref, wout_ref) = refs[:10]
    cast_in = refs[10:10 + n_cast]
    x1_ref, an_ref, bn_ref = refs[10 + n_cast:13 + n_cast]
    cast_out = refs[13 + n_cast:13 + 2 * n_cast]
    ea, eb, s2, s4, s8 = refs[13 + 2 * n_cast:]
    for src, dst in zip(cast_in, cast_out):
        dst[...] = src[...].astype(BF16)
    j = pl.program_id(1)
    HA = _ru8((A_CONV - 1) * RS)
    L1, L2, L3, HB = _pool_levels(RS)
    E = HB + TM
    na, nbp = (A_CONV - 1) * RS, B_PREV * RS

    @pl.when(j == 0)
    def _():
        _slab_put(ea, HA - na, ap_ref[...])
        if HB > nbp:
            _slab_put(eb, 0, jnp.zeros((HB - nbp, B_WIDTH), F32))
        _slab_put(eb, HB - nbp, bp_ref[...])

    x = x_ref[...]
    sh1, sc1, g1 = mod_ref[0], mod_ref[1], mod_ref[2]
    h = _rms(x, ng_ref[0:1, :]) * (1.0 + sc1) + sh1
    proj = jnp.dot(h.astype(BF16), win_ref[...], preferred_element_type=F32)
    hh, bg = proj[:, 0:A_WIDTH], proj[:, A_WIDTH:2 * A_WIDTH]
    cg, u = proj[:, 2 * A_WIDTH:3 * A_WIDTH], proj[:, 3 * A_WIDTH:]

    v = cg * hh
    _slab_put(ea, HA, v)
    z = (aw_ref[2:3, :] * v + aw_ref[1:2, :] * _slab_get(ea, HA - RS, TM)
         + aw_ref[0:1, :] * _slab_get(ea, HA - 2 * RS, TM))
    ya = bg * z
    a_last = _slab_get(ea, HA + TM - na, na)
    an_ref[...] = a_last
    if NT > 1:
        _slab_put(ea, HA - na, a_last)

    _slab_put(eb, HB, u)
    _slab_put(s2, L1, _slab_get(eb, L1, E - L1) + _slab_get(eb, L1 - RS, E - L1))
    _slab_put(s4, L2, _slab_get(s2, L2, E - L2, 1) + _slab_get(s2, L2 - 2 * RS, E - L2, 1))
    _slab_put(s8, L3, _slab_get(s4, L3, E - L3, 1) + _slab_get(s4, L3 - 4 * RS, E - L3, 1))
    wsum = (s2[0, HB:E, :], s4[0, HB:E, :], s8[0, HB:E, :], s8[1, HB:E, :] + s8[1, HB - 8 * RS:E - 8 * RS, :])
    b_last = _slab_get(eb, E - nbp, nbp)
    bn_ref[...] = b_last
    if NT > 1:
        _slab_put(eb, HB - nbp, b_last)
    row = lax.broadcasted_iota(jnp.int32, (TM, B_GROUP), 0) + j * TM
    pos1 = lax.shift_right_logical(row, int(math.log2(RS))) + (pos0 + 1)
    ybs = []
    for g, win in enumerate(B_WINDOWS):
        cnt = jnp.minimum(pos1, win).astype(F32)
        pooled = wsum[g] / cnt - u[:, g * B_GROUP:(g + 1) * B_GROUP]
        ybs.append(jnp.dot(pooled.astype(BF16), bw_ref[g], preferred_element_type=F32))
    yb = jnp.concatenate(ybs, axis=-1) * bs_ref[...]

    ycat = jnp.concatenate([ya, yb], axis=-1).astype(BF16)
    y = jnp.dot(ycat, wout_ref[...], preferred_element_type=F32)
    x1_ref[...] = x + g1 * _rms(y, ng_ref[1:2, :])


def _mixer_ab(x, mod, ng, a_prev, b_prev, w_in, a_w, b_w, b_scale, w_out, *, TM, RS, pos0, cast=()):
    NB, R, _ = x.shape
    NT = R // TM
    MR = mod.shape[2]
    steps = NB * NT
    cast_specs, cast_shapes = [], []
    for a in cast:
        rows = a.shape[1] // steps
        assert rows * steps == a.shape[1] and rows % (2 * SUBLANES) == 0
        cast_specs.append(pl.BlockSpec((a.shape[0], rows, a.shape[2]), lambda n, j: (0, n * NT + j, 0)))
        cast_shapes.append(jax.ShapeDtypeStruct(a.shape, BF16))
    _, _, _, HB = _pool_levels(RS)
    HA = _ru8((A_CONV - 1) * RS)
    E = HB + TM
    na, nbp = (A_CONV - 1) * RS, B_PREV * RS
    tile = lambda c: pl.BlockSpec((None, TM, c), lambda n, j: (n, j, 0))
    per_n = lambda r, c: pl.BlockSpec((None, r, c), lambda n, j: (n, 0, 0))
    outs = pl.pallas_call(
        functools.partial(_ab_body, TM=TM, RS=RS, NT=NT, pos0=pos0, n_cast=len(cast)),
        grid=(NB, NT),
        in_specs=[tile(D_MODEL),
                  pl.BlockSpec((None, 6, MR, D_MODEL), lambda n, j: (n, 0, 0, 0)),
                  _const_spec((4, D_MODEL)),
                  per_n(na, A_WIDTH), per_n(nbp, B_WIDTH),
                  _const_spec(w_in.shape), _const_spec(a_w.shape), _const_spec(b_w.shape),
                  _const_spec(b_scale.shape), _const_spec(w_out.shape)] + cast_specs,
        out_specs=[tile(D_MODEL), per_n(na, A_WIDTH), per_n(nbp, B_WIDTH)] + cast_specs,
        out_shape=[jax.ShapeDtypeStruct((NB, R, D_MODEL), F32),
                   jax.ShapeDtypeStruct((NB, na, A_WIDTH), F32),
                   jax.ShapeDtypeStruct((NB, nbp, B_WIDTH), F32)] + cast_shapes,
        scratch_shapes=[pltpu.VMEM((4, HA + TM, LANES), F32), pltpu.VMEM((4, E, LANES), F32),
                        pltpu.VMEM((4, E, LANES), F32), pltpu.VMEM((3, E, LANES), F32),
                        pltpu.VMEM((2, E, LANES), F32)],
        compiler_params=_cparams(2),
        name="mixer_ab",
    )(x, mod, ng, a_prev, b_prev, w_in, a_w, b_w, b_scale, w_out, *cast)
    return outs[0], outs[1], outs[2], tuple(outs[3:])


def _ffn_body(*refs, TM, RS, NT, mixer_out):
    if mixer_out:
        (x_ref, mod_ref, ng_ref, fp_ref, wup_ref, cw_ref, cb_ref, wdn_ref, yc_ref, yd_ref, wout_ref,
         y_ref, fn_ref, halo, ext, act) = refs
    else:
        (x_ref, mod_ref, ng_ref, fp_ref, wup_ref, cw_ref, cb_ref, wdn_ref,
         y_ref, fn_ref, halo, ext, act) = refs
    j = pl.program_id(1)
    nf = (FFN_CONV - 1) * RS
    HF = _ru8(nf)

    @pl.when(j == 0)
    def _():
        halo[...] = fp_ref[...]

    x = x_ref[...]
    if mixer_out:
        ym = (jnp.dot(yc_ref[...].astype(BF16), wout_ref[0:C_GW, :], preferred_element_type=F32)
              + jnp.dot(yd_ref[...], wout_ref[C_GW:, :], preferred_element_type=F32))
        x = x + mod_ref[2] * _rms(ym, ng_ref[1:2, :])
    sh2, sc2, g2 = mod_ref[3], mod_ref[4], mod_ref[5]
    hb = (_rms(x, ng_ref[2:3, :]) * (1.0 + sc2) + sh2).astype(BF16)

    def conv_part(col, buf):
        up = jnp.dot(hb, wup_ref[:, col:col + FF_CHUNK], preferred_element_type=F32)
        _slab_put(buf, HF - nf, halo[:, col:col + FF_CHUNK])
        _slab_put(buf, HF, up)
        out = (cw_ref[2:3, col:col + FF_CHUNK] * up
               + cw_ref[1:2, col:col + FF_CHUNK] * _slab_get(buf, HF - RS, TM)
               + cw_ref[0:1, col:col + FF_CHUNK] * _slab_get(buf, HF - 2 * RS, TM)
               + cb_ref[:, col:col + FF_CHUNK])
        halo[:, col:col + FF_CHUNK] = _slab_get(buf, HF + TM - nf, nf)
        return out

    for c in range(D_FF // FF_CHUNK):
        a = conv_part(c * FF_CHUNK, ext.at[2 * (c % 2)])
        g = conv_part(D_FF + c * FF_CHUNK, ext.at[2 * (c % 2) + 1])
        act[:, c * FF_CHUNK:(c + 1) * FF_CHUNK] = (a * (g * _sigmoid(g))).astype(BF16)
    fn_ref[...] = halo[...]
    y = jnp.dot(act[...], wdn_ref[...], preferred_element_type=F32)
    y_ref[...] = x + g2 * _rms(y, ng_ref[3:4, :])


def _layer_spec(shape, layer):
    nd = len(shape) - 1
    return pl.BlockSpec((None,) + tuple(shape[1:]), lambda *_: (layer,) + (0,) * nd, pipeline_mode=pl.Buffered(1))


def _conv_ffn(x, mod, ng, f_prev, w_up, conv_w, conv_b, w_down, *, layer, TM, RS, mixer_out=None):
    NB, R, _ = x.shape
    NT = R // TM
    MR = mod.shape[2]
    nf = (FFN_CONV - 1) * RS
    tile_c = lambda c: pl.BlockSpec((None, TM, c), lambda n, j: (n, j, 0))
    tile = tile_c(D_MODEL)
    per_n = pl.BlockSpec((None, nf, 2 * D_FF), lambda n, j: (n, 0, 0))
    mix_in, mix_specs = [], []
    if mixer_out is not None:
        mix_in = list(mixer_out)
        mix_specs = [tile_c(C_GW), tile_c(D_WIDTH), _const_spec(mixer_out[2].shape)]
    return pl.pallas_call(
        functools.partial(_ffn_body, TM=TM, RS=RS, NT=NT, mixer_out=mixer_out is not None),
        grid=(NB, NT),
        in_specs=[tile,
                  pl.BlockSpec((None, 6, MR, D_MODEL), lambda n, j: (n, 0, 0, 0)),
                  _const_spec((4, D_MODEL)),
                  per_n,
                  _layer_spec(w_up.shape, layer), _layer_spec(conv_w.shape, layer),
                  _layer_spec(conv_b.shape, layer), _layer_spec(w_down.shape, layer)] + mix_specs,
        out_specs=[tile, per_n],
        out_shape=[jax.ShapeDtypeStruct((NB, R, D_MODEL), F32),
                   jax.ShapeDtypeStruct((NB, nf, 2 * D_FF), F32)],
        scratch_shapes=[pltpu.VMEM((nf, 2 * D_FF), F32),
                        pltpu.VMEM((4, FF_CHUNK // LANES, _ru8(nf) + TM, LANES), F32),
                        pltpu.VMEM((TM, D_FF), BF16)],
        compiler_params=_cparams(2),
        name="conv_ffn",
    )(x, mod, ng, f_prev, w_up, conv_w, conv_b, w_down, *mix_in)


def _cd_in_body(*refs, TM, RS, NT, prompt_attn):
    if prompt_attn:
        (x_ref, mod_ref, ng_ref, dp_ref, win_ref, dw_ref, db_ref, lg_ref, lb_ref, perm_ref,
         q0_ref, q1_ref, q2_ref, kv0_ref, kv1_ref, kv2_ref, c0_ref, c1_ref, c2_ref, yd_ref, dn_ref,
         ed, kvf_ref) = refs
        cache_refs = (c0_ref, c1_ref, c2_ref)
    else:
        (x_ref, mod_ref, ng_ref, dp_ref, win_ref, dw_ref, db_ref, lg_ref, lb_ref,
         q0_ref, q1_ref, q2_ref, kv0_ref, kv1_ref, kv2_ref, kvf_ref, yd_ref, dn_ref, ed) = refs
    q_refs, kv_refs = (q0_ref, q1_ref, q2_ref), (kv0_ref, kv1_ref, kv2_ref)
    j = pl.program_id(1)
    nd = (D_CONV - 1) * RS
    HD = _ru8(nd)

    @pl.when(j == 0)
    def _():
        _slab_put(ed, HD - nd, dp_ref[...])

    x = x_ref[...]
    sh1, sc1 = mod_ref[0], mod_ref[1]
    h = _rms(x, ng_ref[0:1, :]) * (1.0 + sc1) + sh1
    proj = jnp.dot(h.astype(BF16), win_ref[...], preferred_element_type=F32)
    qs = (proj[:, 0:C_Q] * (ATTN_SCALE * LOG2E if prompt_attn else ATTN_SCALE)).astype(BF16)
    ks = proj[:, C_Q:2 * C_Q].astype(BF16)
    vs = proj[:, 2 * C_Q:C_QKV].astype(BF16)
    for g in range(len(C_PAIRS)):
        cols = slice(g * C_GW, (g + 1) * C_GW)
        xg = jnp.concatenate([qs[:, cols], ks[:, cols], vs[:, cols]], axis=-1)
        if prompt_attn and C_PAIRS[g][1] > 1:
            xg = jnp.dot(perm_ref[g - 1], xg, preferred_element_type=F32).astype(BF16)
        q_refs[g][...] = xg[:, 0:C_GW]
        kv_refs[g][...] = xg[:, C_GW:]
    kvf_ref[...] = proj[:, C_Q:C_QKV]
    dv, dg = proj[:, C_QKV:C_QKV + D_WIDTH], proj[:, C_QKV + D_WIDTH:]
    _slab_put(ed, HD, dv * _sigmoid(dg))

    db, lg, lb = db_ref[...], lg_ref[...], lb_ref[...]
    tiles = D_ROWS // SUBLANES
    for c in range(TM // D_ROWS):
        accs = []
        for s in range(D_WIDTH // LANES):
            acc = jnp.zeros((tiles, SUBLANES, LANES), F32)
            for kk in range(D_CONV):
                off = HD - (D_CONV - 1 - kk) * RS + c * D_ROWS
                tap = ed[s, off:off + D_ROWS, :].reshape(tiles, SUBLANES, LANES)
                acc = acc + dw_ref[kk, :, s * LANES:(s + 1) * LANES] * tap
            accs.append(acc.reshape(D_ROWS, LANES))
        zc = jnp.concatenate(accs, axis=-1) + db
        mu = jnp.mean(zc, axis=-1, keepdims=True)
        zc = zc - mu
        var = jnp.mean(zc * zc, axis=-1, keepdims=True)
        yl = zc * lax.rsqrt(var + EPS) * lg + lb
        yd_ref[c * D_ROWS:(c + 1) * D_ROWS, :] = (yl * _sigmoid(yl)).astype(BF16)

    d_last = _slab_get(ed, HD + TM - nd, nd)
    dn_ref[...] = d_last
    if NT > 1:
        _slab_put(ed, HD - nd, d_last)

    if prompt_attn:
        for g, (win, _) in enumerate(C_PAIRS):
            cols = min(win, TM)

            @pl.when(j >= NT - max(win // TM, 1))
            def _(g=g, cols=cols):
                kt = kvf_ref[:, g * C_GW:(g + 1) * C_GW].T
                vt = kvf_ref[:, C_Q + g * C_GW:C_Q + (g + 1) * C_GW].T
                cache_refs[g][0:C_GW, :] = kt[:, TM - cols:]
                cache_refs[g][C_GW:, :] = vt[:, TM - cols:]


def _residue_perm(tm, dil):
    p = np.zeros((tm, tm), np.float32)
    a, r = np.meshgrid(np.arange(tm // dil), np.arange(dil), indexing="ij")
    p[(r * (tm // dil) + a).ravel(), (dil * a + r).ravel()] = 1.0
    return p


def _cd_in(x, mod, ng, d_prev, w_in, d_w, d_b, ln_g, ln_b, *, TM, RS, prompt_attn):
    NB, R, _ = x.shape
    NT = R // TM
    MR = mod.shape[2]
    nd = (D_CONV - 1) * RS
    tile = lambda c: pl.BlockSpec((None, TM, c), lambda n, j: (n, j, 0))
    per_n = pl.BlockSpec((None, nd, D_WIDTH), lambda n, j: (n, 0, 0))
    perm_in, perm_spec = [], []
    if prompt_attn:
        perm_in = [jnp.asarray(np.stack([_residue_perm(TM, dil) for _, dil in C_PAIRS[1:]]), BF16)]
        perm_spec = [_const_spec((len(C_PAIRS) - 1, TM, TM))]
        assert all(win <= R and (win % TM == 0 or TM % win == 0) for win, _ in C_PAIRS)
        kv_specs = [pl.BlockSpec((None, 2 * C_GW, min(win, TM)),
                                 lambda n, j, first=NT - max(win // TM, 1): (n, 0, jnp.maximum(j - first, 0)))
                    for win, _ in C_PAIRS]
        kv_shapes = [jax.ShapeDtypeStruct((NB, 2 * C_GW, win), F32) for win, _ in C_PAIRS]
        kv_scratch = [pltpu.VMEM((TM, 2 * C_Q), F32)]
    else:
        kv_specs = [tile(2 * C_Q)]
        kv_shapes = [jax.ShapeDtypeStruct((NB, R, 2 * C_Q), F32)]
        kv_scratch = []
    return pl.pallas_call(
        functools.partial(_cd_in_body, TM=TM, RS=RS, NT=NT, prompt_attn=prompt_attn),
        grid=(NB, NT),
        in_specs=[tile(D_MODEL),
                  pl.BlockSpec((None, 6, MR, D_MODEL), lambda n, j: (n, 0, 0, 0)),
                  _const_spec((4, D_MODEL)),
                  per_n,
                  _const_spec(w_in.shape), _const_spec(d_w.shape), _const_spec(d_b.shape),
                  _const_spec(ln_g.shape), _const_spec(ln_b.shape)] + perm_spec,
        out_specs=[tile(C_GW)] * 3 + [tile(2 * C_GW)] * 3 + kv_specs + [tile(D_WIDTH), per_n],
        out_shape=[jax.ShapeDtypeStruct((NB, R, C_GW), BF16)] * 3
                  + [jax.ShapeDtypeStruct((NB, R, 2 * C_GW), BF16)] * 3
                  + kv_shapes
                  + [jax.ShapeDtypeStruct((NB, R, D_WIDTH), BF16),
                     jax.ShapeDtypeStruct((NB, nd, D_WIDTH), F32)],
        scratch_shapes=[pltpu.VMEM((D_WIDTH // LANES, _ru8(nd) + TM, LANES), F32)] + kv_scratch,
        compiler_params=_cparams(2),
        name="cd_in",
    )(x, mod, ng, d_prev, w_in, d_w, d_b, ln_g, ln_b, *perm_in)


def _head_of_lane(shape):
    return lax.shift_right_logical(lax.broadcasted_iota(jnp.int32, shape, 1), 6)


def _attn_blocks(blocks):
    lane_head = _head_of_lane((Q_BLOCK, C_GW))
    first_head = lax.broadcasted_iota(jnp.int32, (Q_BLOCK, LANES), 1) < C_HEAD_DIM
    logits = []
    for q, kvp, kvc, _ in blocks:
        zero = jnp.zeros_like(q)
        qs = jnp.concatenate([jnp.where(lane_head == h, q, zero) for h in range(C_HPG)], axis=0)
        kcat = jnp.concatenate([kvp[:, 0:C_GW], kvc[:, 0:C_GW]], axis=0)
        logits.append(lax.dot_general(qs, kcat, (((1,), (1,)), ((), ())), preferred_element_type=F32))
    logits = [lg + bias for lg, (_, _, _, bias) in zip(logits, blocks)]
    ms = [jnp.max(lg, axis=-1, keepdims=True) for lg in logits]
    ps = [jnp.exp2(lg - m) for lg, m in zip(logits, ms)]
    ss = [jnp.sum(p, axis=-1, keepdims=True) for p in ps]
    pvs = []
    for (_, kvp, kvc, _), p in zip(blocks, ps):
        pb = p.astype(BF16)
        vlo = jnp.concatenate([kvp[:, C_GW:C_GW + LANES], kvc[:, C_GW:C_GW + LANES]], axis=0)
        vhi = jnp.concatenate([kvp[:, C_GW + LANES:], kvc[:, C_GW + LANES:]], axis=0)
        pvs.append((jnp.dot(pb[0:2 * Q_BLOCK], vlo, preferred_element_type=F32),
                    jnp.dot(pb[2 * Q_BLOCK:], vhi, preferred_element_type=F32)))
    outs = []
    for (lo, hi), m, s in zip(pvs, ms, ss):
        pick = lambda a, b: jnp.where(first_head, a, b)
        rows = lambda x, h: x[h * Q_BLOCK:(h + 1) * Q_BLOCK]
        merged = lambda x: jnp.concatenate([pick(rows(x, 0), rows(x, 1)), pick(rows(x, 2), rows(x, 3))], axis=-1)
        s_l, m_l = merged(s), merged(m)
        o = jnp.concatenate([pick(lo[0:Q_BLOCK], lo[Q_BLOCK:]), pick(hi[0:Q_BLOCK], hi[Q_BLOCK:])], axis=-1)
        outs.append((o * (1.0 / s_l), m_l + jnp.log2(s_l)))
    return outs


def _blocks_per_trip(count):
    return max(u for u in (5, 4, 3, 2, 1) if count % u == 0)


def _attn_prompt_body(q0, q1, q2, kv0, kv1, kv2, p0, p1, p2, bias_ref, yc_ref, osc, lsc, *, TM):
    qs_, kvs, prevs = (q0, q1, q2), (kv0, kv1, kv2), (p0, p1, p2)
    var = jnp.minimum(pl.program_id(1), 1)
    n_blocks = ATT_ROWS // Q_BLOCK

    def put(g, start, n, stride, o, l, r0):
        rows = pl.ds(start, n, stride=stride) if stride > 1 else pl.ds(start, n)
        for s in range(C_GW // LANES):
            osc[g, s, rows, :] = o[r0:r0 + n, s * LANES:(s + 1) * LANES]
            lsc[g, s, rows, :] = l[r0:r0 + n, s * LANES:(s + 1) * LANES]

    for g, (_, dil) in enumerate(C_PAIRS):
        q_ref, kv_ref, p_ref = qs_[g], kvs[g], prevs[g]
        cls = TM // dil
        if cls >= Q_BLOCK:
            prev_off = Q_BLOCK if dil == 1 else TM
            n_first = prev_off // Q_BLOCK
            per_tile = TM // Q_BLOCK

            def token_start(idx, dil=dil, per_tile=per_tile):
                return idx * Q_BLOCK if dil == 1 else (idx // per_tile) * TM + idx % per_tile

            first = []
            for idx in range(n_first):
                rows = slice(idx * Q_BLOCK, (idx + 1) * Q_BLOCK)
                first.append((q_ref[rows, :], p_ref[rows, :], kv_ref[rows, :], bias_ref[g, var]))
            for idx, (o, l) in enumerate(_attn_blocks(first)):
                put(g, token_start(idx), Q_BLOCK, dil, o, l, 0)

            per_trip = _blocks_per_trip(n_blocks - n_first)

            def body(trip, carry, g=g, dil=dil, q_ref=q_ref, kv_ref=kv_ref, prev_off=prev_off,
                     token_start=token_start, n_first=n_first, per_trip=per_trip):
                idxs = [n_first + trip * per_trip + u for u in range(per_trip)]
                blocks = []
                for idx in idxs:
                    st = pl.multiple_of(idx * Q_BLOCK, Q_BLOCK)
                    blocks.append((q_ref[pl.ds(st, Q_BLOCK), :], kv_ref[pl.ds(st - prev_off, Q_BLOCK), :],
                                   kv_ref[pl.ds(st, Q_BLOCK), :], bias_ref[g, 1]))
                for idx, (o, l) in zip(idxs, _attn_blocks(blocks)):
                    put(g, token_start(idx), Q_BLOCK, dil, o, l, 0)
                return carry

            lax.fori_loop(0, (n_blocks - n_first) // per_trip, body, 0)
        else:
            tiles = Q_BLOCK // cls
            per_trip = _blocks_per_trip(dil)

            def body(trip, carry, g=g, dil=dil, q_ref=q_ref, kv_ref=kv_ref, p_ref=p_ref, cls=cls, tiles=tiles,
                     per_trip=per_trip):
                rs = [trip * per_trip + u for u in range(per_trip)]
                blocks = []
                for r in rs:
                    st = pl.multiple_of(r * cls, cls)
                    gather = lambda ref, st=st: jnp.concatenate(
                        [ref[pl.ds(c * TM + st, cls), :] for c in range(tiles)], axis=0)
                    blocks.append((gather(q_ref), gather(p_ref), gather(kv_ref), bias_ref[g, var]))
                for r, (o, l) in zip(rs, _attn_blocks(blocks)):
                    for c in range(tiles):
                        put(g, c * TM + r, cls, dil, o, l, c * cls)
                return carry

            lax.fori_loop(0, dil // per_trip, body, 0)

    def merge(ch, carry):
        st = pl.multiple_of(ch * Q_BLOCK, Q_BLOCK)
        get = lambda ref, g: jnp.concatenate([ref[g, s, pl.ds(st, Q_BLOCK), :] for s in range(C_GW // LANES)], axis=-1)
        ls = [get(lsc, g) for g in range(len(C_PAIRS))]
        mm = jnp.maximum(jnp.maximum(ls[0], ls[1]), ls[2])
        es = [jnp.exp2(l - mm) for l in ls]
        den = es[0] + es[1] + es[2]
        yc = (es[0] / den) * get(osc, 0) + (es[1] / den) * get(osc, 1) + (es[2] / den) * get(osc, 2)
        yc_ref[pl.ds(st, Q_BLOCK), :] = yc.astype(BF16)
        return carry

    lax.fori_loop(0, n_blocks, merge, 0)


def _attn_prompt(qs, kvs, bias, *, TM):
    N, S, _ = qs[0].shape
    assert S % ATT_ROWS == 0 and ATT_ROWS == Q_BLOCK * C_PAIRS[-1][1] and ATT_ROWS % TM == 0
    cur = lambda c: pl.BlockSpec((None, ATT_ROWS, c), lambda n, i: (n, i, 0))
    prev_rows = [Q_BLOCK if dil == 1 else (TM if TM // dil >= Q_BLOCK else ATT_ROWS) for _, dil in C_PAIRS]
    prev = [pl.BlockSpec((None, pr, 2 * C_GW), lambda n, i, k=ATT_ROWS // pr: (n, jnp.maximum(i * k - 1, 0), 0))
            for pr in prev_rows]
    slabs = C_GW // LANES
    return pl.pallas_call(
        functools.partial(_attn_prompt_body, TM=TM),
        grid=(N, S // ATT_ROWS),
        in_specs=[cur(C_GW)] * 3 + [cur(2 * C_GW)] * 3 + prev + [_const_spec(bias.shape)],
        out_specs=cur(C_GW),
        out_shape=jax.ShapeDtypeStruct((N, S, C_GW), BF16),
        scratch_shapes=[pltpu.VMEM((len(C_PAIRS), slabs, ATT_ROWS, LANES), F32)] * 2,
        compiler_params=_cparams(2),
        name="attn_prompt",
    )(*qs, *kvs, *kvs, bias)


SROWS = 32
SAMPLE_PER_STEP = 2


def _attn_sample_body(q_ref, kvn_ref, c0_ref, c1_ref, c2_ref, bc0_ref, bc1_ref, bc2_ref, bn_ref,
                      yc_ref, n0_ref, n1_ref, n2_ref):
    for e in range(q_ref.shape[0]):
        _attn_sample_one(q_ref.at[e], kvn_ref.at[e], (c0_ref.at[e], c1_ref.at[e], c2_ref.at[e]),
                         (bc0_ref, bc1_ref, bc2_ref), bn_ref, yc_ref.at[e],
                         (n0_ref.at[e], n1_ref.at[e], n2_ref.at[e]))


def _attn_sample_one(q_ref, kvn_ref, caches, bcs, bn_ref, yc_ref, news):
    lane_head = _head_of_lane((SROWS, C_GW))
    row_head = lax.shift_right_logical(lax.broadcasted_iota(jnp.int32, (SROWS, C_GW), 0), 3)
    own = lane_head == row_head
    lane128 = lax.broadcasted_iota(jnp.int32, (SROWS, LANES), 1)
    kvn = kvn_ref[...]
    r0 = 8 - DEC_SEQ
    outs, lses = [], []
    for g, (win, dil) in enumerate(C_PAIRS):
        wb = win
        qg = q_ref[:, g * C_GW:(g + 1) * C_GW]
        qs = jnp.where(own, qg, jnp.zeros_like(qg))
        cache = caches[g]
        kt = cache[0:C_GW, :].astype(BF16)
        vt = cache[C_GW:2 * C_GW, :].astype(BF16)
        kn = kvn[:, g * C_GW:(g + 1) * C_GW]
        vn = kvn[:, C_Q + g * C_GW:C_Q + (g + 1) * C_GW]
        lc = jnp.dot(qs, kt, preferred_element_type=F32) + bcs[g][...]
        qf = qs.astype(F32)
        ln = bn_ref[g]
        for c in range(DEC_SEQ):
            d = jnp.sum(qf * kn[r0 + c:r0 + c + 1, :], axis=-1, keepdims=True)
            ln = ln + jnp.where(lane128 == c, d, 0.0)
        m = jnp.maximum(jnp.max(lc, axis=-1, keepdims=True), jnp.max(ln, axis=-1, keepdims=True))
        pc = jnp.exp(lc - m)
        pn = jnp.exp(ln - m)
        s = jnp.sum(pc, axis=-1, keepdims=True) + jnp.sum(pn, axis=-1, keepdims=True)
        pv = lax.dot_general(pc.astype(BF16), vt, (((1,), (1,)), ((), ())), preferred_element_type=F32)
        for c in range(DEC_SEQ):
            pcol = jnp.sum(jnp.where(lane128 == c, pn, 0.0), axis=-1, keepdims=True)
            pv = pv + pcol * vn[r0 + c:r0 + c + 1, :]
        outs.append(pv / s)
        lses.append(m + jnp.log(s))
        rolled = pltpu.roll(cache[...], wb - DEC_SEQ, 1)
        new_rows = jnp.concatenate([jnp.zeros((LANES - SUBLANES, 2 * C_GW), F32),
                                    jnp.concatenate([kn, vn], axis=-1)], axis=0)
        new_cols = new_rows.T
        lane_t = lax.broadcasted_iota(jnp.int32, (2 * C_GW, LANES), 1)
        if wb > LANES:
            news[g][:, 0:wb - LANES] = rolled[:, 0:wb - LANES]
        news[g][:, wb - LANES:wb] = jnp.where(lane_t >= LANES - DEC_SEQ, new_cols, rolled[:, wb - LANES:wb])
    mm = jnp.maximum(jnp.maximum(lses[0], lses[1]), lses[2])
    es = [jnp.exp(l - mm) for l in lses]
    den = es[0] + es[1] + es[2]
    y = (es[0] / den) * outs[0] + (es[1] / den) * outs[1] + (es[2] / den) * outs[2]
    y = jnp.where(own, y, 0.0)
    yc_ref[...] = y[0:8, :] + y[8:16, :] + y[16:24, :] + y[24:32, :]


def _attn_sample(q_rep, kvn, caches, bias_c, bias_n):
    NBt = q_rep.shape[0]
    assert NBt % SAMPLE_PER_STEP == 0
    per_b = lambda r, c: pl.BlockSpec((SAMPLE_PER_STEP, r, c), lambda b: (b, 0, 0))
    wbs = [w for w, _ in C_PAIRS]
    return pl.pallas_call(
        _attn_sample_body,
        grid=(NBt // SAMPLE_PER_STEP,),
        in_specs=[per_b(SROWS, C_Q), per_b(8, 2 * C_Q)] + [per_b(2 * C_GW, w) for w in wbs]
                 + [_const_spec((SROWS, w)) for w in wbs] + [_const_spec((len(wbs), SROWS, LANES))],
        out_specs=[per_b(8, C_GW)] + [per_b(2 * C_GW, w) for w in wbs],
        out_shape=[jax.ShapeDtypeStruct((NBt, 8, C_GW), F32)]
                  + [jax.ShapeDtypeStruct((NBt, 2 * C_GW, w), F32) for w in wbs],
        compiler_params=_cparams(1),
        name="attn_sample",
    )(q_rep, kvn, *caches, *bias_c, bias_n)


def _t5_bucket(dist):
    dist = np.asarray(dist)
    max_exact = N_BUCKETS // 2
    large = max_exact + (np.log(np.maximum(dist, max_exact) / max_exact) / np.log(MAX_DISTANCE / max_exact)
                         * (N_BUCKETS - max_exact)).astype(np.int32)
    large = np.minimum(large, N_BUCKETS - 1)
    return np.where(dist < max_exact, dist, large).astype(np.int32)


def _group_bias(rel_bias, g, dil):
    buckets = _t5_bucket(dil * np.arange(C_TAPS + 1))
    return rel_bias[buckets][:, g * C_HPG:(g + 1) * C_HPG].T


def _toeplitz_body(c_ref, o_ref):
    keep = lax.broadcasted_iota(jnp.int32, (Q_BLOCK, 2 * Q_BLOCK), 1) >= Q_BLOCK
    for h in range(C_HPG):
        taps = jnp.broadcast_to(c_ref[h:h + 1, :], (Q_BLOCK, 2 * Q_BLOCK))
        t = pltpu.roll(taps, 0, 1, stride=1, stride_axis=0)
        o_ref[1, h * Q_BLOCK:(h + 1) * Q_BLOCK, :] = t
        o_ref[0, h * Q_BLOCK:(h + 1) * Q_BLOCK, :] = jnp.where(keep, t, NEG)


def _prompt_bias(bias_gs):
    c = jnp.stack([jnp.concatenate([b[:, ::-1].astype(F32) * LOG2E, jnp.full((C_HPG, Q_BLOCK - 1), NEG, F32)], axis=1)
                   for b in bias_gs])
    return pl.pallas_call(
        _toeplitz_body,
        grid=(len(bias_gs),),
        in_specs=[pl.BlockSpec((None, C_HPG, 2 * Q_BLOCK), lambda g: (g, 0, 0))],
        out_specs=pl.BlockSpec((None, 2, C_HPG * Q_BLOCK, 2 * Q_BLOCK), lambda g: (g, 0, 0, 0)),
        out_shape=jax.ShapeDtypeStruct((len(bias_gs), 2, C_HPG * Q_BLOCK, 2 * Q_BLOCK), F32),
        compiler_params=_cparams(1),
        name="attn_bias",
    )(c)


def _sample_bias(bias_g, wb, dil):
    n = wb + DEC_SEQ
    t = np.arange(8)[:, None]
    d = wb + t - np.arange(n)[None, :]
    valid = (t < DEC_SEQ) & (d >= 0) & (d % dil == 0) & (d // dil <= C_TAPS)
    place = np.zeros((8, C_TAPS + 1, n), np.float32)
    ti, ii = np.nonzero(valid)
    place[ti, (d // dil)[ti, ii], ii] = 1.0
    ext = jnp.einsum("hj,tji->hti", bias_g.astype(F32), place, precision=lax.Precision.HIGHEST)
    fill = np.where((t >= DEC_SEQ) & (np.arange(n)[None, :] < wb), 0.0, NEG).astype(np.float32)
    ext = jnp.where(valid[None], ext, fill[None])
    bc = ext[:, :, :wb].reshape(SROWS, wb)
    bn = jnp.pad(ext[:, :, wb:], ((0, 0), (0, 0), (0, LANES - DEC_SEQ)), constant_values=NEG).reshape(SROWS, LANES)
    return bc, bn


def _time_major(s):
    b, k, c = s.shape
    return s.transpose(1, 0, 2).reshape(1, k * b, c)


def _batch_major(s, b):
    _, r, c = s.shape
    return s.reshape(r // b, b, c).transpose(1, 0, 2)


def _stack(x, mod, st, w, *, TM, TM_WIDE, RS, pos0):
    new = {}
    ng = w["norm_g"]
    cast_keys = [k for k in ("ffn_w_up", "ffn_w_down", "cd_w_in", "cd_w_out") if w[k].dtype != BF16]
    x, new["a"], new["b"], cast_out = _mixer_ab(
        x, mod[0], ng[0], st["a"], st["b"], w["ab_w_in"], w["a_conv_w"], w["b_w_grp"], w["b_scale"], w["ab_w_out"],
        TM=TM_WIDE, RS=RS, pos0=pos0, cast=tuple(w[k] for k in cast_keys))
    w = dict(w, **dict(zip(cast_keys, cast_out)))
    x, new["f0"] = _conv_ffn(x, mod[0], ng[0], st["f0"], w["ffn_w_up"], w["ffn_conv_w"],
                             w["ffn_conv_b"], w["ffn_w_down"], layer=0, TM=TM_WIDE, RS=RS)
    prompt = RS == 1
    outs = _cd_in(x, mod[1], ng[1], st["d"], w["cd_w_in"][0], w["d_conv_w"], w["d_conv_b"], w["d_ln_g"], w["d_ln_b"],
                  TM=TM, RS=RS, prompt_attn=prompt)
    qs, kvs, yd, new["d"] = outs[0:3], outs[3:6], outs[-2], outs[-1]
    cache_layout = lambda n: n.reshape(n.shape[0], 2, C_HPG, C_HEAD_DIM, n.shape[2]).transpose(0, 4, 1, 2, 3)
    if prompt:
        yc = _attn_prompt(qs, kvs, _prompt_bias(w["bias_g"]), TM=TM)
        for g in range(len(C_PAIRS)):
            new["c%d" % g] = cache_layout(outs[6 + g])
    else:
        B = RS
        kvf = outs[6]
        qb = _batch_major(jnp.concatenate(qs, axis=-1), B)
        q_rep = jnp.pad(jnp.broadcast_to(qb[:, None], (B, C_HPG, DEC_SEQ, C_Q)),
                        ((0, 0), (0, 0), (0, 8 - DEC_SEQ), (0, 0))).reshape(B, SROWS, C_Q)
        kvn = jnp.pad(_batch_major(kvf, B), ((0, 0), (8 - DEC_SEQ, 0), (0, 0)))
        bias = [_sample_bias(w["bias_g"][g], win, dil) for g, (win, dil) in enumerate(C_PAIRS)]
        yc, n0, n1, n2 = _attn_sample(q_rep, kvn, st["c"], [b[0] for b in bias],
                                      jnp.stack([b[1] for b in bias]))
        for g, n in enumerate((n0, n1, n2)):
            new["c%d" % g] = cache_layout(n)
        yc = _time_major(yc[:, :DEC_SEQ])
    x, new["f1"] = _conv_ffn(x, mod[1], ng[1], st["f1"], w["ffn_w_up"], w["ffn_conv_w"],
                             w["ffn_conv_b"], w["ffn_w_down"], layer=1, TM=TM_WIDE, RS=RS,
                             mixer_out=(yc, yd, w["cd_w_out"][0]))
    return x, new, w


def kernel(x_prompt, x_sample, state_a_conv, state_b_pool, cache_c_win128, cache_c_win512, cache_c_win2048,
           state_d_conv, state_ffn_conv, c_prompt, c_sample, ada_w, ada_b, norm_g, rel_bias, ab_w_in, a_conv_w,
           b_w_grp, b_scale, ab_w_out, cd_w_in, d_conv_w, d_conv_b, d_ln_g, d_ln_b, cd_w_out, ffn_w_up,
           ffn_conv_w, ffn_conv_b, ffn_w_down):
    B, T = DEC_BATCH, DEC_SEQ
    w = dict(norm_g=norm_g,
             ab_w_in=ab_w_in[0].astype(BF16), a_conv_w=a_conv_w[0], b_w_grp=b_w_grp[0].astype(BF16),
             b_scale=b_scale, ab_w_out=ab_w_out[0].astype(BF16),
             cd_w_in=cd_w_in, cd_w_out=cd_w_out,
             d_conv_w=jnp.broadcast_to(d_conv_w[0][:, None, :], (D_CONV, SUBLANES, D_WIDTH)),
             d_conv_b=d_conv_b, d_ln_g=d_ln_g, d_ln_b=d_ln_b,
             ffn_w_up=ffn_w_up, ffn_conv_w=ffn_conv_w, ffn_conv_b=ffn_conv_b[:, None, :], ffn_w_down=ffn_w_down,
             bias_g=[_group_bias(rel_bias, g, dil) for g, (_, dil) in enumerate(C_PAIRS)])

    mod = _ada(jnp.concatenate([c_prompt, c_sample], axis=0), ada_w, ada_b)
    mod_p = mod[:, :BATCH].reshape(DEPTH, BATCH, 6, 1, D_MODEL)
    mod_s = mod[:, BATCH:].reshape(DEPTH, B, 6, D_MODEL).transpose(0, 2, 1, 3)
    mod_s = jnp.broadcast_to(mod_s[:, :, None], (DEPTH, 6, T, B, D_MODEL)).reshape(DEPTH, 1, 6, T * B, D_MODEL)

    zeros = lambda k, c: jnp.zeros((BATCH, k, c), F32)
    st_p = dict(a=zeros(A_CONV - 1, A_WIDTH), b=zeros(B_PREV, B_WIDTH), d=zeros(D_CONV - 1, D_WIDTH),
                f0=zeros(FFN_CONV - 1, 2 * D_FF), f1=zeros(FFN_CONV - 1, 2 * D_FF))
    y_p, np_, w = _stack(x_prompt, mod_p, st_p, w, TM=TM_PROMPT, TM_WIDE=TM_WIDE_PROMPT, RS=1, pos0=0)

    st_s = dict(a=_time_major(state_a_conv[0]), b=_time_major(state_b_pool[0]), d=_time_major(state_d_conv[0]),
                f0=_time_major(state_ffn_conv[0]), f1=_time_major(state_ffn_conv[1]),
                c=[c[0].transpose(0, 2, 3, 4, 1).reshape(B, 2 * C_GW, c.shape[2])
                   for c in (cache_c_win128, cache_c_win512, cache_c_win2048)])
    y_s, ns, _ = _stack(_time_major(x_sample), mod_s, st_s, w, TM=T * B, TM_WIDE=T * B, RS=B, pos0=PAST_LEN)

    bm = lambda s: _batch_major(s, B)
    return (y_p, bm(y_s),
            np_["a"][None], bm(ns["a"])[None], np_["b"][None], bm(ns["b"])[None],
            np_["c0"][None], ns["c0"][None], np_["c1"][None], ns["c1"][None], np_["c2"][None], ns["c2"][None],
            np_["d"][None], bm(ns["d"])[None],
            jnp.stack([np_["f0"], np_["f1"]]), jnp.stack([bm(ns["f0"]), bm(ns["f1"])]))
```

```python
import functools
import math

import numpy as np
import jax
import jax.numpy as jnp
from jax import lax
from jax.experimental import pallas as pl
from jax.experimental.pallas import tpu as pltpu

D_MODEL = 1024
BATCH = 4
SEQ = 4096
DEPTH = 2
DEC_BATCH = 32
DEC_SEQ = 4
PAST_LEN = 8192
EPS = 1e-6
A_WIDTH = 512
A_CONV = 3
B_WIDTH = 512
B_WINDOWS = (2, 4, 8, 16)
B_GROUP = 128
B_PREV = 15
C_PAIRS = ((128, 1), (512, 4), (2048, 16))
C_HPG = 4
C_HEAD_DIM = 64
C_HEADS = 12
C_Q = C_HEADS * C_HEAD_DIM
C_QKV = 3 * C_Q
C_GW = C_HPG * C_HEAD_DIM
C_TAPS = 128
ATTN_SCALE = C_HEAD_DIM ** -0.5
LOG2E = math.log2(math.e)
Q_BLOCK = 128
N_BUCKETS = 32
MAX_DISTANCE = 2048
D_WIDTH = 512
D_CONV = 31
D_FF = 2816
FFN_CONV = 3

SUBLANES = 8
LANES = 128
VMEM_LIMIT = 56 * 1024 * 1024
NEG = -1e30
TM_PROMPT = 512
TM_WIDE_PROMPT = 1024
ATT_ROWS = 2048
ADA_COLS = 3072
FF_CHUNK = 256
D_ROWS = 32

F32 = jnp.float32
BF16 = jnp.bfloat16


def _ru8(n):
    return -(-n // SUBLANES) * SUBLANES


def _pool_levels(rs):
    l1 = _ru8(rs)
    l2 = _ru8(l1 + 2 * rs)
    l3 = _ru8(l2 + 4 * rs)
    l4 = _ru8(l3 + 8 * rs)
    return l1, l2, l3, l4


def _slab_put(ref, r0, val):
    n = val.shape[0]
    for s in range(val.shape[1] // LANES):
        ref[s, r0:r0 + n, :] = val[:, s * LANES:(s + 1) * LANES]


def _slab_get(ref, r0, n, s0=0, ns=None):
    ns = ref.shape[0] - s0 if ns is None else ns
    return jnp.concatenate([ref[s, r0:r0 + n, :] for s in range(s0, s0 + ns)], axis=-1)


def _rms(x, g):
    return x * lax.rsqrt(jnp.mean(x * x, axis=-1, keepdims=True) + EPS) * g


def _sigmoid(x):
    return 1.0 / (1.0 + jnp.exp(-x))


def _cparams(n_axes):
    return pltpu.CompilerParams(dimension_semantics=("arbitrary",) * n_axes, vmem_limit_bytes=VMEM_LIMIT)


def _const_spec(shape):
    nd = len(shape)
    return pl.BlockSpec(shape, lambda *_: (0,) * nd, pipeline_mode=pl.Buffered(1))


def _ada_body(c_ref, w_ref, b_ref, o_ref):
    c = c_ref[...]
    ca = c * _sigmoid(c)
    o_ref[...] = jnp.dot(ca.astype(BF16), w_ref[...].astype(BF16), preferred_element_type=F32) + b_ref[...]


def _ada(c_all, ada_w, ada_b):
    rows = c_all.shape[0]
    tn = ADA_COLS
    return pl.pallas_call(
        _ada_body,
        grid=(DEPTH, 6 * D_MODEL // tn),
        in_specs=[pl.BlockSpec((rows, D_MODEL), lambda l, n: (0, 0)),
                  pl.BlockSpec((None, D_MODEL, tn), lambda l, n: (l, 0, n)),
                  pl.BlockSpec((None, 1, tn), lambda l, n: (l, 0, n))],
        out_specs=pl.BlockSpec((None, rows, tn), lambda l, n: (l, 0, n)),
        out_shape=jax.ShapeDtypeStruct((DEPTH, rows, 6 * D_MODEL), F32),
        compiler_params=_cparams(2),
        name="ada",
    )(c_all, ada_w, ada_b.reshape(DEPTH, 1, 6 * D_MODEL))


def _ab_body(*refs, TM, RS, NT, pos0, n_cast):
    (x_ref, mod_ref, ng_ref, ap_ref, bp_ref, win_ref, aw_ref, bw_ref, bs_ref, wout_ref) = refs[:10]
    cast_in = refs[10:10 + n_cast]
    x1_ref, an_ref, bn_ref = refs[10 + n_cast:13 + n_cast]
    cast_out = refs[13 + n_cast:13 + 2 * n_cast]
    ea, eb, s2, s4, s8 = refs[13 + 2 * n_cast:]
    for src, dst in zip(cast_in, cast_out):
        dst[...] = src[...].astype(BF16)
    j = pl.program_id(1)
    HA = _ru8((A_CONV - 1) * RS)
    L1, L2, L3, HB = _pool_levels(RS)
    E = HB + TM
    na, nbp = (A_CONV - 1) * RS, B_PREV * RS

    @pl.when(j == 0)
    def _():
        _slab_put(ea, HA - na, ap_ref[...])
        if HB > nbp:
            _slab_put(eb, 0, jnp.zeros((HB - nbp, B_WIDTH), F32))
        _slab_put(eb, HB - nbp, bp_ref[...])

    x = x_ref[...]
    sh1, sc1, g1 = mod_ref[0], mod_ref[1], mod_ref[2]
    h = _rms(x, ng_ref[0:1, :]) * (1.0 + sc1) + sh1
    proj = jnp.dot(h.astype(BF16), win_ref[...], preferred_element_type=F32)
    hh, bg = proj[:, 0:A_WIDTH], proj[:, A_WIDTH:2 * A_WIDTH]
    cg, u = proj[:, 2 * A_WIDTH:3 * A_WIDTH], proj[:, 3 * A_WIDTH:]

    v = cg * hh
    _slab_put(ea, HA, v)
    z = (aw_ref[2:3, :] * v + aw_ref[1:2, :] * _slab_get(ea, HA - RS, TM)
         + aw_ref[0:1, :] * _slab_get(ea, HA - 2 * RS, TM))
    ya = bg * z
    a_last = _slab_get(ea, HA + TM - na, na)
    an_ref[...] = a_last
    if NT > 1:
        _slab_put(ea, HA - na, a_last)

    _slab_put(eb, HB, u)
    _slab_put(s2, L1, _slab_get(eb, L1, E - L1) + _slab_get(eb, L1 - RS, E - L1))
    _slab_put(s4, L2, _slab_get(s2, L2, E - L2, 1) + _slab_get(s2, L2 - 2 * RS, E - L2, 1))
    _slab_put(s8, L3, _slab_get(s4, L3, E - L3, 1) + _slab_get(s4, L3 - 4 * RS, E - L3, 1))
    wsum = (s2[0, HB:E, :], s4[0, HB:E, :], s8[0, HB:E, :], s8[1, HB:E, :] + s8[1, HB - 8 * RS:E - 8 * RS, :])
    b_last = _slab_get(eb, E - nbp, nbp)
    bn_ref[...] = b_last
    if NT > 1:
        _slab_put(eb, HB - nbp, b_last)
    row = lax.broadcasted_iota(jnp.int32, (TM, B_GROUP), 0) + j * TM
    pos1 = lax.shift_right_logical(row, int(math.log2(RS))) + (pos0 + 1)
    ybs = []
    for g, win in enumerate(B_WINDOWS):
        cnt = jnp.minimum(pos1, win).astype(F32)
        pooled = wsum[g] / cnt - u[:, g * B_GROUP:(g + 1) * B_GROUP]
        ybs.append(jnp.dot(pooled.astype(BF16), bw_ref[g], preferred_element_type=F32))
    yb = jnp.concatenate(ybs, axis=-1) * bs_ref[...]

    ycat = jnp.concatenate([ya, yb], axis=-1).astype(BF16)
    y = jnp.dot(ycat, wout_ref[...], preferred_element_type=F32)
    x1_ref[...] = x + g1 * _rms(y, ng_ref[1:2, :])


def _mixer_ab(x, mod, ng, a_prev, b_prev, w_in, a_w, b_w, b_scale, w_out, *, TM, RS, pos0, cast=()):
    NB, R, _ = x.shape
    NT = R // TM
    MR = mod.shape[2]
    steps = NB * NT
    cast_specs, cast_shapes = [], []
    for a in cast:
        rows = a.shape[1] // steps
        assert rows * steps == a.shape[1] and rows % (2 * SUBLANES) == 0
        cast_specs.append(pl.BlockSpec((a.shape[0], rows, a.shape[2]), lambda n, j: (0, n * NT + j, 0)))
        cast_shapes.append(jax.ShapeDtypeStruct(a.shape, BF16))
    _, _, _, HB = _pool_levels(RS)
    HA = _ru8((A_CONV - 1) * RS)
    E = HB + TM
    na, nbp = (A_CONV - 1) * RS, B_PREV * RS
    tile = lambda c: pl.BlockSpec((None, TM, c), lambda n, j: (n, j, 0))
    per_n = lambda r, c: pl.BlockSpec((None, r, c), lambda n, j: (n, 0, 0))
    outs = pl.pallas_call(
        functools.partial(_ab_body, TM=TM, RS=RS, NT=NT, pos0=pos0, n_cast=len(cast)),
        grid=(NB, NT),
        in_specs=[tile(D_MODEL),
                  pl.BlockSpec((None, 6, MR, D_MODEL), lambda n, j: (n, 0, 0, 0)),
                  _const_spec((4, D_MODEL)),
                  per_n(na, A_WIDTH), per_n(nbp, B_WIDTH),
                  _const_spec(w_in.shape), _const_spec(a_w.shape), _const_spec(b_w.shape),
                  _const_spec(b_scale.shape), _const_spec(w_out.shape)] + cast_specs,
        out_specs=[tile(D_MODEL), per_n(na, A_WIDTH), per_n(nbp, B_WIDTH)] + cast_specs,
        out_shape=[jax.ShapeDtypeStruct((NB, R, D_MODEL), F32),
                   jax.ShapeDtypeStruct((NB, na, A_WIDTH), F32),
                   jax.ShapeDtypeStruct((NB, nbp, B_WIDTH), F32)] + cast_shapes,
        scratch_shapes=[pltpu.VMEM((4, HA + TM, LANES), F32), pltpu.VMEM((4, E, LANES), F32),
                        pltpu.VMEM((4, E, LANES), F32), pltpu.VMEM((3, E, LANES), F32),
                        pltpu.VMEM((2, E, LANES), F32)],
        compiler_params=_cparams(2),
        name="mixer_ab",
    )(x, mod, ng, a_prev, b_prev, w_in, a_w, b_w, b_scale, w_out, *cast)
    return outs[0], outs[1], outs[2], tuple(outs[3:])


def _ffn_body(*refs, TM, RS, NT, mixer_out):
    if mixer_out:
        (x_ref, mod_ref, ng_ref, fp_ref, wup_ref, cw_ref, cb_ref, wdn_ref, yc_ref, yd_ref, wout_ref,
         y_ref, fn_ref, halo, ext, act) = refs
    else:
        (x_ref, mod_ref, ng_ref, fp_ref, wup_ref, cw_ref, cb_ref, wdn_ref,
         y_ref, fn_ref, halo, ext, act) = refs
    j = pl.program_id(1)
    nf = (FFN_CONV - 1) * RS
    HF = _ru8(nf)

    @pl.when(j == 0)
    def _():
        halo[...] = fp_ref[...]

    x = x_ref[...]
    if mixer_out:
        ym = (jnp.dot(yc_ref[...].astype(BF16), wout_ref[0:C_GW, :], preferred_element_type=F32)
              + jnp.dot(yd_ref[...], wout_ref[C_GW:, :], preferred_element_type=F32))
        x = x + mod_ref[2] * _rms(ym, ng_ref[1:2, :])
    sh2, sc2, g2 = mod_ref[3], mod_ref[4], mod_ref[5]
    hb = (_rms(x, ng_ref[2:3, :]) * (1.0 + sc2) + sh2).astype(BF16)

    def conv_part(col, buf):
        up = jnp.dot(hb, wup_ref[:, col:col + FF_CHUNK], preferred_element_type=F32)
        _slab_put(buf, HF - nf, halo[:, col:col + FF_CHUNK])
        _slab_put(buf, HF, up)
        out = (cw_ref[2:3, col:col + FF_CHUNK] * up
               + cw_ref[1:2, col:col + FF_CHUNK] * _slab_get(buf, HF - RS, TM)
               + cw_ref[0:1, col:col + FF_CHUNK] * _slab_get(buf, HF - 2 * RS, TM)
               + cb_ref[:, col:col + FF_CHUNK])
        halo[:, col:col + FF_CHUNK] = _slab_get(buf, HF + TM - nf, nf)
        return out

    for c in range(D_FF // FF_CHUNK):
        a = conv_part(c * FF_CHUNK, ext.at[2 * (c % 2)])
        g = conv_part(D_FF + c * FF_CHUNK, ext.at[2 * (c % 2) + 1])
        act[:, c * FF_CHUNK:(c + 1) * FF_CHUNK] = (a * (g * _sigmoid(g))).astype(BF16)
    fn_ref[...] = halo[...]
    y = jnp.dot(act[...], wdn_ref[...], preferred_element_type=F32)
    y_ref[...] = x + g2 * _rms(y, ng_ref[3:4, :])


def _layer_spec(shape, layer):
    nd = len(shape) - 1
    return pl.BlockSpec((None,) + tuple(shape[1:]), lambda *_: (layer,) + (0,) * nd, pipeline_mode=pl.Buffered(1))


def _conv_ffn(x, mod, ng, f_prev, w_up, conv_w, conv_b, w_down, *, layer, TM, RS, mixer_out=None):
    NB, R, _ = x.shape
    NT = R // TM
    MR = mod.shape[2]
    nf = (FFN_CONV - 1) * RS
    tile_c = lambda c: pl.BlockSpec((None, TM, c), lambda n, j: (n, j, 0))
    tile = tile_c(D_MODEL)
    per_n = pl.BlockSpec((None, nf, 2 * D_FF), lambda n, j: (n, 0, 0))
    mix_in, mix_specs = [], []
    if mixer_out is not None:
        mix_in = list(mixer_out)
        mix_specs = [tile_c(C_GW), tile_c(D_WIDTH), _const_spec(mixer_out[2].shape)]
    return pl.pallas_call(
        functools.partial(_ffn_body, TM=TM, RS=RS, NT=NT, mixer_out=mixer_out is not None),
        grid=(NB, NT),
        in_specs=[tile,
                  pl.BlockSpec((None, 6, MR, D_MODEL), lambda n, j: (n, 0, 0, 0)),
                  _const_spec((4, D_MODEL)),
                  per_n,
                  _layer_spec(w_up.shape, layer), _layer_spec(conv_w.shape, layer),
                  _layer_spec(conv_b.shape, layer), _layer_spec(w_down.shape, layer)] + mix_specs,
        out_specs=[tile, per_n],
        out_shape=[jax.ShapeDtypeStruct((NB, R, D_MODEL), F32),
                   jax.ShapeDtypeStruct((NB, nf, 2 * D_FF), F32)],
        scratch_shapes=[pltpu.VMEM((nf, 2 * D_FF), F32),
                        pltpu.VMEM((4, FF_CHUNK // LANES, _ru8(nf) + TM, LANES), F32),
                        pltpu.VMEM((TM, D_FF), BF16)],
        compiler_params=_cparams(2),
        name="conv_ffn",
    )(x, mod, ng, f_prev, w_up, conv_w, conv_b, w_down, *mix_in)


def _cd_in_body(*refs, TM, RS, NT, prompt_attn):
    if prompt_attn:
        (x_ref, mod_ref, ng_ref, dp_ref, win_ref, dw_ref, db_ref, lg_ref, lb_ref, perm_ref,
         q0_ref, q1_ref, q2_ref, kv0_ref, kv1_ref, kv2_ref, c0_ref, c1_ref, c2_ref, yd_ref, dn_ref,
         ed, kvf_ref) = refs
        cache_refs = (c0_ref, c1_ref, c2_ref)
    else:
        (x_ref, mod_ref, ng_ref, dp_ref, win_ref, dw_ref, db_ref, lg_ref, lb_ref,
         q0_ref, q1_ref, q2_ref, kv0_ref, kv1_ref, kv2_ref, kvf_ref, yd_ref, dn_ref, ed) = refs
    q_refs, kv_refs = (q0_ref, q1_ref, q2_ref), (kv0_ref, kv1_ref, kv2_ref)
    j = pl.program_id(1)
    nd = (D_CONV - 1) * RS
    HD = _ru8(nd)

    @pl.when(j == 0)
    def _():
        _slab_put(ed, HD - nd, dp_ref[...])

    x = x_ref[...]
    sh1, sc1 = mod_ref[0], mod_ref[1]
    h = _rms(x, ng_ref[0:1, :]) * (1.0 + sc1) + sh1
    proj = jnp.dot(h.astype(BF16), win_ref[...], preferred_element_type=F32)
    qs = (proj[:, 0:C_Q] * (ATTN_SCALE * LOG2E if prompt_attn else ATTN_SCALE)).astype(BF16)
    ks = proj[:, C_Q:2 * C_Q].astype(BF16)
    vs = proj[:, 2 * C_Q:C_QKV].astype(BF16)
    for g in range(len(C_PAIRS)):
        cols = slice(g * C_GW, (g + 1) * C_GW)
        xg = jnp.concatenate([qs[:, cols], ks[:, cols], vs[:, cols]], axis=-1)
        if prompt_attn and C_PAIRS[g][1] > 1:
            xg = jnp.dot(perm_ref[g - 1], xg, preferred_element_type=F32).astype(BF16)
        q_refs[g][...] = xg[:, 0:C_GW]
        kv_refs[g][...] = xg[:, C_GW:]
    kvf_ref[...] = proj[:, C_Q:C_QKV]
    dv, dg = proj[:, C_QKV:C_QKV + D_WIDTH], proj[:, C_QKV + D_WIDTH:]
    _slab_put(ed, HD, dv * _sigmoid(dg))

    db, lg, lb = db_ref[...], lg_ref[...], lb_ref[...]
    tiles = D_ROWS // SUBLANES
    for c in range(TM // D_ROWS):
        accs = []
        for s in range(D_WIDTH // LANES):
            acc = jnp.zeros((tiles, SUBLANES, LANES), F32)
            for kk in range(D_CONV):
                off = HD - (D_CONV - 1 - kk) * RS + c * D_ROWS
                tap = ed[s, off:off + D_ROWS, :].reshape(tiles, SUBLANES, LANES)
                acc = acc + dw_ref[kk, :, s * LANES:(s + 1) * LANES] * tap
            accs.append(acc.reshape(D_ROWS, LANES))
        zc = jnp.concatenate(accs, axis=-1) + db
        mu = jnp.mean(zc, axis=-1, keepdims=True)
        zc = zc - mu
        var = jnp.mean(zc * zc, axis=-1, keepdims=True)
        yl = zc * lax.rsqrt(var + EPS) * lg + lb
        yd_ref[c * D_ROWS:(c + 1) * D_ROWS, :] = (yl * _sigmoid(yl)).astype(BF16)

    d_last = _slab_get(ed, HD + TM - nd, nd)
    dn_ref[...] = d_last
    if NT > 1:
        _slab_put(ed, HD - nd, d_last)

    if prompt_attn:
        for g, (win, _) in enumerate(C_PAIRS):
            cols = min(win, TM)

            @pl.when(j >= NT - max(win // TM, 1))
            def _(g=g, cols=cols):
                kt = kvf_ref[:, g * C_GW:(g + 1) * C_GW].T
                vt = kvf_ref[:, C_Q + g * C_GW:C_Q + (g + 1) * C_GW].T
                cache_refs[g][0:C_GW, :] = kt[:, TM - cols:]
                cache_refs[g][C_GW:, :] = vt[:, TM - cols:]


def _residue_perm(tm, dil):
    p = np.zeros((tm, tm), np.float32)
    a, r = np.meshgrid(np.arange(tm // dil), np.arange(dil), indexing="ij")
    p[(r * (tm // dil) + a).ravel(), (dil * a + r).ravel()] = 1.0
    return p


def _cd_in(x, mod, ng, d_prev, w_in, d_w, d_b, ln_g, ln_b, *, TM, RS, prompt_attn):
    NB, R, _ = x.shape
    NT = R // TM
    MR = mod.shape[2]
    nd = (D_CONV - 1) * RS
    tile = lambda c: pl.BlockSpec((None, TM, c), lambda n, j: (n, j, 0))
    per_n = pl.BlockSpec((None, nd, D_WIDTH), lambda n, j: (n, 0, 0))
    perm_in, perm_spec = [], []
    if prompt_attn:
        perm_in = [jnp.asarray(np.stack([_residue_perm(TM, dil) for _, dil in C_PAIRS[1:]]), BF16)]
        perm_spec = [_const_spec((len(C_PAIRS) - 1, TM, TM))]
        assert all(win <= R and (win % TM == 0 or TM % win == 0) for win, _ in C_PAIRS)
        kv_specs = [pl.BlockSpec((None, 2 * C_GW, min(win, TM)),
                                 lambda n, j, first=NT - max(win // TM, 1): (n, 0, jnp.maximum(j - first, 0)))
                    for win, _ in C_PAIRS]
        kv_shapes = [jax.ShapeDtypeStruct((NB, 2 * C_GW, win), F32) for win, _ in C_PAIRS]
        kv_scratch = [pltpu.VMEM((TM, 2 * C_Q), F32)]
    else:
        kv_specs = [tile(2 * C_Q)]
        kv_shapes = [jax.ShapeDtypeStruct((NB, R, 2 * C_Q), F32)]
        kv_scratch = []
    return pl.pallas_call(
        functools.partial(_cd_in_body, TM=TM, RS=RS, NT=NT, prompt_attn=prompt_attn),
        grid=(NB, NT),
        in_specs=[tile(D_MODEL),
                  pl.BlockSpec((None, 6, MR, D_MODEL), lambda n, j: (n, 0, 0, 0)),
                  _const_spec((4, D_MODEL)),
                  per_n,
                  _const_spec(w_in.shape), _const_spec(d_w.shape), _const_spec(d_b.shape),
                  _const_spec(ln_g.shape), _const_spec(ln_b.shape)] + perm_spec,
        out_specs=[tile(C_GW)] * 3 + [tile(2 * C_GW)] * 3 + kv_specs + [tile(D_WIDTH), per_n],
        out_shape=[jax.ShapeDtypeStruct((NB, R, C_GW), BF16)] * 3
                  + [jax.ShapeDtypeStruct((NB, R, 2 * C_GW), BF16)] * 3
                  + kv_shapes
                  + [jax.ShapeDtypeStruct((NB, R, D_WIDTH), BF16),
                     jax.ShapeDtypeStruct((NB, nd, D_WIDTH), F32)],
        scratch_shapes=[pltpu.VMEM((D_WIDTH // LANES, _ru8(nd) + TM, LANES), F32)] + kv_scratch,
        compiler_params=_cparams(2),
        name="cd_in",
    )(x, mod, ng, d_prev, w_in, d_w, d_b, ln_g, ln_b, *perm_in)


def _head_of_lane(shape):
    return lax.shift_right_logical(lax.broadcasted_iota(jnp.int32, shape, 1), 6)


def _attn_blocks(blocks):
    lane_head = _head_of_lane((Q_BLOCK, C_GW))
    first_head = lax.broadcasted_iota(jnp.int32, (Q_BLOCK, LANES), 1) < C_HEAD_DIM
    logits = []
    for q, kvp, kvc, _ in blocks:
        zero = jnp.zeros_like(q)
        qs = jnp.concatenate([jnp.where(lane_head == h, q, zero) for h in range(C_HPG)], axis=0)
        kcat = jnp.concatenate([kvp[:, 0:C_GW], kvc[:, 0:C_GW]], axis=0)
        logits.append(lax.dot_general(qs, kcat, (((1,), (1,)), ((), ())), preferred_element_type=F32))
    logits = [lg + bias for lg, (_, _, _, bias) in zip(logits, blocks)]
    ms = [jnp.max(lg, axis=-1, keepdims=True) for lg in logits]
    ps = [jnp.exp2(lg - m) for lg, m in zip(logits, ms)]
    ss = [jnp.sum(p, axis=-1, keepdims=True) for p in ps]
    pvs = []
    for (_, kvp, kvc, _), p in zip(blocks, ps):
        pb = p.astype(BF16)
        vlo = jnp.concatenate([kvp[:, C_GW:C_GW + LANES], kvc[:, C_GW:C_GW + LANES]], axis=0)
        vhi = jnp.concatenate([kvp[:, C_GW + LANES:], kvc[:, C_GW + LANES:]], axis=0)
        pvs.append((jnp.dot(pb[0:2 * Q_BLOCK], vlo, preferred_element_type=F32),
                    jnp.dot(pb[2 * Q_BLOCK:], vhi, preferred_element_type=F32)))
    outs = []
    for (lo, hi), m, s in zip(pvs, ms, ss):
        pick = lambda a, b: jnp.where(first_head, a, b)
        rows = lambda x, h: x[h * Q_BLOCK:(h + 1) * Q_BLOCK]
        merged = lambda x: jnp.concatenate([pick(rows(x, 0), rows(x, 1)), pick(rows(x, 2), rows(x, 3))], axis=-1)
        s_l, m_l = merged(s), merged(m)
        o = jnp.concatenate([pick(lo[0:Q_BLOCK], lo[Q_BLOCK:]), pick(hi[0:Q_BLOCK], hi[Q_BLOCK:])], axis=-1)
        outs.append((o * (1.0 / s_l), m_l + jnp.log2(s_l)))
    return outs


def _blocks_per_trip(count):
    return max(u for u in (5, 4, 3, 2, 1) if count % u == 0)


def _attn_prompt_body(q0, q1, q2, kv0, kv1, kv2, p0, p1, p2, bias_ref, yc_ref, osc, lsc, *, TM):
    qs_, kvs, prevs = (q0, q1, q2), (kv0, kv1, kv2), (p0, p1, p2)
    var = jnp.minimum(pl.program_id(1), 1)
    n_blocks = ATT_ROWS // Q_BLOCK

    def put(g, start, n, stride, o, l, r0):
        rows = pl.ds(start, n, stride=stride) if stride > 1 else pl.ds(start, n)
        for s in range(C_GW // LANES):
            osc[g, s, rows, :] = o[r0:r0 + n, s * LANES:(s + 1) * LANES]
            lsc[g, s, rows, :] = l[r0:r0 + n, s * LANES:(s + 1) * LANES]

    for g, (_, dil) in enumerate(C_PAIRS):
        q_ref, kv_ref, p_ref = qs_[g], kvs[g], prevs[g]
        cls = TM // dil
        if cls >= Q_BLOCK:
            prev_off = Q_BLOCK if dil == 1 else TM
            n_first = prev_off // Q_BLOCK
            per_tile = TM // Q_BLOCK

            def token_start(idx, dil=dil, per_tile=per_tile):
                return idx * Q_BLOCK if dil == 1 else (idx // per_tile) * TM + idx % per_tile

            first = []
            for idx in range(n_first):
                rows = slice(idx * Q_BLOCK, (idx + 1) * Q_BLOCK)
                first.append((q_ref[rows, :], p_ref[rows, :], kv_ref[rows, :], bias_ref[g, var]))
            for idx, (o, l) in enumerate(_attn_blocks(first)):
                put(g, token_start(idx), Q_BLOCK, dil, o, l, 0)

            per_trip = _blocks_per_trip(n_blocks - n_first)

            def body(trip, carry, g=g, dil=dil, q_ref=q_ref, kv_ref=kv_ref, prev_off=prev_off,
                     token_start=token_start, n_first=n_first, per_trip=per_trip):
                idxs = [n_first + trip * per_trip + u for u in range(per_trip)]
                blocks = []
                for idx in idxs:
                    st = pl.multiple_of(idx * Q_BLOCK, Q_BLOCK)
                    blocks.append((q_ref[pl.ds(st, Q_BLOCK), :], kv_ref[pl.ds(st - prev_off, Q_BLOCK), :],
                                   kv_ref[pl.ds(st, Q_BLOCK), :], bias_ref[g, 1]))
                for idx, (o, l) in zip(idxs, _attn_blocks(blocks)):
                    put(g, token_start(idx), Q_BLOCK, dil, o, l, 0)
                return carry

            lax.fori_loop(0, (n_blocks - n_first) // per_trip, body, 0)
        else:
            tiles = Q_BLOCK // cls
            per_trip = _blocks_per_trip(dil)

            def body(trip, carry, g=g, dil=dil, q_ref=q_ref, kv_ref=kv_ref, p_ref=p_ref, cls=cls, tiles=tiles,
                     per_trip=per_trip):
                rs = [trip * per_trip + u for u in range(per_trip)]
                blocks = []
                for r in rs:
                    st = pl.multiple_of(r * cls, cls)
                    gather = lambda ref, st=st: jnp.concatenate(
                        [ref[pl.ds(c * TM + st, cls), :] for c in range(tiles)], axis=0)
                    blocks.append((gather(q_ref), gather(p_ref), gather(kv_ref), bias_ref[g, var]))
                for r, (o, l) in zip(rs, _attn_blocks(blocks)):
                    for c in range(tiles):
                        put(g, c * TM + r, cls, dil, o, l, c * cls)
                return carry

            lax.fori_loop(0, dil // per_trip, body, 0)

    def merge(ch, carry):
        st = pl.multiple_of(ch * Q_BLOCK, Q_BLOCK)
        get = lambda ref, g: jnp.concatenate([ref[g, s, pl.ds(st, Q_BLOCK), :] for s in range(C_GW // LANES)], axis=-1)
        ls = [get(lsc, g) for g in range(len(C_PAIRS))]
        mm = jnp.maximum(jnp.maximum(ls[0], ls[1]), ls[2])
        es = [jnp.exp2(l - mm) for l in ls]
        den = es[0] + es[1] + es[2]
        yc = (es[0] / den) * get(osc, 0) + (es[1] / den) * get(osc, 1) + (es[2] / den) * get(osc, 2)
        yc_ref[pl.ds(st, Q_BLOCK), :] = yc.astype(BF16)
        return carry

    lax.fori_loop(0, n_blocks, merge, 0, unroll=2)


def _attn_prompt(qs, kvs, bias, *, TM):
    N, S, _ = qs[0].shape
    assert S % ATT_ROWS == 0 and ATT_ROWS == Q_BLOCK * C_PAIRS[-1][1] and ATT_ROWS % TM == 0
    cur = lambda c: pl.BlockSpec((None, ATT_ROWS, c), lambda n, i: (n, i, 0))
    prev_rows = [Q_BLOCK if dil == 1 else (TM if TM // dil >= Q_BLOCK else ATT_ROWS) for _, dil in C_PAIRS]
    prev = [pl.BlockSpec((None, pr, 2 * C_GW), lambda n, i, k=ATT_ROWS // pr: (n, jnp.maximum(i * k - 1, 0), 0))
            for pr in prev_rows]
    slabs = C_GW // LANES
    return pl.pallas_call(
        functools.partial(_attn_prompt_body, TM=TM),
        grid=(N, S // ATT_ROWS),
        in_specs=[cur(C_GW)] * 3 + [cur(2 * C_GW)] * 3 + prev + [_const_spec(bias.shape)],
        out_specs=cur(C_GW),
        out_shape=jax.ShapeDtypeStruct((N, S, C_GW), BF16),
        scratch_shapes=[pltpu.VMEM((len(C_PAIRS), slabs, ATT_ROWS, LANES), F32)] * 2,
        compiler_params=_cparams(2),
        name="attn_prompt",
    )(*qs, *kvs, *kvs, bias)


SROWS = 32
SAMPLE_PER_STEP = 2


def _attn_sample_body(q_ref, kvn_ref, c0_ref, c1_ref, c2_ref, bc0_ref, bc1_ref, bc2_ref, bn_ref,
                      yc_ref, n0_ref, n1_ref, n2_ref):
    for e in range(q_ref.shape[0]):
        _attn_sample_one(q_ref.at[e], kvn_ref.at[e], (c0_ref.at[e], c1_ref.at[e], c2_ref.at[e]),
                         (bc0_ref, bc1_ref, bc2_ref), bn_ref, yc_ref.at[e],
                         (n0_ref.at[e], n1_ref.at[e], n2_ref.at[e]))


def _attn_sample_one(q_ref, kvn_ref, caches, bcs, bn_ref, yc_ref, news):
    lane_head = _head_of_lane((SROWS, C_GW))
    row_head = lax.shift_right_logical(lax.broadcasted_iota(jnp.int32, (SROWS, C_GW), 0), 3)
    own = lane_head == row_head
    lane128 = lax.broadcasted_iota(jnp.int32, (SROWS, LANES), 1)
    kvn = kvn_ref[...]
    r0 = 8 - DEC_SEQ
    outs, lses = [], []
    for g, (win, dil) in enumerate(C_PAIRS):
        wb = win
        qg = q_ref[:, g * C_GW:(g + 1) * C_GW]
        qs = jnp.where(own, qg, jnp.zeros_like(qg))
        cache = caches[g]
        kt = cache[0:C_GW, :].astype(BF16)
        vt = cache[C_GW:2 * C_GW, :].astype(BF16)
        kn = kvn[:, g * C_GW:(g + 1) * C_GW]
        vn = kvn[:, C_Q + g * C_GW:C_Q + (g + 1) * C_GW]
        lc = jnp.dot(qs, kt, preferred_element_type=F32) + bcs[g][...]
        qf = qs.astype(F32)
        ln = bn_ref[g]
        for c in range(DEC_SEQ):
            d = jnp.sum(qf * kn[r0 + c:r0 + c + 1, :], axis=-1, keepdims=True)
            ln = ln + jnp.where(lane128 == c, d, 0.0)
        m = jnp.maximum(jnp.max(lc, axis=-1, keepdims=True), jnp.max(ln, axis=-1, keepdims=True))
        pc = jnp.exp(lc - m)
        pn = jnp.exp(ln - m)
        s = jnp.sum(pc, axis=-1, keepdims=True) + jnp.sum(pn, axis=-1, keepdims=True)
        pv = lax.dot_general(pc.astype(BF16), vt, (((1,), (1,)), ((), ())), preferred_element_type=F32)
        for c in range(DEC_SEQ):
            pcol = jnp.sum(jnp.where(lane128 == c, pn, 0.0), axis=-1, keepdims=True)
            pv = pv + pcol * vn[r0 + c:r0 + c + 1, :]
        outs.append(pv / s)
        lses.append(m + jnp.log(s))
        rolled = pltpu.roll(cache[...], wb - DEC_SEQ, 1)
        new_rows = jnp.concatenate([jnp.zeros((LANES - SUBLANES, 2 * C_GW), F32),
                                    jnp.concatenate([kn, vn], axis=-1)], axis=0)
        new_cols = new_rows.T
        lane_t = lax.broadcasted_iota(jnp.int32, (2 * C_GW, LANES), 1)
        if wb > LANES:
            news[g][:, 0:wb - LANES] = rolled[:, 0:wb - LANES]
        news[g][:, wb - LANES:wb] = jnp.where(lane_t >= LANES - DEC_SEQ, new_cols, rolled[:, wb - LANES:wb])
    mm = jnp.maximum(jnp.maximum(lses[0], lses[1]), lses[2])
    es = [jnp.exp(l - mm) for l in lses]
    den = es[0] + es[1] + es[2]
    y = (es[0] / den) * outs[0] + (es[1] / den) * outs[1] + (es[2] / den) * outs[2]
    y = jnp.where(own, y, 0.0)
    yc_ref[...] = y[0:8, :] + y[8:16, :] + y[16:24, :] + y[24:32, :]


def _attn_sample(q_rep, kvn, caches, bias_c, bias_n):
    NBt = q_rep.shape[0]
    assert NBt % SAMPLE_PER_STEP == 0
    per_b = lambda r, c: pl.BlockSpec((SAMPLE_PER_STEP, r, c), lambda b: (b, 0, 0))
    wbs = [w for w, _ in C_PAIRS]
    return pl.pallas_call(
        _attn_sample_body,
        grid=(NBt // SAMPLE_PER_STEP,),
        in_specs=[per_b(SROWS, C_Q), per_b(8, 2 * C_Q)] + [per_b(2 * C_GW, w) for w in wbs]
                 + [_const_spec((SROWS, w)) for w in wbs] + [_const_spec((len(wbs), SROWS, LANES))],
        out_specs=[per_b(8, C_GW)] + [per_b(2 * C_GW, w) for w in wbs],
        out_shape=[jax.ShapeDtypeStruct((NBt, 8, C_GW), F32)]
                  + [jax.ShapeDtypeStruct((NBt, 2 * C_GW, w), F32) for w in wbs],
        compiler_params=_cparams(1),
        name="attn_sample",
    )(q_rep, kvn, *caches, *bias_c, bias_n)


def _t5_bucket(dist):
    dist = np.asarray(dist)
    max_exact = N_BUCKETS // 2
    large = max_exact + (np.log(np.maximum(dist, max_exact) / max_exact) / np.log(MAX_DISTANCE / max_exact)
                         * (N_BUCKETS - max_exact)).astype(np.int32)
    large = np.minimum(large, N_BUCKETS - 1)
    return np.where(dist < max_exact, dist, large).astype(np.int32)


def _group_bias(rel_bias, g, dil):
    buckets = _t5_bucket(dil * np.arange(C_TAPS + 1))
    return rel_bias[buckets][:, g * C_HPG:(g + 1) * C_HPG].T


def _toeplitz_body(c_ref, o_ref):
    keep = lax.broadcasted_iota(jnp.int32, (Q_BLOCK, 2 * Q_BLOCK), 1) >= Q_BLOCK
    for h in range(C_HPG):
        taps = jnp.broadcast_to(c_ref[h:h + 1, :], (Q_BLOCK, 2 * Q_BLOCK))
        t = pltpu.roll(taps, 0, 1, stride=1, stride_axis=0)
        o_ref[1, h * Q_BLOCK:(h + 1) * Q_BLOCK, :] = t
        o_ref[0, h * Q_BLOCK:(h + 1) * Q_BLOCK, :] = jnp.where(keep, t, NEG)


def _prompt_bias(bias_gs):
    c = jnp.stack([jnp.concatenate([b[:, ::-1].astype(F32) * LOG2E, jnp.full((C_HPG, Q_BLOCK - 1), NEG, F32)], axis=1)
                   for b in bias_gs])
    return pl.pallas_call(
        _toeplitz_body,
        grid=(len(bias_gs),),
        in_specs=[pl.BlockSpec((None, C_HPG, 2 * Q_BLOCK), lambda g: (g, 0, 0))],
        out_specs=pl.BlockSpec((None, 2, C_HPG * Q_BLOCK, 2 * Q_BLOCK), lambda g: (g, 0, 0, 0)),
        out_shape=jax.ShapeDtypeStruct((len(bias_gs), 2, C_HPG * Q_BLOCK, 2 * Q_BLOCK), F32),
        compiler_params=_cparams(1),
        name="attn_bias",
    )(c)


def _sample_bias(bias_g, wb, dil):
    n = wb + DEC_SEQ
    t = np.arange(DEC_SEQ)[:, None]
    d = wb + t - np.arange(n)[None, :]
    valid = (d >= 0) & (d % dil == 0) & (d // dil <= C_TAPS)
    place = np.zeros((DEC_SEQ, C_TAPS + 1, n), np.float32)
    ti, ii = np.nonzero(valid)
    place[ti, (d // dil)[ti, ii], ii] = 1.0
    ext = jnp.einsum("hj,tji->hti", bias_g.astype(F32), place, precision=lax.Precision.HIGHEST)
    ext = jnp.where(valid[None], ext, NEG)
    pad_rows = (0, SUBLANES - DEC_SEQ)
    bc = jnp.pad(ext[:, :, :wb], ((0, 0), pad_rows, (0, 0))).reshape(SROWS, wb)
    bn = jnp.pad(ext[:, :, wb:], ((0, 0), pad_rows, (0, LANES - DEC_SEQ)), constant_values=NEG).reshape(SROWS, LANES)
    return bc, bn


def _time_major(s):
    b, k, c = s.shape
    return s.transpose(1, 0, 2).reshape(1, k * b, c)


def _batch_major(s, b):
    _, r, c = s.shape
    return s.reshape(r // b, b, c).transpose(1, 0, 2)


def _stack(x, mod, st, w, *, TM, TM_WIDE, RS, pos0):
    new = {}
    ng = w["norm_g"]
    cast_keys = [k for k in ("ffn_w_up", "ffn_w_down", "cd_w_in", "cd_w_out") if w[k].dtype != BF16]
    x, new["a"], new["b"], cast_out = _mixer_ab(
        x, mod[0], ng[0], st["a"], st["b"], w["ab_w_in"], w["a_conv_w"], w["b_w_grp"], w["b_scale"], w["ab_w_out"],
        TM=TM_WIDE, RS=RS, pos0=pos0, cast=tuple(w[k] for k in cast_keys))
    w = dict(w, **dict(zip(cast_keys, cast_out)))
    x, new["f0"] = _conv_ffn(x, mod[0], ng[0], st["f0"], w["ffn_w_up"], w["ffn_conv_w"],
                             w["ffn_conv_b"], w["ffn_w_down"], layer=0, TM=TM_WIDE, RS=RS)
    prompt = RS == 1
    outs = _cd_in(x, mod[1], ng[1], st["d"], w["cd_w_in"][0], w["d_conv_w"], w["d_conv_b"], w["d_ln_g"], w["d_ln_b"],
                  TM=TM, RS=RS, prompt_attn=prompt)
    qs, kvs, yd, new["d"] = outs[0:3], outs[3:6], outs[-2], outs[-1]
    cache_layout = lambda n: n.reshape(n.shape[0], 2, C_HPG, C_HEAD_DIM, n.shape[2]).transpose(0, 4, 1, 2, 3)
    if prompt:
        yc = _attn_prompt(qs, kvs, _prompt_bias(w["bias_g"]), TM=TM)
        for g in range(len(C_PAIRS)):
            new["c%d" % g] = cache_layout(outs[6 + g])
    else:
        B = RS
        kvf = outs[6]
        qb = _batch_major(jnp.concatenate(qs, axis=-1), B)
        q_rep = jnp.pad(jnp.broadcast_to(qb[:, None], (B, C_HPG, DEC_SEQ, C_Q)),
                        ((0, 0), (0, 0), (0, 8 - DEC_SEQ), (0, 0))).reshape(B, SROWS, C_Q)
        kvn = jnp.pad(_batch_major(kvf, B), ((0, 0), (8 - DEC_SEQ, 0), (0, 0)))
        bias = [_sample_bias(w["bias_g"][g], win, dil) for g, (win, dil) in enumerate(C_PAIRS)]
        yc, n0, n1, n2 = _attn_sample(q_rep, kvn, st["c"], [b[0] for b in bias],
                                      jnp.stack([b[1] for b in bias]))
        for g, n in enumerate((n0, n1, n2)):
            new["c%d" % g] = cache_layout(n)
        yc = _time_major(yc[:, :DEC_SEQ])
    x, new["f1"] = _conv_ffn(x, mod[1], ng[1], st["f1"], w["ffn_w_up"], w["ffn_conv_w"],
                             w["ffn_conv_b"], w["ffn_w_down"], layer=1, TM=TM_WIDE, RS=RS,
                             mixer_out=(yc, yd, w["cd_w_out"][0]))
    return x, new, w


def kernel(x_prompt, x_sample, state_a_conv, state_b_pool, cache_c_win128, cache_c_win512, cache_c_win2048,
           state_d_conv, state_ffn_conv, c_prompt, c_sample, ada_w, ada_b, norm_g, rel_bias, ab_w_in, a_conv_w,
           b_w_grp, b_scale, ab_w_out, cd_w_in, d_conv_w, d_conv_b, d_ln_g, d_ln_b, cd_w_out, ffn_w_up,
           ffn_conv_w, ffn_conv_b, ffn_w_down):
    B, T = DEC_BATCH, DEC_SEQ
    w = dict(norm_g=norm_g,
             ab_w_in=ab_w_in[0].astype(BF16), a_conv_w=a_conv_w[0], b_w_grp=b_w_grp[0].astype(BF16),
             b_scale=b_scale, ab_w_out=ab_w_out[0].astype(BF16),
             cd_w_in=cd_w_in, cd_w_out=cd_w_out,
             d_conv_w=jnp.broadcast_to(d_conv_w[0][:, None, :], (D_CONV, SUBLANES, D_WIDTH)),
             d_conv_b=d_conv_b, d_ln_g=d_ln_g, d_ln_b=d_ln_b,
             ffn_w_up=ffn_w_up, ffn_conv_w=ffn_conv_w, ffn_conv_b=ffn_conv_b[:, None, :], ffn_w_down=ffn_w_down,
             bias_g=[_group_bias(rel_bias, g, dil) for g, (_, dil) in enumerate(C_PAIRS)])

    mod = _ada(jnp.concatenate([c_prompt, c_sample], axis=0), ada_w, ada_b)
    mod_p = mod[:, :BATCH].reshape(DEPTH, BATCH, 6, 1, D_MODEL)
    mod_s = mod[:, BATCH:].reshape(DEPTH, B, 6, D_MODEL).transpose(0, 2, 1, 3)
    mod_s = jnp.broadcast_to(mod_s[:, :, None], (DEPTH, 6, T, B, D_MODEL)).reshape(DEPTH, 1, 6, T * B, D_MODEL)

    zeros = lambda k, c: jnp.zeros((BATCH, k, c), F32)
    st_p = dict(a=zeros(A_CONV - 1, A_WIDTH), b=zeros(B_PREV, B_WIDTH), d=zeros(D_CONV - 1, D_WIDTH),
                f0=zeros(FFN_CONV - 1, 2 * D_FF), f1=zeros(FFN_CONV - 1, 2 * D_FF))
    y_p, np_, w = _stack(x_prompt, mod_p, st_p, w, TM=TM_PROMPT, TM_WIDE=TM_WIDE_PROMPT, RS=1, pos0=0)

    st_s = dict(a=_time_major(state_a_conv[0]), b=_time_major(state_b_pool[0]), d=_time_major(state_d_conv[0]),
                f0=_time_major(state_ffn_conv[0]), f1=_time_major(state_ffn_conv[1]),
                c=[c[0].transpose(0, 2, 3, 4, 1).reshape(B, 2 * C_GW, c.shape[2])
                   for c in (cache_c_win128, cache_c_win512, cache_c_win2048)])
    y_s, ns, _ = _stack(_time_major(x_sample), mod_s, st_s, w, TM=T * B, TM_WIDE=T * B, RS=B, pos0=PAST_LEN)

    bm = lambda s: _batch_major(s, B)
    return (y_p, bm(y_s),
            np_["a"][None], bm(ns["a"])[None], np_["b"][None], bm(ns["b"])[None],
            np_["c0"][None], ns["c0"][None], np_["c1"][None], ns["c1"][None], np_["c2"][None], ns["c2"][None],
            np_["d"][None], bm(ns["d"])[None],
            jnp.stack([np_["f0"], np_["f1"]]), jnp.stack([bm(ns["f0"]), bm(ns["f1"])]))
```

```python
import functools
import math

import numpy as np
import jax
import jax.numpy as jnp
from jax import lax
from jax.experimental import pallas as pl
from jax.experimental.pallas import tpu as pltpu

D_MODEL = 1024
BATCH = 4
SEQ = 4096
DEPTH = 2
DEC_BATCH = 32
DEC_SEQ = 4
PAST_LEN = 8192
EPS = 1e-6
A_WIDTH = 512
A_CONV = 3
B_WIDTH = 512
B_WINDOWS = (2, 4, 8, 16)
B_GROUP = 128
B_PREV = 15
C_PAIRS = ((128, 1), (512, 4), (2048, 16))
C_HPG = 4
C_HEAD_DIM = 64
C_HEADS = 12
C_Q = C_HEADS * C_HEAD_DIM
C_QKV = 3 * C_Q
C_GW = C_HPG * C_HEAD_DIM
C_TAPS = 128
ATTN_SCALE = C_HEAD_DIM ** -0.5
LOG2E = math.log2(math.e)
Q_BLOCK = 128
N_BUCKETS = 32
MAX_DISTANCE = 2048
D_WIDTH = 512
D_CONV = 31
D_FF = 2816
FFN_CONV = 3

SUBLANES = 8
LANES = 128
VMEM_LIMIT = 56 * 1024 * 1024
NEG = -1e30
TM_PROMPT = 512
TM_WIDE_PROMPT = 1024
ATT_ROWS = 2048
ADA_COLS = 1536
FF_CHUNK = 256
D_ROWS = 32

F32 = jnp.float32
BF16 = jnp.bfloat16


def _ru8(n):
    return -(-n // SUBLANES) * SUBLANES


def _pool_levels(rs):
    l1 = _ru8(rs)
    l2 = _ru8(l1 + 2 * rs)
    l3 = _ru8(l2 + 4 * rs)
    l4 = _ru8(l3 + 8 * rs)
    return l1, l2, l3, l4


def _slab_put(ref, r0, val):
    n = val.shape[0]
    for s in range(val.shape[1] // LANES):
        ref[s, r0:r0 + n, :] = val[:, s * LANES:(s + 1) * LANES]


def _slab_get(ref, r0, n, s0=0, ns=None):
    ns = ref.shape[0] - s0 if ns is None else ns
    return jnp.concatenate([ref[s, r0:r0 + n, :] for s in range(s0, s0 + ns)], axis=-1)


def _rms(x, g):
    return x * lax.rsqrt(jnp.mean(x * x, axis=-1, keepdims=True) + EPS) * g


def _sigmoid(x):
    return 1.0 / (1.0 + jnp.exp(-x))


def _cparams(n_axes):
    return pltpu.CompilerParams(dimension_semantics=("arbitrary",) * n_axes, vmem_limit_bytes=VMEM_LIMIT)


def _const_spec(shape):
    nd = len(shape)
    return pl.BlockSpec(shape, lambda *_: (0,) * nd, pipeline_mode=pl.Buffered(1))


def _ada_body(c_ref, w_ref, b_ref, o_ref):
    c = c_ref[...]
    ca = c * _sigmoid(c)
    o_ref[...] = jnp.dot(ca.astype(BF16), w_ref[...].astype(BF16), preferred_element_type=F32) + b_ref[...]


def _ada(c_all, ada_w, ada_b):
    rows = c_all.shape[0]
    tn = ADA_COLS
    return pl.pallas_call(
        _ada_body,
        grid=(DEPTH, 6 * D_MODEL // tn),
        in_specs=[pl.BlockSpec((rows, D_MODEL), lambda l, n: (0, 0)),
                  pl.BlockSpec((None, D_MODEL, tn), lambda l, n: (l, 0, n)),
                  pl.BlockSpec((None, 1, tn), lambda l, n: (l, 0, n))],
        out_specs=pl.BlockSpec((None, rows, tn), lambda l, n: (l, 0, n)),
        out_shape=jax.ShapeDtypeStruct((DEPTH, rows, 6 * D_MODEL), F32),
        compiler_params=_cparams(2),
        name="ada",
    )(c_all, ada_w, ada_b.reshape(DEPTH, 1, 6 * D_MODEL))


def _ab_body(*refs, TM, RS, NT, pos0, n_cast):
    (x_ref, mod_ref, ng_ref, ap_ref, bp_ref, win_ref, aw_ref, bw_ref, bs_ref, wout_ref) = refs[:10]
    cast_in = refs[10:10 + n_cast]
    x1_ref, an_ref, bn_ref = refs[10 + n_cast:13 + n_cast]
    cast_out = refs[13 + n_cast:13 + 2 * n_cast]
    ea, eb, s2, s4, s8 = refs[13 + 2 * n_cast:]
    for src, dst in zip(cast_in, cast_out):
        dst[...] = src[...].astype(BF16)
    j = pl.program_id(1)
    HA = _ru8((A_CONV - 1) * RS)
    L1, L2, L3, HB = _pool_levels(RS)
    E = HB + TM
    na, nbp = (A_CONV - 1) * RS, B_PREV * RS

    @pl.when(j == 0)
    def _():
        _slab_put(ea, HA - na, ap_ref[...])
        if HB > nbp:
            _slab_put(eb, 0, jnp.zeros((HB - nbp, B_WIDTH), F32))
        _slab_put(eb, HB - nbp, bp_ref[...])

    x = x_ref[...]
    sh1, sc1, g1 = mod_ref[0], mod_ref[1], mod_ref[2]
    h = _rms(x, ng_ref[0:1, :] * (1.0 + sc1)) + sh1
    proj = jnp.dot(h.astype(BF16), win_ref[...], preferred_element_type=F32)
    hh, bg = proj[:, 0:A_WIDTH], proj[:, A_WIDTH:2 * A_WIDTH]
    cg, u = proj[:, 2 * A_WIDTH:3 * A_WIDTH], proj[:, 3 * A_WIDTH:]

    v = cg * hh
    _slab_put(ea, HA, v)
    z = (aw_ref[2:3, :] * v + aw_ref[1:2, :] * _slab_get(ea, HA - RS, TM)
         + aw_ref[0:1, :] * _slab_get(ea, HA - 2 * RS, TM))
    ya = bg * z
    a_last = _slab_get(ea, HA + TM - na, na)
    an_ref[...] = a_last
    if NT > 1:
        _slab_put(ea, HA - na, a_last)

    _slab_put(eb, HB, u)
    _slab_put(s2, L1, _slab_get(eb, L1, E - L1) + _slab_get(eb, L1 - RS, E - L1))
    _slab_put(s4, L2, _slab_get(s2, L2, E - L2, 1) + _slab_get(s2, L2 - 2 * RS, E - L2, 1))
    _slab_put(s8, L3, _slab_get(s4, L3, E - L3, 1) + _slab_get(s4, L3 - 4 * RS, E - L3, 1))
    wsum = (s2[0, HB:E, :], s4[0, HB:E, :], s8[0, HB:E, :], s8[1, HB:E, :] + s8[1, HB - 8 * RS:E - 8 * RS, :])
    b_last = _slab_get(eb, E - nbp, nbp)
    bn_ref[...] = b_last
    if NT > 1:
        _slab_put(eb, HB - nbp, b_last)
    row = lax.broadcasted_iota(jnp.int32, (TM, B_GROUP), 0) + j * TM
    pos1 = lax.shift_right_logical(row, int(math.log2(RS))) + (pos0 + 1)
    ybs = []
    for g, win in enumerate(B_WINDOWS):
        cnt = jnp.minimum(pos1, win).astype(F32)
        pooled = wsum[g] / cnt - u[:, g * B_GROUP:(g + 1) * B_GROUP]
        ybs.append(jnp.dot(pooled.astype(BF16), bw_ref[g], preferred_element_type=F32))
    yb = jnp.concatenate(ybs, axis=-1) * bs_ref[...]

    ycat = jnp.concatenate([ya, yb], axis=-1).astype(BF16)
    y = jnp.dot(ycat, wout_ref[...], preferred_element_type=F32)
    x1_ref[...] = x + _rms(y, ng_ref[1:2, :] * g1)


def _mixer_ab(x, mod, ng, a_prev, b_prev, w_in, a_w, b_w, b_scale, w_out, *, TM, RS, pos0, cast=()):
    NB, R, _ = x.shape
    NT = R // TM
    MR = mod.shape[2]
    steps = NB * NT
    cast_specs, cast_shapes = [], []
    for a in cast:
        rows = a.shape[1] // steps
        assert rows * steps == a.shape[1] and rows % (2 * SUBLANES) == 0
        cast_specs.append(pl.BlockSpec((a.shape[0], rows, a.shape[2]), lambda n, j: (0, n * NT + j, 0)))
        cast_shapes.append(jax.ShapeDtypeStruct(a.shape, BF16))
    _, _, _, HB = _pool_levels(RS)
    HA = _ru8((A_CONV - 1) * RS)
    E = HB + TM
    na, nbp = (A_CONV - 1) * RS, B_PREV * RS
    tile = lambda c: pl.BlockSpec((None, TM, c), lambda n, j: (n, j, 0))
    per_n = lambda r, c: pl.BlockSpec((None, r, c), lambda n, j: (n, 0, 0))
    outs = pl.pallas_call(
        functools.partial(_ab_body, TM=TM, RS=RS, NT=NT, pos0=pos0, n_cast=len(cast)),
        grid=(NB, NT),
        in_specs=[tile(D_MODEL),
                  pl.BlockSpec((None, 6, MR, D_MODEL), lambda n, j: (n, 0, 0, 0)),
                  _const_spec((4, D_MODEL)),
                  per_n(na, A_WIDTH), per_n(nbp, B_WIDTH),
                  _const_spec(w_in.shape), _const_spec(a_w.shape), _const_spec(b_w.shape),
                  _const_spec(b_scale.shape), _const_spec(w_out.shape)] + cast_specs,
        out_specs=[tile(D_MODEL), per_n(na, A_WIDTH), per_n(nbp, B_WIDTH)] + cast_specs,
        out_shape=[jax.ShapeDtypeStruct((NB, R, D_MODEL), F32),
                   jax.ShapeDtypeStruct((NB, na, A_WIDTH), F32),
                   jax.ShapeDtypeStruct((NB, nbp, B_WIDTH), F32)] + cast_shapes,
        scratch_shapes=[pltpu.VMEM((4, HA + TM, LANES), F32), pltpu.VMEM((4, E, LANES), F32),
                        pltpu.VMEM((4, E, LANES), F32), pltpu.VMEM((3, E, LANES), F32),
                        pltpu.VMEM((2, E, LANES), F32)],
        compiler_params=_cparams(2),
        name="mixer_ab",
    )(x, mod, ng, a_prev, b_prev, w_in, a_w, b_w, b_scale, w_out, *cast)
    return outs[0], outs[1], outs[2], tuple(outs[3:])


def _ffn_body(*refs, TM, RS, NT, mixer_out):
    if mixer_out:
        (x_ref, mod_ref, ng_ref, fp_ref, wup_ref, cw_ref, cb_ref, wdn_ref, yc_ref, yd_ref, wout_ref,
         y_ref, fn_ref, halo, ext, act) = refs
    else:
        (x_ref, mod_ref, ng_ref, fp_ref, wup_ref, cw_ref, cb_ref, wdn_ref,
         y_ref, fn_ref, halo, ext, act) = refs
    j = pl.program_id(1)
    nf = (FFN_CONV - 1) * RS
    HF = _ru8(nf)

    @pl.when(j == 0)
    def _():
        halo[...] = fp_ref[...]

    x = x_ref[...]
    if mixer_out:
        ym = (jnp.dot(yc_ref[...].astype(BF16), wout_ref[0:C_GW, :], preferred_element_type=F32)
              + jnp.dot(yd_ref[...], wout_ref[C_GW:, :], preferred_element_type=F32))
        x = x + _rms(ym, ng_ref[1:2, :] * mod_ref[2])
    sh2, sc2, g2 = mod_ref[3], mod_ref[4], mod_ref[5]
    hb = (_rms(x, ng_ref[2:3, :] * (1.0 + sc2)) + sh2).astype(BF16)

    def conv_part(col, buf):
        up = jnp.dot(hb, wup_ref[:, col:col + FF_CHUNK], preferred_element_type=F32)
        _slab_put(buf, HF - nf, halo[:, col:col + FF_CHUNK])
        _slab_put(buf, HF, up)
        out = (cw_ref[2:3, col:col + FF_CHUNK] * up
               + cw_ref[1:2, col:col + FF_CHUNK] * _slab_get(buf, HF - RS, TM)
               + cw_ref[0:1, col:col + FF_CHUNK] * _slab_get(buf, HF - 2 * RS, TM)
               + cb_ref[:, col:col + FF_CHUNK])
        halo[:, col:col + FF_CHUNK] = _slab_get(buf, HF + TM - nf, nf)
        return out

    for c in range(D_FF // FF_CHUNK):
        a = conv_part(c * FF_CHUNK, ext.at[2 * (c % 2)])
        g = conv_part(D_FF + c * FF_CHUNK, ext.at[2 * (c % 2) + 1])
        act[:, c * FF_CHUNK:(c + 1) * FF_CHUNK] = (a * (g * _sigmoid(g))).astype(BF16)
    fn_ref[...] = halo[...]
    y = jnp.dot(act[...], wdn_ref[...], preferred_element_type=F32)
    y_ref[...] = x + _rms(y, ng_ref[3:4, :] * g2)


def _layer_spec(shape, layer):
    nd = len(shape) - 1
    return pl.BlockSpec((None,) + tuple(shape[1:]), lambda *_: (layer,) + (0,) * nd, pipeline_mode=pl.Buffered(1))


def _conv_ffn(x, mod, ng, f_prev, w_up, conv_w, conv_b, w_down, *, layer, TM, RS, mixer_out=None):
    NB, R, _ = x.shape
    NT = R // TM
    MR = mod.shape[2]
    nf = (FFN_CONV - 1) * RS
    tile_c = lambda c: pl.BlockSpec((None, TM, c), lambda n, j: (n, j, 0))
    tile = tile_c(D_MODEL)
    per_n = pl.BlockSpec((None, nf, 2 * D_FF), lambda n, j: (n, 0, 0))
    mix_in, mix_specs = [], []
    if mixer_out is not None:
        mix_in = list(mixer_out)
        mix_specs = [tile_c(C_GW), tile_c(D_WIDTH), _const_spec(mixer_out[2].shape)]
    return pl.pallas_call(
        functools.partial(_ffn_body, TM=TM, RS=RS, NT=NT, mixer_out=mixer_out is not None),
        grid=(NB, NT),
        in_specs=[tile,
                  pl.BlockSpec((None, 6, MR, D_MODEL), lambda n, j: (n, 0, 0, 0)),
                  _const_spec((4, D_MODEL)),
                  per_n,
                  _layer_spec(w_up.shape, layer), _layer_spec(conv_w.shape, layer),
                  _layer_spec(conv_b.shape, layer), _layer_spec(w_down.shape, layer)] + mix_specs,
        out_specs=[tile, per_n],
        out_shape=[jax.ShapeDtypeStruct((NB, R, D_MODEL), F32),
                   jax.ShapeDtypeStruct((NB, nf, 2 * D_FF), F32)],
        scratch_shapes=[pltpu.VMEM((nf, 2 * D_FF), F32),
                        pltpu.VMEM((4, FF_CHUNK // LANES, _ru8(nf) + TM, LANES), F32),
                        pltpu.VMEM((TM, D_FF), BF16)],
        compiler_params=_cparams(2),
        name="conv_ffn",
    )(x, mod, ng, f_prev, w_up, conv_w, conv_b, w_down, *mix_in)


def _cd_in_body(*refs, TM, RS, NT, prompt_attn):
    if prompt_attn:
        (x_ref, mod_ref, ng_ref, dp_ref, win_ref, dw_ref, db_ref, lg_ref, lb_ref, perm_ref,
         q0_ref, q1_ref, q2_ref, kv0_ref, kv1_ref, kv2_ref, c0_ref, c1_ref, c2_ref, yd_ref, dn_ref,
         ed, kvf_ref) = refs
        cache_refs = (c0_ref, c1_ref, c2_ref)
    else:
        (x_ref, mod_ref, ng_ref, dp_ref, win_ref, dw_ref, db_ref, lg_ref, lb_ref,
         q0_ref, q1_ref, q2_ref, kv0_ref, kv1_ref, kv2_ref, kvf_ref, yd_ref, dn_ref, ed) = refs
    q_refs, kv_refs = (q0_ref, q1_ref, q2_ref), (kv0_ref, kv1_ref, kv2_ref)
    j = pl.program_id(1)
    nd = (D_CONV - 1) * RS
    HD = _ru8(nd)

    @pl.when(j == 0)
    def _():
        _slab_put(ed, HD - nd, dp_ref[...])

    x = x_ref[...]
    sh1, sc1 = mod_ref[0], mod_ref[1]
    h = _rms(x, ng_ref[0:1, :] * (1.0 + sc1)) + sh1
    proj = jnp.dot(h.astype(BF16), win_ref[...], preferred_element_type=F32)
    qs = (proj[:, 0:C_Q] * (ATTN_SCALE * LOG2E if prompt_attn else ATTN_SCALE)).astype(BF16)
    ks = proj[:, C_Q:2 * C_Q].astype(BF16)
    vs = proj[:, 2 * C_Q:C_QKV].astype(BF16)
    for g in range(len(C_PAIRS)):
        cols = slice(g * C_GW, (g + 1) * C_GW)
        xg = jnp.concatenate([qs[:, cols], ks[:, cols], vs[:, cols]], axis=-1)
        if prompt_attn and C_PAIRS[g][1] > 1:
            xg = jnp.dot(perm_ref[g - 1], xg, preferred_element_type=F32).astype(BF16)
        q_refs[g][...] = xg[:, 0:C_GW]
        kv_refs[g][...] = xg[:, C_GW:]
    kvf_ref[...] = proj[:, C_Q:C_QKV]
    dv, dg = proj[:, C_QKV:C_QKV + D_WIDTH], proj[:, C_QKV + D_WIDTH:]
    _slab_put(ed, HD, dv * _sigmoid(dg))

    db, lg, lb = db_ref[...], lg_ref[...], lb_ref[...]
    tiles = D_ROWS // SUBLANES
    for c in range(TM // D_ROWS):
        accs = []
        for s in range(D_WIDTH // LANES):
            acc = jnp.zeros((tiles, SUBLANES, LANES), F32)
            for kk in range(D_CONV):
                off = HD - (D_CONV - 1 - kk) * RS + c * D_ROWS
                tap = ed[s, off:off + D_ROWS, :].reshape(tiles, SUBLANES, LANES)
                acc = acc + dw_ref[kk, :, s * LANES:(s + 1) * LANES] * tap
            accs.append(acc.reshape(D_ROWS, LANES))
        zc = jnp.concatenate(accs, axis=-1) + db
        mu = jnp.mean(zc, axis=-1, keepdims=True)
        zc = zc - mu
        var = jnp.mean(zc * zc, axis=-1, keepdims=True)
        yl = zc * lax.rsqrt(var + EPS) * lg + lb
        yd_ref[c * D_ROWS:(c + 1) * D_ROWS, :] = (yl * _sigmoid(yl)).astype(BF16)

    d_last = _slab_get(ed, HD + TM - nd, nd)
    dn_ref[...] = d_last
    if NT > 1:
        _slab_put(ed, HD - nd, d_last)

    if prompt_attn:
        for g, (win, _) in enumerate(C_PAIRS):
            cols = min(win, TM)

            @pl.when(j >= NT - max(win // TM, 1))
            def _(g=g, cols=cols):
                kt = kvf_ref[:, g * C_GW:(g + 1) * C_GW].T
                vt = kvf_ref[:, C_Q + g * C_GW:C_Q + (g + 1) * C_GW].T
                cache_refs[g][0:C_GW, :] = kt[:, TM - cols:]
                cache_refs[g][C_GW:, :] = vt[:, TM - cols:]


def _residue_perm(tm, dil):
    p = np.zeros((tm, tm), np.float32)
    a, r = np.meshgrid(np.arange(tm // dil), np.arange(dil), indexing="ij")
    p[(r * (tm // dil) + a).ravel(), (dil * a + r).ravel()] = 1.0
    return p


def _cd_in(x, mod, ng, d_prev, w_in, d_w, d_b, ln_g, ln_b, *, TM, RS, prompt_attn):
    NB, R, _ = x.shape
    NT = R // TM
    MR = mod.shape[2]
    nd = (D_CONV - 1) * RS
    tile = lambda c: pl.BlockSpec((None, TM, c), lambda n, j: (n, j, 0))
    per_n = pl.BlockSpec((None, nd, D_WIDTH), lambda n, j: (n, 0, 0))
    perm_in, perm_spec = [], []
    if prompt_attn:
        perm_in = [jnp.asarray(np.stack([_residue_perm(TM, dil) for _, dil in C_PAIRS[1:]]), BF16)]
        perm_spec = [_const_spec((len(C_PAIRS) - 1, TM, TM))]
        assert all(win <= R and (win % TM == 0 or TM % win == 0) for win, _ in C_PAIRS)
        kv_specs = [pl.BlockSpec((None, 2 * C_GW, min(win, TM)),
                                 lambda n, j, first=NT - max(win // TM, 1): (n, 0, jnp.maximum(j - first, 0)))
                    for win, _ in C_PAIRS]
        kv_shapes = [jax.ShapeDtypeStruct((NB, 2 * C_GW, win), F32) for win, _ in C_PAIRS]
        kv_scratch = [pltpu.VMEM((TM, 2 * C_Q), F32)]
    else:
        kv_specs = [tile(2 * C_Q)]
        kv_shapes = [jax.ShapeDtypeStruct((NB, R, 2 * C_Q), F32)]
        kv_scratch = []
    return pl.pallas_call(
        functools.partial(_cd_in_body, TM=TM, RS=RS, NT=NT, prompt_attn=prompt_attn),
        grid=(NB, NT),
        in_specs=[tile(D_MODEL),
                  pl.BlockSpec((None, 6, MR, D_MODEL), lambda n, j: (n, 0, 0, 0)),
                  _const_spec((4, D_MODEL)),
                  per_n,
                  _const_spec(w_in.shape), _const_spec(d_w.shape), _const_spec(d_b.shape),
                  _const_spec(ln_g.shape), _const_spec(ln_b.shape)] + perm_spec,
        out_specs=[tile(C_GW)] * 3 + [tile(2 * C_GW)] * 3 + kv_specs + [tile(D_WIDTH), per_n],
        out_shape=[jax.ShapeDtypeStruct((NB, R, C_GW), BF16)] * 3
                  + [jax.ShapeDtypeStruct((NB, R, 2 * C_GW), BF16)] * 3
                  + kv_shapes
                  + [jax.ShapeDtypeStruct((NB, R, D_WIDTH), BF16),
                     jax.ShapeDtypeStruct((NB, nd, D_WIDTH), F32)],
        scratch_shapes=[pltpu.VMEM((D_WIDTH // LANES, _ru8(nd) + TM, LANES), F32)] + kv_scratch,
        compiler_params=_cparams(2),
        name="cd_in",
    )(x, mod, ng, d_prev, w_in, d_w, d_b, ln_g, ln_b, *perm_in)


def _head_of_lane(shape):
    return lax.shift_right_logical(lax.broadcasted_iota(jnp.int32, shape, 1), 6)


def _attn_blocks(blocks):
    lane_head = _head_of_lane((Q_BLOCK, C_GW))
    first_head = lax.broadcasted_iota(jnp.int32, (Q_BLOCK, LANES), 1) < C_HEAD_DIM
    logits = []
    for q, kvp, kvc, _ in blocks:
        zero = jnp.zeros_like(q)
        qs = jnp.concatenate([jnp.where(lane_head == h, q, zero) for h in range(C_HPG)], axis=0)
        kcat = jnp.concatenate([kvp[:, 0:C_GW], kvc[:, 0:C_GW]], axis=0)
        logits.append(lax.dot_general(qs, kcat, (((1,), (1,)), ((), ())), preferred_element_type=F32))
    logits = [lg + bias for lg, (_, _, _, bias) in zip(logits, blocks)]
    ms = [jnp.max(lg, axis=-1, keepdims=True) for lg in logits]
    ps = [jnp.exp2(lg - m) for lg, m in zip(logits, ms)]
    ss = [jnp.sum(p, axis=-1, keepdims=True) for p in ps]
    pvs = []
    for (_, kvp, kvc, _), p in zip(blocks, ps):
        pb = p.astype(BF16)
        vlo = jnp.concatenate([kvp[:, C_GW:C_GW + LANES], kvc[:, C_GW:C_GW + LANES]], axis=0)
        vhi = jnp.concatenate([kvp[:, C_GW + LANES:], kvc[:, C_GW + LANES:]], axis=0)
        pvs.append((jnp.dot(pb[0:2 * Q_BLOCK], vlo, preferred_element_type=F32),
                    jnp.dot(pb[2 * Q_BLOCK:], vhi, preferred_element_type=F32)))
    outs = []
    for (lo, hi), m, s in zip(pvs, ms, ss):
        pick = lambda a, b: jnp.where(first_head, a, b)
        rows = lambda x, h: x[h * Q_BLOCK:(h + 1) * Q_BLOCK]
        merged = lambda x: jnp.concatenate([pick(rows(x, 0), rows(x, 1)), pick(rows(x, 2), rows(x, 3))], axis=-1)
        s_l, m_l = merged(s), merged(m)
        o = jnp.concatenate([pick(lo[0:Q_BLOCK], lo[Q_BLOCK:]), pick(hi[0:Q_BLOCK], hi[Q_BLOCK:])], axis=-1)
        outs.append((o * (1.0 / s_l), m_l + jnp.log2(s_l)))
    return outs


def _blocks_per_trip(count):
    return max(u for u in (5, 4, 3, 2, 1) if count % u == 0)


def _attn_prompt_body(q0, q1, q2, kv0, kv1, kv2, p0, p1, p2, bias_ref, yc_ref, osc, lsc, *, TM):
    qs_, kvs, prevs = (q0, q1, q2), (kv0, kv1, kv2), (p0, p1, p2)
    var = jnp.minimum(pl.program_id(1), 1)
    n_blocks = ATT_ROWS // Q_BLOCK

    def put(g, start, n, stride, o, l, r0):
        rows = pl.ds(start, n, stride=stride) if stride > 1 else pl.ds(start, n)
        for s in range(C_GW // LANES):
            osc[g, s, rows, :] = o[r0:r0 + n, s * LANES:(s + 1) * LANES]
            lsc[g, s, rows, :] = l[r0:r0 + n, s * LANES:(s + 1) * LANES]

    for g, (_, dil) in enumerate(C_PAIRS):
        q_ref, kv_ref, p_ref = qs_[g], kvs[g], prevs[g]
        cls = TM // dil
        if cls >= Q_BLOCK:
            prev_off = Q_BLOCK if dil == 1 else TM
            n_first = prev_off // Q_BLOCK
            per_tile = TM // Q_BLOCK

            def token_start(idx, dil=dil, per_tile=per_tile):
                return idx * Q_BLOCK if dil == 1 else (idx // per_tile) * TM + idx % per_tile

            first = []
            for idx in range(n_first):
                rows = slice(idx * Q_BLOCK, (idx + 1) * Q_BLOCK)
                first.append((q_ref[rows, :], p_ref[rows, :], kv_ref[rows, :], bias_ref[g, var]))
            for idx, (o, l) in enumerate(_attn_blocks(first)):
                put(g, token_start(idx), Q_BLOCK, dil, o, l, 0)

            per_trip = _blocks_per_trip(n_blocks - n_first)

            def body(trip, carry, g=g, dil=dil, q_ref=q_ref, kv_ref=kv_ref, prev_off=prev_off,
                     token_start=token_start, n_first=n_first, per_trip=per_trip):
                idxs = [n_first + trip * per_trip + u for u in range(per_trip)]
                blocks = []
                for idx in idxs:
                    st = pl.multiple_of(idx * Q_BLOCK, Q_BLOCK)
                    blocks.append((q_ref[pl.ds(st, Q_BLOCK), :], kv_ref[pl.ds(st - prev_off, Q_BLOCK), :],
                                   kv_ref[pl.ds(st, Q_BLOCK), :], bias_ref[g, 1]))
                for idx, (o, l) in zip(idxs, _attn_blocks(blocks)):
                    put(g, token_start(idx), Q_BLOCK, dil, o, l, 0)
                return carry

            lax.fori_loop(0, (n_blocks - n_first) // per_trip, body, 0)
        else:
            tiles = Q_BLOCK // cls
            per_trip = _blocks_per_trip(dil)

            def body(trip, carry, g=g, dil=dil, q_ref=q_ref, kv_ref=kv_ref, p_ref=p_ref, cls=cls, tiles=tiles,
                     per_trip=per_trip):
                rs = [trip * per_trip + u for u in range(per_trip)]
                blocks = []
                for r in rs:
                    st = pl.multiple_of(r * cls, cls)
                    gather = lambda ref, st=st: jnp.concatenate(
                        [ref[pl.ds(c * TM + st, cls), :] for c in range(tiles)], axis=0)
                    blocks.append((gather(q_ref), gather(p_ref), gather(kv_ref), bias_ref[g, var]))
                for r, (o, l) in zip(rs, _attn_blocks(blocks)):
                    for c in range(tiles):
                        put(g, c * TM + r, cls, dil, o, l, c * cls)
                return carry

            lax.fori_loop(0, dil // per_trip, body, 0)

    def merge(ch, carry):
        st = pl.multiple_of(ch * Q_BLOCK, Q_BLOCK)
        get = lambda ref, g: jnp.concatenate([ref[g, s, pl.ds(st, Q_BLOCK), :] for s in range(C_GW // LANES)], axis=-1)
        ls = [get(lsc, g) for g in range(len(C_PAIRS))]
        mm = jnp.maximum(jnp.maximum(ls[0], ls[1]), ls[2])
        es = [jnp.exp2(l - mm) for l in ls]
        den = es[0] + es[1] + es[2]
        yc = (es[0] / den) * get(osc, 0) + (es[1] / den) * get(osc, 1) + (es[2] / den) * get(osc, 2)
        yc_ref[pl.ds(st, Q_BLOCK), :] = yc.astype(BF16)
        return carry

    lax.fori_loop(0, n_blocks, merge, 0, unroll=2)


def _attn_prompt(qs, kvs, bias, *, TM):
    N, S, _ = qs[0].shape
    assert S % ATT_ROWS == 0 and ATT_ROWS == Q_BLOCK * C_PAIRS[-1][1] and ATT_ROWS % TM == 0
    cur = lambda c: pl.BlockSpec((None, ATT_ROWS, c), lambda n, i: (n, i, 0))
    prev_rows = [Q_BLOCK if dil == 1 else (TM if TM // dil >= Q_BLOCK else ATT_ROWS) for _, dil in C_PAIRS]
    prev = [pl.BlockSpec((None, pr, 2 * C_GW), lambda n, i, k=ATT_ROWS // pr: (n, jnp.maximum(i * k - 1, 0), 0))
            for pr in prev_rows]
    slabs = C_GW // LANES
    return pl.pallas_call(
        functools.partial(_attn_prompt_body, TM=TM),
        grid=(N, S // ATT_ROWS),
        in_specs=[cur(C_GW)] * 3 + [cur(2 * C_GW)] * 3 + prev + [_const_spec(bias.shape)],
        out_specs=cur(C_GW),
        out_shape=jax.ShapeDtypeStruct((N, S, C_GW), BF16),
        scratch_shapes=[pltpu.VMEM((len(C_PAIRS), slabs, ATT_ROWS, LANES), F32)] * 2,
        compiler_params=_cparams(2),
        name="attn_prompt",
    )(*qs, *kvs, *kvs, bias)


SROWS = 32
SAMPLE_PER_STEP = 2


def _attn_sample_body(q_ref, kvn_ref, c0_ref, c1_ref, c2_ref, bc0_ref, bc1_ref, bc2_ref, bn_ref,
                      yc_ref, n0_ref, n1_ref, n2_ref):
    for e in range(q_ref.shape[0]):
        _attn_sample_one(q_ref.at[e], kvn_ref.at[e], (c0_ref.at[e], c1_ref.at[e], c2_ref.at[e]),
                         (bc0_ref, bc1_ref, bc2_ref), bn_ref, yc_ref.at[e],
                         (n0_ref.at[e], n1_ref.at[e], n2_ref.at[e]))


def _attn_sample_one(q_ref, kvn_ref, caches, bcs, bn_ref, yc_ref, news):
    lane_head = _head_of_lane((SROWS, C_GW))
    row_head = lax.shift_right_logical(lax.broadcasted_iota(jnp.int32, (SROWS, C_GW), 0), 3)
    own = lane_head == row_head
    lane128 = lax.broadcasted_iota(jnp.int32, (SROWS, LANES), 1)
    kvn = kvn_ref[...]
    r0 = 8 - DEC_SEQ
    outs, lses = [], []
    for g, (win, dil) in enumerate(C_PAIRS):
        wb = win
        qg = q_ref[:, g * C_GW:(g + 1) * C_GW]
        qs = jnp.where(own, qg, jnp.zeros_like(qg))
        cache = caches[g]
        kt = cache[0:C_GW, :].astype(BF16)
        vt = cache[C_GW:2 * C_GW, :].astype(BF16)
        kn = kvn[:, g * C_GW:(g + 1) * C_GW]
        vn = kvn[:, C_Q + g * C_GW:C_Q + (g + 1) * C_GW]
        lc = jnp.dot(qs, kt, preferred_element_type=F32) + bcs[g][...]
        qf = qs.astype(F32)
        ln = bn_ref[g]
        for c in range(DEC_SEQ):
            d = jnp.sum(qf * kn[r0 + c:r0 + c + 1, :], axis=-1, keepdims=True)
            ln = ln + jnp.where(lane128 == c, d, 0.0)
        m = jnp.maximum(jnp.max(lc, axis=-1, keepdims=True), jnp.max(ln, axis=-1, keepdims=True))
        pc = jnp.exp(lc - m)
        pn = jnp.exp(ln - m)
        s = jnp.sum(pc, axis=-1, keepdims=True) + jnp.sum(pn, axis=-1, keepdims=True)
        pv = lax.dot_general(pc.astype(BF16), vt, (((1,), (1,)), ((), ())), preferred_element_type=F32)
        for c in range(DEC_SEQ):
            pcol = jnp.sum(jnp.where(lane128 == c, pn, 0.0), axis=-1, keepdims=True)
            pv = pv + pcol * vn[r0 + c:r0 + c + 1, :]
        outs.append(pv / s)
        lses.append(m + jnp.log(s))
        rolled = pltpu.roll(cache[...], wb - DEC_SEQ, 1)
        new_rows = jnp.concatenate([jnp.zeros((LANES - SUBLANES, 2 * C_GW), F32),
                                    jnp.concatenate([kn, vn], axis=-1)], axis=0)
        new_cols = new_rows.T
        lane_t = lax.broadcasted_iota(jnp.int32, (2 * C_GW, LANES), 1)
        if wb > LANES:
            news[g][:, 0:wb - LANES] = rolled[:, 0:wb - LANES]
        news[g][:, wb - LANES:wb] = jnp.where(lane_t >= LANES - DEC_SEQ, new_cols, rolled[:, wb - LANES:wb])
    mm = jnp.maximum(jnp.maximum(lses[0], lses[1]), lses[2])
    es = [jnp.exp(l - mm) for l in lses]
    den = es[0] + es[1] + es[2]
    y = (es[0] / den) * outs[0] + (es[1] / den) * outs[1] + (es[2] / den) * outs[2]
    y = jnp.where(own, y, 0.0)
    yc_ref[...] = y[0:8, :] + y[8:16, :] + y[16:24, :] + y[24:32, :]


def _attn_sample(q_rep, kvn, caches, bias_c, bias_n):
    NBt = q_rep.shape[0]
    assert NBt % SAMPLE_PER_STEP == 0
    per_b = lambda r, c: pl.BlockSpec((SAMPLE_PER_STEP, r, c), lambda b: (b, 0, 0))
    wbs = [w for w, _ in C_PAIRS]
    return pl.pallas_call(
        _attn_sample_body,
        grid=(NBt // SAMPLE_PER_STEP,),
        in_specs=[per_b(SROWS, C_Q), per_b(8, 2 * C_Q)] + [per_b(2 * C_GW, w) for w in wbs]
                 + [_const_spec((SROWS, w)) for w in wbs] + [_const_spec((len(wbs), SROWS, LANES))],
        out_specs=[per_b(8, C_GW)] + [per_b(2 * C_GW, w) for w in wbs],
        out_shape=[jax.ShapeDtypeStruct((NBt, 8, C_GW), F32)]
                  + [jax.ShapeDtypeStruct((NBt, 2 * C_GW, w), F32) for w in wbs],
        compiler_params=_cparams(1),
        name="attn_sample",
    )(q_rep, kvn, *caches, *bias_c, bias_n)


def _t5_bucket(dist):
    dist = np.asarray(dist)
    max_exact = N_BUCKETS // 2
    large = max_exact + (np.log(np.maximum(dist, max_exact) / max_exact) / np.log(MAX_DISTANCE / max_exact)
                         * (N_BUCKETS - max_exact)).astype(np.int32)
    large = np.minimum(large, N_BUCKETS - 1)
    return np.where(dist < max_exact, dist, large).astype(np.int32)


def _group_bias(rel_bias, g, dil):
    buckets = _t5_bucket(dil * np.arange(C_TAPS + 1))
    return rel_bias[buckets][:, g * C_HPG:(g + 1) * C_HPG].T


def _toeplitz_body(c_ref, o_ref):
    keep = lax.broadcasted_iota(jnp.int32, (Q_BLOCK, 2 * Q_BLOCK), 1) >= Q_BLOCK
    for h in range(C_HPG):
        taps = jnp.broadcast_to(c_ref[h:h + 1, :], (Q_BLOCK, 2 * Q_BLOCK))
        t = pltpu.roll(taps, 0, 1, stride=1, stride_axis=0)
        o_ref[1, h * Q_BLOCK:(h + 1) * Q_BLOCK, :] = t
        o_ref[0, h * Q_BLOCK:(h + 1) * Q_BLOCK, :] = jnp.where(keep, t, NEG)


def _prompt_bias(bias_gs):
    c = jnp.stack([jnp.concatenate([b[:, ::-1].astype(F32) * LOG2E, jnp.full((C_HPG, Q_BLOCK - 1), NEG, F32)], axis=1)
                   for b in bias_gs])
    return pl.pallas_call(
        _toeplitz_body,
        grid=(len(bias_gs),),
        in_specs=[pl.BlockSpec((None, C_HPG, 2 * Q_BLOCK), lambda g: (g, 0, 0))],
        out_specs=pl.BlockSpec((None, 2, C_HPG * Q_BLOCK, 2 * Q_BLOCK), lambda g: (g, 0, 0, 0)),
        out_shape=jax.ShapeDtypeStruct((len(bias_gs), 2, C_HPG * Q_BLOCK, 2 * Q_BLOCK), F32),
        compiler_params=_cparams(1),
        name="attn_bias",
    )(c)


def _sample_bias(bias_g, wb, dil):
    n = wb + DEC_SEQ
    t = np.arange(DEC_SEQ)[:, None]
    d = wb + t - np.arange(n)[None, :]
    valid = (d >= 0) & (d % dil == 0) & (d // dil <= C_TAPS)
    place = np.zeros((DEC_SEQ, C_TAPS + 1, n), np.float32)
    ti, ii = np.nonzero(valid)
    place[ti, (d // dil)[ti, ii], ii] = 1.0
    ext = jnp.einsum("hj,tji->hti", bias_g.astype(F32), place, precision=lax.Precision.HIGHEST)
    ext = jnp.where(valid[None], ext, NEG)
    pad_rows = (0, SUBLANES - DEC_SEQ)
    bc = jnp.pad(ext[:, :, :wb], ((0, 0), pad_rows, (0, 0))).reshape(SROWS, wb)
    bn = jnp.pad(ext[:, :, wb:], ((0, 0), pad_rows, (0, LANES - DEC_SEQ)), constant_values=NEG).reshape(SROWS, LANES)
    return bc, bn


def _time_major(s):
    b, k, c = s.shape
    return s.transpose(1, 0, 2).reshape(1, k * b, c)


def _batch_major(s, b):
    _, r, c = s.shape
    return s.reshape(r // b, b, c).transpose(1, 0, 2)


def _stack(x, mod, st, w, *, TM, TM_WIDE, RS, pos0):
    new = {}
    ng = w["norm_g"]
    cast_keys = [k for k in ("ffn_w_up", "ffn_w_down", "cd_w_in", "cd_w_out") if w[k].dtype != BF16]
    x, new["a"], new["b"], cast_out = _mixer_ab(
        x, mod[0], ng[0], st["a"], st["b"], w["ab_w_in"], w["a_conv_w"], w["b_w_grp"], w["b_scale"], w["ab_w_out"],
        TM=TM_WIDE, RS=RS, pos0=pos0, cast=tuple(w[k] for k in cast_keys))
    w = dict(w, **dict(zip(cast_keys, cast_out)))
    x, new["f0"] = _conv_ffn(x, mod[0], ng[0], st["f0"], w["ffn_w_up"], w["ffn_conv_w"],
                             w["ffn_conv_b"], w["ffn_w_down"], layer=0, TM=TM_WIDE, RS=RS)
    prompt = RS == 1
    outs = _cd_in(x, mod[1], ng[1], st["d"], w["cd_w_in"][0], w["d_conv_w"], w["d_conv_b"], w["d_ln_g"], w["d_ln_b"],
                  TM=TM, RS=RS, prompt_attn=prompt)
    qs, kvs, yd, new["d"] = outs[0:3], outs[3:6], outs[-2], outs[-1]
    cache_layout = lambda n: n.reshape(n.shape[0], 2, C_HPG, C_HEAD_DIM, n.shape[2]).transpose(0, 4, 1, 2, 3)
    if prompt:
        yc = _attn_prompt(qs, kvs, _prompt_bias(w["bias_g"]), TM=TM)
        for g in range(len(C_PAIRS)):
            new["c%d" % g] = cache_layout(outs[6 + g])
    else:
        B = RS
        kvf = outs[6]
        qb = _batch_major(jnp.concatenate(qs, axis=-1), B)
        q_rep = jnp.pad(jnp.broadcast_to(qb[:, None], (B, C_HPG, DEC_SEQ, C_Q)),
                        ((0, 0), (0, 0), (0, 8 - DEC_SEQ), (0, 0))).reshape(B, SROWS, C_Q)
        kvn = jnp.pad(_batch_major(kvf, B), ((0, 0), (8 - DEC_SEQ, 0), (0, 0)))
        bias = [_sample_bias(w["bias_g"][g], win, dil) for g, (win, dil) in enumerate(C_PAIRS)]
        yc, n0, n1, n2 = _attn_sample(q_rep, kvn, st["c"], [b[0] for b in bias],
                                      jnp.stack([b[1] for b in bias]))
        for g, n in enumerate((n0, n1, n2)):
            new["c%d" % g] = cache_layout(n)
        yc = _time_major(yc[:, :DEC_SEQ])
    x, new["f1"] = _conv_ffn(x, mod[1], ng[1], st["f1"], w["ffn_w_up"], w["ffn_conv_w"],
                             w["ffn_conv_b"], w["ffn_w_down"], layer=1, TM=TM_WIDE, RS=RS,
                             mixer_out=(yc, yd, w["cd_w_out"][0]))
    return x, new, w


def kernel(x_prompt, x_sample, state_a_conv, state_b_pool, cache_c_win128, cache_c_win512, cache_c_win2048,
           state_d_conv, state_ffn_conv, c_prompt, c_sample, ada_w, ada_b, norm_g, rel_bias, ab_w_in, a_conv_w,
           b_w_grp, b_scale, ab_w_out, cd_w_in, d_conv_w, d_conv_b, d_ln_g, d_ln_b, cd_w_out, ffn_w_up,
           ffn_conv_w, ffn_conv_b, ffn_w_down):
    B, T = DEC_BATCH, DEC_SEQ
    w = dict(norm_g=norm_g,
             ab_w_in=ab_w_in[0].astype(BF16), a_conv_w=a_conv_w[0], b_w_grp=b_w_grp[0].astype(BF16),
             b_scale=b_scale, ab_w_out=ab_w_out[0].astype(BF16),
             cd_w_in=cd_w_in, cd_w_out=cd_w_out,
             d_conv_w=jnp.broadcast_to(d_conv_w[0][:, None, :], (D_CONV, SUBLANES, D_WIDTH)),
             d_conv_b=d_conv_b, d_ln_g=d_ln_g, d_ln_b=d_ln_b,
             ffn_w_up=ffn_w_up, ffn_conv_w=ffn_conv_w, ffn_conv_b=ffn_conv_b[:, None, :], ffn_w_down=ffn_w_down,
             bias_g=[_group_bias(rel_bias, g, dil) for g, (_, dil) in enumerate(C_PAIRS)])

    mod = _ada(jnp.concatenate([c_prompt, c_sample], axis=0), ada_w, ada_b)
    mod_p = mod[:, :BATCH].reshape(DEPTH, BATCH, 6, 1, D_MODEL)
    mod_s = mod[:, BATCH:].reshape(DEPTH, B, 6, D_MODEL).transpose(0, 2, 1, 3)
    mod_s = jnp.broadcast_to(mod_s[:, :, None], (DEPTH, 6, T, B, D_MODEL)).reshape(DEPTH, 1, 6, T * B, D_MODEL)

    zeros = lambda k, c: jnp.zeros((BATCH, k, c), F32)
    st_p = dict(a=zeros(A_CONV - 1, A_WIDTH), b=zeros(B_PREV, B_WIDTH), d=zeros(D_CONV - 1, D_WIDTH),
                f0=zeros(FFN_CONV - 1, 2 * D_FF), f1=zeros(FFN_CONV - 1, 2 * D_FF))
    y_p, np_, w = _stack(x_prompt, mod_p, st_p, w, TM=TM_PROMPT, TM_WIDE=TM_WIDE_PROMPT, RS=1, pos0=0)

    st_s = dict(a=_time_major(state_a_conv[0]), b=_time_major(state_b_pool[0]), d=_time_major(state_d_conv[0]),
                f0=_time_major(state_ffn_conv[0]), f1=_time_major(state_ffn_conv[1]),
                c=[c[0].transpose(0, 2, 3, 4, 1).reshape(B, 2 * C_GW, c.shape[2])
                   for c in (cache_c_win128, cache_c_win512, cache_c_win2048)])
    y_s, ns, _ = _stack(_time_major(x_sample), mod_s, st_s, w, TM=T * B, TM_WIDE=T * B, RS=B, pos0=PAST_LEN)

    bm = lambda s: _batch_major(s, B)
    return (y_p, bm(y_s),
            np_["a"][None], bm(ns["a"])[None], np_["b"][None], bm(ns["b"])[None],
            np_["c0"][None], ns["c0"][None], np_["c1"][None], ns["c1"][None], np_["c2"][None], ns["c2"][None],
            np_["d"][None], bm(ns["d"])[None],
            jnp.stack([np_["f0"], np_["f1"]]), jnp.stack([bm(ns["f0"]), bm(ns["f1"])]))
```

```python
import functools
import math

import numpy as np
import jax
import jax.numpy as jnp
from jax import lax
from jax.experimental import pallas as pl
from jax.experimental.pallas import tpu as pltpu

D_MODEL = 1024
BATCH = 4
SEQ = 4096
DEPTH = 2
DEC_BATCH = 32
DEC_SEQ = 4
PAST_LEN = 8192
EPS = 1e-6
A_WIDTH = 512
A_CONV = 3
B_WIDTH = 512
B_WINDOWS = (2, 4, 8, 16)
B_GROUP = 128
B_PREV = 15
C_PAIRS = ((128, 1), (512, 4), (2048, 16))
C_HPG = 4
C_HEAD_DIM = 64
C_HEADS = 12
C_Q = C_HEADS * C_HEAD_DIM
C_QKV = 3 * C_Q
C_GW = C_HPG * C_HEAD_DIM
C_TAPS = 128
ATTN_SCALE = C_HEAD_DIM ** -0.5
LOG2E = math.log2(math.e)
Q_BLOCK = 128
N_BUCKETS = 32
MAX_DISTANCE = 2048
D_WIDTH = 512
D_CONV = 31
D_FF = 2816
FFN_CONV = 3

SUBLANES = 8
LANES = 128
VMEM_LIMIT = 56 * 1024 * 1024
NEG = -1e30
TM_PROMPT = 512
TM_WIDE_PROMPT = 1024
ATT_ROWS = 2048
ADA_COLS = 1536
FF_CHUNK = 256
D_ROWS = 32

F32 = jnp.float32
BF16 = jnp.bfloat16


def _ru8(n):
    return -(-n // SUBLANES) * SUBLANES


def _pool_levels(rs):
    l1 = _ru8(rs)
    l2 = _ru8(l1 + 2 * rs)
    l3 = _ru8(l2 + 4 * rs)
    l4 = _ru8(l3 + 8 * rs)
    return l1, l2, l3, l4


def _slab_put(ref, r0, val):
    n = val.shape[0]
    for s in range(val.shape[1] // LANES):
        ref[s, r0:r0 + n, :] = val[:, s * LANES:(s + 1) * LANES]


def _slab_get(ref, r0, n, s0=0, ns=None):
    ns = ref.shape[0] - s0 if ns is None else ns
    return jnp.concatenate([ref[s, r0:r0 + n, :] for s in range(s0, s0 + ns)], axis=-1)


def _rms(x, g):
    return x * lax.rsqrt(jnp.mean(x * x, axis=-1, keepdims=True) + EPS) * g


def _sigmoid(x):
    return 1.0 / (1.0 + jnp.exp(-x))


def _cparams(n_axes):
    return pltpu.CompilerParams(dimension_semantics=("arbitrary",) * n_axes, vmem_limit_bytes=VMEM_LIMIT)


def _const_spec(shape):
    nd = len(shape)
    return pl.BlockSpec(shape, lambda *_: (0,) * nd, pipeline_mode=pl.Buffered(1))


def _ada_body(c_ref, w_ref, b_ref, o_ref):
    c = c_ref[...]
    ca = c * _sigmoid(c)
    o_ref[...] = jnp.dot(ca.astype(BF16), w_ref[...].astype(BF16), preferred_element_type=F32) + b_ref[...]


def _ada(c_all, ada_w, ada_b):
    rows = c_all.shape[0]
    tn = ADA_COLS
    return pl.pallas_call(
        _ada_body,
        grid=(DEPTH, 6 * D_MODEL // tn),
        in_specs=[pl.BlockSpec((rows, D_MODEL), lambda l, n: (0, 0)),
                  pl.BlockSpec((None, D_MODEL, tn), lambda l, n: (l, 0, n)),
                  pl.BlockSpec((None, 1, tn), lambda l, n: (l, 0, n))],
        out_specs=pl.BlockSpec((None, rows, tn), lambda l, n: (l, 0, n)),
        out_shape=jax.ShapeDtypeStruct((DEPTH, rows, 6 * D_MODEL), F32),
        compiler_params=_cparams(2),
        name="ada",
    )(c_all, ada_w, ada_b.reshape(DEPTH, 1, 6 * D_MODEL))


def _ab_body(*refs, TM, RS, NT, pos0, n_cast):
    (x_ref, mod_ref, ng_ref, ap_ref, bp_ref, win_ref, aw_ref, bw_ref, bs_ref, wout_ref) = refs[:10]
    cast_in = refs[10:10 + n_cast]
    x1_ref, an_ref, bn_ref = refs[10 + n_cast:13 + n_cast]
    cast_out = refs[13 + n_cast:13 + 2 * n_cast]
    ea, eb, s2, s4, s8 = refs[13 + 2 * n_cast:]
    for src, dst in zip(cast_in, cast_out):
        dst[...] = src[...].astype(BF16)
    j = pl.program_id(1)
    HA = _ru8((A_CONV - 1) * RS)
    L1, L2, L3, HB = _pool_levels(RS)
    E = HB + TM
    na, nbp = (A_CONV - 1) * RS, B_PREV * RS

    @pl.when(j == 0)
    def _():
        _slab_put(ea, HA - na, ap_ref[...])
        if HB > nbp:
            _slab_put(eb, 0, jnp.zeros((HB - nbp, B_WIDTH), F32))
        _slab_put(eb, HB - nbp, bp_ref[...])

    x = x_ref[...]
    sh1, sc1, g1 = mod_ref[0], mod_ref[1], mod_ref[2]
    h = _rms(x, ng_ref[0:1, :] * (1.0 + sc1)) + sh1
    proj = jnp.dot(h.astype(BF16), win_ref[...], preferred_element_type=F32)
    hh, bg = proj[:, 0:A_WIDTH], proj[:, A_WIDTH:2 * A_WIDTH]
    cg, u = proj[:, 2 * A_WIDTH:3 * A_WIDTH], proj[:, 3 * A_WIDTH:]

    v = cg * hh
    _slab_put(ea, HA, v)
    z = (aw_ref[2:3, :] * v + aw_ref[1:2, :] * _slab_get(ea, HA - RS, TM)
         + aw_ref[0:1, :] * _slab_get(ea, HA - 2 * RS, TM))
    ya = bg * z
    a_last = _slab_get(ea, HA + TM - na, na)
    an_ref[...] = a_last
    if NT > 1:
        _slab_put(ea, HA - na, a_last)

    _slab_put(eb, HB, u)
    _slab_put(s2, L1, _slab_get(eb, L1, E - L1) + _slab_get(eb, L1 - RS, E - L1))
    _slab_put(s4, L2, _slab_get(s2, L2, E - L2, 1) + _slab_get(s2, L2 - 2 * RS, E - L2, 1))
    _slab_put(s8, L3, _slab_get(s4, L3, E - L3, 1) + _slab_get(s4, L3 - 4 * RS, E - L3, 1))
    wsum = (s2[0, HB:E, :], s4[0, HB:E, :], s8[0, HB:E, :], s8[1, HB:E, :] + s8[1, HB - 8 * RS:E - 8 * RS, :])
    b_last = _slab_get(eb, E - nbp, nbp)
    bn_ref[...] = b_last
    if NT > 1:
        _slab_put(eb, HB - nbp, b_last)
    row = lax.broadcasted_iota(jnp.int32, (TM, B_GROUP), 0) + j * TM
    pos1 = lax.shift_right_logical(row, int(math.log2(RS))) + (pos0 + 1)
    ybs = []
    for g, win in enumerate(B_WINDOWS):
        cnt = jnp.minimum(pos1, win).astype(F32)
        pooled = wsum[g] / cnt - u[:, g * B_GROUP:(g + 1) * B_GROUP]
        ybs.append(jnp.dot(pooled.astype(BF16), bw_ref[g], preferred_element_type=F32))
    yb = jnp.concatenate(ybs, axis=-1) * bs_ref[...]

    ycat = jnp.concatenate([ya, yb], axis=-1).astype(BF16)
    y = jnp.dot(ycat, wout_ref[...], preferred_element_type=F32)
    x1_ref[...] = x + _rms(y, ng_ref[1:2, :] * g1)


def _mixer_ab(x, mod, ng, a_prev, b_prev, w_in, a_w, b_w, b_scale, w_out, *, TM, RS, pos0, cast=()):
    NB, R, _ = x.shape
    NT = R // TM
    MR = mod.shape[2]
    steps = NB * NT
    cast_specs, cast_shapes = [], []
    for a in cast:
        rows = a.shape[1] // steps
        assert rows * steps == a.shape[1] and rows % (2 * SUBLANES) == 0
        cast_specs.append(pl.BlockSpec((a.shape[0], rows, a.shape[2]), lambda n, j: (0, n * NT + j, 0)))
        cast_shapes.append(jax.ShapeDtypeStruct(a.shape, BF16))
    _, _, _, HB = _pool_levels(RS)
    HA = _ru8((A_CONV - 1) * RS)
    E = HB + TM
    na, nbp = (A_CONV - 1) * RS, B_PREV * RS
    tile = lambda c: pl.BlockSpec((None, TM, c), lambda n, j: (n, j, 0))
    per_n = lambda r, c: pl.BlockSpec((None, r, c), lambda n, j: (n, 0, 0))
    outs = pl.pallas_call(
        functools.partial(_ab_body, TM=TM, RS=RS, NT=NT, pos0=pos0, n_cast=len(cast)),
        grid=(NB, NT),
        in_specs=[tile(D_MODEL),
                  pl.BlockSpec((None, 6, MR, D_MODEL), lambda n, j: (n, 0, 0, 0)),
                  _const_spec((4, D_MODEL)),
                  per_n(na, A_WIDTH), per_n(nbp, B_WIDTH),
                  _const_spec(w_in.shape), _const_spec(a_w.shape), _const_spec(b_w.shape),
                  _const_spec(b_scale.shape), _const_spec(w_out.shape)] + cast_specs,
        out_specs=[tile(D_MODEL), per_n(na, A_WIDTH), per_n(nbp, B_WIDTH)] + cast_specs,
        out_shape=[jax.ShapeDtypeStruct((NB, R, D_MODEL), F32),
                   jax.ShapeDtypeStruct((NB, na, A_WIDTH), F32),
                   jax.ShapeDtypeStruct((NB, nbp, B_WIDTH), F32)] + cast_shapes,
        scratch_shapes=[pltpu.VMEM((4, HA + TM, LANES), F32), pltpu.VMEM((4, E, LANES), F32),
                        pltpu.VMEM((4, E, LANES), F32), pltpu.VMEM((3, E, LANES), F32),
                        pltpu.VMEM((2, E, LANES), F32)],
        compiler_params=_cparams(2),
        name="mixer_ab",
    )(x, mod, ng, a_prev, b_prev, w_in, a_w, b_w, b_scale, w_out, *cast)
    return outs[0], outs[1], outs[2], tuple(outs[3:])


def _ffn_body(*refs, TM, RS, NT, mixer_out):
    if mixer_out:
        (x_ref, mod_ref, ng_ref, fp_ref, wup_ref, cw_ref, cb_ref, wdn_ref, yc_ref, yd_ref, wout_ref,
         y_ref, fn_ref, halo, ext, act) = refs
    else:
        (x_ref, mod_ref, ng_ref, fp_ref, wup_ref, cw_ref, cb_ref, wdn_ref,
         y_ref, fn_ref, halo, ext, act) = refs
    j = pl.program_id(1)
    nf = (FFN_CONV - 1) * RS
    HF = _ru8(nf)

    @pl.when(j == 0)
    def _():
        halo[...] = fp_ref[...]

    x = x_ref[...]
    if mixer_out:
        ym = (jnp.dot(yc_ref[...].astype(BF16), wout_ref[0:C_GW, :], preferred_element_type=F32)
              + jnp.dot(yd_ref[...], wout_ref[C_GW:, :], preferred_element_type=F32))
        x = x + _rms(ym, ng_ref[1:2, :] * mod_ref[2])
    sh2, sc2, g2 = mod_ref[3], mod_ref[4], mod_ref[5]
    hb = (_rms(x, ng_ref[2:3, :] * (1.0 + sc2)) + sh2).astype(BF16)

    def conv_part(col, buf):
        up = jnp.dot(hb, wup_ref[:, col:col + FF_CHUNK], preferred_element_type=F32)
        _slab_put(buf, HF - nf, halo[:, col:col + FF_CHUNK])
        _slab_put(buf, HF, up)
        out = (cw_ref[2:3, col:col + FF_CHUNK] * up
               + cw_ref[1:2, col:col + FF_CHUNK] * _slab_get(buf, HF - RS, TM)
               + cw_ref[0:1, col:col + FF_CHUNK] * _slab_get(buf, HF - 2 * RS, TM)
               + cb_ref[:, col:col + FF_CHUNK])
        halo[:, col:col + FF_CHUNK] = _slab_get(buf, HF + TM - nf, nf)
        return out

    for c in range(D_FF // FF_CHUNK):
        a = conv_part(c * FF_CHUNK, ext.at[2 * (c % 2)])
        g = conv_part(D_FF + c * FF_CHUNK, ext.at[2 * (c % 2) + 1])
        act[:, c * FF_CHUNK:(c + 1) * FF_CHUNK] = (a * (g * _sigmoid(g))).astype(BF16)
    fn_ref[...] = halo[...]
    y = jnp.dot(act[...], wdn_ref[...], preferred_element_type=F32)
    y_ref[...] = x + _rms(y, ng_ref[3:4, :] * g2)


def _layer_spec(shape, layer):
    nd = len(shape) - 1
    return pl.BlockSpec((None,) + tuple(shape[1:]), lambda *_: (layer,) + (0,) * nd, pipeline_mode=pl.Buffered(1))


def _conv_ffn(x, mod, ng, f_prev, w_up, conv_w, conv_b, w_down, *, layer, TM, RS, mixer_out=None):
    NB, R, _ = x.shape
    NT = R // TM
    MR = mod.shape[2]
    nf = (FFN_CONV - 1) * RS
    tile_c = lambda c: pl.BlockSpec((None, TM, c), lambda n, j: (n, j, 0))
    tile = tile_c(D_MODEL)
    per_n = pl.BlockSpec((None, nf, 2 * D_FF), lambda n, j: (n, 0, 0))
    mix_in, mix_specs = [], []
    if mixer_out is not None:
        mix_in = list(mixer_out)
        mix_specs = [tile_c(C_GW), tile_c(D_WIDTH), _const_spec(mixer_out[2].shape)]
    return pl.pallas_call(
        functools.partial(_ffn_body, TM=TM, RS=RS, NT=NT, mixer_out=mixer_out is not None),
        grid=(NB, NT),
        in_specs=[tile,
                  pl.BlockSpec((None, 6, MR, D_MODEL), lambda n, j: (n, 0, 0, 0)),
                  _const_spec((4, D_MODEL)),
                  per_n,
                  _layer_spec(w_up.shape, layer), _layer_spec(conv_w.shape, layer),
                  _layer_spec(conv_b.shape, layer), _layer_spec(w_down.shape, layer)] + mix_specs,
        out_specs=[tile, per_n],
        out_shape=[jax.ShapeDtypeStruct((NB, R, D_MODEL), F32),
                   jax.ShapeDtypeStruct((NB, nf, 2 * D_FF), F32)],
        scratch_shapes=[pltpu.VMEM((nf, 2 * D_FF), F32),
                        pltpu.VMEM((4, FF_CHUNK // LANES, _ru8(nf) + TM, LANES), F32),
                        pltpu.VMEM((TM, D_FF), BF16)],
        compiler_params=_cparams(2),
        name="conv_ffn",
    )(x, mod, ng, f_prev, w_up, conv_w, conv_b, w_down, *mix_in)


def _cd_in_body(*refs, TM, RS, NT, prompt_attn):
    if prompt_attn:
        (x_ref, mod_ref, ng_ref, dp_ref, win_ref, dw_ref, db_ref, lg_ref, lb_ref, perm_ref,
         q0_ref, q1_ref, q2_ref, kv0_ref, kv1_ref, kv2_ref, c0_ref, c1_ref, c2_ref, yd_ref, dn_ref,
         ed, kvf_ref) = refs
        cache_refs = (c0_ref, c1_ref, c2_ref)
    else:
        (x_ref, mod_ref, ng_ref, dp_ref, win_ref, dw_ref, db_ref, lg_ref, lb_ref,
         q0_ref, q1_ref, q2_ref, kv0_ref, kv1_ref, kv2_ref, kvf_ref, yd_ref, dn_ref, ed) = refs
    q_refs, kv_refs = (q0_ref, q1_ref, q2_ref), (kv0_ref, kv1_ref, kv2_ref)
    j = pl.program_id(1)
    nd = (D_CONV - 1) * RS
    HD = _ru8(nd)

    @pl.when(j == 0)
    def _():
        _slab_put(ed, HD - nd, dp_ref[...])

    x = x_ref[...]
    sh1, sc1 = mod_ref[0], mod_ref[1]
    h = _rms(x, ng_ref[0:1, :] * (1.0 + sc1)) + sh1
    proj = jnp.dot(h.astype(BF16), win_ref[...], preferred_element_type=F32)
    qs = (proj[:, 0:C_Q] * (ATTN_SCALE * LOG2E if prompt_attn else ATTN_SCALE)).astype(BF16)
    ks = proj[:, C_Q:2 * C_Q].astype(BF16)
    vs = proj[:, 2 * C_Q:C_QKV].astype(BF16)
    for g in range(len(C_PAIRS)):
        cols = slice(g * C_GW, (g + 1) * C_GW)
        xg = jnp.concatenate([qs[:, cols], ks[:, cols], vs[:, cols]], axis=-1)
        if prompt_attn and C_PAIRS[g][1] > 1:
            xg = jnp.dot(perm_ref[g - 1], xg, preferred_element_type=F32).astype(BF16)
        q_refs[g][...] = xg[:, 0:C_GW]
        kv_refs[g][...] = xg[:, C_GW:]
    kvf_ref[...] = proj[:, C_Q:C_QKV]
    dv, dg = proj[:, C_QKV:C_QKV + D_WIDTH], proj[:, C_QKV + D_WIDTH:]
    _slab_put(ed, HD, dv * _sigmoid(dg))

    db, lg, lb = db_ref[...], lg_ref[...], lb_ref[...]
    tiles = D_ROWS // SUBLANES
    for c in range(TM // D_ROWS):
        accs = []
        for s in range(D_WIDTH // LANES):
            acc = jnp.zeros((tiles, SUBLANES, LANES), F32)
            for kk in range(D_CONV):
                off = HD - (D_CONV - 1 - kk) * RS + c * D_ROWS
                tap = ed[s, off:off + D_ROWS, :].reshape(tiles, SUBLANES, LANES)
                acc = acc + dw_ref[kk, :, s * LANES:(s + 1) * LANES] * tap
            accs.append(acc.reshape(D_ROWS, LANES))
        zc = jnp.concatenate(accs, axis=-1) + db
        mu = jnp.mean(zc, axis=-1, keepdims=True)
        zc = zc - mu
        var = jnp.mean(zc * zc, axis=-1, keepdims=True)
        yl = zc * lax.rsqrt(var + EPS) * lg + lb
        yd_ref[c * D_ROWS:(c + 1) * D_ROWS, :] = (yl * _sigmoid(yl)).astype(BF16)

    d_last = _slab_get(ed, HD + TM - nd, nd)
    dn_ref[...] = d_last
    if NT > 1:
        _slab_put(ed, HD - nd, d_last)

    if prompt_attn:
        for g, (win, _) in enumerate(C_PAIRS):
            cols = min(win, TM)

            @pl.when(j >= NT - max(win // TM, 1))
            def _(g=g, cols=cols):
                kt = kvf_ref[:, g * C_GW:(g + 1) * C_GW].T
                vt = kvf_ref[:, C_Q + g * C_GW:C_Q + (g + 1) * C_GW].T
                cache_refs[g][0:C_GW, :] = kt[:, TM - cols:]
                cache_refs[g][C_GW:, :] = vt[:, TM - cols:]


def _residue_perm(tm, dil):
    p = np.zeros((tm, tm), np.float32)
    a, r = np.meshgrid(np.arange(tm // dil), np.arange(dil), indexing="ij")
    p[(r * (tm // dil) + a).ravel(), (dil * a + r).ravel()] = 1.0
    return p


def _cd_in(x, mod, ng, d_prev, w_in, d_w, d_b, ln_g, ln_b, *, TM, RS, prompt_attn):
    NB, R, _ = x.shape
    NT = R // TM
    MR = mod.shape[2]
    nd = (D_CONV - 1) * RS
    tile = lambda c: pl.BlockSpec((None, TM, c), lambda n, j: (n, j, 0))
    per_n = pl.BlockSpec((None, nd, D_WIDTH), lambda n, j: (n, 0, 0))
    perm_in, perm_spec = [], []
    if prompt_attn:
        perm_in = [jnp.asarray(np.stack([_residue_perm(TM, dil) for _, dil in C_PAIRS[1:]]), BF16)]
        perm_spec = [_const_spec((len(C_PAIRS) - 1, TM, TM))]
        assert all(win <= R and (win % TM == 0 or TM % win == 0) for win, _ in C_PAIRS)
        kv_specs = [pl.BlockSpec((None, 2 * C_GW, min(win, TM)),
                                 lambda n, j, first=NT - max(win // TM, 1): (n, 0, jnp.maximum(j - first, 0)))
                    for win, _ in C_PAIRS]
        kv_shapes = [jax.ShapeDtypeStruct((NB, 2 * C_GW, win), F32) for win, _ in C_PAIRS]
        kv_scratch = [pltpu.VMEM((TM, 2 * C_Q), F32)]
    else:
        kv_specs = [tile(2 * C_Q)]
        kv_shapes = [jax.ShapeDtypeStruct((NB, R, 2 * C_Q), F32)]
        kv_scratch = []
    return pl.pallas_call(
        functools.partial(_cd_in_body, TM=TM, RS=RS, NT=NT, prompt_attn=prompt_attn),
        grid=(NB, NT),
        in_specs=[tile(D_MODEL),
                  pl.BlockSpec((None, 6, MR, D_MODEL), lambda n, j: (n, 0, 0, 0)),
                  _const_spec((4, D_MODEL)),
                  per_n,
                  _const_spec(w_in.shape), _const_spec(d_w.shape), _const_spec(d_b.shape),
                  _const_spec(ln_g.shape), _const_spec(ln_b.shape)] + perm_spec,
        out_specs=[tile(C_GW)] * 3 + [tile(2 * C_GW)] * 3 + kv_specs + [tile(D_WIDTH), per_n],
        out_shape=[jax.ShapeDtypeStruct((NB, R, C_GW), BF16)] * 3
                  + [jax.ShapeDtypeStruct((NB, R, 2 * C_GW), BF16)] * 3
                  + kv_shapes
                  + [jax.ShapeDtypeStruct((NB, R, D_WIDTH), BF16),
                     jax.ShapeDtypeStruct((NB, nd, D_WIDTH), F32)],
        scratch_shapes=[pltpu.VMEM((D_WIDTH // LANES, _ru8(nd) + TM, LANES), F32)] + kv_scratch,
        compiler_params=_cparams(2),
        name="cd_in",
    )(x, mod, ng, d_prev, w_in, d_w, d_b, ln_g, ln_b, *perm_in)


def _head_of_lane(shape):
    return lax.shift_right_logical(lax.broadcasted_iota(jnp.int32, shape, 1), 6)


def _attn_blocks(blocks):
    lane_head = _head_of_lane((Q_BLOCK, C_GW))
    first_head = lax.broadcasted_iota(jnp.int32, (Q_BLOCK, LANES), 1) < C_HEAD_DIM
    logits = []
    for q, kvp, kvc, _ in blocks:
        zero = jnp.zeros_like(q)
        qs = jnp.concatenate([jnp.where(lane_head == h, q, zero) for h in range(C_HPG)], axis=0)
        kcat = jnp.concatenate([kvp[:, 0:C_GW], kvc[:, 0:C_GW]], axis=0)
        logits.append(lax.dot_general(qs, kcat, (((1,), (1,)), ((), ())), preferred_element_type=F32))
    logits = [lg + bias for lg, (_, _, _, bias) in zip(logits, blocks)]
    ms = [jnp.max(lg, axis=-1, keepdims=True) for lg in logits]
    ps = [jnp.exp2(lg - m) for lg, m in zip(logits, ms)]
    ss = [jnp.sum(p, axis=-1, keepdims=True) for p in ps]
    pvs = []
    for (_, kvp, kvc, _), p in zip(blocks, ps):
        pb = p.astype(BF16)
        vlo = jnp.concatenate([kvp[:, C_GW:C_GW + LANES], kvc[:, C_GW:C_GW + LANES]], axis=0)
        vhi = jnp.concatenate([kvp[:, C_GW + LANES:], kvc[:, C_GW + LANES:]], axis=0)
        pvs.append((jnp.dot(pb[0:2 * Q_BLOCK], vlo, preferred_element_type=F32),
                    jnp.dot(pb[2 * Q_BLOCK:], vhi, preferred_element_type=F32)))
    outs = []
    for (lo, hi), m, s in zip(pvs, ms, ss):
        pick = lambda a, b: jnp.where(first_head, a, b)
        rows = lambda x, h: x[h * Q_BLOCK:(h + 1) * Q_BLOCK]
        merged = lambda x: jnp.concatenate([pick(rows(x, 0), rows(x, 1)), pick(rows(x, 2), rows(x, 3))], axis=-1)
        s_l, m_l = merged(s), merged(m)
        o = jnp.concatenate([pick(lo[0:Q_BLOCK], lo[Q_BLOCK:]), pick(hi[0:Q_BLOCK], hi[Q_BLOCK:])], axis=-1)
        outs.append((o * (1.0 / s_l), m_l + jnp.log2(s_l)))
    return outs


def _blocks_per_trip(count):
    return max(u for u in (5, 4, 3, 2, 1) if count % u == 0)


def _attn_prompt_body(q0, q1, q2, kv0, kv1, kv2, p0, p1, p2, bias_ref, yc_ref, osc, lsc, *, TM):
    qs_, kvs, prevs = (q0, q1, q2), (kv0, kv1, kv2), (p0, p1, p2)
    var = jnp.minimum(pl.program_id(1), 1)
    n_blocks = ATT_ROWS // Q_BLOCK

    def put(g, start, n, stride, o, l, r0):
        rows = pl.ds(start, n, stride=stride) if stride > 1 else pl.ds(start, n)
        for s in range(C_GW // LANES):
            osc[g, s, rows, :] = o[r0:r0 + n, s * LANES:(s + 1) * LANES]
            lsc[g, s, rows, :] = l[r0:r0 + n, s * LANES:(s + 1) * LANES]

    for g, (_, dil) in enumerate(C_PAIRS):
        q_ref, kv_ref, p_ref = qs_[g], kvs[g], prevs[g]
        cls = TM // dil
        if cls >= Q_BLOCK:
            prev_off = Q_BLOCK if dil == 1 else TM
            n_first = prev_off // Q_BLOCK
            per_tile = TM // Q_BLOCK

            def token_start(idx, dil=dil, per_tile=per_tile):
                return idx * Q_BLOCK if dil == 1 else (idx // per_tile) * TM + idx % per_tile

            first = []
            for idx in range(n_first):
                rows = slice(idx * Q_BLOCK, (idx + 1) * Q_BLOCK)
                first.append((q_ref[rows, :], p_ref[rows, :], kv_ref[rows, :], bias_ref[g, var]))
            for idx, (o, l) in enumerate(_attn_blocks(first)):
                put(g, token_start(idx), Q_BLOCK, dil, o, l, 0)

            per_trip = _blocks_per_trip(n_blocks - n_first)

            def body(trip, carry, g=g, dil=dil, q_ref=q_ref, kv_ref=kv_ref, prev_off=prev_off,
                     token_start=token_start, n_first=n_first, per_trip=per_trip):
                idxs = [n_first + trip * per_trip + u for u in range(per_trip)]
                blocks = []
                for idx in idxs:
                    st = pl.multiple_of(idx * Q_BLOCK, Q_BLOCK)
                    blocks.append((q_ref[pl.ds(st, Q_BLOCK), :], kv_ref[pl.ds(st - prev_off, Q_BLOCK), :],
                                   kv_ref[pl.ds(st, Q_BLOCK), :], bias_ref[g, 1]))
                for idx, (o, l) in zip(idxs, _attn_blocks(blocks)):
                    put(g, token_start(idx), Q_BLOCK, dil, o, l, 0)
                return carry

            lax.fori_loop(0, (n_blocks - n_first) // per_trip, body, 0)
        else:
            tiles = Q_BLOCK // cls
            per_trip = _blocks_per_trip(dil)

            def body(trip, carry, g=g, dil=dil, q_ref=q_ref, kv_ref=kv_ref, p_ref=p_ref, cls=cls, tiles=tiles,
                     per_trip=per_trip):
                rs = [trip * per_trip + u for u in range(per_trip)]
                blocks = []
                for r in rs:
                    st = pl.multiple_of(r * cls, cls)
                    gather = lambda ref, st=st: jnp.concatenate(
                        [ref[pl.ds(c * TM + st, cls), :] for c in range(tiles)], axis=0)
                    blocks.append((gather(q_ref), gather(p_ref), gather(kv_ref), bias_ref[g, var]))
                for r, (o, l) in zip(rs, _attn_blocks(blocks)):
                    for c in range(tiles):
                        put(g, c * TM + r, cls, dil, o, l, c * cls)
                return carry

            lax.fori_loop(0, dil // per_trip, body, 0)

    def merge(ch, carry):
        st = pl.multiple_of(ch * Q_BLOCK, Q_BLOCK)
        get = lambda ref, g: jnp.concatenate([ref[g, s, pl.ds(st, Q_BLOCK), :] for s in range(C_GW // LANES)], axis=-1)
        ls = [get(lsc, g) for g in range(len(C_PAIRS))]
        mm = jnp.maximum(jnp.maximum(ls[0], ls[1]), ls[2])
        es = [jnp.exp2(l - mm) for l in ls]
        den = es[0] + es[1] + es[2]
        yc = (es[0] / den) * get(osc, 0) + (es[1] / den) * get(osc, 1) + (es[2] / den) * get(osc, 2)
        yc_ref[pl.ds(st, Q_BLOCK), :] = yc.astype(BF16)
        return carry

    lax.fori_loop(0, n_blocks, merge, 0, unroll=2)


def _attn_prompt(qs, kvs, bias, *, TM):
    N, S, _ = qs[0].shape
    assert S % ATT_ROWS == 0 and ATT_ROWS == Q_BLOCK * C_PAIRS[-1][1] and ATT_ROWS % TM == 0
    cur = lambda c: pl.BlockSpec((None, ATT_ROWS, c), lambda n, i: (n, i, 0))
    prev_rows = [Q_BLOCK if dil == 1 else (TM if TM // dil >= Q_BLOCK else ATT_ROWS) for _, dil in C_PAIRS]
    prev = [pl.BlockSpec((None, pr, 2 * C_GW), lambda n, i, k=ATT_ROWS // pr: (n, jnp.maximum(i * k - 1, 0), 0))
            for pr in prev_rows]
    slabs = C_GW // LANES
    return pl.pallas_call(
        functools.partial(_attn_prompt_body, TM=TM),
        grid=(N, S // ATT_ROWS),
        in_specs=[cur(C_GW)] * 3 + [cur(2 * C_GW)] * 3 + prev + [_const_spec(bias.shape)],
        out_specs=cur(C_GW),
        out_shape=jax.ShapeDtypeStruct((N, S, C_GW), BF16),
        scratch_shapes=[pltpu.VMEM((len(C_PAIRS), slabs, ATT_ROWS, LANES), F32)] * 2,
        compiler_params=_cparams(2),
        name="attn_prompt",
    )(*qs, *kvs, *kvs, bias)


SROWS = 32
CACHE_SLOTS = 3


def _attn_sample_body(q_ref, kvn_ref, c0_hbm, c1_hbm, c2_hbm, bc0_ref, bc1_ref, bc2_ref, bn_ref,
                      yc_ref, n0_ref, n1_ref, n2_ref, ring0, ring1, ring2, sems):
    step, n_steps = pl.program_id(0), pl.num_programs(0)
    hbms, rings = (c0_hbm, c1_hbm, c2_hbm), (ring0, ring1, ring2)
    ahead = CACHE_SLOTS - 1

    def fetch(g, k):
        slot = k % CACHE_SLOTS if isinstance(k, int) else lax.rem(k, CACHE_SLOTS)
        return pltpu.make_async_copy(hbms[g].at[k], rings[g].at[slot], sems.at[g, slot])

    @pl.when(step == 0)
    def _():
        for k in range(ahead):
            for g in range(len(hbms)):
                fetch(g, k).start()

    @pl.when(step + ahead < n_steps)
    def _():
        for g in range(len(hbms)):
            fetch(g, step + ahead).start()

    for g in range(len(hbms)):
        fetch(g, step).wait()
    slot = lax.rem(step, CACHE_SLOTS)
    _attn_sample_one(q_ref.at[0], kvn_ref.at[0], tuple(r.at[slot] for r in rings),
                     (bc0_ref, bc1_ref, bc2_ref), bn_ref, yc_ref.at[0],
                     (n0_ref.at[0], n1_ref.at[0], n2_ref.at[0]))


def _attn_sample_one(q_ref, kvn_ref, caches, bcs, bn_ref, yc_ref, news):
    lane_head = _head_of_lane((SROWS, C_GW))
    row_head = lax.shift_right_logical(lax.broadcasted_iota(jnp.int32, (SROWS, C_GW), 0), 3)
    own = lane_head == row_head
    lane128 = lax.broadcasted_iota(jnp.int32, (SROWS, LANES), 1)
    kvn = kvn_ref[...]
    r0 = 8 - DEC_SEQ
    outs, lses = [], []
    for g, (win, dil) in enumerate(C_PAIRS):
        wb = win
        qg = q_ref[:, g * C_GW:(g + 1) * C_GW]
        qs = jnp.where(own, qg, jnp.zeros_like(qg))
        cache = caches[g]
        kt = cache[0:C_GW, :].astype(BF16)
        vt = cache[C_GW:2 * C_GW, :].astype(BF16)
        kn = kvn[:, g * C_GW:(g + 1) * C_GW]
        vn = kvn[:, C_Q + g * C_GW:C_Q + (g + 1) * C_GW]
        lc = jnp.dot(qs, kt, preferred_element_type=F32) + bcs[g][...]
        qf = qs.astype(F32)
        ln = bn_ref[g]
        for c in range(DEC_SEQ):
            d = jnp.sum(qf * kn[r0 + c:r0 + c + 1, :], axis=-1, keepdims=True)
            ln = ln + jnp.where(lane128 == c, d, 0.0)
        m = jnp.maximum(jnp.max(lc, axis=-1, keepdims=True), jnp.max(ln, axis=-1, keepdims=True))
        pc = jnp.exp(lc - m)
        pn = jnp.exp(ln - m)
        s = jnp.sum(pc, axis=-1, keepdims=True) + jnp.sum(pn, axis=-1, keepdims=True)
        pv = lax.dot_general(pc.astype(BF16), vt, (((1,), (1,)), ((), ())), preferred_element_type=F32)
        for c in range(DEC_SEQ):
            pcol = jnp.sum(jnp.where(lane128 == c, pn, 0.0), axis=-1, keepdims=True)
            pv = pv + pcol * vn[r0 + c:r0 + c + 1, :]
        outs.append(pv / s)
        lses.append(m + jnp.log(s))
        rolled = pltpu.roll(cache[...], wb - DEC_SEQ, 1)
        new_rows = jnp.concatenate([jnp.zeros((LANES - SUBLANES, 2 * C_GW), F32),
                                    jnp.concatenate([kn, vn], axis=-1)], axis=0)
        new_cols = new_rows.T
        lane_t = lax.broadcasted_iota(jnp.int32, (2 * C_GW, LANES), 1)
        if wb > LANES:
            news[g][:, 0:wb - LANES] = rolled[:, 0:wb - LANES]
        news[g][:, wb - LANES:wb] = jnp.where(lane_t >= LANES - DEC_SEQ, new_cols, rolled[:, wb - LANES:wb])
    mm = jnp.maximum(jnp.maximum(lses[0], lses[1]), lses[2])
    es = [jnp.exp(l - mm) for l in lses]
    den = es[0] + es[1] + es[2]
    y = (es[0] / den) * outs[0] + (es[1] / den) * outs[1] + (es[2] / den) * outs[2]
    y = jnp.where(own, y, 0.0)
    yc_ref[...] = y[0:8, :] + y[8:16, :] + y[16:24, :] + y[24:32, :]


def _attn_sample(q_rep, kvn, caches, bias_c, bias_n):
    NBt = q_rep.shape[0]
    assert NBt >= CACHE_SLOTS
    per_b = lambda r, c: pl.BlockSpec((1, r, c), lambda b: (b, 0, 0))
    wbs = [w for w, _ in C_PAIRS]
    return pl.pallas_call(
        _attn_sample_body,
        grid=(NBt,),
        in_specs=[per_b(SROWS, C_Q), per_b(8, 2 * C_Q)] + [pl.BlockSpec(memory_space=pl.ANY)] * len(wbs)
                 + [_const_spec((SROWS, w)) for w in wbs] + [_const_spec((len(wbs), SROWS, LANES))],
        out_specs=[per_b(8, C_GW)] + [per_b(2 * C_GW, w) for w in wbs],
        out_shape=[jax.ShapeDtypeStruct((NBt, 8, C_GW), F32)]
                  + [jax.ShapeDtypeStruct((NBt, 2 * C_GW, w), F32) for w in wbs],
        scratch_shapes=[pltpu.VMEM((CACHE_SLOTS, 2 * C_GW, w), F32) for w in wbs]
                       + [pltpu.SemaphoreType.DMA((len(wbs), CACHE_SLOTS))],
        compiler_params=_cparams(1),
        name="attn_sample",
    )(q_rep, kvn, *caches, *bias_c, bias_n)


def _t5_bucket(dist):
    dist = np.asarray(dist)
    max_exact = N_BUCKETS // 2
    large = max_exact + (np.log(np.maximum(dist, max_exact) / max_exact) / np.log(MAX_DISTANCE / max_exact)
                         * (N_BUCKETS - max_exact)).astype(np.int32)
    large = np.minimum(large, N_BUCKETS - 1)
    return np.where(dist < max_exact, dist, large).astype(np.int32)


def _group_bias(rel_bias, g, dil):
    buckets = _t5_bucket(dil * np.arange(C_TAPS + 1))
    return rel_bias[buckets][:, g * C_HPG:(g + 1) * C_HPG].T


def _toeplitz_body(c_ref, o_ref):
    keep = lax.broadcasted_iota(jnp.int32, (Q_BLOCK, 2 * Q_BLOCK), 1) >= Q_BLOCK
    for h in range(C_HPG):
        taps = jnp.broadcast_to(c_ref[h:h + 1, :], (Q_BLOCK, 2 * Q_BLOCK))
        t = pltpu.roll(taps, 0, 1, stride=1, stride_axis=0)
        o_ref[1, h * Q_BLOCK:(h + 1) * Q_BLOCK, :] = t
        o_ref[0, h * Q_BLOCK:(h + 1) * Q_BLOCK, :] = jnp.where(keep, t, NEG)


def _prompt_bias(bias_gs):
    c = jnp.stack([jnp.concatenate([b[:, ::-1].astype(F32) * LOG2E, jnp.full((C_HPG, Q_BLOCK - 1), NEG, F32)], axis=1)
                   for b in bias_gs])
    return pl.pallas_call(
        _toeplitz_body,
        grid=(len(bias_gs),),
        in_specs=[pl.BlockSpec((None, C_HPG, 2 * Q_BLOCK), lambda g: (g, 0, 0))],
        out_specs=pl.BlockSpec((None, 2, C_HPG * Q_BLOCK, 2 * Q_BLOCK), lambda g: (g, 0, 0, 0)),
        out_shape=jax.ShapeDtypeStruct((len(bias_gs), 2, C_HPG * Q_BLOCK, 2 * Q_BLOCK), F32),
        compiler_params=_cparams(1),
        name="attn_bias",
    )(c)


def _sample_bias(bias_g, wb, dil):
    n = wb + DEC_SEQ
    t = np.arange(DEC_SEQ)[:, None]
    d = wb + t - np.arange(n)[None, :]
    valid = (d >= 0) & (d % dil == 0) & (d // dil <= C_TAPS)
    place = np.zeros((DEC_SEQ, C_TAPS + 1, n), np.float32)
    ti, ii = np.nonzero(valid)
    place[ti, (d // dil)[ti, ii], ii] = 1.0
    ext = jnp.einsum("hj,tji->hti", bias_g.astype(F32), place, precision=lax.Precision.HIGHEST)
    ext = jnp.where(valid[None], ext, NEG)
    pad_rows = (0, SUBLANES - DEC_SEQ)
    bc = jnp.pad(ext[:, :, :wb], ((0, 0), pad_rows, (0, 0))).reshape(SROWS, wb)
    bn = jnp.pad(ext[:, :, wb:], ((0, 0), pad_rows, (0, LANES - DEC_SEQ)), constant_values=NEG).reshape(SROWS, LANES)
    return bc, bn


def _time_major(s):
    b, k, c = s.shape
    return s.transpose(1, 0, 2).reshape(1, k * b, c)


def _batch_major(s, b):
    _, r, c = s.shape
    return s.reshape(r // b, b, c).transpose(1, 0, 2)


def _stack(x, mod, st, w, *, TM, TM_WIDE, RS, pos0):
    new = {}
    ng = w["norm_g"]
    cast_keys = [k for k in ("ffn_w_up", "ffn_w_down", "cd_w_in", "cd_w_out") if w[k].dtype != BF16]
    x, new["a"], new["b"], cast_out = _mixer_ab(
        x, mod[0], ng[0], st["a"], st["b"], w["ab_w_in"], w["a_conv_w"], w["b_w_grp"], w["b_scale"], w["ab_w_out"],
        TM=TM_WIDE, RS=RS, pos0=pos0, cast=tuple(w[k] for k in cast_keys))
    w = dict(w, **dict(zip(cast_keys, cast_out)))
    x, new["f0"] = _conv_ffn(x, mod[0], ng[0], st["f0"], w["ffn_w_up"], w["ffn_conv_w"],
                             w["ffn_conv_b"], w["ffn_w_down"], layer=0, TM=TM_WIDE, RS=RS)
    prompt = RS == 1
    outs = _cd_in(x, mod[1], ng[1], st["d"], w["cd_w_in"][0], w["d_conv_w"], w["d_conv_b"], w["d_ln_g"], w["d_ln_b"],
                  TM=TM, RS=RS, prompt_attn=prompt)
    qs, kvs, yd, new["d"] = outs[0:3], outs[3:6], outs[-2], outs[-1]
    cache_layout = lambda n: n.reshape(n.shape[0], 2, C_HPG, C_HEAD_DIM, n.shape[2]).transpose(0, 4, 1, 2, 3)
    if prompt:
        yc = _attn_prompt(qs, kvs, _prompt_bias(w["bias_g"]), TM=TM)
        for g in range(len(C_PAIRS)):
            new["c%d" % g] = cache_layout(outs[6 + g])
    else:
        B = RS
        kvf = outs[6]
        qb = _batch_major(jnp.concatenate(qs, axis=-1), B)
        q_rep = jnp.pad(jnp.broadcast_to(qb[:, None], (B, C_HPG, DEC_SEQ, C_Q)),
                        ((0, 0), (0, 0), (0, 8 - DEC_SEQ), (0, 0))).reshape(B, SROWS, C_Q)
        kvn = jnp.pad(_batch_major(kvf, B), ((0, 0), (8 - DEC_SEQ, 0), (0, 0)))
        bias = [_sample_bias(w["bias_g"][g], win, dil) for g, (win, dil) in enumerate(C_PAIRS)]
        yc, n0, n1, n2 = _attn_sample(q_rep, kvn, st["c"], [b[0] for b in bias],
                                      jnp.stack([b[1] for b in bias]))
        for g, n in enumerate((n0, n1, n2)):
            new["c%d" % g] = cache_layout(n)
        yc = _time_major(yc[:, :DEC_SEQ])
    x, new["f1"] = _conv_ffn(x, mod[1], ng[1], st["f1"], w["ffn_w_up"], w["ffn_conv_w"],
                             w["ffn_conv_b"], w["ffn_w_down"], layer=1, TM=TM_WIDE, RS=RS,
                             mixer_out=(yc, yd, w["cd_w_out"][0]))
    return x, new, w


def kernel(x_prompt, x_sample, state_a_conv, state_b_pool, cache_c_win128, cache_c_win512, cache_c_win2048,
           state_d_conv, state_ffn_conv, c_prompt, c_sample, ada_w, ada_b, norm_g, rel_bias, ab_w_in, a_conv_w,
           b_w_grp, b_scale, ab_w_out, cd_w_in, d_conv_w, d_conv_b, d_ln_g, d_ln_b, cd_w_out, ffn_w_up,
           ffn_conv_w, ffn_conv_b, ffn_w_down):
    B, T = DEC_BATCH, DEC_SEQ
    w = dict(norm_g=norm_g,
             ab_w_in=ab_w_in[0].astype(BF16), a_conv_w=a_conv_w[0], b_w_grp=b_w_grp[0].astype(BF16),
             b_scale=b_scale, ab_w_out=ab_w_out[0].astype(BF16),
             cd_w_in=cd_w_in, cd_w_out=cd_w_out,
             d_conv_w=jnp.broadcast_to(d_conv_w[0][:, None, :], (D_CONV, SUBLANES, D_WIDTH)),
             d_conv_b=d_conv_b, d_ln_g=d_ln_g, d_ln_b=d_ln_b,
             ffn_w_up=ffn_w_up, ffn_conv_w=ffn_conv_w, ffn_conv_b=ffn_conv_b[:, None, :], ffn_w_down=ffn_w_down,
             bias_g=[_group_bias(rel_bias, g, dil) for g, (_, dil) in enumerate(C_PAIRS)])

    mod = _ada(jnp.concatenate([c_prompt, c_sample], axis=0), ada_w, ada_b)
    mod_p = mod[:, :BATCH].reshape(DEPTH, BATCH, 6, 1, D_MODEL)
    mod_s = mod[:, BATCH:].reshape(DEPTH, B, 6, D_MODEL).transpose(0, 2, 1, 3)
    mod_s = jnp.broadcast_to(mod_s[:, :, None], (DEPTH, 6, T, B, D_MODEL)).reshape(DEPTH, 1, 6, T * B, D_MODEL)

    zeros = lambda k, c: jnp.zeros((BATCH, k, c), F32)
    st_p = dict(a=zeros(A_CONV - 1, A_WIDTH), b=zeros(B_PREV, B_WIDTH), d=zeros(D_CONV - 1, D_WIDTH),
                f0=zeros(FFN_CONV - 1, 2 * D_FF), f1=zeros(FFN_CONV - 1, 2 * D_FF))
    y_p, np_, w = _stack(x_prompt, mod_p, st_p, w, TM=TM_PROMPT, TM_WIDE=TM_WIDE_PROMPT, RS=1, pos0=0)

    st_s = dict(a=_time_major(state_a_conv[0]), b=_time_major(state_b_pool[0]), d=_time_major(state_d_conv[0]),
                f0=_time_major(state_ffn_conv[0]), f1=_time_major(state_ffn_conv[1]),
                c=[c[0].transpose(0, 2, 3, 4, 1).reshape(B, 2 * C_GW, c.shape[2])
                   for c in (cache_c_win128, cache_c_win512, cache_c_win2048)])
    y_s, ns, _ = _stack(_time_major(x_sample), mod_s, st_s, w, TM=T * B, TM_WIDE=T * B, RS=B, pos0=PAST_LEN)

    bm = lambda s: _batch_major(s, B)
    return (y_p, bm(y_s),
            np_["a"][None], bm(ns["a"])[None], np_["b"][None], bm(ns["b"])[None],
            np_["c0"][None], ns["c0"][None], np_["c1"][None], ns["c1"][None], np_["c2"][None], ns["c2"][None],
            np_["d"][None], bm(ns["d"])[None],
            jnp.stack([np_["f0"], np_["f1"]]), jnp.stack([bm(ns["f0"]), bm(ns["f1"])]))
```

```python
import functools
import math

import numpy as np
import jax
import jax.numpy as jnp
from jax import lax
from jax.experimental import pallas as pl
from jax.experimental.pallas import tpu as pltpu

D_MODEL = 1024
BATCH = 4
SEQ = 4096
DEPTH = 2
DEC_BATCH = 32
DEC_SEQ = 4
PAST_LEN = 8192
EPS = 1e-6
A_WIDTH = 512
A_CONV = 3
B_WIDTH = 512
B_WINDOWS = (2, 4, 8, 16)
B_GROUP = 128
B_PREV = 15
C_PAIRS = ((128, 1), (512, 4), (2048, 16))
C_HPG = 4
C_HEAD_DIM = 64
C_HEADS = 12
C_Q = C_HEADS * C_HEAD_DIM
C_QKV = 3 * C_Q
C_GW = C_HPG * C_HEAD_DIM
C_TAPS = 128
ATTN_SCALE = C_HEAD_DIM ** -0.5
LOG2E = math.log2(math.e)
Q_BLOCK = 128
N_BUCKETS = 32
MAX_DISTANCE = 2048
D_WIDTH = 512
D_CONV = 31
D_FF = 2816
FFN_CONV = 3

SUBLANES = 8
LANES = 128
VMEM_LIMIT = 56 * 1024 * 1024
NEG = -1e30
TM_PROMPT = 512
TM_WIDE_PROMPT = 1024
ATT_ROWS = 2048
ADA_COLS = 1536
FF_CHUNK = 256
D_ROWS = 32

F32 = jnp.float32
BF16 = jnp.bfloat16


def _ru8(n):
    return -(-n // SUBLANES) * SUBLANES


def _pool_levels(rs):
    l1 = _ru8(rs)
    l2 = _ru8(l1 + 2 * rs)
    l3 = _ru8(l2 + 4 * rs)
    l4 = _ru8(l3 + 8 * rs)
    return l1, l2, l3, l4


def _slab_put(ref, r0, val):
    n = val.shape[0]
    for s in range(val.shape[1] // LANES):
        ref[s, r0:r0 + n, :] = val[:, s * LANES:(s + 1) * LANES]


def _slab_get(ref, r0, n, s0=0, ns=None):
    ns = ref.shape[0] - s0 if ns is None else ns
    return jnp.concatenate([ref[s, r0:r0 + n, :] for s in range(s0, s0 + ns)], axis=-1)


def _rms(x, g):
    return x * lax.rsqrt(jnp.mean(x * x, axis=-1, keepdims=True) + EPS) * g


def _sigmoid(x):
    return 1.0 / (1.0 + jnp.exp(-x))


def _cparams(n_axes):
    return pltpu.CompilerParams(dimension_semantics=("arbitrary",) * n_axes, vmem_limit_bytes=VMEM_LIMIT)


def _const_spec(shape):
    nd = len(shape)
    return pl.BlockSpec(shape, lambda *_: (0,) * nd, pipeline_mode=pl.Buffered(1))


def _ada_body(c_ref, w_ref, b_ref, o_ref):
    c = c_ref[...]
    ca = c * _sigmoid(c)
    o_ref[...] = jnp.dot(ca.astype(BF16), w_ref[...].astype(BF16), preferred_element_type=F32) + b_ref[...]


def _ada(c_all, ada_w, ada_b):
    rows = c_all.shape[0]
    tn = ADA_COLS
    return pl.pallas_call(
        _ada_body,
        grid=(DEPTH, 6 * D_MODEL // tn),
        in_specs=[pl.BlockSpec((rows, D_MODEL), lambda l, n: (0, 0)),
                  pl.BlockSpec((None, D_MODEL, tn), lambda l, n: (l, 0, n)),
                  pl.BlockSpec((None, 1, tn), lambda l, n: (l, 0, n))],
        out_specs=pl.BlockSpec((None, rows, tn), lambda l, n: (l, 0, n)),
        out_shape=jax.ShapeDtypeStruct((DEPTH, rows, 6 * D_MODEL), F32),
        compiler_params=_cparams(2),
        name="ada",
    )(c_all, ada_w, ada_b.reshape(DEPTH, 1, 6 * D_MODEL))


def _ab_body(*refs, TM, RS, NT, pos0, n_cast):
    (x_ref, mod_ref, ng_ref, ap_ref, bp_ref, win_ref, aw_ref, bw_ref, bs_ref, wout_ref) = refs[:10]
    cast_in = refs[10:10 + n_cast]
    x1_ref, an_ref, bn_ref = refs[10 + n_cast:13 + n_cast]
    cast_out = refs[13 + n_cast:13 + 2 * n_cast]
    ea, eb, s2, s4, s8 = refs[13 + 2 * n_cast:]
    for src, dst in zip(cast_in, cast_out):
        dst[...] = src[...].astype(BF16)
    j = pl.program_id(1)
    HA = _ru8((A_CONV - 1) * RS)
    L1, L2, L3, HB = _pool_levels(RS)
    E = HB + TM
    na, nbp = (A_CONV - 1) * RS, B_PREV * RS

    @pl.when(j == 0)
    def _():
        _slab_put(ea, HA - na, ap_ref[...])
        if HB > nbp:
            _slab_put(eb, 0, jnp.zeros((HB - nbp, B_WIDTH), F32))
        _slab_put(eb, HB - nbp, bp_ref[...])

    x = x_ref[...]
    sh1, sc1, g1 = mod_ref[0], mod_ref[1], mod_ref[2]
    h = _rms(x, ng_ref[0:1, :] * (1.0 + sc1)) + sh1
    proj = jnp.dot(h.astype(BF16), win_ref[...], preferred_element_type=F32)
    hh, bg = proj[:, 0:A_WIDTH], proj[:, A_WIDTH:2 * A_WIDTH]
    cg, u = proj[:, 2 * A_WIDTH:3 * A_WIDTH], proj[:, 3 * A_WIDTH:]

    v = cg * hh
    _slab_put(ea, HA, v)
    z = (aw_ref[2:3, :] * v + aw_ref[1:2, :] * _slab_get(ea, HA - RS, TM)
         + aw_ref[0:1, :] * _slab_get(ea, HA - 2 * RS, TM))
    ya = bg * z
    a_last = _slab_get(ea, HA + TM - na, na)
    an_ref[...] = a_last
    if NT > 1:
        _slab_put(ea, HA - na, a_last)

    _slab_put(eb, HB, u)
    _slab_put(s2, L1, _slab_get(eb, L1, E - L1) + _slab_get(eb, L1 - RS, E - L1))
    _slab_put(s4, L2, _slab_get(s2, L2, E - L2, 1) + _slab_get(s2, L2 - 2 * RS, E - L2, 1))
    _slab_put(s8, L3, _slab_get(s4, L3, E - L3, 1) + _slab_get(s4, L3 - 4 * RS, E - L3, 1))
    wsum = (s2[0, HB:E, :], s4[0, HB:E, :], s8[0, HB:E, :], s8[1, HB:E, :] + s8[1, HB - 8 * RS:E - 8 * RS, :])
    b_last = _slab_get(eb, E - nbp, nbp)
    bn_ref[...] = b_last
    if NT > 1:
        _slab_put(eb, HB - nbp, b_last)
    row = lax.broadcasted_iota(jnp.int32, (TM, B_GROUP), 0) + j * TM
    pos1 = lax.shift_right_logical(row, int(math.log2(RS))) + (pos0 + 1)
    ybs = []
    for g, win in enumerate(B_WINDOWS):
        cnt = jnp.minimum(pos1, win).astype(F32)
        pooled = wsum[g] / cnt - u[:, g * B_GROUP:(g + 1) * B_GROUP]
        ybs.append(jnp.dot(pooled.astype(BF16), bw_ref[g], preferred_element_type=F32))
    yb = jnp.concatenate(ybs, axis=-1) * bs_ref[...]

    ycat = jnp.concatenate([ya, yb], axis=-1).astype(BF16)
    y = jnp.dot(ycat, wout_ref[...], preferred_element_type=F32)
    x1_ref[...] = x + _rms(y, ng_ref[1:2, :] * g1)


def _mixer_ab(x, mod, ng, a_prev, b_prev, w_in, a_w, b_w, b_scale, w_out, *, TM, RS, pos0, cast=()):
    NB, R, _ = x.shape
    NT = R // TM
    MR = mod.shape[2]
    steps = NB * NT
    cast_specs, cast_shapes = [], []
    for a in cast:
        rows = a.shape[1] // steps
        assert rows * steps == a.shape[1] and rows % (2 * SUBLANES) == 0
        cast_specs.append(pl.BlockSpec((a.shape[0], rows, a.shape[2]), lambda n, j: (0, n * NT + j, 0)))
        cast_shapes.append(jax.ShapeDtypeStruct(a.shape, BF16))
    _, _, _, HB = _pool_levels(RS)
    HA = _ru8((A_CONV - 1) * RS)
    E = HB + TM
    na, nbp = (A_CONV - 1) * RS, B_PREV * RS
    tile = lambda c: pl.BlockSpec((None, TM, c), lambda n, j: (n, j, 0))
    per_n = lambda r, c: pl.BlockSpec((None, r, c), lambda n, j: (n, 0, 0))
    outs = pl.pallas_call(
        functools.partial(_ab_body, TM=TM, RS=RS, NT=NT, pos0=pos0, n_cast=len(cast)),
        grid=(NB, NT),
        in_specs=[tile(D_MODEL),
                  pl.BlockSpec((None, 6, MR, D_MODEL), lambda n, j: (n, 0, 0, 0)),
                  _const_spec((4, D_MODEL)),
                  per_n(na, A_WIDTH), per_n(nbp, B_WIDTH),
                  _const_spec(w_in.shape), _const_spec(a_w.shape), _const_spec(b_w.shape),
                  _const_spec(b_scale.shape), _const_spec(w_out.shape)] + cast_specs,
        out_specs=[tile(D_MODEL), per_n(na, A_WIDTH), per_n(nbp, B_WIDTH)] + cast_specs,
        out_shape=[jax.ShapeDtypeStruct((NB, R, D_MODEL), F32),
                   jax.ShapeDtypeStruct((NB, na, A_WIDTH), F32),
                   jax.ShapeDtypeStruct((NB, nbp, B_WIDTH), F32)] + cast_shapes,
        scratch_shapes=[pltpu.VMEM((4, HA + TM, LANES), F32), pltpu.VMEM((4, E, LANES), F32),
                        pltpu.VMEM((4, E, LANES), F32), pltpu.VMEM((3, E, LANES), F32),
                        pltpu.VMEM((2, E, LANES), F32)],
        compiler_params=_cparams(2),
        name="mixer_ab",
    )(x, mod, ng, a_prev, b_prev, w_in, a_w, b_w, b_scale, w_out, *cast)
    return outs[0], outs[1], outs[2], tuple(outs[3:])


def _ffn_body(*refs, TM, RS, NT, mixer_out):
    if mixer_out:
        (x_ref, mod_ref, ng_ref, fp_ref, wup_ref, cw_ref, cb_ref, wdn_ref, yc_ref, yd_ref, wout_ref,
         y_ref, fn_ref, halo, ext, act) = refs
    else:
        (x_ref, mod_ref, ng_ref, fp_ref, wup_ref, cw_ref, cb_ref, wdn_ref,
         y_ref, fn_ref, halo, ext, act) = refs
    j = pl.program_id(1)
    nf = (FFN_CONV - 1) * RS
    HF = _ru8(nf)

    @pl.when(j == 0)
    def _():
        halo[...] = fp_ref[...]

    x = x_ref[...]
    if mixer_out:
        ym = (jnp.dot(yc_ref[...].astype(BF16), wout_ref[0:C_GW, :], preferred_element_type=F32)
              + jnp.dot(yd_ref[...], wout_ref[C_GW:, :], preferred_element_type=F32))
        x = x + _rms(ym, ng_ref[1:2, :] * mod_ref[2])
    sh2, sc2, g2 = mod_ref[3], mod_ref[4], mod_ref[5]
    hb = (_rms(x, ng_ref[2:3, :] * (1.0 + sc2)) + sh2).astype(BF16)

    def conv_part(col, buf):
        up = jnp.dot(hb, wup_ref[:, col:col + FF_CHUNK], preferred_element_type=F32)
        _slab_put(buf, HF - nf, halo[:, col:col + FF_CHUNK])
        _slab_put(buf, HF, up)
        out = (cw_ref[2:3, col:col + FF_CHUNK] * up
               + cw_ref[1:2, col:col + FF_CHUNK] * _slab_get(buf, HF - RS, TM)
               + cw_ref[0:1, col:col + FF_CHUNK] * _slab_get(buf, HF - 2 * RS, TM)
               + cb_ref[:, col:col + FF_CHUNK])
        halo[:, col:col + FF_CHUNK] = _slab_get(buf, HF + TM - nf, nf)
        return out

    for c in range(D_FF // FF_CHUNK):
        a = conv_part(c * FF_CHUNK, ext.at[2 * (c % 2)])
        g = conv_part(D_FF + c * FF_CHUNK, ext.at[2 * (c % 2) + 1])
        act[:, c * FF_CHUNK:(c + 1) * FF_CHUNK] = (a * (g * _sigmoid(g))).astype(BF16)
    fn_ref[...] = halo[...]
    y = jnp.dot(act[...], wdn_ref[...], preferred_element_type=F32)
    y_ref[...] = x + _rms(y, ng_ref[3:4, :] * g2)


def _layer_spec(shape, layer):
    nd = len(shape) - 1
    return pl.BlockSpec((None,) + tuple(shape[1:]), lambda *_: (layer,) + (0,) * nd, pipeline_mode=pl.Buffered(1))


def _conv_ffn(x, mod, ng, f_prev, w_up, conv_w, conv_b, w_down, *, layer, TM, RS, mixer_out=None):
    NB, R, _ = x.shape
    NT = R // TM
    MR = mod.shape[2]
    nf = (FFN_CONV - 1) * RS
    tile_c = lambda c: pl.BlockSpec((None, TM, c), lambda n, j: (n, j, 0))
    tile = tile_c(D_MODEL)
    per_n = pl.BlockSpec((None, nf, 2 * D_FF), lambda n, j: (n, 0, 0))
    mix_in, mix_specs = [], []
    if mixer_out is not None:
        mix_in = list(mixer_out)
        mix_specs = [tile_c(C_GW), tile_c(D_WIDTH), _const_spec(mixer_out[2].shape)]
    return pl.pallas_call(
        functools.partial(_ffn_body, TM=TM, RS=RS, NT=NT, mixer_out=mixer_out is not None),
        grid=(NB, NT),
        in_specs=[tile,
                  pl.BlockSpec((None, 6, MR, D_MODEL), lambda n, j: (n, 0, 0, 0)),
                  _const_spec((4, D_MODEL)),
                  per_n,
                  _layer_spec(w_up.shape, layer), _layer_spec(conv_w.shape, layer),
                  _layer_spec(conv_b.shape, layer), _layer_spec(w_down.shape, layer)] + mix_specs,
        out_specs=[tile, per_n],
        out_shape=[jax.ShapeDtypeStruct((NB, R, D_MODEL), F32),
                   jax.ShapeDtypeStruct((NB, nf, 2 * D_FF), F32)],
        scratch_shapes=[pltpu.VMEM((nf, 2 * D_FF), F32),
                        pltpu.VMEM((4, FF_CHUNK // LANES, _ru8(nf) + TM, LANES), F32),
                        pltpu.VMEM((TM, D_FF), BF16)],
        compiler_params=_cparams(2),
        name="conv_ffn",
    )(x, mod, ng, f_prev, w_up, conv_w, conv_b, w_down, *mix_in)


def _cd_in_body(*refs, TM, RS, NT, prompt_attn):
    if prompt_attn:
        (x_ref, mod_ref, ng_ref, dp_ref, win_ref, dw_ref, db_ref, lg_ref, lb_ref, perm_ref,
         q0_ref, q1_ref, q2_ref, kv0_ref, kv1_ref, kv2_ref, c0_ref, c1_ref, c2_ref, yd_ref, dn_ref,
         ed, kvf_ref) = refs
        cache_refs = (c0_ref, c1_ref, c2_ref)
    else:
        (x_ref, mod_ref, ng_ref, dp_ref, win_ref, dw_ref, db_ref, lg_ref, lb_ref,
         q0_ref, q1_ref, q2_ref, kv0_ref, kv1_ref, kv2_ref, kvf_ref, yd_ref, dn_ref, ed) = refs
    q_refs, kv_refs = (q0_ref, q1_ref, q2_ref), (kv0_ref, kv1_ref, kv2_ref)
    j = pl.program_id(1)
    nd = (D_CONV - 1) * RS
    HD = _ru8(nd)

    @pl.when(j == 0)
    def _():
        _slab_put(ed, HD - nd, dp_ref[...])

    x = x_ref[...]
    sh1, sc1 = mod_ref[0], mod_ref[1]
    h = _rms(x, ng_ref[0:1, :] * (1.0 + sc1)) + sh1
    proj = jnp.dot(h.astype(BF16), win_ref[...], preferred_element_type=F32)
    qs = (proj[:, 0:C_Q] * (ATTN_SCALE * LOG2E if prompt_attn else ATTN_SCALE)).astype(BF16)
    ks = proj[:, C_Q:2 * C_Q].astype(BF16)
    vs = proj[:, 2 * C_Q:C_QKV].astype(BF16)
    for g in range(len(C_PAIRS)):
        cols = slice(g * C_GW, (g + 1) * C_GW)
        xg = jnp.concatenate([qs[:, cols], ks[:, cols], vs[:, cols]], axis=-1)
        if prompt_attn and C_PAIRS[g][1] > 1:
            xg = jnp.dot(perm_ref[g - 1], xg, preferred_element_type=F32).astype(BF16)
        q_refs[g][...] = xg[:, 0:C_GW]
        kv_refs[g][...] = xg[:, C_GW:]
    kvf_ref[...] = proj[:, C_Q:C_QKV]
    dv, dg = proj[:, C_QKV:C_QKV + D_WIDTH], proj[:, C_QKV + D_WIDTH:]
    _slab_put(ed, HD, dv * _sigmoid(dg))

    db, lg, lb = db_ref[...], lg_ref[...], lb_ref[...]
    tiles = D_ROWS // SUBLANES
    for c in range(TM // D_ROWS):
        accs = []
        for s in range(D_WIDTH // LANES):
            acc = jnp.zeros((tiles, SUBLANES, LANES), F32)
            for kk in range(D_CONV):
                off = HD - (D_CONV - 1 - kk) * RS + c * D_ROWS
                tap = ed[s, off:off + D_ROWS, :].reshape(tiles, SUBLANES, LANES)
                acc = acc + dw_ref[kk, :, s * LANES:(s + 1) * LANES] * tap
            accs.append(acc.reshape(D_ROWS, LANES))
        zc = jnp.concatenate(accs, axis=-1) + db
        mu = jnp.mean(zc, axis=-1, keepdims=True)
        zc = zc - mu
        var = jnp.mean(zc * zc, axis=-1, keepdims=True)
        yl = zc * lax.rsqrt(var + EPS) * lg + lb
        yd_ref[c * D_ROWS:(c + 1) * D_ROWS, :] = (yl * _sigmoid(yl)).astype(BF16)

    d_last = _slab_get(ed, HD + TM - nd, nd)
    dn_ref[...] = d_last
    if NT > 1:
        _slab_put(ed, HD - nd, d_last)

    if prompt_attn:
        for g, (win, _) in enumerate(C_PAIRS):
            cols = min(win, TM)

            @pl.when(j >= NT - max(win // TM, 1))
            def _(g=g, cols=cols):
                kt = kvf_ref[:, g * C_GW:(g + 1) * C_GW].T
                vt = kvf_ref[:, C_Q + g * C_GW:C_Q + (g + 1) * C_GW].T
                cache_refs[g][0:C_GW, :] = kt[:, TM - cols:]
                cache_refs[g][C_GW:, :] = vt[:, TM - cols:]


def _residue_perm(tm, dil):
    p = np.zeros((tm, tm), np.float32)
    a, r = np.meshgrid(np.arange(tm // dil), np.arange(dil), indexing="ij")
    p[(r * (tm // dil) + a).ravel(), (dil * a + r).ravel()] = 1.0
    return p


def _cd_in(x, mod, ng, d_prev, w_in, d_w, d_b, ln_g, ln_b, *, TM, RS, prompt_attn):
    NB, R, _ = x.shape
    NT = R // TM
    MR = mod.shape[2]
    nd = (D_CONV - 1) * RS
    tile = lambda c: pl.BlockSpec((None, TM, c), lambda n, j: (n, j, 0))
    per_n = pl.BlockSpec((None, nd, D_WIDTH), lambda n, j: (n, 0, 0))
    perm_in, perm_spec = [], []
    if prompt_attn:
        perm_in = [jnp.asarray(np.stack([_residue_perm(TM, dil) for _, dil in C_PAIRS[1:]]), BF16)]
        perm_spec = [_const_spec((len(C_PAIRS) - 1, TM, TM))]
        assert all(win <= R and (win % TM == 0 or TM % win == 0) for win, _ in C_PAIRS)
        kv_specs = [pl.BlockSpec((None, 2 * C_GW, min(win, TM)),
                                 lambda n, j, first=NT - max(win // TM, 1): (n, 0, jnp.maximum(j - first, 0)))
                    for win, _ in C_PAIRS]
        kv_shapes = [jax.ShapeDtypeStruct((NB, 2 * C_GW, win), F32) for win, _ in C_PAIRS]
        kv_scratch = [pltpu.VMEM((TM, 2 * C_Q), F32)]
    else:
        kv_specs = [tile(2 * C_Q)]
        kv_shapes = [jax.ShapeDtypeStruct((NB, R, 2 * C_Q), F32)]
        kv_scratch = []
    return pl.pallas_call(
        functools.partial(_cd_in_body, TM=TM, RS=RS, NT=NT, prompt_attn=prompt_attn),
        grid=(NB, NT),
        in_specs=[tile(D_MODEL),
                  pl.BlockSpec((None, 6, MR, D_MODEL), lambda n, j: (n, 0, 0, 0)),
                  _const_spec((4, D_MODEL)),
                  per_n,
                  _const_spec(w_in.shape), _const_spec(d_w.shape), _const_spec(d_b.shape),
                  _const_spec(ln_g.shape), _const_spec(ln_b.shape)] + perm_spec,
        out_specs=[tile(C_GW)] * 3 + [tile(2 * C_GW)] * 3 + kv_specs + [tile(D_WIDTH), per_n],
        out_shape=[jax.ShapeDtypeStruct((NB, R, C_GW), BF16)] * 3
                  + [jax.ShapeDtypeStruct((NB, R, 2 * C_GW), BF16)] * 3
                  + kv_shapes
                  + [jax.ShapeDtypeStruct((NB, R, D_WIDTH), BF16),
                     jax.ShapeDtypeStruct((NB, nd, D_WIDTH), F32)],
        scratch_shapes=[pltpu.VMEM((D_WIDTH // LANES, _ru8(nd) + TM, LANES), F32)] + kv_scratch,
        compiler_params=_cparams(2),
        name="cd_in",
    )(x, mod, ng, d_prev, w_in, d_w, d_b, ln_g, ln_b, *perm_in)


def _head_of_lane(shape):
    return lax.shift_right_logical(lax.broadcasted_iota(jnp.int32, shape, 1), 6)


def _attn_blocks(blocks):
    lane_head = _head_of_lane((Q_BLOCK, C_GW))
    first_head = lax.broadcasted_iota(jnp.int32, (Q_BLOCK, LANES), 1) < C_HEAD_DIM
    logits = []
    for q, kvp, kvc, _ in blocks:
        zero = jnp.zeros_like(q)
        qs = jnp.concatenate([jnp.where(lane_head == h, q, zero) for h in range(C_HPG)], axis=0)
        kcat = jnp.concatenate([kvp[:, 0:C_GW], kvc[:, 0:C_GW]], axis=0)
        logits.append(lax.dot_general(qs, kcat, (((1,), (1,)), ((), ())), preferred_element_type=F32))
    logits = [lg + bias for lg, (_, _, _, bias) in zip(logits, blocks)]
    ms = [jnp.max(lg, axis=-1, keepdims=True) for lg in logits]
    ps = [jnp.exp2(lg - m) for lg, m in zip(logits, ms)]
    ss = [jnp.sum(p, axis=-1, keepdims=True) for p in ps]
    pvs = []
    for (_, kvp, kvc, _), p in zip(blocks, ps):
        pb = p.astype(BF16)
        vlo = jnp.concatenate([kvp[:, C_GW:C_GW + LANES], kvc[:, C_GW:C_GW + LANES]], axis=0)
        vhi = jnp.concatenate([kvp[:, C_GW + LANES:], kvc[:, C_GW + LANES:]], axis=0)
        pvs.append((jnp.dot(pb[0:2 * Q_BLOCK], vlo, preferred_element_type=F32),
                    jnp.dot(pb[2 * Q_BLOCK:], vhi, preferred_element_type=F32)))
    outs = []
    for (lo, hi), m, s in zip(pvs, ms, ss):
        pick = lambda a, b: jnp.where(first_head, a, b)
        rows = lambda x, h: x[h * Q_BLOCK:(h + 1) * Q_BLOCK]
        merged = lambda x: jnp.concatenate([pick(rows(x, 0), rows(x, 1)), pick(rows(x, 2), rows(x, 3))], axis=-1)
        s_l, m_l = merged(s), merged(m)
        o = jnp.concatenate([pick(lo[0:Q_BLOCK], lo[Q_BLOCK:]), pick(hi[0:Q_BLOCK], hi[Q_BLOCK:])], axis=-1)
        outs.append((o * (1.0 / s_l), m_l + jnp.log2(s_l)))
    return outs


def _blocks_per_trip(count):
    return max(u for u in (5, 4, 3, 2, 1) if count % u == 0)


def _attn_prompt_body(q0, q1, q2, kv0, kv1, kv2, p0, p1, p2, bias_ref, yc_ref, osc, lsc, *, TM):
    qs_, kvs, prevs = (q0, q1, q2), (kv0, kv1, kv2), (p0, p1, p2)
    var = jnp.minimum(pl.program_id(1), 1)
    n_blocks = ATT_ROWS // Q_BLOCK

    def put(g, start, n, stride, o, l, r0):
        rows = pl.ds(start, n, stride=stride) if stride > 1 else pl.ds(start, n)
        for s in range(C_GW // LANES):
            osc[g, s, rows, :] = o[r0:r0 + n, s * LANES:(s + 1) * LANES]
            lsc[g, s, rows, :] = l[r0:r0 + n, s * LANES:(s + 1) * LANES]

    for g, (_, dil) in enumerate(C_PAIRS):
        q_ref, kv_ref, p_ref = qs_[g], kvs[g], prevs[g]
        cls = TM // dil
        if cls >= Q_BLOCK:
            prev_off = Q_BLOCK if dil == 1 else TM
            n_first = prev_off // Q_BLOCK
            per_tile = TM // Q_BLOCK

            def token_start(idx, dil=dil, per_tile=per_tile):
                return idx * Q_BLOCK if dil == 1 else (idx // per_tile) * TM + idx % per_tile

            first = []
            for idx in range(n_first):
                rows = slice(idx * Q_BLOCK, (idx + 1) * Q_BLOCK)
                first.append((q_ref[rows, :], p_ref[rows, :], kv_ref[rows, :], bias_ref[g, var]))
            for idx, (o, l) in enumerate(_attn_blocks(first)):
                put(g, token_start(idx), Q_BLOCK, dil, o, l, 0)

            per_trip = _blocks_per_trip(n_blocks - n_first)

            def body(trip, carry, g=g, dil=dil, q_ref=q_ref, kv_ref=kv_ref, prev_off=prev_off,
                     token_start=token_start, n_first=n_first, per_trip=per_trip):
                idxs = [n_first + trip * per_trip + u for u in range(per_trip)]
                blocks = []
                for idx in idxs:
                    st = pl.multiple_of(idx * Q_BLOCK, Q_BLOCK)
                    blocks.append((q_ref[pl.ds(st, Q_BLOCK), :], kv_ref[pl.ds(st - prev_off, Q_BLOCK), :],
                                   kv_ref[pl.ds(st, Q_BLOCK), :], bias_ref[g, 1]))
                for idx, (o, l) in zip(idxs, _attn_blocks(blocks)):
                    put(g, token_start(idx), Q_BLOCK, dil, o, l, 0)
                return carry

            lax.fori_loop(0, (n_blocks - n_first) // per_trip, body, 0)
        else:
            tiles = Q_BLOCK // cls
            per_trip = _blocks_per_trip(dil)

            def body(trip, carry, g=g, dil=dil, q_ref=q_ref, kv_ref=kv_ref, p_ref=p_ref, cls=cls, tiles=tiles,
                     per_trip=per_trip):
                rs = [trip * per_trip + u for u in range(per_trip)]
                blocks = []
                for r in rs:
                    st = pl.multiple_of(r * cls, cls)
                    gather = lambda ref, st=st: jnp.concatenate(
                        [ref[pl.ds(c * TM + st, cls), :] for c in range(tiles)], axis=0)
                    blocks.append((gather(q_ref), gather(p_ref), gather(kv_ref), bias_ref[g, var]))
                for r, (o, l) in zip(rs, _attn_blocks(blocks)):
                    for c in range(tiles):
                        put(g, c * TM + r, cls, dil, o, l, c * cls)
                return carry

            lax.fori_loop(0, dil // per_trip, body, 0)

    def merge(ch, carry):
        st = pl.multiple_of(ch * Q_BLOCK, Q_BLOCK)
        get = lambda ref, g: jnp.concatenate([ref[g, s, pl.ds(st, Q_BLOCK), :] for s in range(C_GW // LANES)], axis=-1)
        ls = [get(lsc, g) for g in range(len(C_PAIRS))]
        mm = jnp.maximum(jnp.maximum(ls[0], ls[1]), ls[2])
        es = [jnp.exp2(l - mm) for l in ls]
        den = es[0] + es[1] + es[2]
        yc = (es[0] / den) * get(osc, 0) + (es[1] / den) * get(osc, 1) + (es[2] / den) * get(osc, 2)
        yc_ref[pl.ds(st, Q_BLOCK), :] = yc.astype(BF16)
        return carry

    lax.fori_loop(0, n_blocks, merge, 0, unroll=2)


def _attn_prompt(qs, kvs, bias, *, TM):
    N, S, _ = qs[0].shape
    assert S % ATT_ROWS == 0 and ATT_ROWS == Q_BLOCK * C_PAIRS[-1][1] and ATT_ROWS % TM == 0
    cur = lambda c: pl.BlockSpec((None, ATT_ROWS, c), lambda n, i: (n, i, 0))
    prev_rows = [Q_BLOCK if dil == 1 else (TM if TM // dil >= Q_BLOCK else ATT_ROWS) for _, dil in C_PAIRS]
    prev = [pl.BlockSpec((None, pr, 2 * C_GW), lambda n, i, k=ATT_ROWS // pr: (n, jnp.maximum(i * k - 1, 0), 0))
            for pr in prev_rows]
    slabs = C_GW // LANES
    return pl.pallas_call(
        functools.partial(_attn_prompt_body, TM=TM),
        grid=(N, S // ATT_ROWS),
        in_specs=[cur(C_GW)] * 3 + [cur(2 * C_GW)] * 3 + prev + [_const_spec(bias.shape)],
        out_specs=cur(C_GW),
        out_shape=jax.ShapeDtypeStruct((N, S, C_GW), BF16),
        scratch_shapes=[pltpu.VMEM((len(C_PAIRS), slabs, ATT_ROWS, LANES), F32)] * 2,
        compiler_params=_cparams(2),
        name="attn_prompt",
    )(*qs, *kvs, *kvs, bias)


SROWS = 32
CACHE_SLOTS = 4


def _attn_sample_body(q_ref, kvn_ref, c0_hbm, c1_hbm, c2_hbm, bc0_ref, bc1_ref, bc2_ref, bn_ref,
                      yc_ref, n0_ref, n1_ref, n2_ref, ring0, ring1, ring2, sems):
    step, n_steps = pl.program_id(0), pl.num_programs(0)
    hbms, rings = (c0_hbm, c1_hbm, c2_hbm), (ring0, ring1, ring2)
    ahead = CACHE_SLOTS - 1

    def fetch(g, k):
        slot = k % CACHE_SLOTS if isinstance(k, int) else lax.rem(k, CACHE_SLOTS)
        return pltpu.make_async_copy(hbms[g].at[k], rings[g].at[slot], sems.at[g, slot])

    @pl.when(step == 0)
    def _():
        for k in range(ahead):
            for g in range(len(hbms)):
                fetch(g, k).start()

    @pl.when(step + ahead < n_steps)
    def _():
        for g in range(len(hbms)):
            fetch(g, step + ahead).start()

    for g in range(len(hbms)):
        fetch(g, step).wait()
    slot = lax.rem(step, CACHE_SLOTS)
    _attn_sample_one(q_ref.at[0], kvn_ref.at[0], tuple(r.at[slot] for r in rings),
                     (bc0_ref, bc1_ref, bc2_ref), bn_ref, yc_ref.at[0],
                     (n0_ref.at[0], n1_ref.at[0], n2_ref.at[0]))


def _attn_sample_one(q_ref, kvn_ref, caches, bcs, bn_ref, yc_ref, news):
    lane_head = _head_of_lane((SROWS, C_GW))
    row_head = lax.shift_right_logical(lax.broadcasted_iota(jnp.int32, (SROWS, C_GW), 0), 3)
    own = lane_head == row_head
    lane128 = lax.broadcasted_iota(jnp.int32, (SROWS, LANES), 1)
    kvn = kvn_ref[...]
    r0 = 8 - DEC_SEQ
    outs, lses = [], []
    for g, (win, dil) in enumerate(C_PAIRS):
        wb = win
        qg = q_ref[:, g * C_GW:(g + 1) * C_GW]
        qs = jnp.where(own, qg, jnp.zeros_like(qg))
        cache = caches[g]
        kt = cache[0:C_GW, :].astype(BF16)
        vt = cache[C_GW:2 * C_GW, :].astype(BF16)
        kn = kvn[:, g * C_GW:(g + 1) * C_GW]
        vn = kvn[:, C_Q + g * C_GW:C_Q + (g + 1) * C_GW]
        lc = jnp.dot(qs, kt, preferred_element_type=F32) + bcs[g][...]
        qf = qs.astype(F32)
        ln = bn_ref[g]
        for c in range(DEC_SEQ):
            d = jnp.sum(qf * kn[r0 + c:r0 + c + 1, :], axis=-1, keepdims=True)
            ln = ln + jnp.where(lane128 == c, d, 0.0)
        m = jnp.maximum(jnp.max(lc, axis=-1, keepdims=True), jnp.max(ln, axis=-1, keepdims=True))
        pc = jnp.exp(lc - m)
        pn = jnp.exp(ln - m)
        s = jnp.sum(pc, axis=-1, keepdims=True) + jnp.sum(pn, axis=-1, keepdims=True)
        pv = lax.dot_general(pc.astype(BF16), vt, (((1,), (1,)), ((), ())), preferred_element_type=F32)
        for c in range(DEC_SEQ):
            pcol = jnp.sum(jnp.where(lane128 == c, pn, 0.0), axis=-1, keepdims=True)
            pv = pv + pcol * vn[r0 + c:r0 + c + 1, :]
        outs.append(pv / s)
        lses.append(m + jnp.log(s))
        rolled = pltpu.roll(cache[...], wb - DEC_SEQ, 1)
        new_rows = jnp.concatenate([jnp.zeros((LANES - SUBLANES, 2 * C_GW), F32),
                                    jnp.concatenate([kn, vn], axis=-1)], axis=0)
        new_cols = new_rows.T
        lane_t = lax.broadcasted_iota(jnp.int32, (2 * C_GW, LANES), 1)
        if wb > LANES:
            news[g][:, 0:wb - LANES] = rolled[:, 0:wb - LANES]
        news[g][:, wb - LANES:wb] = jnp.where(lane_t >= LANES - DEC_SEQ, new_cols, rolled[:, wb - LANES:wb])
    mm = jnp.maximum(jnp.maximum(lses[0], lses[1]), lses[2])
    es = [jnp.exp(l - mm) for l in lses]
    den = es[0] + es[1] + es[2]
    y = (es[0] / den) * outs[0] + (es[1] / den) * outs[1] + (es[2] / den) * outs[2]
    y = jnp.where(own, y, 0.0)
    yc_ref[...] = y[0:8, :] + y[8:16, :] + y[16:24, :] + y[24:32, :]


def _attn_sample(q_rep, kvn, caches, bias_c, bias_n):
    NBt = q_rep.shape[0]
    assert NBt >= CACHE_SLOTS
    per_b = lambda r, c: pl.BlockSpec((1, r, c), lambda b: (b, 0, 0))
    wbs = [w for w, _ in C_PAIRS]
    return pl.pallas_call(
        _attn_sample_body,
        grid=(NBt,),
        in_specs=[per_b(SROWS, C_Q), per_b(8, 2 * C_Q)] + [pl.BlockSpec(memory_space=pl.ANY)] * len(wbs)
                 + [_const_spec((SROWS, w)) for w in wbs] + [_const_spec((len(wbs), SROWS, LANES))],
        out_specs=[per_b(8, C_GW)] + [per_b(2 * C_GW, w) for w in wbs],
        out_shape=[jax.ShapeDtypeStruct((NBt, 8, C_GW), F32)]
                  + [jax.ShapeDtypeStruct((NBt, 2 * C_GW, w), F32) for w in wbs],
        scratch_shapes=[pltpu.VMEM((CACHE_SLOTS, 2 * C_GW, w), F32) for w in wbs]
                       + [pltpu.SemaphoreType.DMA((len(wbs), CACHE_SLOTS))],
        compiler_params=_cparams(1),
        name="attn_sample",
    )(q_rep, kvn, *caches, *bias_c, bias_n)


def _t5_bucket(dist):
    dist = np.asarray(dist)
    max_exact = N_BUCKETS // 2
    large = max_exact + (np.log(np.maximum(dist, max_exact) / max_exact) / np.log(MAX_DISTANCE / max_exact)
                         * (N_BUCKETS - max_exact)).astype(np.int32)
    large = np.minimum(large, N_BUCKETS - 1)
    return np.where(dist < max_exact, dist, large).astype(np.int32)


def _group_bias(rel_bias, g, dil):
    buckets = _t5_bucket(dil * np.arange(C_TAPS + 1))
    return rel_bias[buckets][:, g * C_HPG:(g + 1) * C_HPG].T


def _toeplitz_body(c_ref, o_ref):
    keep = lax.broadcasted_iota(jnp.int32, (Q_BLOCK, 2 * Q_BLOCK), 1) >= Q_BLOCK
    for h in range(C_HPG):
        taps = jnp.broadcast_to(c_ref[h:h + 1, :], (Q_BLOCK, 2 * Q_BLOCK))
        t = pltpu.roll(taps, 0, 1, stride=1, stride_axis=0)
        o_ref[1, h * Q_BLOCK:(h + 1) * Q_BLOCK, :] = t
        o_ref[0, h * Q_BLOCK:(h + 1) * Q_BLOCK, :] = jnp.where(keep, t, NEG)


def _prompt_bias(bias_gs):
    c = jnp.stack([jnp.concatenate([b[:, ::-1].astype(F32) * LOG2E, jnp.full((C_HPG, Q_BLOCK - 1), NEG, F32)], axis=1)
                   for b in bias_gs])
    return pl.pallas_call(
        _toeplitz_body,
        grid=(len(bias_gs),),
        in_specs=[pl.BlockSpec((None, C_HPG, 2 * Q_BLOCK), lambda g: (g, 0, 0))],
        out_specs=pl.BlockSpec((None, 2, C_HPG * Q_BLOCK, 2 * Q_BLOCK), lambda g: (g, 0, 0, 0)),
        out_shape=jax.ShapeDtypeStruct((len(bias_gs), 2, C_HPG * Q_BLOCK, 2 * Q_BLOCK), F32),
        compiler_params=_cparams(1),
        name="attn_bias",
    )(c)


def _sample_bias(bias_g, wb, dil):
    n = wb + DEC_SEQ
    t = np.arange(DEC_SEQ)[:, None]
    d = wb + t - np.arange(n)[None, :]
    valid = (d >= 0) & (d % dil == 0) & (d // dil <= C_TAPS)
    place = np.zeros((DEC_SEQ, C_TAPS + 1, n), np.float32)
    ti, ii = np.nonzero(valid)
    place[ti, (d // dil)[ti, ii], ii] = 1.0
    ext = jnp.einsum("hj,tji->hti", bias_g.astype(F32), place, precision=lax.Precision.HIGHEST)
    ext = jnp.where(valid[None], ext, NEG)
    pad_rows = (0, SUBLANES - DEC_SEQ)
    bc = jnp.pad(ext[:, :, :wb], ((0, 0), pad_rows, (0, 0))).reshape(SROWS, wb)
    bn = jnp.pad(ext[:, :, wb:], ((0, 0), pad_rows, (0, LANES - DEC_SEQ)), constant_values=NEG).reshape(SROWS, LANES)
    return bc, bn


def _time_major(s):
    b, k, c = s.shape
    return s.transpose(1, 0, 2).reshape(1, k * b, c)


def _batch_major(s, b):
    _, r, c = s.shape
    return s.reshape(r // b, b, c).transpose(1, 0, 2)


def _stack(x, mod, st, w, *, TM, TM_WIDE, RS, pos0):
    new = {}
    ng = w["norm_g"]
    cast_keys = [k for k in ("ffn_w_up", "ffn_w_down", "cd_w_in", "cd_w_out") if w[k].dtype != BF16]
    x, new["a"], new["b"], cast_out = _mixer_ab(
        x, mod[0], ng[0], st["a"], st["b"], w["ab_w_in"], w["a_conv_w"], w["b_w_grp"], w["b_scale"], w["ab_w_out"],
        TM=TM_WIDE, RS=RS, pos0=pos0, cast=tuple(w[k] for k in cast_keys))
    w = dict(w, **dict(zip(cast_keys, cast_out)))
    x, new["f0"] = _conv_ffn(x, mod[0], ng[0], st["f0"], w["ffn_w_up"], w["ffn_conv_w"],
                             w["ffn_conv_b"], w["ffn_w_down"], layer=0, TM=TM_WIDE, RS=RS)
    prompt = RS == 1
    outs = _cd_in(x, mod[1], ng[1], st["d"], w["cd_w_in"][0], w["d_conv_w"], w["d_conv_b"], w["d_ln_g"], w["d_ln_b"],
                  TM=TM, RS=RS, prompt_attn=prompt)
    qs, kvs, yd, new["d"] = outs[0:3], outs[3:6], outs[-2], outs[-1]
    cache_layout = lambda n: n.reshape(n.shape[0], 2, C_HPG, C_HEAD_DIM, n.shape[2]).transpose(0, 4, 1, 2, 3)
    if prompt:
        yc = _attn_prompt(qs, kvs, _prompt_bias(w["bias_g"]), TM=TM)
        for g in range(len(C_PAIRS)):
            new["c%d" % g] = cache_layout(outs[6 + g])
    else:
        B = RS
        kvf = outs[6]
        qb = _batch_major(jnp.concatenate(qs, axis=-1), B)
        q_rep = jnp.pad(jnp.broadcast_to(qb[:, None], (B, C_HPG, DEC_SEQ, C_Q)),
                        ((0, 0), (0, 0), (0, 8 - DEC_SEQ), (0, 0))).reshape(B, SROWS, C_Q)
        kvn = jnp.pad(_batch_major(kvf, B), ((0, 0), (8 - DEC_SEQ, 0), (0, 0)))
        bias = [_sample_bias(w["bias_g"][g], win, dil) for g, (win, dil) in enumerate(C_PAIRS)]
        yc, n0, n1, n2 = _attn_sample(q_rep, kvn, st["c"], [b[0] for b in bias],
                                      jnp.stack([b[1] for b in bias]))
        for g, n in enumerate((n0, n1, n2)):
            new["c%d" % g] = cache_layout(n)
        yc = _time_major(yc[:, :DEC_SEQ])
    x, new["f1"] = _conv_ffn(x, mod[1], ng[1], st["f1"], w["ffn_w_up"], w["ffn_conv_w"],
                             w["ffn_conv_b"], w["ffn_w_down"], layer=1, TM=TM_WIDE, RS=RS,
                             mixer_out=(yc, yd, w["cd_w_out"][0]))
    return x, new, w


def kernel(x_prompt, x_sample, state_a_conv, state_b_pool, cache_c_win128, cache_c_win512, cache_c_win2048,
           state_d_conv, state_ffn_conv, c_prompt, c_sample, ada_w, ada_b, norm_g, rel_bias, ab_w_in, a_conv_w,
           b_w_grp, b_scale, ab_w_out, cd_w_in, d_conv_w, d_conv_b, d_ln_g, d_ln_b, cd_w_out, ffn_w_up,
           ffn_conv_w, ffn_conv_b, ffn_w_down):
    B, T = DEC_BATCH, DEC_SEQ
    w = dict(norm_g=norm_g,
             ab_w_in=ab_w_in[0].astype(BF16), a_conv_w=a_conv_w[0], b_w_grp=b_w_grp[0].astype(BF16),
             b_scale=b_scale, ab_w_out=ab_w_out[0].astype(BF16),
             cd_w_in=cd_w_in, cd_w_out=cd_w_out,
             d_conv_w=jnp.broadcast_to(d_conv_w[0][:, None, :], (D_CONV, SUBLANES, D_WIDTH)),
             d_conv_b=d_conv_b, d_ln_g=d_ln_g, d_ln_b=d_ln_b,
             ffn_w_up=ffn_w_up, ffn_conv_w=ffn_conv_w, ffn_conv_b=ffn_conv_b[:, None, :], ffn_w_down=ffn_w_down,
             bias_g=[_group_bias(rel_bias, g, dil) for g, (_, dil) in enumerate(C_PAIRS)])

    mod = _ada(jnp.concatenate([c_prompt, c_sample], axis=0), ada_w, ada_b)
    mod_p = mod[:, :BATCH].reshape(DEPTH, BATCH, 6, 1, D_MODEL)
    mod_s = mod[:, BATCH:].reshape(DEPTH, B, 6, D_MODEL).transpose(0, 2, 1, 3)
    mod_s = jnp.broadcast_to(mod_s[:, :, None], (DEPTH, 6, T, B, D_MODEL)).reshape(DEPTH, 1, 6, T * B, D_MODEL)

    zeros = lambda k, c: jnp.zeros((BATCH, k, c), F32)
    st_p = dict(a=zeros(A_CONV - 1, A_WIDTH), b=zeros(B_PREV, B_WIDTH), d=zeros(D_CONV - 1, D_WIDTH),
                f0=zeros(FFN_CONV - 1, 2 * D_FF), f1=zeros(FFN_CONV - 1, 2 * D_FF))
    y_p, np_, w = _stack(x_prompt, mod_p, st_p, w, TM=TM_PROMPT, TM_WIDE=TM_WIDE_PROMPT, RS=1, pos0=0)

    st_s = dict(a=_time_major(state_a_conv[0]), b=_time_major(state_b_pool[0]), d=_time_major(state_d_conv[0]),
                f0=_time_major(state_ffn_conv[0]), f1=_time_major(state_ffn_conv[1]),
                c=[c[0].transpose(0, 2, 3, 4, 1).reshape(B, 2 * C_GW, c.shape[2])
                   for c in (cache_c_win128, cache_c_win512, cache_c_win2048)])
    y_s, ns, _ = _stack(_time_major(x_sample), mod_s, st_s, w, TM=T * B, TM_WIDE=T * B, RS=B, pos0=PAST_LEN)

    bm = lambda s: _batch_major(s, B)
    return (y_p, bm(y_s),
            np_["a"][None], bm(ns["a"])[None], np_["b"][None], bm(ns["b"])[None],
            np_["c0"][None], ns["c0"][None], np_["c1"][None], ns["c1"][None], np_["c2"][None], ns["c2"][None],
            np_["d"][None], bm(ns["d"])[None],
            jnp.stack([np_["f0"], np_["f1"]]), jnp.stack([bm(ns["f0"]), bm(ns["f1"])]))
```
